```python
import math
import jax, jax.numpy as jnp
from jax import lax
import numpy as np

D_MODEL = 1024
BATCH = 16
SEQ = 2048
DEPTH = 1
DEC_BATCH = 32
DEC_SEQ = 1
PAST_LEN = 16384
PAGE_SIZE = 128

N_MEM = 256
CHUNK = 128
A_GROUPS = 4
A_GROUP_DIM = 128
A_WIDTH = A_GROUPS * A_GROUP_DIM
B_PAIRS = ((128, 1), (512, 4), (2048, 16))
B_HEADS_PER_GROUP = 4
B_HEADS = B_HEADS_PER_GROUP * len(B_PAIRS)
B_HEAD_DIM = 64
B_WIDTH = B_HEADS * B_HEAD_DIM
B_OUT = B_HEADS_PER_GROUP * B_HEAD_DIM
BAND_BLOCK = 128
C_HEADS = 4
C_HEAD_DIM = 128
C_WIDTH = C_HEADS * C_HEAD_DIM
N_BRANCH = 3
REL_BUCKETS = 32
REL_MAX_DIST = 2048
N_EXPERTS = 32
TOP_K = 4
D_FF = D_MODEL
SWIGLU_LIMIT = 7.0
SWIGLU_ALPHA = 1.702
MOE_BLOCK = 128
EPS = 1e-6
NEG_INF = -1e30
IN_SPLITS = (B_WIDTH, 2 * B_WIDTH, 3 * B_WIDTH, 3 * B_WIDTH + A_WIDTH, 3 * B_WIDTH + 2 * A_WIDTH, 3 * B_WIDTH + 2 * A_WIDTH + C_WIDTH)
IN_WIDTH = 3 * B_WIDTH + 2 * A_WIDTH + C_WIDTH + N_BRANCH * D_MODEL

kernel_name = 'hybrid_gmlp_dilated_moe_step'


def rmsnorm(x, g):
    xf = x.astype(jnp.float32)
    xf = xf * lax.rsqrt(jnp.mean(xf * xf, axis=-1, keepdims=True) + EPS)
    return (xf * g.astype(jnp.float32)).astype(x.dtype)


def layernorm(x, g, b):
    xf = x.astype(jnp.float32)
    xc = xf - jnp.mean(xf, axis=-1, keepdims=True)
    xf = xc * lax.rsqrt(jnp.mean(xc * xc, axis=-1, keepdims=True) + EPS)
    return (xf * g.astype(jnp.float32) + b.astype(jnp.float32)).astype(x.dtype)


def t5_bucket(dist):
    dist = np.maximum(np.asarray(dist), 0)
    max_exact = REL_BUCKETS // 2
    log_ratio = np.log(np.maximum(dist, max_exact) / max_exact) / math.log(REL_MAX_DIST / max_exact)
    large = np.minimum(max_exact + (log_ratio * (REL_BUCKETS - max_exact)).astype(np.int32), REL_BUCKETS - 1)
    return np.where(dist < max_exact, dist, large).astype(np.int32)


def chunk_spatial_gate(u, v, w_spatial, b_spatial):
    b, t, g, c = v.shape
    n_chunks = -(-t // CHUNK)
    vp = jnp.pad(v, ((0, 0), (0, n_chunks * CHUNK - t), (0, 0), (0, 0))).reshape(b, n_chunks, CHUNK, g, c)
    w_causal = jnp.where(np.tril(np.ones((CHUNK, CHUNK), bool)), w_spatial, 0.0)
    s = jnp.einsum('gij,bnjgc->bnigc', w_causal, vp) + b_spatial.T[:, :, None]
    return u * s.reshape(b, n_chunks * CHUNK, g, c)[:, :t]


def mixer_inputs(x, norm_mix, w_in, ln_v_g, ln_v_b, w_spatial, b_spatial):
    b, t, _ = x.shape
    h = rmsnorm(x, norm_mix)
    q_b, k_b, v_b, z_u, z_v, q_c, gate_logits = jnp.split(h @ w_in, IN_SPLITS, axis=-1)
    u = jax.nn.gelu(z_u).reshape(b, t, A_GROUPS, A_GROUP_DIM)
    v = layernorm(jax.nn.gelu(z_v), ln_v_g, ln_v_b).reshape(b, t, A_GROUPS, A_GROUP_DIM)
    a_out = chunk_spatial_gate(u, v, w_spatial, b_spatial).reshape(b, t, A_WIDTH)
    heads = (b, t, B_HEADS, B_HEAD_DIM)
    return (q_b.reshape(heads), k_b.reshape(heads), v_b.reshape(heads),
            q_c.reshape(b, t, C_HEADS, C_HEAD_DIM), gate_logits, a_out, v)


def dilated_band_attention(q, k, v, bias_table, window, dilation):
    b, s, h, hd = q.shape
    n = s // dilation
    reach = window // dilation
    blk = BAND_BLOCK
    nb = -(-n // blk)

    def subseq(t, front):
        t = t.reshape(b, n, dilation, h, hd).transpose(0, 2, 1, 3, 4)
        return jnp.pad(t, ((0, 0), (0, 0), (front, nb * blk - n), (0, 0), (0, 0)))

    def band(t):
        tb = subseq(t, blk).reshape(b, dilation, nb + 1, blk, h, hd)
        return jnp.concatenate([tb[:, :, :-1], tb[:, :, 1:]], axis=3)

    qb = subseq(q, 0).reshape(b, dilation, nb, blk, h, hd)
    kb, vb = band(k), band(v)
    steps = np.arange(blk)[:, None] + blk - np.arange(2 * blk)[None, :]
    key_idx = np.arange(nb)[:, None, None] * blk + np.arange(2 * blk)[None, None, :] - blk
    mask = ((steps >= 0) & (steps <= reach))[None] & (key_idx >= 0)
    bias = jnp.transpose(bias_table[t5_bucket(np.clip(steps, 0, reach) * dilation)], (2, 0, 1))
    logits = jnp.einsum('brnqhc,brnkhc->brnhqk', qb, kb).astype(jnp.float32) / math.sqrt(hd)
    logits = jnp.where(mask[:, None], logits + bias.astype(jnp.float32), NEG_INF)
    lse = jax.nn.logsumexp(logits, axis=-1)
    p = jnp.exp(logits - lse[..., None])
    out = jnp.einsum('brnhqk,brnkhc->brnqhc', p.astype(v.dtype), vb)
    out = out.reshape(b, dilation, nb * blk, h, hd)[:, :, :n].transpose(0, 2, 1, 3, 4).reshape(b, s, h, hd)
    lse = lse.transpose(0, 1, 2, 4, 3).reshape(b, dilation, nb * blk, h)[:, :, :n].transpose(0, 2, 1, 3).reshape(b, s, h)
    return out, lse


def dilated_cached_attention(q, kv_all, bias_table, window, dilation):
    b, t, h, hd = q.shape
    l = kv_all.shape[1] - t
    steps = np.arange(window // dilation + 1)
    idx = l + np.arange(t)[:, None] - steps[None, :] * dilation
    valid = idx >= 0
    kv = kv_all[:, np.maximum(idx, 0)]
    bias = bias_table[t5_bucket(steps * dilation)].T.astype(jnp.float32)
    logits = jnp.einsum('bthc,btjhc->bhtj', q, kv[:, :, :, 0]).astype(jnp.float32) / math.sqrt(hd)
    logits = jnp.where(valid[None, None], logits + bias[:, None, :], NEG_INF)
    lse = jax.nn.logsumexp(logits, axis=-1)
    p = jnp.exp(logits - lse[..., None])
    out = jnp.einsum('bhtj,btjhc->bthc', p.astype(q.dtype), kv[:, :, :, 1])
    return out, lse.transpose(0, 2, 1)


def combine_by_denominator(outs, lses):
    w = jax.nn.softmax(jnp.stack(lses), axis=0)
    o = jnp.einsum('gbth,gbthc->bthc', w.astype(outs[0].dtype), jnp.stack(outs))
    return o.reshape(o.shape[0], o.shape[1], B_OUT)


def memory_kv(mem, norm_mem, w_mem_kv):
    b, m, _ = mem.shape
    return (rmsnorm(mem, norm_mem) @ w_mem_kv).reshape(b, m, 2, C_HEADS, C_HEAD_DIM)


def memory_attention(q, mem_kv):
    b, t = q.shape[:2]
    logits = jnp.einsum('bthc,bmhc->bhtm', q, mem_kv[:, :, 0]).astype(jnp.float32) / math.sqrt(C_HEAD_DIM)
    p = jax.nn.softmax(logits, axis=-1)
    out = jnp.einsum('bhtm,bmhc->bthc', p.astype(q.dtype), mem_kv[:, :, 1])
    return out.reshape(b, t, C_WIDTH)


def merge_branches(x, a_out, b_out, c_out, gate_logits, w_branch_a, w_branch_b, w_branch_c, w_out):
    gates = jax.nn.sigmoid(gate_logits.reshape(*gate_logits.shape[:-1], N_BRANCH, D_MODEL))
    merged = (gates[..., 0, :] * (a_out @ w_branch_a) + gates[..., 1, :] * (b_out @ w_branch_b)
              + gates[..., 2, :] * (c_out @ w_branch_c))
    return x + merged @ w_out


def moe_swiglu(h, w_router, b_router, w_gate_up, b_gate_up, w_down, b_down):
    lead = h.shape[:-1]
    xt = h.reshape(-1, D_MODEL)
    n_tok = xt.shape[0]
    n_rows = n_tok * TOP_K
    logits = (xt @ w_router + b_router).astype(jnp.float32)
    top_vals, top_idx = lax.top_k(logits, TOP_K)
    gates = jax.nn.softmax(top_vals, axis=-1)
    flat_e = top_idx.reshape(-1)
    order = jnp.argsort(flat_e)
    sorted_e = flat_e[order]
    counts = jnp.bincount(flat_e, length=N_EXPERTS)
    padded = (counts + MOE_BLOCK - 1) // MOE_BLOCK * MOE_BLOCK
    start = jnp.cumsum(counts) - counts
    pend = jnp.cumsum(padded)
    dest = (pend - padded)[sorted_e] + jnp.arange(n_rows) - start[sorted_e]
    n_blocks = -(-(n_rows + N_EXPERTS * (MOE_BLOCK - 1)) // MOE_BLOCK)
    cap = n_blocks * MOE_BLOCK
    x_disp = jnp.zeros((cap, D_MODEL), xt.dtype).at[dest].set(xt[order // TOP_K])
    block_expert = jnp.clip(jnp.searchsorted(pend, jnp.arange(n_blocks) * MOE_BLOCK, side='right'), 0, N_EXPERTS - 1)

    def expert_block(args):
        xb, e = args
        gate, up = jnp.split(xb @ w_gate_up[e] + b_gate_up[e], 2, axis=-1)
        gate = jnp.minimum(gate, SWIGLU_LIMIT)
        up = jnp.clip(up, -SWIGLU_LIMIT, SWIGLU_LIMIT)
        act = gate * jax.nn.sigmoid(SWIGLU_ALPHA * gate) * (up + 1.0)
        return act @ w_down[e] + b_down[e]

    y_disp = lax.map(expert_block, (x_disp.reshape(n_blocks, MOE_BLOCK, D_MODEL), block_expert))
    y_sorted = y_disp.reshape(cap, D_MODEL)[dest]
    y_rows = jnp.zeros((n_rows, D_MODEL), y_sorted.dtype).at[order].set(y_sorted)
    y = jnp.einsum('nk,nkd->nd', gates.astype(y_rows.dtype), y_rows.reshape(n_tok, TOP_K, D_MODEL))
    return y.reshape(*lead, D_MODEL)


def channel_mixer(x, norm_ffn, w_router, b_router, w_gate_up, b_gate_up, w_down, b_down):
    return x + moe_swiglu(rmsnorm(x, norm_ffn), w_router, b_router, w_gate_up, b_gate_up, w_down, b_down)


def setup_inputs(seed: int = 0) -> dict:
    key = jax.random.key(seed)
    ks = iter(jax.random.split(key, 32))

    def nrm(shape, scale=1.0):
        return scale * jax.random.normal(next(ks), shape, jnp.float32)

    def gain(shape):
        return 1.0 + nrm(shape, 0.02)

    win_caches = [nrm((DEPTH, DEC_BATCH, min(w, PAST_LEN), 2, B_HEADS_PER_GROUP, B_HEAD_DIM)) for w, _ in B_PAIRS]
    return {
        'x_prompt': nrm((BATCH, SEQ, D_MODEL)),
        'x_sample': nrm((DEC_BATCH, DEC_SEQ, D_MODEL)),
        'cache_win0_kv': win_caches[0],
        'cache_win1_kv': win_caches[1],
        'cache_win2_kv': win_caches[2],
        'cache_mem_kv': nrm((DEPTH, DEC_BATCH, N_MEM, 2, C_HEADS, C_HEAD_DIM)),
        'mem_prompt': nrm((BATCH, N_MEM, D_MODEL)),
        'rel_bias': nrm((REL_BUCKETS, B_HEADS), 0.5),
        'norm_mix': gain((DEPTH, D_MODEL)),
        'norm_mem': gain((DEPTH, D_MODEL)),
        'w_in': nrm((DEPTH, D_MODEL, IN_WIDTH), D_MODEL ** -0.5),
        'ln_v_g': gain((DEPTH, A_WIDTH)),
        'ln_v_b': nrm((DEPTH, A_WIDTH), 0.02),
        'w_spatial': nrm((DEPTH, A_GROUPS, CHUNK, CHUNK), 0.5 * CHUNK ** -0.5),
        'b_spatial': 1.0 + nrm((DEPTH, A_GROUPS, CHUNK), 0.1),
        'w_mem_kv': nrm((DEPTH, D_MODEL, 2 * C_WIDTH), D_MODEL ** -0.5),
        'w_branch_a': nrm((DEPTH, A_WIDTH, D_MODEL), A_WIDTH ** -0.5),
        'w_branch_b': nrm((DEPTH, B_OUT, D_MODEL), B_OUT ** -0.5),
        'w_branch_c': nrm((DEPTH, C_WIDTH, D_MODEL), C_WIDTH ** -0.5),
        'w_out': nrm((DEPTH, D_MODEL, D_MODEL), D_MODEL ** -0.5),
        'norm_ffn': gain((DEPTH, D_MODEL)),
        'w_router': nrm((DEPTH, D_MODEL, N_EXPERTS), D_MODEL ** -0.5),
        'b_router': nrm((DEPTH, N_EXPERTS), 0.01),
        'w_gate_up': nrm((DEPTH, N_EXPERTS, D_MODEL, 2 * D_FF), D_MODEL ** -0.5),
        'b_gate_up': nrm((DEPTH, N_EXPERTS, 2 * D_FF), 0.02),
        'w_down': nrm((DEPTH, N_EXPERTS, D_FF, D_MODEL), D_FF ** -0.5),
        'b_down': nrm((DEPTH, N_EXPERTS, D_MODEL), 0.02),
        'norm_final': gain((D_MODEL,)),
    }


def reference(x_prompt, x_sample, cache_win0_kv, cache_win1_kv, cache_win2_kv, cache_mem_kv, mem_prompt,
              rel_bias, norm_mix, norm_mem, w_in, ln_v_g, ln_v_b, w_spatial, b_spatial, w_mem_kv,
              w_branch_a, w_branch_b, w_branch_c, w_out, norm_ffn, w_router, b_router,
              w_gate_up, b_gate_up, w_down, b_down, norm_final):
    caches = (cache_win0_kv, cache_win1_kv, cache_win2_kv)
    xp, xs = x_prompt, x_sample
    t_p = x_prompt.shape[1]
    win_p = [[] for _ in B_PAIRS]
    win_s = [[] for _ in B_PAIRS]
    mem_p, chunk_s = [], []
    for layer in range(DEPTH):
        mix_w = (norm_mix[layer], w_in[layer], ln_v_g[layer], ln_v_b[layer], w_spatial[layer], b_spatial[layer])
        merge_w = (w_branch_a[layer], w_branch_b[layer], w_branch_c[layer], w_out[layer])
        ffn_w = (norm_ffn[layer], w_router[layer], b_router[layer], w_gate_up[layer], b_gate_up[layer],
                 w_down[layer], b_down[layer])

        q_b, k_b, v_b, q_c, gate_logits, a_out, _ = mixer_inputs(xp, *mix_w)
        kv_b = jnp.stack([k_b, v_b], axis=2)
        outs, lses = [], []
        for g, (window, dilation) in enumerate(B_PAIRS):
            hs = slice(g * B_HEADS_PER_GROUP, (g + 1) * B_HEADS_PER_GROUP)
            o, l = dilated_band_attention(q_b[:, :, hs], k_b[:, :, hs], v_b[:, :, hs], rel_bias[:, hs], window, dilation)
            outs.append(o)
            lses.append(l)
            win_p[g].append(kv_b[:, t_p - min(window, t_p):, :, hs])
        mem_kv = memory_kv(mem_prompt, norm_mem[layer], w_mem_kv[layer])
        mem_p.append(mem_kv)
        xp = merge_branches(xp, a_out, combine_by_denominator(outs, lses), memory_attention(q_c, mem_kv),
                            gate_logits, *merge_w)
        xp = channel_mixer(xp, *ffn_w)

        q_b, k_b, v_b, q_c, gate_logits, a_out, v_rows = mixer_inputs(xs, *mix_w)
        kv_b = jnp.stack([k_b, v_b], axis=2)
        outs, lses = [], []
        for g, (window, dilation) in enumerate(B_PAIRS):
            hs = slice(g * B_HEADS_PER_GROUP, (g + 1) * B_HEADS_PER_GROUP)
            kv_all = jnp.concatenate([caches[g][layer], kv_b[:, :, :, hs]], axis=1)
            o, l = dilated_cached_attention(q_b[:, :, hs], kv_all, rel_bias[:, hs], window, dilation)
            outs.append(o)
            lses.append(l)
            n_all = kv_all.shape[1]
            win_s[g].append(kv_all[:, n_all - min(window, n_all):])
        chunk_s.append(v_rows)
        xs = merge_branches(xs, a_out, combine_by_denominator(outs, lses), memory_attention(q_c, cache_mem_kv[layer]),
                            gate_logits, *merge_w)
        xs = channel_mixer(xs, *ffn_w)

    y_prompt = rmsnorm(xp, norm_final)
    y_sample = rmsnorm(xs, norm_final)
    return (y_prompt, y_sample,
            jnp.stack(win_p[0]), jnp.stack(win_p[1]), jnp.stack(win_p[2]), jnp.stack(mem_p),
            jnp.stack(win_s[0]), jnp.stack(win_s[1]), jnp.stack(win_s[2]), jnp.stack(chunk_s))
```

```python
import functools
import math

import numpy as np
import jax
import jax.numpy as jnp
from jax import lax
from jax.experimental import pallas as pl
from jax.experimental.pallas import tpu as pltpu

F32 = jnp.float32
BF16 = jnp.bfloat16
I32 = jnp.int32
HIGHEST = lax.Precision.HIGHEST

D_MODEL = 1024
N_MEM = 256
CHUNK = 128
A_GROUPS = 4
A_WIDTH = 512
B_PAIRS = ((128, 1), (512, 4), (2048, 16))
B_GROUP_W = 256
B_WIDTH = 768
B_HEAD_DIM = 64
BAND = 128
C_HEADS = 4
C_HEAD_DIM = 128
C_WIDTH = 512
REL_BUCKETS = 32
REL_MAX_DIST = 2048
N_EXPERTS = 32
TOP_K = 4
D_FF = 1024
SWIGLU_LIMIT = 7.0
SWIGLU_ALPHA = 1.702
EPS = 1e-6
NEG_INF = -1e30
MIX_WIDTH = 3 * B_WIDTH + 2 * A_WIDTH + C_WIDTH

LANES = 128
SUBLANES = 8
ROW_TILES = D_MODEL // LANES
VMEM_LIMIT = 56 * 1024 * 1024

TM_PROJ = 512
TM_MERGE = 256
TM_ROUTE = 512
TM_SAMPLE = 128
BM_EXPERT = 256
TM_COMBINE = 256


def _cparams(*sem):
    return pltpu.CompilerParams(dimension_semantics=sem, vmem_limit_bytes=VMEM_LIMIT)


def _t5_bucket(dist):
    dist = np.maximum(np.asarray(dist), 0)
    max_exact = REL_BUCKETS // 2
    log_ratio = np.log(np.maximum(dist, max_exact) / max_exact) / math.log(REL_MAX_DIST / max_exact)
    large = np.minimum(max_exact + (log_ratio * (REL_BUCKETS - max_exact)).astype(np.int32), REL_BUCKETS - 1)
    return np.where(dist < max_exact, dist, large).astype(np.int32)


def _gelu(x):
    c = math.sqrt(2.0 / math.pi)
    return x * (0.5 * (1.0 + jnp.tanh(c * (x + 0.044715 * (x * x * x)))))


def _rms(x, g):
    return x * lax.rsqrt(jnp.mean(x * x, axis=-1, keepdims=True) + EPS) * g


def _layernorm(x, g, b):
    xc = x - jnp.mean(x, axis=-1, keepdims=True)
    return xc * lax.rsqrt(jnp.mean(xc * xc, axis=-1, keepdims=True) + EPS) * g + b


def _proj_kernel(x_ref, g_ref, w_ref, lng_ref, lnb_ref, wsp_ref, bsp_ref,
                 qkv0_ref, qkv1_ref, qkv2_ref, win0_ref, win1_ref, win2_ref, a_ref, qc_ref, scr_ref):
    t = pl.program_id(1)
    last = pl.num_programs(1) - 1
    tm = x_ref.shape[1]
    hb = _rms(x_ref[0], g_ref[...]).astype(BF16)

    def proj(c0, width):
        return jnp.dot(hb, w_ref[:, c0:c0 + width], preferred_element_type=F32)

    for g, (_, dil) in enumerate(B_PAIRS):
        q, k, v = (proj(i * B_WIDTH + g * B_GROUP_W, B_GROUP_W) for i in range(3))
        kv = jnp.concatenate([k, v], axis=1)
        if g == 2:
            win2_ref[0] = kv
        elif g == 1:
            @pl.when(t == last)
            def _():
                win1_ref[0] = kv
        else:
            @pl.when(t == last)
            def _():
                win0_ref[0] = kv[tm - B_PAIRS[0][0]:]
        if dil == 1:
            qkv0_ref[0] = jnp.concatenate([q, k, v], axis=1).astype(BF16)
            continue
        out_ref = qkv1_ref if g == 1 else qkv2_ref
        for i, arr in enumerate((q, k, v)):
            for ct in range(2):
                scr_ref[2 * i + ct] = arr[:, ct * LANES:(ct + 1) * LANES]
        for r in range(dil):
            rows = [scr_ref[j, pl.ds(r, tm // dil, stride=dil), :] for j in range(6)]
            out_ref[0, r] = jnp.concatenate(rows, axis=1).astype(BF16)

    u = _gelu(proj(3 * B_WIDTH, A_WIDTH))
    v = _layernorm(_gelu(proj(3 * B_WIDTH + A_WIDTH, A_WIDTH)), lng_ref[...], lnb_ref[...]).astype(BF16)
    for c in range(tm // CHUNK):
        rs = slice(c * CHUNK, (c + 1) * CHUNK)
        for g in range(A_GROUPS):
            cs = slice(g * LANES, (g + 1) * LANES)
            s = jnp.dot(wsp_ref[g], v[rs, cs], preferred_element_type=F32) + bsp_ref[:, g:g + 1]
            a_ref[0, rs, cs] = (u[rs, cs] * s).astype(BF16)

    qc_ref[0] = proj(3 * B_WIDTH + 2 * A_WIDTH, C_WIDTH).astype(BF16)


def _prompt_proj(x, norm_mix, w_mix, ln_g, ln_b, wsp, bsp):
    b, s, d = x.shape
    tm = TM_PROJ
    assert s % tm == 0 and s >= B_PAIRS[2][0] and tm == B_PAIRS[1][0]
    nt = s // tm
    const = lambda *shape: pl.BlockSpec(shape, lambda i, j: (0,) * len(shape))
    out_shape = (
        jax.ShapeDtypeStruct((b, s, 3 * B_GROUP_W), BF16),
        jax.ShapeDtypeStruct((b, 4, s // 4, 3 * B_GROUP_W), BF16),
        jax.ShapeDtypeStruct((b, 16, s // 16, 3 * B_GROUP_W), BF16),
        jax.ShapeDtypeStruct((b, B_PAIRS[0][0], 2 * B_GROUP_W), F32),
        jax.ShapeDtypeStruct((b, B_PAIRS[1][0], 2 * B_GROUP_W), F32),
        jax.ShapeDtypeStruct((b, s, 2 * B_GROUP_W), F32),
        jax.ShapeDtypeStruct((b, s, A_WIDTH), BF16),
        jax.ShapeDtypeStruct((b, s, C_WIDTH), BF16),
    )
    out_specs = (
        pl.BlockSpec((1, tm, 3 * B_GROUP_W), lambda i, j: (i, j, 0)),
        pl.BlockSpec((1, 4, tm // 4, 3 * B_GROUP_W), lambda i, j: (i, 0, j, 0)),
        pl.BlockSpec((1, 16, tm // 16, 3 * B_GROUP_W), lambda i, j: (i, 0, j, 0)),
        pl.BlockSpec((1, B_PAIRS[0][0], 2 * B_GROUP_W), lambda i, j: (i, 0, 0)),
        pl.BlockSpec((1, B_PAIRS[1][0], 2 * B_GROUP_W), lambda i, j: (i, 0, 0)),
        pl.BlockSpec((1, tm, 2 * B_GROUP_W), lambda i, j: (i, j, 0)),
        pl.BlockSpec((1, tm, A_WIDTH), lambda i, j: (i, j, 0)),
        pl.BlockSpec((1, tm, C_WIDTH), lambda i, j: (i, j, 0)),
    )
    return pl.pallas_call(
        _proj_kernel,
        grid=(b, nt),
        in_specs=[pl.BlockSpec((1, tm, d), lambda i, j: (i, j, 0)), const(1, d), const(d, MIX_WIDTH),
                  const(1, A_WIDTH), const(1, A_WIDTH), const(A_GROUPS, CHUNK, CHUNK), const(CHUNK, A_GROUPS)],
        out_specs=out_specs,
        out_shape=out_shape,
        scratch_shapes=[pltpu.VMEM((6, tm, LANES), F32)],
        compiler_params=_cparams("arbitrary", "arbitrary"),
        name="prompt_proj",
    )(x, norm_mix, w_mix, ln_g, ln_b, wsp, bsp)


def _attend(q, k, v, bias_ref, g, k0, hmask):
    kn = k.shape[0]
    out = jnp.zeros((BAND, B_GROUP_W), F32)
    lse = jnp.zeros((BAND, B_GROUP_W), F32)
    for h in range(4):
        hm = hmask[h]
        s = lax.dot_general(q * hm, k, (((1,), (1,)), ((), ())), preferred_element_type=F32)
        s = s * (1.0 / math.sqrt(B_HEAD_DIM)) + bias_ref[g, h, :, k0:k0 + kn]
        m = jnp.max(s, axis=-1, keepdims=True)
        p = jnp.exp(s - m)
        l = jnp.sum(p, axis=-1, keepdims=True)
        pv = jnp.dot(p.astype(BF16), v * hm, preferred_element_type=F32)
        sel = hm > 0
        out = jnp.where(sel, pv / l, out)
        lse = jnp.where(sel, m + jnp.log(l), lse)
    return out, lse


def _attn_kernel(q0_ref, q1_ref, q2_ref, bias_ref, o_ref, out_ref, lse_ref):
    s = q0_ref.shape[1]
    lane_head = lax.broadcasted_iota(I32, (1, B_GROUP_W), 1) // B_HEAD_DIM
    hmask = [(lane_head == h).astype(BF16) for h in range(4)]
    qs, ks, vs = (slice(i * B_GROUP_W, (i + 1) * B_GROUP_W) for i in range(3))

    def store(g, start, dil, out, lse):
        rows = pl.ds(start, BAND) if dil == 1 else pl.ds(start, BAND, stride=dil)
        for ct in range(2):
            out_ref[g, ct, rows, :] = out[:, ct * LANES:(ct + 1) * LANES]
            lse_ref[g, ct, rows, :] = lse[:, ct * LANES:(ct + 1) * LANES]

    for g, (_, dil) in enumerate(B_PAIRS):
        n = s // dil
        nb = n // BAND

        def load(rows, cols, r, g=g):
            if g == 0:
                return q0_ref[0, rows, cols]
            return (q1_ref if g == 1 else q2_ref)[0, r, rows, cols]

        def first_block(r, carry, g=g, dil=dil, load=load):
            rows = pl.ds(0, BAND)
            out, lse = _attend(load(rows, qs, r), load(rows, ks, r), load(rows, vs, r), bias_ref, g, BAND, hmask)
            store(g, r, dil, out, lse)
            return carry

        def later_block(i, carry, g=g, dil=dil, nb=nb, load=load):
            r = i // (nb - 1)
            qb = i % (nb - 1) + 1
            q0 = pl.multiple_of(qb * BAND, BAND)
            rows_q = pl.ds(q0, BAND)
            rows_k = pl.ds(q0 - BAND, 2 * BAND)
            out, lse = _attend(load(rows_q, qs, r), load(rows_k, ks, r), load(rows_k, vs, r), bias_ref, g, 0, hmask)
            store(g, qb * (BAND * dil) + r, dil, out, lse)
            return carry

        lax.fori_loop(0, dil, first_block, 0)
        if nb > 1:
            lax.fori_loop(0, dil * (nb - 1), later_block, 0)

    rc = 256
    def combine(c, carry):
        rows = pl.ds(pl.multiple_of(c * rc, rc), rc)
        for ct in range(2):
            ls = [lse_ref[g, ct, rows, :] for g in range(3)]
            m = jnp.maximum(jnp.maximum(ls[0], ls[1]), ls[2])
            ws = [jnp.exp(l - m) for l in ls]
            den = ws[0] + ws[1] + ws[2]
            num = ws[0] * out_ref[0, ct, rows, :] + ws[1] * out_ref[1, ct, rows, :] + ws[2] * out_ref[2, ct, rows, :]
            o_ref[0, rows, ct * LANES:(ct + 1) * LANES] = (num / den).astype(BF16)
        return carry
    lax.fori_loop(0, s // rc, combine, 0)


def _prompt_attention(qkv0, qkv1, qkv2, bias):
    b, s, w = qkv0.shape
    return pl.pallas_call(
        _attn_kernel,
        grid=(b,),
        in_specs=[pl.BlockSpec((1, s, w), lambda i: (i, 0, 0)),
                  pl.BlockSpec((1, 4, s // 4, w), lambda i: (i, 0, 0, 0)),
                  pl.BlockSpec((1, 16, s // 16, w), lambda i: (i, 0, 0, 0)),
                  pl.BlockSpec(bias.shape, lambda i: (0, 0, 0, 0))],
        out_specs=pl.BlockSpec((1, s, B_GROUP_W), lambda i: (i, 0, 0)),
        out_shape=jax.ShapeDtypeStruct((b, s, B_GROUP_W), BF16),
        scratch_shapes=[pltpu.VMEM((3, 2, s, LANES), F32), pltpu.VMEM((3, 2, s, LANES), F32)],
        compiler_params=_cparams("arbitrary"),
        name="prompt_attention",
    )(qkv0, qkv1, qkv2, bias)


def _band_bias(rel_bias):
    qi = np.arange(BAND)[:, None]
    kj = np.arange(2 * BAND)[None, :]
    steps = qi + BAND - kj
    valid = (steps >= 0) & (steps <= BAND)
    out = []
    for g, (_, dil) in enumerate(B_PAIRS):
        idx = _t5_bucket(np.clip(steps, 0, BAND) * dil)
        tbl = rel_bias[:, 4 * g:4 * g + 4][idx]
        out.append(jnp.where(valid[None], jnp.transpose(tbl, (2, 0, 1)), NEG_INF))
    return jnp.stack(out).astype(F32)


def _memkv_kernel(mem_ref, g_ref, w_ref, kv_ref, kvb_ref):
    hb = _rms(mem_ref[0], g_ref[...]).astype(BF16)
    kv = jnp.dot(hb, w_ref[...], preferred_element_type=F32)
    kv_ref[0] = kv
    kvb_ref[0] = kv.astype(BF16)


def _memory_kv(mem, norm_mem, w_mem):
    b, m, d = mem.shape
    w = w_mem.shape[1]
    return pl.pallas_call(
        _memkv_kernel,
        grid=(b,),
        in_specs=[pl.BlockSpec((1, m, d), lambda i: (i, 0, 0)), pl.BlockSpec((1, d), lambda i: (0, 0)),
                  pl.BlockSpec((d, w), lambda i: (0, 0))],
        out_specs=(pl.BlockSpec((1, m, w), lambda i: (i, 0, 0)), pl.BlockSpec((1, m, w), lambda i: (i, 0, 0))),
        out_shape=(jax.ShapeDtypeStruct((b, m, w), F32), jax.ShapeDtypeStruct((b, m, w), BF16)),
        compiler_params=_cparams("arbitrary"),
        name="memory_kv",
    )(mem, norm_mem, w_mem)


def _merge_kernel(x_ref, g_ref, a_ref, b_ref, qc_ref, kv_ref, wg_ref, wa_ref, wb_ref, wc_ref, wo_ref, o_ref):
    x = x_ref[0]
    hb = _rms(x, g_ref[...]).astype(BF16)
    qc = qc_ref[0]
    cs = []
    for h in range(C_HEADS):
        hs = slice(h * C_HEAD_DIM, (h + 1) * C_HEAD_DIM)
        s = lax.dot_general(qc[:, hs], kv_ref[0, :, hs], (((1,), (1,)), ((), ())), preferred_element_type=F32)
        s = s * (1.0 / math.sqrt(C_HEAD_DIM))
        p = jnp.exp(s - jnp.max(s, axis=-1, keepdims=True))
        p = p / jnp.sum(p, axis=-1, keepdims=True)
        vs = slice(C_WIDTH + h * C_HEAD_DIM, C_WIDTH + (h + 1) * C_HEAD_DIM)
        cs.append(jnp.dot(p.astype(BF16), kv_ref[0, :, vs], preferred_element_type=F32))
    c = jnp.concatenate(cs, axis=1).astype(BF16)
    branches = ((a_ref[0], wa_ref), (b_ref[0], wb_ref), (c, wc_ref))
    merged = None
    for i, (act, w_ref) in enumerate(branches):
        gate = jax.nn.sigmoid(jnp.dot(hb, wg_ref[:, i * D_MODEL:(i + 1) * D_MODEL], preferred_element_type=F32))
        term = gate * jnp.dot(act, w_ref[...], preferred_element_type=F32)
        merged = term if merged is None else merged + term
    o_ref[0] = x + jnp.dot(merged.astype(BF16), wo_ref[...], preferred_element_type=F32)


def _prompt_merge(x, norm_mix, a, bo, qc, kvb, wg, wa, wb, wc, wo):
    b, s, d = x.shape
    tm = TM_MERGE
    tile = lambda w: pl.BlockSpec((1, tm, w), lambda i, j: (i, j, 0))
    const = lambda arr: pl.BlockSpec(arr.shape, lambda i, j: (0,) * arr.ndim)
    return pl.pallas_call(
        _merge_kernel,
        grid=(b, s // tm),
        in_specs=[tile(d), const(norm_mix), tile(A_WIDTH), tile(B_GROUP_W), tile(C_WIDTH),
                  pl.BlockSpec((1,) + kvb.shape[1:], lambda i, j: (i, 0, 0)),
                  const(wg), const(wa), const(wb), const(wc), const(wo)],
        out_specs=tile(d),
        out_shape=jax.ShapeDtypeStruct((b, s, d), F32),
        compiler_params=_cparams("arbitrary", "arbitrary"),
        name="prompt_merge",
    )(x, norm_mix, a, bo, qc, kvb, wg, wa, wb, wc, wo)


def _prompt_mixers(x_prompt, mem_prompt, rel_bias, norm_mix, norm_mem, ln_v_g, ln_v_b, w_spatial, b_spatial,
                   w_mem_kv, wts):
    b, s, _ = x_prompt.shape
    wsp = jnp.where(np.tril(np.ones((CHUNK, CHUNK), bool)), w_spatial, 0.0).astype(BF16)
    qkv0, qkv1, qkv2, win0, win1, win2, a_out, qc = _prompt_proj(
        x_prompt, norm_mix[None], wts["mix"], ln_v_g[None], ln_v_b[None], wsp, b_spatial.T)
    b_out = _prompt_attention(qkv0, qkv1, qkv2, _band_bias(rel_bias))
    mem_kv, mem_kv_b = _memory_kv(mem_prompt, norm_mem[None], w_mem_kv.astype(BF16))
    x_new = _prompt_merge(x_prompt, norm_mix[None], a_out, b_out, qc, mem_kv_b, wts["gate"],
                          wts["a"], wts["b"], wts["c"], wts["out"])
    wins = [w.reshape(1, b, w.shape[1], 2, 4, B_HEAD_DIM) for w in (win0, win1, win2)]
    return x_new, wins, mem_kv.reshape(1, b, N_MEM, 2, C_HEADS, C_HEAD_DIM)


def _route_kernel(n_tiles, x_ref, g_ref, wr_ref, br_ref, cnt_in_ref, *rest):
    if len(rest) == 7:
        tail_ref, h_ref = rest[:2]
        step = pl.program_id(0)

        @pl.when(step == n_tiles)
        def _():
            n_tail = tail_ref.shape[0]
            h_ref[:n_tail] = tail_ref[...]
            h_ref[n_tail:] = jnp.zeros((h_ref.shape[0] - n_tail, LANES), F32)

        @pl.when(step < n_tiles)
        def _():
            _route_tile(x_ref, g_ref, wr_ref, br_ref, cnt_in_ref, *rest[1:])
    else:
        _route_tile(x_ref, g_ref, wr_ref, br_ref, cnt_in_ref, *rest)


def _route_tile(x_ref, g_ref, wr_ref, br_ref, cnt_in_ref, h_ref, e_ref, rank_ref, gate_ref, cnt_ref, carry_ref):
    tm = x_ref.shape[0]

    @pl.when(pl.program_id(0) == 0)
    def _():
        carry_ref[...] = cnt_in_ref[...].astype(F32)

    h = _rms(x_ref[...], g_ref[...])
    for c in range(ROW_TILES):
        h_ref[pl.ds(c, tm, stride=ROW_TILES), :] = h[:, c * LANES:(c + 1) * LANES]
    logits = lax.dot_general(wr_ref[...], h.astype(BF16), (((1,), (1,)), ((), ())),
                             preferred_element_type=F32) + br_ref[...]
    expert = lax.broadcasted_iota(I32, logits.shape, 0)
    vals, idxs = [], []
    member = jnp.zeros(logits.shape, F32)
    for _ in range(TOP_K):
        m = jnp.max(logits, axis=0, keepdims=True)
        idx = jnp.min(jnp.where(logits == m, expert, N_EXPERTS), axis=0, keepdims=True)
        hit = expert == idx
        vals.append(m)
        idxs.append(idx)
        member = jnp.where(hit, 1.0, member)
        logits = jnp.where(hit, -jnp.inf, logits)
    p = [jnp.exp(v - vals[0]) for v in vals]
    den = p[0] + p[1] + p[2] + p[3]
    gates = jnp.concatenate([pk / den for pk in p] + [jnp.zeros((SUBLANES - TOP_K, tm), F32)], axis=0)
    gate_ref[...] = jnp.transpose(gates)
    e_ref[...] = jnp.concatenate(idxs, axis=0)
    before = (lax.broadcasted_iota(I32, (tm, tm), 0) < lax.broadcasted_iota(I32, (tm, tm), 1)).astype(BF16)
    prefix = jnp.dot(member.astype(BF16), before, preferred_element_type=F32) + carry_ref[:, 0:1]
    rank_ref[...] = jnp.concatenate(
        [jnp.sum(jnp.where(expert == idx, prefix, 0.0), axis=0, keepdims=True) for idx in idxs], axis=0).astype(I32)
    carry_ref[...] = carry_ref[...] + jnp.sum(member, axis=1, keepdims=True)
    cnt_ref[...] = carry_ref[...].astype(I32)


def _route(x, norm_ffn, w_router_t, b_router, cnt_in, tm, tail_rows=None):
    n, d = x.shape
    assert n % tm == 0
    nt = n // tm
    extra = 0 if tail_rows is None else 1
    const = lambda arr: pl.BlockSpec(arr.shape, lambda i: (0,) * arr.ndim)
    tile_idx = lambda i: jnp.minimum(i, nt - 1)
    in_specs = [pl.BlockSpec((tm, d), lambda i: (tile_idx(i), 0)), const(norm_ffn), const(w_router_t),
                const(b_router), const(cnt_in)]
    args = [x, norm_ffn, w_router_t, b_router, cnt_in]
    if extra:
        assert tail_rows.shape[0] <= tm * ROW_TILES
        in_specs.append(const(tail_rows))
        args.append(tail_rows)
    return pl.pallas_call(
        functools.partial(_route_kernel, nt),
        grid=(nt + extra,),
        in_specs=in_specs,
        out_specs=(pl.BlockSpec((tm * ROW_TILES, LANES), lambda i: (i, 0)),
                   pl.BlockSpec((TOP_K, tm), lambda i: (0, tile_idx(i))),
                   pl.BlockSpec((TOP_K, tm), lambda i: (0, tile_idx(i))),
                   pl.BlockSpec((tm, SUBLANES), lambda i: (tile_idx(i), 0)),
                   pl.BlockSpec((N_EXPERTS, LANES), lambda i: (0, 0))),
        out_shape=(jax.ShapeDtypeStruct(((nt + extra) * tm * ROW_TILES, LANES), F32),
                   jax.ShapeDtypeStruct((TOP_K, n), I32), jax.ShapeDtypeStruct((TOP_K, n), I32),
                   jax.ShapeDtypeStruct((n, SUBLANES), F32), jax.ShapeDtypeStruct((N_EXPERTS, LANES), I32)),
        scratch_shapes=[pltpu.VMEM((N_EXPERTS, LANES), F32)],
        compiler_params=_cparams("arbitrary"),
        name="moe_route",
    )(*args)


DISPATCH_CHUNK = 2048


def _dispatch_kernel(n_blocks, trips_ref, cnt_ref, e_ref, r_ref, es_ref, rs_ref, dst_ref, plan_ref, start_ref):
    c = pl.program_id(0)
    ch, n_sample = trips_ref[0], trips_ref[1]
    bm = BM_EXPERT
    shift = bm.bit_length() - 1

    @pl.when(c == 0)
    def _():
        def per_expert(e, carry):
            blk0, row0, last_e = carry
            cnt = cnt_ref[e]
            nb = (cnt + (bm - 1)) >> shift
            start_ref[e] = row0

            def fill(j, cc):
                plan_ref[blk0 + j] = e
                plan_ref[n_blocks + blk0 + j] = jnp.minimum(cnt - j * bm, bm)
                plan_ref[2 * n_blocks + blk0 + j] = row0 + j * bm
                return cc
            lax.fori_loop(0, nb, fill, 0)
            return blk0 + nb, row0 + cnt, jnp.where(nb > 0, e, last_e)
        n_used, _, last_e = lax.fori_loop(0, N_EXPERTS, per_expert, (0, 0, 0))

        def tail(j, cc):
            plan_ref[j] = last_e
            plan_ref[n_blocks + j] = 0
            plan_ref[2 * n_blocks + j] = 0
            return cc
        lax.fori_loop(n_used, n_blocks, tail, 0)
        plan_ref[3 * n_blocks] = n_used

    def place(tok, e_src, r_src, j):
        for k in range(TOP_K):
            dst_ref[start_ref[e_src[k, j]] + r_src[k, j]] = tok * TOP_K + k

    def body(j, carry):
        place(c * ch + j, e_ref, r_ref, j)
        return carry
    lax.fori_loop(0, ch, body, 0)

    @pl.when(c == pl.num_programs(0) - 1)
    def _():
        def body_s(j, carry):
            place(pl.num_programs(0) * ch + j, es_ref, rs_ref, j)
            return carry
        lax.fori_loop(0, n_sample, body_s, 0)


def _dispatch_plan(cnt, e_p, r_p, e_s, r_s, n_blocks):
    n_p, n_s = e_p.shape[1], e_s.shape[1]
    ch = DISPATCH_CHUNK
    assert n_p % ch == 0
    trips = jnp.array([ch, n_s], I32)
    smem = lambda: pl.BlockSpec(memory_space=pltpu.SMEM)
    chunk = lambda: pl.BlockSpec((TOP_K, ch), lambda i: (0, i), memory_space=pltpu.SMEM)
    return pl.pallas_call(
        functools.partial(_dispatch_kernel, n_blocks),
        grid=(n_p // ch,),
        in_specs=[smem(), smem(), chunk(), chunk(), smem(), smem()],
        out_specs=(smem(), smem()),
        out_shape=(jax.ShapeDtypeStruct(((n_p + n_s) * TOP_K,), I32), jax.ShapeDtypeStruct((3 * n_blocks + 1,), I32)),
        scratch_shapes=[pltpu.SMEM((N_EXPERTS,), I32)],
        compiler_params=_cparams("arbitrary"),
        name="moe_dispatch_plan",
    )(trips, cnt, e_p, r_p, e_s, r_s)


DMA_UNROLL = 8


def _expert_kernel(n_blocks, dst_ref, plan_ref, h_hbm, wgu_ref, bgu_ref, wd_ref, bd_ref, y_hbm,
                   xbuf, ybuf, gsem, ssem):
    i = pl.program_id(0)
    n_used = plan_ref[3 * n_blocks]
    slot = i % 2
    bm = BM_EXPERT
    tile = lambda row: pl.ds(pl.multiple_of(row * ROW_TILES, ROW_TILES), ROW_TILES)
    n_valid = lambda blk: plan_ref[n_blocks + blk]
    first_row = lambda blk: plan_ref[2 * n_blocks + blk]

    def for_rows(lo, hi, fn):
        n_full = (hi - lo) // DMA_UNROLL

        def body(j, carry):
            for u in range(DMA_UNROLL):
                fn(lo + j * DMA_UNROLL + u)
            return carry
        lax.fori_loop(0, n_full, body, 0)

        def rest(r, carry):
            fn(r)
            return carry
        lax.fori_loop(lo + n_full * DMA_UNROLL, hi, rest, 0)

    def gather_start(blk, sl):
        q0 = first_row(blk)

        def valid(r):
            tok = dst_ref[q0 + r] >> 2
            pltpu.make_async_copy(h_hbm.at[tile(tok), :], xbuf.at[sl, tile(r), :], gsem.at[sl]).start()

        def filler(r):
            pltpu.make_async_copy(h_hbm.at[tile(0), :], xbuf.at[sl, tile(r), :], gsem.at[sl]).start()
        for_rows(0, n_valid(blk), valid)
        for_rows(n_valid(blk), bm, filler)

    def gather_wait(sl):
        pltpu.make_async_copy(h_hbm.at[pl.ds(0, bm * ROW_TILES), :], xbuf.at[sl], gsem.at[sl]).wait()

    def scatter_start(blk, sl):
        q0 = first_row(blk)

        def valid(r):
            pltpu.make_async_copy(ybuf.at[sl, tile(r), :], y_hbm.at[tile(dst_ref[q0 + r]), :], ssem.at[sl]).start()
        for_rows(0, n_valid(blk), valid)

    def scatter_wait(blk, sl):
        nv = n_valid(blk)
        for bit in range(bm.bit_length()):
            rows = (1 << bit) * ROW_TILES

            @pl.when((nv >> bit) & 1 == 1)
            def _():
                pltpu.make_async_copy(ybuf.at[sl, pl.ds(0, rows), :], y_hbm.at[pl.ds(0, rows), :], ssem.at[sl]).wait()

    @pl.when(i == 0)
    def _():
        gather_start(0, 0)

    @pl.when(i < n_used)
    def _():
        gather_wait(slot)

        @pl.when(i + 1 < n_used)
        def _():
            gather_start(i + 1, 1 - slot)

        x = jnp.concatenate([xbuf[slot, pl.ds(c, bm, stride=ROW_TILES), :] for c in range(ROW_TILES)], axis=1)
        gu = jnp.dot(x.astype(BF16), wgu_ref[0], preferred_element_type=F32) + bgu_ref[0]
        gate = jnp.minimum(gu[:, :D_FF], SWIGLU_LIMIT)
        up = jnp.clip(gu[:, D_FF:], -SWIGLU_LIMIT, SWIGLU_LIMIT)
        act = gate * jax.nn.sigmoid(SWIGLU_ALPHA * gate) * (up + 1.0)
        y = jnp.dot(act.astype(BF16), wd_ref[0], preferred_element_type=F32) + bd_ref[0]

        @pl.when(i >= 2)
        def _():
            scatter_wait(i - 2, slot)

        for c in range(ROW_TILES):
            ybuf[slot, pl.ds(c, bm, stride=ROW_TILES), :] = y[:, c * LANES:(c + 1) * LANES]
        scatter_start(i, slot)

    @pl.when(i == n_blocks - 1)
    def _():
        scatter_wait(n_used - 1, (n_used - 1) % 2)

        @pl.when(n_used >= 2)
        def _():
            scatter_wait(n_used - 2, n_used % 2)


def _experts(dst, plan, h_buf, wgu, bgu, wd, bd, n_blocks):
    bm = BM_EXPERT
    by_expert = lambda *shape: pl.BlockSpec((1,) + shape, lambda i, dst, plan: (plan[i],) + (0,) * len(shape))
    grid_spec = pltpu.PrefetchScalarGridSpec(
        num_scalar_prefetch=2,
        grid=(n_blocks,),
        in_specs=[pl.BlockSpec(memory_space=pl.ANY), by_expert(D_MODEL, 2 * D_FF), by_expert(1, 2 * D_FF),
                  by_expert(D_FF, D_MODEL), by_expert(1, D_MODEL)],
        out_specs=pl.BlockSpec(memory_space=pl.ANY),
        scratch_shapes=[pltpu.VMEM((2, bm * ROW_TILES, LANES), F32), pltpu.VMEM((2, bm * ROW_TILES, LANES), F32),
                        pltpu.SemaphoreType.DMA((2,)), pltpu.SemaphoreType.DMA((2,))],
    )
    return pl.pallas_call(
        functools.partial(_expert_kernel, n_blocks),
        grid_spec=grid_spec,
        out_shape=jax.ShapeDtypeStruct((dst.shape[0] * ROW_TILES, LANES), F32),
        compiler_params=_cparams("arbitrary"),
        name="moe_experts",
    )(dst, plan, h_buf, wgu, bgu, wd, bd)


def _combine_kernel(x_ref, y_ref, gate_ref, g_ref, o_ref):
    tm = x_ref.shape[0]
    gates = gate_ref[...]
    cols = []
    for c in range(ROW_TILES):
        acc = x_ref[:, c * LANES:(c + 1) * LANES]
        for k in range(TOP_K):
            acc = acc + gates[:, k:k + 1] * y_ref[pl.ds(k * ROW_TILES + c, tm, stride=TOP_K * ROW_TILES), :]
        cols.append(acc)
    o_ref[...] = _rms(jnp.concatenate(cols, axis=1), g_ref[...])


def _combine(x, y_rows, gates, norm_final, row0, tm):
    n, d = x.shape
    assert n % tm == 0 and row0 % tm == 0
    blk0 = row0 // tm
    return pl.pallas_call(
        _combine_kernel,
        grid=(n // tm,),
        in_specs=[pl.BlockSpec((tm, d), lambda i: (i, 0)),
                  pl.BlockSpec((tm * TOP_K * ROW_TILES, LANES), lambda i: (i + blk0, 0)),
                  pl.BlockSpec((tm, SUBLANES), lambda i: (i, 0)), pl.BlockSpec((1, d), lambda i: (0, 0))],
        out_specs=pl.BlockSpec((tm, d), lambda i: (i, 0)),
        out_shape=jax.ShapeDtypeStruct((n, d), F32),
        compiler_params=_cparams("arbitrary"),
        name="moe_combine",
    )(x, y_rows, gates, norm_final)


def _moe_and_final_norm(x_p, x_s, norm_ffn, w_router, b_router, w_gate_up, b_gate_up, w_down, b_down, norm_final):
    n_p, n_s = x_p.shape[0], x_s.shape[0]
    n_rows = (n_p + n_s) * TOP_K
    n_blocks = -(-(n_rows + N_EXPERTS * (BM_EXPERT - 1)) // BM_EXPERT)
    w_router_t = w_router.T.astype(BF16)
    b_r = b_router[:, None]
    g_ffn = norm_ffn[None]
    zero_cnt = jnp.zeros((N_EXPERTS, LANES), I32)
    h_s, e_s, r_s, gates_s, cnt_s = _route(x_s, g_ffn, w_router_t, b_r, zero_cnt, n_s)
    h_buf, e_p, r_p, gates_p, cnt = _route(x_p, g_ffn, w_router_t, b_r, cnt_s, TM_ROUTE, tail_rows=h_s)
    dst, plan = _dispatch_plan(cnt[:, 0], e_p, r_p, e_s, r_s, n_blocks)
    y_rows = _experts(dst, plan, h_buf, w_gate_up.astype(BF16), b_gate_up[:, None, :], w_down.astype(BF16),
                      b_down[:, None, :], n_blocks)
    y_p = _combine(x_p, y_rows, gates_p, norm_final[None], 0, TM_COMBINE)
    y_s = _combine(x_s, y_rows, gates_s, norm_final[None], n_p, n_s)
    return y_p, y_s


def _rb(x):
    return x.astype(BF16).astype(F32)


def _sample_proj_kernel(x_ref, g_ref, wm_ref, wg_ref, z_ref):
    hb = _rms(x_ref[...], g_ref[...]).astype(BF16)
    z_ref[:, :MIX_WIDTH] = jnp.dot(hb, wm_ref[...], preferred_element_type=F32)
    z_ref[:, MIX_WIDTH:] = jnp.dot(hb, wg_ref[...], preferred_element_type=F32)


def _sample_proj(x, norm_mix, w_mix, w_gate):
    n, d = x.shape
    args = (x, norm_mix, w_mix, w_gate)
    width = w_mix.shape[1] + w_gate.shape[1]
    return pl.pallas_call(
        _sample_proj_kernel,
        grid=(1,),
        in_specs=[pl.BlockSpec(a.shape, lambda j: (0, 0)) for a in args],
        out_specs=pl.BlockSpec((n, width), lambda j: (0, 0)),
        out_shape=jax.ShapeDtypeStruct((n, width), F32),
        compiler_params=_cparams("arbitrary"),
        name="sample_proj",
    )(*args)


def _sample_mix_kernel(z_ref, c0_ref, c1_ref, c2_ref, mem_ref, sb_ref, hsel_ref, hselt_ref, lng_ref, lnb_ref,
                       wsp_ref, bsp_ref, bo_ref, co_ref, a_ref, v_ref):
    z = z_ref[0]
    hsel = hsel_ref[...]
    hselt = hselt_ref[...]
    dot = lambda a, b: jnp.dot(a, b, precision=HIGHEST, preferred_element_type=F32)
    scale = 1.0 / math.sqrt(B_HEAD_DIM)
    outs, lses = [], []
    for g, cache_ref in enumerate((c0_ref, c1_ref, c2_ref)):
        q, kn, vn = (_rb(z[:, i * B_WIDTH + g * B_GROUP_W:i * B_WIDTH + (g + 1) * B_GROUP_W]) for i in range(3))
        k = _rb(cache_ref[0, :, :B_GROUP_W])
        v = _rb(cache_ref[0, :, B_GROUP_W:])
        lg = dot(k * q, hsel) * scale + sb_ref[g, :BAND]
        ln = dot(kn * q, hsel) * scale + sb_ref[g, BAND:BAND + 1]
        m = jnp.maximum(jnp.max(lg, axis=0, keepdims=True), ln)
        lse = m + jnp.log(jnp.sum(jnp.exp(lg - m), axis=0, keepdims=True) + jnp.exp(ln - m))
        p = _rb(jnp.exp(lg - lse))
        pn = _rb(jnp.exp(ln - lse))
        outs.append(jnp.sum(dot(p, hselt) * v, axis=0, keepdims=True) + dot(pn, hselt) * vn)
        lses.append(dot(lse, hselt))
    m = jnp.maximum(jnp.maximum(lses[0], lses[1]), lses[2])
    ws = [jnp.exp(l - m) for l in lses]
    den = ws[0] + ws[1] + ws[2]
    bo_ref[0] = _rb(ws[0] / den) * _rb(outs[0]) + _rb(ws[1] / den) * _rb(outs[1]) + _rb(ws[2] / den) * _rb(outs[2])

    cs = []
    for h in range(C_HEADS):
        c0 = MIX_WIDTH - C_WIDTH + h * C_HEAD_DIM
        qh = _rb(z[:, c0:c0 + C_HEAD_DIM])
        kh = _rb(mem_ref[0, pl.ds(h, N_MEM, stride=2 * C_HEADS), :])
        vh = _rb(mem_ref[0, pl.ds(C_HEADS + h, N_MEM, stride=2 * C_HEADS), :])
        s = jnp.sum(kh * qh, axis=-1, keepdims=True) * (1.0 / math.sqrt(C_HEAD_DIM))
        p = jnp.exp(s - jnp.max(s, axis=0, keepdims=True))
        p = _rb(p / jnp.sum(p, axis=0, keepdims=True))
        cs.append(jnp.sum(p * vh, axis=0, keepdims=True))
    co_ref[0] = jnp.concatenate(cs, axis=1)

    u = _gelu(z[:, 3 * B_WIDTH:3 * B_WIDTH + A_WIDTH])
    v = _layernorm(_gelu(z[:, 3 * B_WIDTH + A_WIDTH:3 * B_WIDTH + 2 * A_WIDTH]), lng_ref[...], lnb_ref[...])
    v_ref[0] = v
    a_ref[0] = u * (_rb(wsp_ref[...]) * _rb(v) + bsp_ref[...])


def _sample_mix(z, caches, mem, sbias, hsel, hselt, ln_g, ln_b, wsp0, bsp0):
    n = z.shape[0]
    const = lambda arr: pl.BlockSpec(arr.shape, lambda i: (0,) * arr.ndim)
    row = lambda w: pl.BlockSpec((1, 1, w), lambda i: (i, 0, 0))
    widths = (B_GROUP_W, C_WIDTH, A_WIDTH, A_WIDTH)
    return pl.pallas_call(
        _sample_mix_kernel,
        grid=(n,),
        in_specs=[row(z.shape[2])] + [pl.BlockSpec((1, BAND, 2 * B_GROUP_W), lambda i: (i, 0, 0))] * 3
        + [pl.BlockSpec((1,) + mem.shape[1:], lambda i: (i, 0, 0))]
        + [const(a) for a in (sbias, hsel, hselt, ln_g, ln_b, wsp0, bsp0)],
        out_specs=tuple(row(w) for w in widths),
        out_shape=tuple(jax.ShapeDtypeStruct((n, 1, w), F32) for w in widths),
        compiler_params=_cparams("arbitrary"),
        name="sample_mix",
    )(z, *caches, mem, sbias, hsel, hselt, ln_g, ln_b, wsp0, bsp0)


def _sample_merge_kernel(x_ref, z_ref, a_ref, b_ref, c_ref, wa_ref, wb_ref, wc_ref, wo_ref, o_ref):
    dot = lambda a, b: jnp.dot(a.astype(BF16), b, preferred_element_type=F32)
    merged = None
    for i, (act_ref, w_ref) in enumerate(((a_ref, wa_ref), (b_ref, wb_ref), (c_ref, wc_ref))):
        gate = jax.nn.sigmoid(z_ref[:, MIX_WIDTH + i * D_MODEL:MIX_WIDTH + (i + 1) * D_MODEL])
        term = gate * dot(act_ref[...], w_ref[...])
        merged = term if merged is None else merged + term
    o_ref[...] = x_ref[...] + dot(merged, wo_ref[...])


def _sample_merge(x, z, a, bo, co, wa, wb, wc, wo):
    args = (x, z, a, bo, co, wa, wb, wc, wo)
    return pl.pallas_call(
        _sample_merge_kernel,
        grid=(1,),
        in_specs=[pl.BlockSpec(arr.shape, lambda i: (0, 0)) for arr in args],
        out_specs=pl.BlockSpec(x.shape, lambda i: (0, 0)),
        out_shape=jax.ShapeDtypeStruct(x.shape, F32),
        compiler_params=_cparams("arbitrary"),
        name="sample_merge",
    )(*args)


def _sample_bias(rel_bias):
    steps = np.concatenate([BAND - np.arange(BAND), [0]])
    out = []
    for g, (_, dil) in enumerate(B_PAIRS):
        tbl = rel_bias[:, 4 * g:4 * g + 4][_t5_bucket(steps * dil)]
        out.append(jnp.pad(tbl, ((0, 136 - (BAND + 1)), (0, LANES - 4))))
    return jnp.stack(out).astype(F32)


def _sample_mixers(x_sample, caches, cache_mem, rel_bias, norm_mix, ln_v_g, ln_v_b, w_spatial, b_spatial, wts):
    n = x_sample.shape[0]
    x = x_sample[:, 0]
    z = _sample_proj(x, norm_mix[None], wts["mix"], wts["gate"])
    head_of = np.arange(B_GROUP_W) // B_HEAD_DIM
    hsel = jnp.asarray((head_of[:, None] == np.arange(LANES)[None, :]).astype(np.float32))
    strided = [c.reshape(n, BAND, dil * 2 * B_GROUP_W) for c, (_, dil) in zip(caches, B_PAIRS)]
    mem = cache_mem.reshape(n, N_MEM * 2 * C_HEADS, C_HEAD_DIM)
    wsp0 = jnp.repeat(w_spatial[:, 0, 0], LANES)[None]
    bsp0 = jnp.repeat(b_spatial[:, 0], LANES)[None]
    bo, co, a, v = _sample_mix(z[:, None], strided, mem, _sample_bias(rel_bias), hsel, hsel.T,
                               ln_v_g[None], ln_v_b[None], wsp0, bsp0)
    x_new = _sample_merge(x, z, a[:, 0], bo[:, 0], co[:, 0], wts["a"], wts["b"], wts["c"], wts["out"])
    return x_new, z, v[:, 0]


def _mixer_weights(w_in, w_branch_a, w_branch_b, w_branch_c, w_out):
    return {"mix": w_in[:, :MIX_WIDTH].astype(BF16), "gate": w_in[:, MIX_WIDTH:].astype(BF16),
            "a": w_branch_a.astype(BF16), "b": w_branch_b.astype(BF16), "c": w_branch_c.astype(BF16),
            "out": w_out.astype(BF16)}


def kernel(x_prompt, x_sample, cache_win0_kv, cache_win1_kv, cache_win2_kv, cache_mem_kv, mem_prompt, rel_bias,
           norm_mix, norm_mem, w_in, ln_v_g, ln_v_b, w_spatial, b_spatial, w_mem_kv, w_branch_a, w_branch_b,
           w_branch_c, w_out, norm_ffn, w_router, b_router, w_gate_up, b_gate_up, w_down, b_down, norm_final):
    assert norm_mix.shape[0] == 1, "one layer"
    b, s, d = x_prompt.shape
    n_s = x_sample.shape[0]
    caches = (cache_win0_kv[0], cache_win1_kv[0], cache_win2_kv[0])
    assert all(c.shape[1] == w for c, (w, _) in zip(caches, B_PAIRS)), "window buffers hold a full window"
    wts = _mixer_weights(w_in[0], w_branch_a[0], w_branch_b[0], w_branch_c[0], w_out[0])
    xp, win_p, mem_p = _prompt_mixers(x_prompt, mem_prompt, rel_bias, norm_mix[0], norm_mem[0], ln_v_g[0],
                                      ln_v_b[0], w_spatial[0], b_spatial[0], w_mem_kv[0], wts)
    xs, z_s, v_s = _sample_mixers(x_sample, caches, cache_mem_kv[0], rel_bias, norm_mix[0], ln_v_g[0],
                                  ln_v_b[0], w_spatial[0], b_spatial[0], wts)
    y_p, y_s = _moe_and_final_norm(xp.reshape(b * s, d), xs, norm_ffn[0], w_router[0], b_router[0], w_gate_up[0],
                                   b_gate_up[0], w_down[0], b_down[0], norm_final)
    win_s = []
    for g, cache in enumerate(caches):
        k_new, v_new = (z_s[:, i * B_WIDTH + g * B_GROUP_W:i * B_WIDTH + (g + 1) * B_GROUP_W] for i in (1, 2))
        new = jnp.stack([k_new, v_new], axis=1).reshape(n_s, 1, 2, 4, B_HEAD_DIM)
        win_s.append(jnp.concatenate([cache[:, 1:], new], axis=1)[None])
    chunk_v = v_s.reshape(1, n_s, 1, A_GROUPS, LANES)
    return (y_p.reshape(b, s, d), y_s[:, None], win_p[0], win_p[1], win_p[2], mem_p,
            win_s[0], win_s[1], win_s[2], chunk_v)
```

```python
import functools
import math

import numpy as np
import jax
import jax.numpy as jnp
from jax import lax
from jax.experimental import pallas as pl
from jax.experimental.pallas import tpu as pltpu

F32 = jnp.float32
BF16 = jnp.bfloat16
I32 = jnp.int32
HIGHEST = lax.Precision.HIGHEST

D_MODEL = 1024
N_MEM = 256
CHUNK = 128
A_GROUPS = 4
A_WIDTH = 512
B_PAIRS = ((128, 1), (512, 4), (2048, 16))
B_GROUP_W = 256
B_WIDTH = 768
B_HEAD_DIM = 64
BAND = 128
C_HEADS = 4
C_HEAD_DIM = 128
C_WIDTH = 512
REL_BUCKETS = 32
REL_MAX_DIST = 2048
N_EXPERTS = 32
TOP_K = 4
D_FF = 1024
SWIGLU_LIMIT = 7.0
SWIGLU_ALPHA = 1.702
EPS = 1e-6
NEG_INF = -1e30
MIX_WIDTH = 3 * B_WIDTH + 2 * A_WIDTH + C_WIDTH

LANES = 128
SUBLANES = 8
ROW_TILES = D_MODEL // LANES
VMEM_LIMIT = 56 * 1024 * 1024

TM_PROJ = 512
TM_MERGE = 256
TM_ROUTE = 512
TM_SAMPLE = 128
BM_EXPERT = 256
TM_COMBINE = 256


def _cparams(*sem):
    return pltpu.CompilerParams(dimension_semantics=sem, vmem_limit_bytes=VMEM_LIMIT)


def _t5_bucket(dist):
    dist = np.maximum(np.asarray(dist), 0)
    max_exact = REL_BUCKETS // 2
    log_ratio = np.log(np.maximum(dist, max_exact) / max_exact) / math.log(REL_MAX_DIST / max_exact)
    large = np.minimum(max_exact + (log_ratio * (REL_BUCKETS - max_exact)).astype(np.int32), REL_BUCKETS - 1)
    return np.where(dist < max_exact, dist, large).astype(np.int32)


def _gelu(x):
    c = math.sqrt(2.0 / math.pi)
    return x * (0.5 * (1.0 + jnp.tanh(c * (x + 0.044715 * (x * x * x)))))


def _rms(x, g):
    return x * lax.rsqrt(jnp.mean(x * x, axis=-1, keepdims=True) + EPS) * g


def _layernorm(x, g, b):
    xc = x - jnp.mean(x, axis=-1, keepdims=True)
    return xc * lax.rsqrt(jnp.mean(xc * xc, axis=-1, keepdims=True) + EPS) * g + b


def _proj_kernel(x_ref, g_ref, w_ref, lng_ref, lnb_ref, wsp_ref, bsp_ref,
                 qkv0_ref, qkv1_ref, qkv2_ref, win0_ref, win1_ref, win2_ref, a_ref, qc_ref, scr_ref):
    t = pl.program_id(1)
    last = pl.num_programs(1) - 1
    tm = x_ref.shape[1]
    hb = _rms(x_ref[0], g_ref[...]).astype(BF16)

    def proj(c0, width):
        return jnp.dot(hb, w_ref[:, c0:c0 + width], preferred_element_type=F32)

    for g, (_, dil) in enumerate(B_PAIRS):
        q, k, v = (proj(i * B_WIDTH + g * B_GROUP_W, B_GROUP_W) for i in range(3))
        kv = jnp.concatenate([k, v], axis=1)
        if g == 2:
            win2_ref[0] = jnp.transpose(kv)
        elif g == 1:
            @pl.when(t == last)
            def _():
                win1_ref[0] = jnp.transpose(kv)
        else:
            @pl.when(t == last)
            def _():
                win0_ref[0] = jnp.transpose(kv[tm - B_PAIRS[0][0]:])
        if dil == 1:
            qkv0_ref[0] = jnp.concatenate([q, k, v], axis=1).astype(BF16)
            continue
        out_ref = qkv1_ref if g == 1 else qkv2_ref
        for i, arr in enumerate((q, k, v)):
            for ct in range(2):
                scr_ref[2 * i + ct] = arr[:, ct * LANES:(ct + 1) * LANES]
        for r in range(dil):
            rows = [scr_ref[j, pl.ds(r, tm // dil, stride=dil), :] for j in range(6)]
            out_ref[0, r] = jnp.concatenate(rows, axis=1).astype(BF16)

    u = _gelu(proj(3 * B_WIDTH, A_WIDTH))
    v = _layernorm(_gelu(proj(3 * B_WIDTH + A_WIDTH, A_WIDTH)), lng_ref[...], lnb_ref[...]).astype(BF16)
    for c in range(tm // CHUNK):
        rs = slice(c * CHUNK, (c + 1) * CHUNK)
        for g in range(A_GROUPS):
            cs = slice(g * LANES, (g + 1) * LANES)
            s = jnp.dot(wsp_ref[g], v[rs, cs], preferred_element_type=F32) + bsp_ref[:, g:g + 1]
            a_ref[0, rs, cs] = (u[rs, cs] * s).astype(BF16)

    qc_ref[0] = proj(3 * B_WIDTH + 2 * A_WIDTH, C_WIDTH).astype(BF16)


def _prompt_proj(x, norm_mix, w_mix, ln_g, ln_b, wsp, bsp):
    b, s, d = x.shape
    tm = TM_PROJ
    assert s % tm == 0 and s >= B_PAIRS[2][0] and tm == B_PAIRS[1][0]
    nt = s // tm
    const = lambda *shape: pl.BlockSpec(shape, lambda i, j: (0,) * len(shape))
    out_shape = (
        jax.ShapeDtypeStruct((b, s, 3 * B_GROUP_W), BF16),
        jax.ShapeDtypeStruct((b, 4, s // 4, 3 * B_GROUP_W), BF16),
        jax.ShapeDtypeStruct((b, 16, s // 16, 3 * B_GROUP_W), BF16),
        jax.ShapeDtypeStruct((b, 2 * B_GROUP_W, B_PAIRS[0][0]), F32),
        jax.ShapeDtypeStruct((b, 2 * B_GROUP_W, B_PAIRS[1][0]), F32),
        jax.ShapeDtypeStruct((b, 2 * B_GROUP_W, s), F32),
        jax.ShapeDtypeStruct((b, s, A_WIDTH), BF16),
        jax.ShapeDtypeStruct((b, s, C_WIDTH), BF16),
    )
    out_specs = (
        pl.BlockSpec((1, tm, 3 * B_GROUP_W), lambda i, j: (i, j, 0)),
        pl.BlockSpec((1, 4, tm // 4, 3 * B_GROUP_W), lambda i, j: (i, 0, j, 0)),
        pl.BlockSpec((1, 16, tm // 16, 3 * B_GROUP_W), lambda i, j: (i, 0, j, 0)),
        pl.BlockSpec((1, 2 * B_GROUP_W, B_PAIRS[0][0]), lambda i, j: (i, 0, 0)),
        pl.BlockSpec((1, 2 * B_GROUP_W, B_PAIRS[1][0]), lambda i, j: (i, 0, 0)),
        pl.BlockSpec((1, 2 * B_GROUP_W, tm), lambda i, j: (i, 0, j)),
        pl.BlockSpec((1, tm, A_WIDTH), lambda i, j: (i, j, 0)),
        pl.BlockSpec((1, tm, C_WIDTH), lambda i, j: (i, j, 0)),
    )
    return pl.pallas_call(
        _proj_kernel,
        grid=(b, nt),
        in_specs=[pl.BlockSpec((1, tm, d), lambda i, j: (i, j, 0)), const(1, d), const(d, MIX_WIDTH),
                  const(1, A_WIDTH), const(1, A_WIDTH), const(A_GROUPS, CHUNK, CHUNK), const(CHUNK, A_GROUPS)],
        out_specs=out_specs,
        out_shape=out_shape,
        scratch_shapes=[pltpu.VMEM((6, tm, LANES), F32)],
        compiler_params=_cparams("arbitrary", "arbitrary"),
        name="prompt_proj",
    )(x, norm_mix, w_mix, ln_g, ln_b, wsp, bsp)


def _attend(q, k, v, bias_ref, g, k0, hmask):
    kn = k.shape[0]
    out = jnp.zeros((BAND, B_GROUP_W), F32)
    lse = jnp.zeros((BAND, B_GROUP_W), F32)
    for h in range(4):
        hm = hmask[h]
        s = lax.dot_general(q * hm, k, (((1,), (1,)), ((), ())), preferred_element_type=F32)
        s = s * (1.0 / math.sqrt(B_HEAD_DIM)) + bias_ref[g, h, :, k0:k0 + kn]
        m = jnp.max(s, axis=-1, keepdims=True)
        p = jnp.exp(s - m)
        l = jnp.sum(p, axis=-1, keepdims=True)
        pv = jnp.dot(p.astype(BF16), v * hm, preferred_element_type=F32)
        sel = hm > 0
        out = jnp.where(sel, pv / l, out)
        lse = jnp.where(sel, m + jnp.log(l), lse)
    return out, lse


def _attn_kernel(q0_ref, q1_ref, q2_ref, bias_ref, o_ref, out_ref, lse_ref):
    s = q0_ref.shape[1]
    lane_head = lax.broadcasted_iota(I32, (1, B_GROUP_W), 1) // B_HEAD_DIM
    hmask = [(lane_head == h).astype(BF16) for h in range(4)]
    qs, ks, vs = (slice(i * B_GROUP_W, (i + 1) * B_GROUP_W) for i in range(3))

    def store(g, start, dil, out, lse):
        rows = pl.ds(start, BAND) if dil == 1 else pl.ds(start, BAND, stride=dil)
        for ct in range(2):
            out_ref[g, ct, rows, :] = out[:, ct * LANES:(ct + 1) * LANES]
            lse_ref[g, ct, rows, :] = lse[:, ct * LANES:(ct + 1) * LANES]

    for g, (_, dil) in enumerate(B_PAIRS):
        n = s // dil
        nb = n // BAND

        def load(rows, cols, r, g=g):
            if g == 0:
                return q0_ref[0, rows, cols]
            return (q1_ref if g == 1 else q2_ref)[0, r, rows, cols]

        def first_block(r, carry, g=g, dil=dil, load=load):
            rows = pl.ds(0, BAND)
            out, lse = _attend(load(rows, qs, r), load(rows, ks, r), load(rows, vs, r), bias_ref, g, BAND, hmask)
            store(g, r, dil, out, lse)
            return carry

        def later_block(i, carry, g=g, dil=dil, nb=nb, load=load):
            r = i // (nb - 1)
            qb = i % (nb - 1) + 1
            q0 = pl.multiple_of(qb * BAND, BAND)
            rows_q = pl.ds(q0, BAND)
            rows_k = pl.ds(q0 - BAND, 2 * BAND)
            out, lse = _attend(load(rows_q, qs, r), load(rows_k, ks, r), load(rows_k, vs, r), bias_ref, g, 0, hmask)
            store(g, qb * (BAND * dil) + r, dil, out, lse)
            return carry

        lax.fori_loop(0, dil, first_block, 0)
        if nb > 1:
            lax.fori_loop(0, dil * (nb - 1), later_block, 0)

    rc = 256
    def combine(c, carry):
        rows = pl.ds(pl.multiple_of(c * rc, rc), rc)
        for ct in range(2):
            ls = [lse_ref[g, ct, rows, :] for g in range(3)]
            m = jnp.maximum(jnp.maximum(ls[0], ls[1]), ls[2])
            ws = [jnp.exp(l - m) for l in ls]
            den = ws[0] + ws[1] + ws[2]
            num = ws[0] * out_ref[0, ct, rows, :] + ws[1] * out_ref[1, ct, rows, :] + ws[2] * out_ref[2, ct, rows, :]
            o_ref[0, rows, ct * LANES:(ct + 1) * LANES] = (num / den).astype(BF16)
        return carry
    lax.fori_loop(0, s // rc, combine, 0)


def _prompt_attention(qkv0, qkv1, qkv2, bias):
    b, s, w = qkv0.shape
    return pl.pallas_call(
        _attn_kernel,
        grid=(b,),
        in_specs=[pl.BlockSpec((1, s, w), lambda i: (i, 0, 0)),
                  pl.BlockSpec((1, 4, s // 4, w), lambda i: (i, 0, 0, 0)),
                  pl.BlockSpec((1, 16, s // 16, w), lambda i: (i, 0, 0, 0)),
                  pl.BlockSpec(bias.shape, lambda i: (0, 0, 0, 0))],
        out_specs=pl.BlockSpec((1, s, B_GROUP_W), lambda i: (i, 0, 0)),
        out_shape=jax.ShapeDtypeStruct((b, s, B_GROUP_W), BF16),
        scratch_shapes=[pltpu.VMEM((3, 2, s, LANES), F32), pltpu.VMEM((3, 2, s, LANES), F32)],
        compiler_params=_cparams("arbitrary"),
        name="prompt_attention",
    )(qkv0, qkv1, qkv2, bias)


def _bias_kernel(rel_ref, band_idx_ref, s0_ref, s1_ref, s2_ref, band_ref, sb0_ref, sb1_ref, sb2_ref, nb_ref):
    def lookup(idx, col):
        acc = jnp.full(idx.shape, NEG_INF, F32)
        for bucket in range(REL_BUCKETS):
            acc = jnp.where(idx == bucket, rel_ref[bucket, col], acc)
        return acc

    for g, (s_ref, sb_ref) in enumerate(((s0_ref, sb0_ref), (s1_ref, sb1_ref), (s2_ref, sb2_ref))):
        width = s_ref.shape[1]
        for h in range(4):
            band_ref[g, h] = lookup(band_idx_ref[g], 4 * g + h)
        sb_ref[...] = jnp.concatenate([lookup(s_ref[...], 4 * g + h) for h in range(4)]
                                      + [jnp.full((SUBLANES - 4, width), NEG_INF, F32)], axis=0)
        nb_ref[g] = jnp.concatenate([jnp.full((1, LANES), rel_ref[0, 4 * g + h], F32) for h in range(4)]
                                    + [jnp.zeros((SUBLANES - 4, LANES), F32)], axis=0)


def _bias_tables(rel_bias):
    steps = np.arange(BAND)[:, None] + BAND - np.arange(2 * BAND)[None, :]
    valid = (steps >= 0) & (steps <= BAND)
    band_idx = np.stack([np.where(valid, _t5_bucket(np.clip(steps, 0, BAND) * dil), -1) for _, dil in B_PAIRS])
    s_idx = []
    for window, dil in B_PAIRS:
        w = np.arange(window)
        s_idx.append(np.where(w % dil == 0, _t5_bucket(window - w), -1)[None].astype(np.int32))
    args = (rel_bias, jnp.asarray(band_idx.astype(np.int32))) + tuple(jnp.asarray(s) for s in s_idx)
    vmem = lambda a: pl.BlockSpec(a.shape, lambda i: (0,) * a.ndim)
    out_shape = (jax.ShapeDtypeStruct((3, 4, BAND, 2 * BAND), F32),) + tuple(
        jax.ShapeDtypeStruct((SUBLANES, window), F32) for window, _ in B_PAIRS) + (
        jax.ShapeDtypeStruct((3, SUBLANES, LANES), F32),)
    return pl.pallas_call(
        _bias_kernel,
        grid=(1,),
        in_specs=[pl.BlockSpec(memory_space=pltpu.SMEM)] + [vmem(a) for a in args[1:]],
        out_specs=tuple(pl.BlockSpec(s.shape, lambda i, n=len(s.shape): (0,) * n) for s in out_shape),
        out_shape=out_shape,
        compiler_params=_cparams("arbitrary"),
        name="bias_tables",
    )(*args)


def _memkv_kernel(mem_ref, g_ref, w_ref, kv_ref, kvb_ref):
    hb = _rms(mem_ref[0], g_ref[...]).astype(BF16)
    kv = jnp.dot(hb, w_ref[...], preferred_element_type=F32)
    kv_ref[0] = kv
    kvb_ref[0] = kv.astype(BF16)


def _memory_kv(mem, norm_mem, w_mem):
    b, m, d = mem.shape
    w = w_mem.shape[1]
    return pl.pallas_call(
        _memkv_kernel,
        grid=(b,),
        in_specs=[pl.BlockSpec((1, m, d), lambda i: (i, 0, 0)), pl.BlockSpec((1, d), lambda i: (0, 0)),
                  pl.BlockSpec((d, w), lambda i: (0, 0))],
        out_specs=(pl.BlockSpec((1, m, w), lambda i: (i, 0, 0)), pl.BlockSpec((1, m, w), lambda i: (i, 0, 0))),
        out_shape=(jax.ShapeDtypeStruct((b, m, w), F32), jax.ShapeDtypeStruct((b, m, w), BF16)),
        compiler_params=_cparams("arbitrary"),
        name="memory_kv",
    )(mem, norm_mem, w_mem)


def _merge_kernel(x_ref, g_ref, a_ref, b_ref, qc_ref, kv_ref, wg_ref, wa_ref, wb_ref, wc_ref, wo_ref, o_ref):
    x = x_ref[0]
    hb = _rms(x, g_ref[...]).astype(BF16)
    qc = qc_ref[0]
    cs = []
    for h in range(C_HEADS):
        hs = slice(h * C_HEAD_DIM, (h + 1) * C_HEAD_DIM)
        s = lax.dot_general(qc[:, hs], kv_ref[0, :, hs], (((1,), (1,)), ((), ())), preferred_element_type=F32)
        s = s * (1.0 / math.sqrt(C_HEAD_DIM))
        p = jnp.exp(s - jnp.max(s, axis=-1, keepdims=True))
        p = p / jnp.sum(p, axis=-1, keepdims=True)
        vs = slice(C_WIDTH + h * C_HEAD_DIM, C_WIDTH + (h + 1) * C_HEAD_DIM)
        cs.append(jnp.dot(p.astype(BF16), kv_ref[0, :, vs], preferred_element_type=F32))
    c = jnp.concatenate(cs, axis=1).astype(BF16)
    branches = ((a_ref[0], wa_ref), (b_ref[0], wb_ref), (c, wc_ref))
    merged = None
    for i, (act, w_ref) in enumerate(branches):
        gate = jax.nn.sigmoid(jnp.dot(hb, wg_ref[:, i * D_MODEL:(i + 1) * D_MODEL], preferred_element_type=F32))
        term = gate * jnp.dot(act, w_ref[...], preferred_element_type=F32)
        merged = term if merged is None else merged + term
    o_ref[0] = x + jnp.dot(merged.astype(BF16), wo_ref[...], preferred_element_type=F32)


def _prompt_merge(x, norm_mix, a, bo, qc, kvb, wg, wa, wb, wc, wo):
    b, s, d = x.shape
    tm = TM_MERGE
    tile = lambda w: pl.BlockSpec((1, tm, w), lambda i, j: (i, j, 0))
    const = lambda arr: pl.BlockSpec(arr.shape, lambda i, j: (0,) * arr.ndim)
    return pl.pallas_call(
        _merge_kernel,
        grid=(b, s // tm),
        in_specs=[tile(d), const(norm_mix), tile(A_WIDTH), tile(B_GROUP_W), tile(C_WIDTH),
                  pl.BlockSpec((1,) + kvb.shape[1:], lambda i, j: (i, 0, 0)),
                  const(wg), const(wa), const(wb), const(wc), const(wo)],
        out_specs=tile(d),
        out_shape=jax.ShapeDtypeStruct((b, s, d), F32),
        compiler_params=_cparams("arbitrary", "arbitrary"),
        name="prompt_merge",
    )(x, norm_mix, a, bo, qc, kvb, wg, wa, wb, wc, wo)


def _to_window_layout(w):
    b, _, window = w.shape
    return jnp.transpose(w.reshape(b, 2, 4, B_HEAD_DIM, window), (0, 4, 1, 2, 3))[None]


def _from_window_layout(c):
    b, window = c.shape[:2]
    return jnp.transpose(c, (0, 2, 3, 4, 1)).reshape(b, 2 * B_GROUP_W, window)


def _prompt_mixers(x_prompt, mem_prompt, band_bias, norm_mix, norm_mem, ln_v_g, ln_v_b, w_spatial, b_spatial,
                   w_mem_kv, wts):
    b, s, _ = x_prompt.shape
    wsp = jnp.where(np.tril(np.ones((CHUNK, CHUNK), bool)), w_spatial, 0.0).astype(BF16)
    qkv0, qkv1, qkv2, win0, win1, win2, a_out, qc = _prompt_proj(
        x_prompt, norm_mix[None], wts["mix"], ln_v_g[None], ln_v_b[None], wsp, b_spatial.T)
    b_out = _prompt_attention(qkv0, qkv1, qkv2, band_bias)
    mem_kv, mem_kv_b = _memory_kv(mem_prompt, norm_mem[None], w_mem_kv.astype(BF16))
    x_new = _prompt_merge(x_prompt, norm_mix[None], a_out, b_out, qc, mem_kv_b, wts["gate"],
                          wts["a"], wts["b"], wts["c"], wts["out"])
    wins = [_to_window_layout(w) for w in (win0, win1, win2)]
    return x_new, wins, mem_kv.reshape(1, b, N_MEM, 2, C_HEADS, C_HEAD_DIM)


def _route_kernel(n_tiles, x_ref, g_ref, wr_ref, br_ref, cnt_in_ref, *rest):
    if len(rest) == 7:
        tail_ref, h_ref = rest[:2]
        step = pl.program_id(0)

        @pl.when(step == n_tiles)
        def _():
            n_tail = tail_ref.shape[0]
            h_ref[:n_tail] = tail_ref[...]
            h_ref[n_tail:] = jnp.zeros((h_ref.shape[0] - n_tail, LANES), F32)

        @pl.when(step < n_tiles)
        def _():
            _route_tile(x_ref, g_ref, wr_ref, br_ref, cnt_in_ref, *rest[1:])
    else:
        _route_tile(x_ref, g_ref, wr_ref, br_ref, cnt_in_ref, *rest)


def _route_tile(x_ref, g_ref, wr_ref, br_ref, cnt_in_ref, h_ref, e_ref, rank_ref, gate_ref, cnt_ref, carry_ref):
    tm = x_ref.shape[0]

    @pl.when(pl.program_id(0) == 0)
    def _():
        carry_ref[...] = cnt_in_ref[...].astype(F32)

    h = _rms(x_ref[...], g_ref[...])
    for c in range(ROW_TILES):
        h_ref[pl.ds(c, tm, stride=ROW_TILES), :] = h[:, c * LANES:(c + 1) * LANES]
    logits = lax.dot_general(wr_ref[...], h.astype(BF16), (((1,), (1,)), ((), ())),
                             preferred_element_type=F32) + br_ref[...]
    expert = lax.broadcasted_iota(I32, logits.shape, 0)
    vals, idxs = [], []
    member = jnp.zeros(logits.shape, F32)
    for _ in range(TOP_K):
        m = jnp.max(logits, axis=0, keepdims=True)
        idx = jnp.min(jnp.where(logits == m, expert, N_EXPERTS), axis=0, keepdims=True)
        hit = expert == idx
        vals.append(m)
        idxs.append(idx)
        member = jnp.where(hit, 1.0, member)
        logits = jnp.where(hit, -jnp.inf, logits)
    p = [jnp.exp(v - vals[0]) for v in vals]
    den = p[0] + p[1] + p[2] + p[3]
    gates = jnp.concatenate([pk / den for pk in p] + [jnp.zeros((SUBLANES - TOP_K, tm), F32)], axis=0)
    gate_ref[...] = jnp.transpose(gates)
    e_ref[...] = jnp.concatenate(idxs, axis=0)
    before = (lax.broadcasted_iota(I32, (tm, tm), 0) < lax.broadcasted_iota(I32, (tm, tm), 1)).astype(BF16)
    prefix = jnp.dot(member.astype(BF16), before, preferred_element_type=F32) + carry_ref[:, 0:1]
    rank_ref[...] = jnp.concatenate(
        [jnp.sum(jnp.where(expert == idx, prefix, 0.0), axis=0, keepdims=True) for idx in idxs], axis=0).astype(I32)
    carry_ref[...] = carry_ref[...] + jnp.sum(member, axis=1, keepdims=True)
    cnt_ref[...] = carry_ref[...].astype(I32)


def _route(x, norm_ffn, w_router_t, b_router, cnt_in, tm, tail_rows=None):
    n, d = x.shape
    assert n % tm == 0
    nt = n // tm
    extra = 0 if tail_rows is None else 1
    const = lambda arr: pl.BlockSpec(arr.shape, lambda i: (0,) * arr.ndim)
    tile_idx = lambda i: jnp.minimum(i, nt - 1)
    in_specs = [pl.BlockSpec((tm, d), lambda i: (tile_idx(i), 0)), const(norm_ffn), const(w_router_t),
                const(b_router), const(cnt_in)]
    args = [x, norm_ffn, w_router_t, b_router, cnt_in]
    if extra:
        assert tail_rows.shape[0] <= tm * ROW_TILES
        in_specs.append(const(tail_rows))
        args.append(tail_rows)
    return pl.pallas_call(
        functools.partial(_route_kernel, nt),
        grid=(nt + extra,),
        in_specs=in_specs,
        out_specs=(pl.BlockSpec((tm * ROW_TILES, LANES), lambda i: (i, 0)),
                   pl.BlockSpec((TOP_K, tm), lambda i: (0, tile_idx(i))),
                   pl.BlockSpec((TOP_K, tm), lambda i: (0, tile_idx(i))),
                   pl.BlockSpec((tm, SUBLANES), lambda i: (tile_idx(i), 0)),
                   pl.BlockSpec((N_EXPERTS, LANES), lambda i: (0, 0))),
        out_shape=(jax.ShapeDtypeStruct(((nt + extra) * tm * ROW_TILES, LANES), F32),
                   jax.ShapeDtypeStruct((TOP_K, n), I32), jax.ShapeDtypeStruct((TOP_K, n), I32),
                   jax.ShapeDtypeStruct((n, SUBLANES), F32), jax.ShapeDtypeStruct((N_EXPERTS, LANES), I32)),
        scratch_shapes=[pltpu.VMEM((N_EXPERTS, LANES), F32)],
        compiler_params=_cparams("arbitrary"),
        name="moe_route",
    )(*args)


DISPATCH_CHUNK = 2048


def _dispatch_kernel(n_blocks, trips_ref, cnt_ref, e_ref, r_ref, es_ref, rs_ref, dst_ref, plan_ref, start_ref):
    c = pl.program_id(0)
    ch, n_sample = trips_ref[0], trips_ref[1]
    bm = BM_EXPERT
    shift = bm.bit_length() - 1

    @pl.when(c == 0)
    def _():
        def per_expert(e, carry):
            blk0, row0, last_e = carry
            cnt = cnt_ref[e]
            nb = (cnt + (bm - 1)) >> shift
            start_ref[e] = row0

            def fill(j, cc):
                plan_ref[blk0 + j] = e
                plan_ref[n_blocks + blk0 + j] = jnp.minimum(cnt - j * bm, bm)
                plan_ref[2 * n_blocks + blk0 + j] = row0 + j * bm
                return cc
            lax.fori_loop(0, nb, fill, 0)
            return blk0 + nb, row0 + cnt, jnp.where(nb > 0, e, last_e)
        n_used, _, last_e = lax.fori_loop(0, N_EXPERTS, per_expert, (0, 0, 0))

        def tail(j, cc):
            plan_ref[j] = last_e
            plan_ref[n_blocks + j] = 0
            plan_ref[2 * n_blocks + j] = 0
            return cc
        lax.fori_loop(n_used, n_blocks, tail, 0)
        plan_ref[3 * n_blocks] = n_used

    def place(tok, e_src, r_src, j):
        for k in range(TOP_K):
            dst_ref[start_ref[e_src[k, j]] + r_src[k, j]] = tok * TOP_K + k

    def body(j, carry):
        place(c * ch + j, e_ref, r_ref, j)
        return carry
    lax.fori_loop(0, ch, body, 0)

    @pl.when(c == pl.num_programs(0) - 1)
    def _():
        def body_s(j, carry):
            place(pl.num_programs(0) * ch + j, es_ref, rs_ref, j)
            return carry
        lax.fori_loop(0, n_sample, body_s, 0)


def _dispatch_plan(cnt, e_p, r_p, e_s, r_s, n_blocks):
    n_p, n_s = e_p.shape[1], e_s.shape[1]
    ch = DISPATCH_CHUNK
    assert n_p % ch == 0
    trips = jnp.array([ch, n_s], I32)
    smem = lambda: pl.BlockSpec(memory_space=pltpu.SMEM)
    chunk = lambda: pl.BlockSpec((TOP_K, ch), lambda i: (0, i), memory_space=pltpu.SMEM)
    return pl.pallas_call(
        functools.partial(_dispatch_kernel, n_blocks),
        grid=(n_p // ch,),
        in_specs=[smem(), smem(), chunk(), chunk(), smem(), smem()],
        out_specs=(smem(), smem()),
        out_shape=(jax.ShapeDtypeStruct(((n_p + n_s) * TOP_K,), I32), jax.ShapeDtypeStruct((3 * n_blocks + 1,), I32)),
        scratch_shapes=[pltpu.SMEM((N_EXPERTS,), I32)],
        compiler_params=_cparams("arbitrary"),
        name="moe_dispatch_plan",
    )(trips, cnt, e_p, r_p, e_s, r_s)


DMA_UNROLL = 8


def _expert_kernel(n_blocks, dst_ref, plan_ref, h_hbm, wgu_ref, bgu_ref, wd_ref, bd_ref, y_hbm,
                   xbuf, ybuf, gsem, ssem):
    i = pl.program_id(0)
    n_used = plan_ref[3 * n_blocks]
    slot = i % 2
    bm = BM_EXPERT
    tile = lambda row: pl.ds(pl.multiple_of(row * ROW_TILES, ROW_TILES), ROW_TILES)
    n_valid = lambda blk: plan_ref[n_blocks + blk]
    first_row = lambda blk: plan_ref[2 * n_blocks + blk]

    def for_rows(lo, hi, fn):
        n_full = (hi - lo) // DMA_UNROLL

        def body(j, carry):
            for u in range(DMA_UNROLL):
                fn(lo + j * DMA_UNROLL + u)
            return carry
        lax.fori_loop(0, n_full, body, 0)

        def rest(r, carry):
            fn(r)
            return carry
        lax.fori_loop(lo + n_full * DMA_UNROLL, hi, rest, 0)

    def gather_start(blk, sl):
        q0 = first_row(blk)

        def valid(r):
            tok = dst_ref[q0 + r] >> 2
            pltpu.make_async_copy(h_hbm.at[tile(tok), :], xbuf.at[sl, tile(r), :], gsem.at[sl]).start()

        def filler(r):
            pltpu.make_async_copy(h_hbm.at[tile(0), :], xbuf.at[sl, tile(r), :], gsem.at[sl]).start()
        for_rows(0, n_valid(blk), valid)
        for_rows(n_valid(blk), bm, filler)

    def gather_wait(sl):
        pltpu.make_async_copy(h_hbm.at[pl.ds(0, bm * ROW_TILES), :], xbuf.at[sl], gsem.at[sl]).wait()

    def scatter_start(blk, sl):
        q0 = first_row(blk)

        def valid(r):
            pltpu.make_async_copy(ybuf.at[sl, tile(r), :], y_hbm.at[tile(dst_ref[q0 + r]), :], ssem.at[sl]).start()
        for_rows(0, n_valid(blk), valid)

    def scatter_wait(blk, sl):
        nv = n_valid(blk)
        for bit in range(bm.bit_length()):
            rows = (1 << bit) * ROW_TILES

            @pl.when((nv >> bit) & 1 == 1)
            def _():
                pltpu.make_async_copy(ybuf.at[sl, pl.ds(0, rows), :], y_hbm.at[pl.ds(0, rows), :], ssem.at[sl]).wait()

    @pl.when(i == 0)
    def _():
        gather_start(0, 0)

    @pl.when(i < n_used)
    def _():
        gather_wait(slot)

        @pl.when(i + 1 < n_used)
        def _():
            gather_start(i + 1, 1 - slot)

        x = jnp.concatenate([xbuf[slot, pl.ds(c, bm, stride=ROW_TILES), :] for c in range(ROW_TILES)], axis=1)
        gu = jnp.dot(x.astype(BF16), wgu_ref[0], preferred_element_type=F32) + bgu_ref[0]
        gate = jnp.minimum(gu[:, :D_FF], SWIGLU_LIMIT)
        up = jnp.clip(gu[:, D_FF:], -SWIGLU_LIMIT, SWIGLU_LIMIT)
        act = gate * jax.nn.sigmoid(SWIGLU_ALPHA * gate) * (up + 1.0)
        y = jnp.dot(act.astype(BF16), wd_ref[0], preferred_element_type=F32) + bd_ref[0]

        @pl.when(i >= 2)
        def _():
            scatter_wait(i - 2, slot)

        for c in range(ROW_TILES):
            ybuf[slot, pl.ds(c, bm, stride=ROW_TILES), :] = y[:, c * LANES:(c + 1) * LANES]
        scatter_start(i, slot)

    @pl.when(i == n_blocks - 1)
    def _():
        scatter_wait(n_used - 1, (n_used - 1) % 2)

        @pl.when(n_used >= 2)
        def _():
            scatter_wait(n_used - 2, n_used % 2)


def _experts(dst, plan, h_buf, wgu, bgu, wd, bd, n_blocks):
    bm = BM_EXPERT
    by_expert = lambda *shape: pl.BlockSpec((1,) + shape, lambda i, dst, plan: (plan[i],) + (0,) * len(shape))
    grid_spec = pltpu.PrefetchScalarGridSpec(
        num_scalar_prefetch=2,
        grid=(n_blocks,),
        in_specs=[pl.BlockSpec(memory_space=pl.ANY), by_expert(D_MODEL, 2 * D_FF), by_expert(1, 2 * D_FF),
                  by_expert(D_FF, D_MODEL), by_expert(1, D_MODEL)],
        out_specs=pl.BlockSpec(memory_space=pl.ANY),
        scratch_shapes=[pltpu.VMEM((2, bm * ROW_TILES, LANES), F32), pltpu.VMEM((2, bm * ROW_TILES, LANES), F32),
                        pltpu.SemaphoreType.DMA((2,)), pltpu.SemaphoreType.DMA((2,))],
    )
    return pl.pallas_call(
        functools.partial(_expert_kernel, n_blocks),
        grid_spec=grid_spec,
        out_shape=jax.ShapeDtypeStruct((dst.shape[0] * ROW_TILES, LANES), F32),
        compiler_params=_cparams("arbitrary"),
        name="moe_experts",
    )(dst, plan, h_buf, wgu, bgu, wd, bd)


def _combine_kernel(x_ref, y_ref, gate_ref, g_ref, o_ref):
    tm = x_ref.shape[0]
    gates = gate_ref[...]
    cols = []
    for c in range(ROW_TILES):
        acc = x_ref[:, c * LANES:(c + 1) * LANES]
        for k in range(TOP_K):
            acc = acc + gates[:, k:k + 1] * y_ref[pl.ds(k * ROW_TILES + c, tm, stride=TOP_K * ROW_TILES), :]
        cols.append(acc)
    o_ref[...] = _rms(jnp.concatenate(cols, axis=1), g_ref[...])


def _combine(x, y_rows, gates, norm_final, row0, tm):
    n, d = x.shape
    assert n % tm == 0 and row0 % tm == 0
    blk0 = row0 // tm
    return pl.pallas_call(
        _combine_kernel,
        grid=(n // tm,),
        in_specs=[pl.BlockSpec((tm, d), lambda i: (i, 0)),
                  pl.BlockSpec((tm * TOP_K * ROW_TILES, LANES), lambda i: (i + blk0, 0)),
                  pl.BlockSpec((tm, SUBLANES), lambda i: (i, 0)), pl.BlockSpec((1, d), lambda i: (0, 0))],
        out_specs=pl.BlockSpec((tm, d), lambda i: (i, 0)),
        out_shape=jax.ShapeDtypeStruct((n, d), F32),
        compiler_params=_cparams("arbitrary"),
        name="moe_combine",
    )(x, y_rows, gates, norm_final)


def _moe_and_final_norm(x_p, x_s, norm_ffn, w_router, b_router, w_gate_up, b_gate_up, w_down, b_down, norm_final):
    n_p, n_s = x_p.shape[0], x_s.shape[0]
    n_rows = (n_p + n_s) * TOP_K
    n_blocks = -(-(n_rows + N_EXPERTS * (BM_EXPERT - 1)) // BM_EXPERT)
    w_router_t = w_router.T.astype(BF16)
    b_r = b_router[:, None]
    g_ffn = norm_ffn[None]
    zero_cnt = jnp.zeros((N_EXPERTS, LANES), I32)
    h_s, e_s, r_s, gates_s, cnt_s = _route(x_s, g_ffn, w_router_t, b_r, zero_cnt, n_s)
    h_buf, e_p, r_p, gates_p, cnt = _route(x_p, g_ffn, w_router_t, b_r, cnt_s, TM_ROUTE, tail_rows=h_s)
    dst, plan = _dispatch_plan(cnt[:, 0], e_p, r_p, e_s, r_s, n_blocks)
    y_rows = _experts(dst, plan, h_buf, w_gate_up.astype(BF16), b_gate_up[:, None, :], w_down.astype(BF16),
                      b_down[:, None, :], n_blocks)
    y_p = _combine(x_p, y_rows, gates_p, norm_final[None], 0, TM_COMBINE)
    y_s = _combine(x_s, y_rows, gates_s, norm_final[None], n_p, n_s)
    return y_p, y_s


def _rb(x):
    return x.astype(BF16).astype(F32)


def _sample_proj_kernel(x_ref, g_ref, wm_ref, wg_ref, z_ref):
    hb = _rms(x_ref[...], g_ref[...]).astype(BF16)
    z_ref[:, :MIX_WIDTH] = jnp.dot(hb, wm_ref[...], preferred_element_type=F32)
    z_ref[:, MIX_WIDTH:] = jnp.dot(hb, wg_ref[...], preferred_element_type=F32)


def _sample_proj(x, norm_mix, w_mix, w_gate):
    n, d = x.shape
    args = (x, norm_mix, w_mix, w_gate)
    width = w_mix.shape[1] + w_gate.shape[1]
    return pl.pallas_call(
        _sample_proj_kernel,
        grid=(1,),
        in_specs=[pl.BlockSpec(a.shape, lambda j: (0, 0)) for a in args],
        out_specs=pl.BlockSpec((n, width), lambda j: (0, 0)),
        out_shape=jax.ShapeDtypeStruct((n, width), F32),
        compiler_params=_cparams("arbitrary"),
        name="sample_proj",
    )(*args)


def _sample_mix_kernel(z_ref, c0_ref, c1_ref, c2_ref, mem_ref, sb0_ref, sb1_ref, sb2_ref, nb_ref, lng_ref, lnb_ref,
                       wsp_ref, bsp_ref, bo_ref, co_ref, a_ref, v_ref, n0_ref, n1_ref, n2_ref):
    z = z_ref[0]
    scale = 1.0 / math.sqrt(B_HEAD_DIM)
    row_head = lax.broadcasted_iota(I32, (SUBLANES, B_GROUP_W), 0)
    lane_head = lax.broadcasted_iota(I32, (SUBLANES, B_GROUP_W), 1) // B_HEAD_DIM
    own = (row_head == lane_head).astype(F32)
    kv_cols = jnp.transpose(jnp.broadcast_to(z[:, B_WIDTH:3 * B_WIDTH], (SUBLANES, 2 * B_WIDTH)))[:, 0:1]
    outs, lses = [], []
    for g, (cache_ref, sb_ref, new_ref) in enumerate(((c0_ref, sb0_ref, n0_ref), (c1_ref, sb1_ref, n1_ref),
                                                      (c2_ref, sb2_ref, n2_ref))):
        window = cache_ref.shape[2]
        q, kn, vn = (_rb(z[:, i * B_WIDTH + g * B_GROUP_W:i * B_WIDTH + (g + 1) * B_GROUP_W]) for i in range(3))
        qh = own * q
        kt = cache_ref[0, :B_GROUP_W, :].astype(BF16)
        vt = cache_ref[0, B_GROUP_W:, :].astype(BF16)
        lg = jnp.dot(qh.astype(BF16), kt, preferred_element_type=F32) * scale + sb_ref[...]
        ln = jnp.sum(qh * kn, axis=1, keepdims=True) * scale + nb_ref[g, :, 0:1]
        m = jnp.maximum(jnp.max(lg, axis=1, keepdims=True), ln)
        lse = m + jnp.log(jnp.sum(jnp.exp(lg - m), axis=1, keepdims=True) + jnp.exp(ln - m))
        p = jnp.exp(lg - lse).astype(BF16)
        pn = _rb(jnp.exp(ln - lse))
        pv = lax.dot_general(p, vt, (((1,), (1,)), ((), ())), preferred_element_type=F32) + pn * vn
        outs.append(jnp.sum(own * pv, axis=0, keepdims=True))
        lses.append(jnp.sum(own * lse, axis=0, keepdims=True))
        new_col = jnp.concatenate([kv_cols[i * B_WIDTH + g * B_GROUP_W:i * B_WIDTH + (g + 1) * B_GROUP_W]
                                   for i in range(2)], axis=0)
        lane = lax.broadcasted_iota(I32, (1, window), 1)
        new_ref[0] = jnp.where(lane == window - 1, new_col, pltpu.roll(cache_ref[0], window - 1, axis=1))
    m = jnp.maximum(jnp.maximum(lses[0], lses[1]), lses[2])
    ws = [jnp.exp(l - m) for l in lses]
    den = ws[0] + ws[1] + ws[2]
    bo_ref[0] = _rb(ws[0] / den) * _rb(outs[0]) + _rb(ws[1] / den) * _rb(outs[1]) + _rb(ws[2] / den) * _rb(outs[2])

    cs = []
    for h in range(C_HEADS):
        c0 = MIX_WIDTH - C_WIDTH + h * C_HEAD_DIM
        qh = _rb(z[:, c0:c0 + C_HEAD_DIM])
        kh = _rb(mem_ref[0, pl.ds(h, N_MEM, stride=2 * C_HEADS), :])
        vh = _rb(mem_ref[0, pl.ds(C_HEADS + h, N_MEM, stride=2 * C_HEADS), :])
        s = jnp.sum(kh * qh, axis=-1, keepdims=True) * (1.0 / math.sqrt(C_HEAD_DIM))
        p = jnp.exp(s - jnp.max(s, axis=0, keepdims=True))
        p = _rb(p / jnp.sum(p, axis=0, keepdims=True))
        cs.append(jnp.sum(p * vh, axis=0, keepdims=True))
    co_ref[0] = jnp.concatenate(cs, axis=1)

    u = _gelu(z[:, 3 * B_WIDTH:3 * B_WIDTH + A_WIDTH])
    v = _layernorm(_gelu(z[:, 3 * B_WIDTH + A_WIDTH:3 * B_WIDTH + 2 * A_WIDTH]), lng_ref[...], lnb_ref[...])
    v_ref[0] = v
    a_ref[0] = u * (_rb(wsp_ref[...]) * _rb(v) + bsp_ref[...])


def _sample_mix(z, caches, mem, sbias, nbias, ln_g, ln_b, wsp0, bsp0):
    n = z.shape[0]
    const = lambda arr: pl.BlockSpec(arr.shape, lambda i: (0,) * arr.ndim)
    row = lambda w: pl.BlockSpec((1, 1, w), lambda i: (i, 0, 0))
    per_req = lambda arr: pl.BlockSpec((1,) + arr.shape[1:], lambda i: (i, 0, 0))
    widths = (B_GROUP_W, C_WIDTH, A_WIDTH, A_WIDTH)
    return pl.pallas_call(
        _sample_mix_kernel,
        grid=(n,),
        in_specs=[row(z.shape[2])] + [per_req(c) for c in caches] + [per_req(mem)]
        + [const(a) for a in (*sbias, nbias, ln_g, ln_b, wsp0, bsp0)],
        out_specs=tuple(row(w) for w in widths) + tuple(per_req(c) for c in caches),
        out_shape=tuple(jax.ShapeDtypeStruct((n, 1, w), F32) for w in widths)
        + tuple(jax.ShapeDtypeStruct(c.shape, F32) for c in caches),
        compiler_params=_cparams("arbitrary"),
        name="sample_mix",
    )(z, *caches, mem, *sbias, nbias, ln_g, ln_b, wsp0, bsp0)


def _sample_merge_kernel(x_ref, z_ref, a_ref, b_ref, c_ref, wa_ref, wb_ref, wc_ref, wo_ref, o_ref):
    dot = lambda a, b: jnp.dot(a.astype(BF16), b, preferred_element_type=F32)
    merged = None
    for i, (act_ref, w_ref) in enumerate(((a_ref, wa_ref), (b_ref, wb_ref), (c_ref, wc_ref))):
        gate = jax.nn.sigmoid(z_ref[:, MIX_WIDTH + i * D_MODEL:MIX_WIDTH + (i + 1) * D_MODEL])
        term = gate * dot(act_ref[...], w_ref[...])
        merged = term if merged is None else merged + term
    o_ref[...] = x_ref[...] + dot(merged, wo_ref[...])


def _sample_merge(x, z, a, bo, co, wa, wb, wc, wo):
    args = (x, z, a, bo, co, wa, wb, wc, wo)
    return pl.pallas_call(
        _sample_merge_kernel,
        grid=(1,),
        in_specs=[pl.BlockSpec(arr.shape, lambda i: (0, 0)) for arr in args],
        out_specs=pl.BlockSpec(x.shape, lambda i: (0, 0)),
        out_shape=jax.ShapeDtypeStruct(x.shape, F32),
        compiler_params=_cparams("arbitrary"),
        name="sample_merge",
    )(*args)


def _sample_mixers(x_sample, caches, cache_mem, sbias, nbias, norm_mix, ln_v_g, ln_v_b, w_spatial, b_spatial, wts):
    n = x_sample.shape[0]
    x = x_sample[:, 0]
    z = _sample_proj(x, norm_mix[None], wts["mix"], wts["gate"])
    mem = cache_mem.reshape(n, N_MEM * 2 * C_HEADS, C_HEAD_DIM)
    wsp0 = jnp.repeat(w_spatial[:, 0, 0], LANES)[None]
    bsp0 = jnp.repeat(b_spatial[:, 0], LANES)[None]
    bo, co, a, v, *new_caches = _sample_mix(z[:, None], [_from_window_layout(c) for c in caches], mem, sbias, nbias,
                                            ln_v_g[None], ln_v_b[None], wsp0, bsp0)
    x_new = _sample_merge(x, z, a[:, 0], bo[:, 0], co[:, 0], wts["a"], wts["b"], wts["c"], wts["out"])
    return x_new, [_to_window_layout(c) for c in new_caches], v[:, 0]


def _mixer_weights(w_in, w_branch_a, w_branch_b, w_branch_c, w_out):
    return {"mix": w_in[:, :MIX_WIDTH].astype(BF16), "gate": w_in[:, MIX_WIDTH:].astype(BF16),
            "a": w_branch_a.astype(BF16), "b": w_branch_b.astype(BF16), "c": w_branch_c.astype(BF16),
            "out": w_out.astype(BF16)}


def kernel(x_prompt, x_sample, cache_win0_kv, cache_win1_kv, cache_win2_kv, cache_mem_kv, mem_prompt, rel_bias,
           norm_mix, norm_mem, w_in, ln_v_g, ln_v_b, w_spatial, b_spatial, w_mem_kv, w_branch_a, w_branch_b,
           w_branch_c, w_out, norm_ffn, w_router, b_router, w_gate_up, b_gate_up, w_down, b_down, norm_final):
    assert norm_mix.shape[0] == 1, "one layer"
    b, s, d = x_prompt.shape
    n_s = x_sample.shape[0]
    caches = (cache_win0_kv[0], cache_win1_kv[0], cache_win2_kv[0])
    assert all(c.shape[1] == w for c, (w, _) in zip(caches, B_PAIRS)), "window buffers hold a full window"
    wts = _mixer_weights(w_in[0], w_branch_a[0], w_branch_b[0], w_branch_c[0], w_out[0])
    band_bias, *sbias, nbias = _bias_tables(rel_bias)
    xp, win_p, mem_p = _prompt_mixers(x_prompt, mem_prompt, band_bias, norm_mix[0], norm_mem[0], ln_v_g[0],
                                      ln_v_b[0], w_spatial[0], b_spatial[0], w_mem_kv[0], wts)
    xs, win_s, v_s = _sample_mixers(x_sample, caches, cache_mem_kv[0], sbias, nbias, norm_mix[0], ln_v_g[0],
                                    ln_v_b[0], w_spatial[0], b_spatial[0], wts)
    y_p, y_s = _moe_and_final_norm(xp.reshape(b * s, d), xs, norm_ffn[0], w_router[0], b_router[0], w_gate_up[0],
                                   b_gate_up[0], w_down[0], b_down[0], norm_final)
    chunk_v = v_s.reshape(1, n_s, 1, A_GROUPS, LANES)
    return (y_p.reshape(b, s, d), y_s[:, None], win_p[0], win_p[1], win_p[2], mem_p,
            win_s[0], win_s[1], win_s[2], chunk_v)
```

```python
import functools
import math

import numpy as np
import jax
import jax.numpy as jnp
from jax import lax
from jax.experimental import pallas as pl
from jax.experimental.pallas import tpu as pltpu

F32 = jnp.float32
BF16 = jnp.bfloat16
I32 = jnp.int32

D_MODEL = 1024
N_MEM = 256
CHUNK = 128
A_GROUPS = 4
A_WIDTH = 512
B_PAIRS = ((128, 1), (512, 4), (2048, 16))
B_GROUP_W = 256
B_WIDTH = 768
B_HEAD_DIM = 64
BAND = 128
C_HEADS = 4
C_HEAD_DIM = 128
C_WIDTH = 512
REL_BUCKETS = 32
REL_MAX_DIST = 2048
N_EXPERTS = 32
TOP_K = 4
D_FF = 1024
SWIGLU_LIMIT = 7.0
SWIGLU_ALPHA = 1.702
EPS = 1e-6
NEG_INF = -1e30
MIX_WIDTH = 3 * B_WIDTH + 2 * A_WIDTH + C_WIDTH

LANES = 128
SUBLANES = 8
ROW_TILES = D_MODEL // LANES
VMEM_LIMIT = 56 * 1024 * 1024

TM_PROJ = 512
TM_MERGE = 256
TM_ROUTE = 512
BM_EXPERT = 256
TM_COMBINE = 256


def _cparams(*sem):
    return pltpu.CompilerParams(dimension_semantics=sem, vmem_limit_bytes=VMEM_LIMIT)


def _t5_bucket(dist):
    dist = np.maximum(np.asarray(dist), 0)
    max_exact = REL_BUCKETS // 2
    log_ratio = np.log(np.maximum(dist, max_exact) / max_exact) / math.log(REL_MAX_DIST / max_exact)
    large = np.minimum(max_exact + (log_ratio * (REL_BUCKETS - max_exact)).astype(np.int32), REL_BUCKETS - 1)
    return np.where(dist < max_exact, dist, large).astype(np.int32)


def _gelu(x):
    c = math.sqrt(2.0 / math.pi)
    return x * (0.5 * (1.0 + jnp.tanh(c * (x + 0.044715 * (x * x * x)))))


def _rms(x, g):
    return x * lax.rsqrt(jnp.mean(x * x, axis=-1, keepdims=True) + EPS) * g


def _layernorm(x, g, b):
    xc = x - jnp.mean(x, axis=-1, keepdims=True)
    return xc * lax.rsqrt(jnp.mean(xc * xc, axis=-1, keepdims=True) + EPS) * g + b


def _proj_kernel(x_ref, g_ref, w_ref, lng_ref, lnb_ref, wsp_ref, bsp_ref,
                 qkv0_ref, qkv1_ref, qkv2_ref, win0_ref, win1_ref, win2_ref, a_ref, qc_ref, scr_ref):
    t = pl.program_id(1)
    last = pl.num_programs(1) - 1
    tm = x_ref.shape[1]
    hb = _rms(x_ref[0], g_ref[...]).astype(BF16)

    def proj(c0, width):
        return jnp.dot(hb, w_ref[:, c0:c0 + width], preferred_element_type=F32)

    for g, (_, dil) in enumerate(B_PAIRS):
        q, k, v = (proj(i * B_WIDTH + g * B_GROUP_W, B_GROUP_W) for i in range(3))
        kv = jnp.concatenate([k, v], axis=1)
        if g == 2:
            win2_ref[0] = jnp.transpose(kv)
        elif g == 1:
            @pl.when(t == last)
            def _():
                win1_ref[0] = jnp.transpose(kv)
        else:
            @pl.when(t == last)
            def _():
                win0_ref[0] = jnp.transpose(kv[tm - B_PAIRS[0][0]:])
        if dil == 1:
            qkv0_ref[0] = jnp.concatenate([q, k, v], axis=1).astype(BF16)
            continue
        out_ref = qkv1_ref if g == 1 else qkv2_ref
        for i, arr in enumerate((q, k, v)):
            for ct in range(2):
                scr_ref[2 * i + ct] = arr[:, ct * LANES:(ct + 1) * LANES]
        for r in range(dil):
            rows = [scr_ref[j, pl.ds(r, tm // dil, stride=dil), :] for j in range(6)]
            out_ref[0, r] = jnp.concatenate(rows, axis=1).astype(BF16)

    u = _gelu(proj(3 * B_WIDTH, A_WIDTH))
    v = _layernorm(_gelu(proj(3 * B_WIDTH + A_WIDTH, A_WIDTH)), lng_ref[...], lnb_ref[...]).astype(BF16)
    for c in range(tm // CHUNK):
        rs = slice(c * CHUNK, (c + 1) * CHUNK)
        for g in range(A_GROUPS):
            cs = slice(g * LANES, (g + 1) * LANES)
            s = jnp.dot(wsp_ref[g], v[rs, cs], preferred_element_type=F32) + bsp_ref[:, g:g + 1]
            a_ref[0, rs, cs] = (u[rs, cs] * s).astype(BF16)

    qc_ref[0] = proj(3 * B_WIDTH + 2 * A_WIDTH, C_WIDTH).astype(BF16)


def _prompt_proj(x, norm_mix, w_mix, ln_g, ln_b, wsp, bsp):
    b, s, d = x.shape
    tm = TM_PROJ
    assert s % tm == 0 and s >= B_PAIRS[2][0] and tm == B_PAIRS[1][0]
    nt = s // tm
    const = lambda *shape: pl.BlockSpec(shape, lambda i, j: (0,) * len(shape))
    out_shape = (
        jax.ShapeDtypeStruct((b, s, 3 * B_GROUP_W), BF16),
        jax.ShapeDtypeStruct((b, 4, s // 4, 3 * B_GROUP_W), BF16),
        jax.ShapeDtypeStruct((b, 16, s // 16, 3 * B_GROUP_W), BF16),
        jax.ShapeDtypeStruct((b, 2 * B_GROUP_W, B_PAIRS[0][0]), F32),
        jax.ShapeDtypeStruct((b, 2 * B_GROUP_W, B_PAIRS[1][0]), F32),
        jax.ShapeDtypeStruct((b, 2 * B_GROUP_W, s), F32),
        jax.ShapeDtypeStruct((b, s, A_WIDTH), BF16),
        jax.ShapeDtypeStruct((b, s, C_WIDTH), BF16),
    )
    out_specs = (
        pl.BlockSpec((1, tm, 3 * B_GROUP_W), lambda i, j: (i, j, 0)),
        pl.BlockSpec((1, 4, tm // 4, 3 * B_GROUP_W), lambda i, j: (i, 0, j, 0)),
        pl.BlockSpec((1, 16, tm // 16, 3 * B_GROUP_W), lambda i, j: (i, 0, j, 0)),
        pl.BlockSpec((1, 2 * B_GROUP_W, B_PAIRS[0][0]), lambda i, j: (i, 0, 0)),
        pl.BlockSpec((1, 2 * B_GROUP_W, B_PAIRS[1][0]), lambda i, j: (i, 0, 0)),
        pl.BlockSpec((1, 2 * B_GROUP_W, tm), lambda i, j: (i, 0, j)),
        pl.BlockSpec((1, tm, A_WIDTH), lambda i, j: (i, j, 0)),
        pl.BlockSpec((1, tm, C_WIDTH), lambda i, j: (i, j, 0)),
    )
    return pl.pallas_call(
        _proj_kernel,
        grid=(b, nt),
        in_specs=[pl.BlockSpec((1, tm, d), lambda i, j: (i, j, 0)), const(1, d), const(d, MIX_WIDTH),
                  const(1, A_WIDTH), const(1, A_WIDTH), const(A_GROUPS, CHUNK, CHUNK), const(CHUNK, A_GROUPS)],
        out_specs=out_specs,
        out_shape=out_shape,
        scratch_shapes=[pltpu.VMEM((6, tm, LANES), F32)],
        compiler_params=_cparams("arbitrary", "arbitrary"),
        name="prompt_proj",
    )(x, norm_mix, w_mix, ln_g, ln_b, wsp, bsp)


def _attend(q, k, v, bias_ref, g, k0, hmask):
    kn = k.shape[0]
    out = jnp.zeros((BAND, B_GROUP_W), F32)
    lse = jnp.zeros((BAND, B_GROUP_W), F32)
    for h in range(4):
        hm = hmask[h]
        s = lax.dot_general(q * hm, k, (((1,), (1,)), ((), ())), preferred_element_type=F32)
        s = s * (1.0 / math.sqrt(B_HEAD_DIM)) + bias_ref[g, h, :, k0:k0 + kn]
        m = jnp.max(s, axis=-1, keepdims=True)
        p = jnp.exp(s - m)
        l = jnp.sum(p, axis=-1, keepdims=True)
        pv = jnp.dot(p.astype(BF16), v * hm, preferred_element_type=F32)
        sel = hm > 0
        out = jnp.where(sel, pv / l, out)
        lse = jnp.where(sel, m + jnp.log(l), lse)
    return out, lse


def _attn_kernel(q0_ref, q1_ref, q2_ref, bias_ref, o_ref, out_ref, lse_ref):
    s = q0_ref.shape[1]
    lane_head = lax.broadcasted_iota(I32, (1, B_GROUP_W), 1) // B_HEAD_DIM
    hmask = [(lane_head == h).astype(BF16) for h in range(4)]
    qs, ks, vs = (slice(i * B_GROUP_W, (i + 1) * B_GROUP_W) for i in range(3))

    def store(g, start, dil, out, lse):
        rows = pl.ds(start, BAND) if dil == 1 else pl.ds(start, BAND, stride=dil)
        for ct in range(2):
            out_ref[g, ct, rows, :] = out[:, ct * LANES:(ct + 1) * LANES]
            lse_ref[g, ct, rows, :] = lse[:, ct * LANES:(ct + 1) * LANES]

    for g, (_, dil) in enumerate(B_PAIRS):
        n = s // dil
        nb = n // BAND

        def load(rows, cols, r, g=g):
            if g == 0:
                return q0_ref[0, rows, cols]
            return (q1_ref if g == 1 else q2_ref)[0, r, rows, cols]

        def first_block(r, carry, g=g, dil=dil, load=load):
            rows = pl.ds(0, BAND)
            out, lse = _attend(load(rows, qs, r), load(rows, ks, r), load(rows, vs, r), bias_ref, g, BAND, hmask)
            store(g, r, dil, out, lse)
            return carry

        def later_block(i, carry, g=g, dil=dil, nb=nb, load=load):
            r = i // (nb - 1)
            qb = i % (nb - 1) + 1
            q0 = pl.multiple_of(qb * BAND, BAND)
            rows_q = pl.ds(q0, BAND)
            rows_k = pl.ds(q0 - BAND, 2 * BAND)
            out, lse = _attend(load(rows_q, qs, r), load(rows_k, ks, r), load(rows_k, vs, r), bias_ref, g, 0, hmask)
            store(g, qb * (BAND * dil) + r, dil, out, lse)
            return carry

        lax.fori_loop(0, dil, first_block, 0)
        if nb > 1:
            lax.fori_loop(0, dil * (nb - 1), later_block, 0)

    rc = 256
    def combine(c, carry):
        rows = pl.ds(pl.multiple_of(c * rc, rc), rc)
        for ct in range(2):
            ls = [lse_ref[g, ct, rows, :] for g in range(3)]
            m = jnp.maximum(jnp.maximum(ls[0], ls[1]), ls[2])
            ws = [jnp.exp(l - m) for l in ls]
            den = ws[0] + ws[1] + ws[2]
            num = ws[0] * out_ref[0, ct, rows, :] + ws[1] * out_ref[1, ct, rows, :] + ws[2] * out_ref[2, ct, rows, :]
            o_ref[0, rows, ct * LANES:(ct + 1) * LANES] = (num / den).astype(BF16)
        return carry
    lax.fori_loop(0, s // rc, combine, 0)


def _prompt_attention(qkv0, qkv1, qkv2, bias):
    b, s, w = qkv0.shape
    return pl.pallas_call(
        _attn_kernel,
        grid=(b,),
        in_specs=[pl.BlockSpec((1, s, w), lambda i: (i, 0, 0)),
                  pl.BlockSpec((1, 4, s // 4, w), lambda i: (i, 0, 0, 0)),
                  pl.BlockSpec((1, 16, s // 16, w), lambda i: (i, 0, 0, 0)),
                  pl.BlockSpec(bias.shape, lambda i: (0, 0, 0, 0))],
        out_specs=pl.BlockSpec((1, s, B_GROUP_W), lambda i: (i, 0, 0)),
        out_shape=jax.ShapeDtypeStruct((b, s, B_GROUP_W), BF16),
        scratch_shapes=[pltpu.VMEM((3, 2, s, LANES), F32), pltpu.VMEM((3, 2, s, LANES), F32)],
        compiler_params=_cparams("arbitrary"),
        name="prompt_attention",
    )(qkv0, qkv1, qkv2, bias)


def _bias_kernel(rel_ref, band_idx_ref, s0_ref, s1_ref, s2_ref, band_ref, sb0_ref, sb1_ref, sb2_ref, nb_ref):
    def lookup(idx, col):
        acc = jnp.full(idx.shape, NEG_INF, F32)
        for bucket in range(REL_BUCKETS):
            acc = jnp.where(idx == bucket, rel_ref[bucket, col], acc)
        return acc

    for g, (s_ref, sb_ref) in enumerate(((s0_ref, sb0_ref), (s1_ref, sb1_ref), (s2_ref, sb2_ref))):
        width = s_ref.shape[1]
        for h in range(4):
            band_ref[g, h] = lookup(band_idx_ref[g], 4 * g + h)
        sb_ref[...] = jnp.concatenate([lookup(s_ref[...], 4 * g + h) for h in range(4)]
                                      + [jnp.full((SUBLANES - 4, width), NEG_INF, F32)], axis=0)
        nb_ref[g] = jnp.concatenate([jnp.full((1, LANES), rel_ref[0, 4 * g + h], F32) for h in range(4)]
                                    + [jnp.zeros((SUBLANES - 4, LANES), F32)], axis=0)


def _bias_tables(rel_bias):
    steps = np.arange(BAND)[:, None] + BAND - np.arange(2 * BAND)[None, :]
    valid = (steps >= 0) & (steps <= BAND)
    band_idx = np.stack([np.where(valid, _t5_bucket(np.clip(steps, 0, BAND) * dil), -1) for _, dil in B_PAIRS])
    s_idx = []
    for window, dil in B_PAIRS:
        w = np.arange(window)
        s_idx.append(np.where(w % dil == 0, _t5_bucket(window - w), -1)[None].astype(np.int32))
    args = (rel_bias, jnp.asarray(band_idx.astype(np.int32))) + tuple(jnp.asarray(s) for s in s_idx)
    vmem = lambda a: pl.BlockSpec(a.shape, lambda i: (0,) * a.ndim)
    out_shape = (jax.ShapeDtypeStruct((3, 4, BAND, 2 * BAND), F32),) + tuple(
        jax.ShapeDtypeStruct((SUBLANES, window), F32) for window, _ in B_PAIRS) + (
        jax.ShapeDtypeStruct((3, SUBLANES, LANES), F32),)
    return pl.pallas_call(
        _bias_kernel,
        grid=(1,),
        in_specs=[pl.BlockSpec(memory_space=pltpu.SMEM)] + [vmem(a) for a in args[1:]],
        out_specs=tuple(pl.BlockSpec(s.shape, lambda i, n=len(s.shape): (0,) * n) for s in out_shape),
        out_shape=out_shape,
        compiler_params=_cparams("arbitrary"),
        name="bias_tables",
    )(*args)


def _memkv_kernel(mem_ref, g_ref, w_ref, kv_ref, kvb_ref):
    hb = _rms(mem_ref[0], g_ref[...]).astype(BF16)
    kv = jnp.dot(hb, w_ref[...], preferred_element_type=F32)
    kv_ref[0] = kv
    kvb_ref[0] = kv.astype(BF16)


def _memory_kv(mem, norm_mem, w_mem):
    b, m, d = mem.shape
    w = w_mem.shape[1]
    return pl.pallas_call(
        _memkv_kernel,
        grid=(b,),
        in_specs=[pl.BlockSpec((1, m, d), lambda i: (i, 0, 0)), pl.BlockSpec((1, d), lambda i: (0, 0)),
                  pl.BlockSpec((d, w), lambda i: (0, 0))],
        out_specs=(pl.BlockSpec((1, m, w), lambda i: (i, 0, 0)), pl.BlockSpec((1, m, w), lambda i: (i, 0, 0))),
        out_shape=(jax.ShapeDtypeStruct((b, m, w), F32), jax.ShapeDtypeStruct((b, m, w), BF16)),
        compiler_params=_cparams("arbitrary"),
        name="memory_kv",
    )(mem, norm_mem, w_mem)


def _merge_kernel(x_ref, g_ref, a_ref, b_ref, qc_ref, kv_ref, wg_ref, wa_ref, wb_ref, wc_ref, wo_ref, o_ref):
    x = x_ref[0]
    hb = _rms(x, g_ref[...]).astype(BF16)
    qc = qc_ref[0]
    cs = []
    for h in range(C_HEADS):
        hs = slice(h * C_HEAD_DIM, (h + 1) * C_HEAD_DIM)
        s = lax.dot_general(qc[:, hs], kv_ref[0, :, hs], (((1,), (1,)), ((), ())), preferred_element_type=F32)
        s = s * (1.0 / math.sqrt(C_HEAD_DIM))
        p = jnp.exp(s - jnp.max(s, axis=-1, keepdims=True))
        p = p / jnp.sum(p, axis=-1, keepdims=True)
        vs = slice(C_WIDTH + h * C_HEAD_DIM, C_WIDTH + (h + 1) * C_HEAD_DIM)
        cs.append(jnp.dot(p.astype(BF16), kv_ref[0, :, vs], preferred_element_type=F32))
    c = jnp.concatenate(cs, axis=1).astype(BF16)
    branches = ((a_ref[0], wa_ref), (b_ref[0], wb_ref), (c, wc_ref))
    merged = None
    for i, (act, w_ref) in enumerate(branches):
        gate = jax.nn.sigmoid(jnp.dot(hb, wg_ref[:, i * D_MODEL:(i + 1) * D_MODEL], preferred_element_type=F32))
        term = gate * jnp.dot(act, w_ref[...], preferred_element_type=F32)
        merged = term if merged is None else merged + term
    o_ref[0] = x + jnp.dot(merged.astype(BF16), wo_ref[...], preferred_element_type=F32)


def _prompt_merge(x, norm_mix, a, bo, qc, kvb, wg, wa, wb, wc, wo):
    b, s, d = x.shape
    tm = TM_MERGE
    tile = lambda w: pl.BlockSpec((1, tm, w), lambda i, j: (i, j, 0))
    const = lambda arr: pl.BlockSpec(arr.shape, lambda i, j: (0,) * arr.ndim)
    return pl.pallas_call(
        _merge_kernel,
        grid=(b, s // tm),
        in_specs=[tile(d), const(norm_mix), tile(A_WIDTH), tile(B_GROUP_W), tile(C_WIDTH),
                  pl.BlockSpec((1,) + kvb.shape[1:], lambda i, j: (i, 0, 0)),
                  const(wg), const(wa), const(wb), const(wc), const(wo)],
        out_specs=tile(d),
        out_shape=jax.ShapeDtypeStruct((b, s, d), F32),
        compiler_params=_cparams("arbitrary", "arbitrary"),
        name="prompt_merge",
    )(x, norm_mix, a, bo, qc, kvb, wg, wa, wb, wc, wo)


def _to_window_layout(w):
    b, _, window = w.shape
    return jnp.transpose(w.reshape(b, 2, 4, B_HEAD_DIM, window), (0, 4, 1, 2, 3))[None]


def _from_window_layout(c):
    b, window = c.shape[:2]
    return jnp.transpose(c, (0, 2, 3, 4, 1)).reshape(b, 2 * B_GROUP_W, window)


def _prompt_mixers(x_prompt, mem_prompt, band_bias, norm_mix, norm_mem, ln_v_g, ln_v_b, w_spatial, b_spatial,
                   w_mem_kv, wts):
    b, s, _ = x_prompt.shape
    wsp = jnp.where(np.tril(np.ones((CHUNK, CHUNK), bool)), w_spatial, 0.0).astype(BF16)
    qkv0, qkv1, qkv2, win0, win1, win2, a_out, qc = _prompt_proj(
        x_prompt, norm_mix[None], wts["mix"], ln_v_g[None], ln_v_b[None], wsp, b_spatial.T)
    b_out = _prompt_attention(qkv0, qkv1, qkv2, band_bias)
    mem_kv, mem_kv_b = _memory_kv(mem_prompt, norm_mem[None], w_mem_kv.astype(BF16))
    x_new = _prompt_merge(x_prompt, norm_mix[None], a_out, b_out, qc, mem_kv_b, wts["gate"],
                          wts["a"], wts["b"], wts["c"], wts["out"])
    wins = [_to_window_layout(w) for w in (win0, win1, win2)]
    return x_new, wins, mem_kv.reshape(1, b, N_MEM, 2, C_HEADS, C_HEAD_DIM)


def _route_kernel(x_ref, g_ref, wr_ref, br_ref, cnt_in_ref, e_ref, rank_ref, gate_ref, cnt_ref, carry_ref):
    tm = x_ref.shape[0]

    @pl.when(pl.program_id(0) == 0)
    def _():
        carry_ref[...] = cnt_in_ref[...].astype(F32)

    h = _rms(x_ref[...], g_ref[...])
    logits = lax.dot_general(wr_ref[...], h.astype(BF16), (((1,), (1,)), ((), ())),
                             preferred_element_type=F32) + br_ref[...]
    expert = lax.broadcasted_iota(I32, logits.shape, 0)
    vals, idxs = [], []
    member = jnp.zeros(logits.shape, F32)
    for _ in range(TOP_K):
        m = jnp.max(logits, axis=0, keepdims=True)
        idx = jnp.min(jnp.where(logits == m, expert, N_EXPERTS), axis=0, keepdims=True)
        hit = expert == idx
        vals.append(m)
        idxs.append(idx)
        member = jnp.where(hit, 1.0, member)
        logits = jnp.where(hit, -jnp.inf, logits)
    p = [jnp.exp(v - vals[0]) for v in vals]
    den = p[0] + p[1] + p[2] + p[3]
    gates = jnp.concatenate([pk / den for pk in p] + [jnp.zeros((SUBLANES - TOP_K, tm), F32)], axis=0)
    gate_ref[...] = jnp.transpose(gates)
    e_ref[...] = jnp.concatenate(idxs, axis=0)
    before = (lax.broadcasted_iota(I32, (tm, tm), 0) < lax.broadcasted_iota(I32, (tm, tm), 1)).astype(BF16)
    prefix = jnp.dot(member.astype(BF16), before, preferred_element_type=F32) + carry_ref[:, 0:1]
    rank_ref[...] = jnp.concatenate(
        [jnp.sum(jnp.where(expert == idx, prefix, 0.0), axis=0, keepdims=True) for idx in idxs], axis=0).astype(I32)
    carry_ref[...] = carry_ref[...] + jnp.sum(member, axis=1, keepdims=True)
    cnt_ref[...] = carry_ref[...].astype(I32)


def _route(x, norm_ffn, w_router_t, b_router, cnt_in, tm):
    n, d = x.shape
    assert n % tm == 0
    const = lambda arr: pl.BlockSpec(arr.shape, lambda i: (0,) * arr.ndim)
    return pl.pallas_call(
        _route_kernel,
        grid=(n // tm,),
        in_specs=[pl.BlockSpec((tm, d), lambda i: (i, 0)), const(norm_ffn), const(w_router_t), const(b_router),
                  const(cnt_in)],
        out_specs=(pl.BlockSpec((TOP_K, tm), lambda i: (0, i)), pl.BlockSpec((TOP_K, tm), lambda i: (0, i)),
                   pl.BlockSpec((tm, SUBLANES), lambda i: (i, 0)), pl.BlockSpec((N_EXPERTS, LANES), lambda i: (0, 0))),
        out_shape=(jax.ShapeDtypeStruct((TOP_K, n), I32), jax.ShapeDtypeStruct((TOP_K, n), I32),
                   jax.ShapeDtypeStruct((n, SUBLANES), F32), jax.ShapeDtypeStruct((N_EXPERTS, LANES), I32)),
        scratch_shapes=[pltpu.VMEM((N_EXPERTS, LANES), F32)],
        compiler_params=_cparams("arbitrary"),
        name="moe_route",
    )(x, norm_ffn, w_router_t, b_router, cnt_in)


DMA_ROWS_PER_ITER = 2
N_PLAN_TAIL = 1 + N_EXPERTS


def _row_tile(r):
    return pl.ds(pl.multiple_of(r * ROW_TILES, ROW_TILES), ROW_TILES)


def _plan_kernel(n_blocks, cnt_ref, plan_ref):
    shift = BM_EXPERT.bit_length() - 1

    def per_expert(e, carry):
        blk0, last_e = carry
        nb = (cnt_ref[e] + (BM_EXPERT - 1)) >> shift
        plan_ref[n_blocks + 1 + e] = blk0 << shift

        def fill(j, cc):
            plan_ref[blk0 + j] = e
            return cc
        lax.fori_loop(0, nb, fill, 0)
        return blk0 + nb, jnp.where(nb > 0, e, last_e)
    n_used, last_e = lax.fori_loop(0, N_EXPERTS, per_expert, (0, 0))

    def tail(j, cc):
        plan_ref[j] = last_e
        return cc
    lax.fori_loop(n_used, n_blocks, tail, 0)
    plan_ref[n_blocks] = n_used


def _moe_plan(cnt, n_blocks):
    smem = pl.BlockSpec(memory_space=pltpu.SMEM)
    return pl.pallas_call(
        functools.partial(_plan_kernel, n_blocks),
        in_specs=[smem], out_specs=smem,
        out_shape=jax.ShapeDtypeStruct((n_blocks + N_PLAN_TAIL,), I32),
        name="moe_plan",
    )(cnt)


def _positions_kernel(n_blocks, plan_ref, e_ref, r_ref, pos_ref):
    e = e_ref[...]
    first = jnp.zeros(e.shape, I32)
    for ex in range(N_EXPERTS):
        first = jnp.where(e == ex, plan_ref[n_blocks + 1 + ex], first)
    pos_ref[0] = first + r_ref[...]


def _positions(plan, e, r, n_blocks, tm):
    n = e.shape[1]
    assert n % tm == 0
    pos = pl.pallas_call(
        functools.partial(_positions_kernel, n_blocks),
        grid=(n // tm,),
        in_specs=[pl.BlockSpec(memory_space=pltpu.SMEM), pl.BlockSpec((TOP_K, tm), lambda i: (0, i)),
                  pl.BlockSpec((TOP_K, tm), lambda i: (0, i))],
        out_specs=pl.BlockSpec((1, TOP_K, tm), lambda i: (i, 0, 0)),
        out_shape=jax.ShapeDtypeStruct((n // tm, TOP_K, tm), I32),
        compiler_params=_cparams("arbitrary"),
        name="moe_positions",
    )(plan, e, r)
    return pos.reshape(-1)


def _scatter_rows_kernel(n_blocks, cnt_ref, plan_ref, x_ref, pos_ref, xs_ref, poss_ref, g_ref, h_hbm,
                         hbuf, zbuf, sem, zsem):
    t = pl.program_id(0)
    n_tiles = pl.num_programs(0) - 1
    tm = x_ref.shape[0]
    n_s = xs_ref.shape[0]
    bm = BM_EXPERT
    shift = bm.bit_length() - 1
    slot = t % 2

    def zero_rows(first, count, wait):
        for bit in range(shift):
            n = 1 << bit
            lo = first + (count & (n - 1))

            @pl.when((count >> bit) & 1 == 1)
            def _():
                cp = pltpu.make_async_copy(zbuf.at[pl.ds(0, n * ROW_TILES), :], h_hbm.at[_row_tile_n(lo, n), :],
                                           zsem.at[0])
                cp.wait() if wait else cp.start()

    def zero_block(blk, wait):
        cp = pltpu.make_async_copy(zbuf, h_hbm.at[_row_tile_n(blk * bm, bm), :], zsem.at[0])
        cp.wait() if wait else cp.start()

    @pl.when(t == 0)
    def _():
        zbuf[...] = jnp.zeros(zbuf.shape, F32)
        for wait in (False, True):
            def pad_expert(e, cc, wait=wait):
                cnt = cnt_ref[e]
                zero_rows(plan_ref[n_blocks + 1 + e] + cnt, (-cnt) & (bm - 1), wait)
                return cc
            lax.fori_loop(0, N_EXPERTS, pad_expert, 0)

            def pad_block(blk, cc, wait=wait):
                zero_block(blk, wait)
                return cc
            lax.fori_loop(plan_ref[n_blocks], n_blocks, pad_block, 0)

    def wait_copies(sl, n_tok):
        for _ in range(TOP_K):
            pltpu.make_async_copy(hbuf.at[sl, pl.ds(0, n_tok * ROW_TILES), :],
                                  h_hbm.at[pl.ds(0, n_tok * ROW_TILES), :], sem.at[sl]).wait()

    def copy_rows(src_ref, rows_ref, n_tok):
        h = _rms(src_ref[...], g_ref[...])
        for c in range(ROW_TILES):
            hbuf[slot, pl.ds(c, n_tok, stride=ROW_TILES), :] = h[:, c * LANES:(c + 1) * LANES]

        def body(it, carry):
            for u in range(DMA_ROWS_PER_ITER):
                j = it * DMA_ROWS_PER_ITER + u
                for k in range(TOP_K):
                    pltpu.make_async_copy(hbuf.at[slot, _row_tile(j), :],
                                          h_hbm.at[_row_tile(rows_ref[k * n_tok + j]), :], sem.at[slot]).start()
            return carry
        lax.fori_loop(0, n_tok // DMA_ROWS_PER_ITER, body, 0)

    @pl.when(t >= 2)
    def _():
        wait_copies(slot, tm)

    @pl.when(t < n_tiles)
    def _():
        copy_rows(x_ref, pos_ref, tm)

    @pl.when(t == n_tiles)
    def _():
        copy_rows(xs_ref, poss_ref, n_s)
        wait_copies(slot, n_s)

        @pl.when(n_tiles >= 1)
        def _():
            wait_copies(1 - slot, tm)


def _row_tile_n(r, n):
    return pl.ds(pl.multiple_of(r * ROW_TILES, ROW_TILES), n * ROW_TILES)


def _scatter_rows(cnt, plan, x_p, pos_p, x_s, pos_s, norm_ffn, n_blocks, tm):
    n_p, d = x_p.shape
    n_s = x_s.shape[0]
    assert n_p % tm == 0 and tm % DMA_ROWS_PER_ITER == 0 and n_s % DMA_ROWS_PER_ITER == 0 and n_s <= tm
    nt = n_p // tm
    smem = lambda: pl.BlockSpec(memory_space=pltpu.SMEM)
    tile_idx = lambda i: jnp.minimum(i, nt - 1)
    return pl.pallas_call(
        functools.partial(_scatter_rows_kernel, n_blocks),
        grid=(nt + 1,),
        in_specs=[smem(), smem(), pl.BlockSpec((tm, d), lambda i: (tile_idx(i), 0)),
                  pl.BlockSpec((TOP_K * tm,), lambda i: (tile_idx(i),), memory_space=pltpu.SMEM),
                  pl.BlockSpec((n_s, d), lambda i: (0, 0)), smem(), pl.BlockSpec((1, d), lambda i: (0, 0))],
        out_specs=pl.BlockSpec(memory_space=pl.ANY),
        out_shape=jax.ShapeDtypeStruct((n_blocks * BM_EXPERT * ROW_TILES, LANES), F32),
        scratch_shapes=[pltpu.VMEM((2, tm * ROW_TILES, LANES), F32), pltpu.VMEM((BM_EXPERT * ROW_TILES, LANES), F32),
                        pltpu.SemaphoreType.DMA((2,)), pltpu.SemaphoreType.DMA((1,))],
        compiler_params=_cparams("arbitrary"),
        name="moe_scatter_rows",
    )(cnt, plan, x_p, pos_p, x_s, pos_s, norm_ffn)


def _expert_blocks_kernel(n_blocks, plan_ref, x_ref, wgu_ref, bgu_ref, wd_ref, bd_ref, y_ref):
    bm = BM_EXPERT

    @pl.when(pl.program_id(0) < plan_ref[n_blocks])
    def _():
        x = jnp.concatenate([x_ref[pl.ds(c, bm, stride=ROW_TILES), :] for c in range(ROW_TILES)], axis=1)
        gu = jnp.dot(x.astype(BF16), wgu_ref[0], preferred_element_type=F32) + bgu_ref[0]
        gate = jnp.minimum(gu[:, :D_FF], SWIGLU_LIMIT)
        up = jnp.clip(gu[:, D_FF:], -SWIGLU_LIMIT, SWIGLU_LIMIT)
        act = gate * jax.nn.sigmoid(SWIGLU_ALPHA * gate) * (up + 1.0)
        y = jnp.dot(act.astype(BF16), wd_ref[0], preferred_element_type=F32) + bd_ref[0]
        for c in range(ROW_TILES):
            y_ref[pl.ds(c, bm, stride=ROW_TILES), :] = y[:, c * LANES:(c + 1) * LANES]

    @pl.when(pl.program_id(0) >= plan_ref[n_blocks])
    def _():
        y_ref[...] = jnp.zeros(y_ref.shape, F32)


def _expert_blocks(plan, h_sorted, wgu, bgu, wd, bd, n_blocks):
    bm = BM_EXPERT
    by_expert = lambda *shape: pl.BlockSpec((1,) + shape, lambda i, plan: (plan[i],) + (0,) * len(shape))
    used = lambda i, plan: (jnp.minimum(i, plan[n_blocks] - 1), 0)
    grid_spec = pltpu.PrefetchScalarGridSpec(
        num_scalar_prefetch=1,
        grid=(n_blocks,),
        in_specs=[pl.BlockSpec((bm * ROW_TILES, LANES), used), by_expert(D_MODEL, 2 * D_FF), by_expert(1, 2 * D_FF),
                  by_expert(D_FF, D_MODEL), by_expert(1, D_MODEL)],
        out_specs=pl.BlockSpec((bm * ROW_TILES, LANES), lambda i, plan: (i, 0)),
    )
    return pl.pallas_call(
        functools.partial(_expert_blocks_kernel, n_blocks),
        grid_spec=grid_spec,
        out_shape=jax.ShapeDtypeStruct(h_sorted.shape, F32),
        compiler_params=_cparams("arbitrary"),
        name="moe_experts",
    )(plan, h_sorted, wgu, bgu, wd, bd)


def _gather_combine_kernel(x_ref, pos_ref, pos_next_ref, gate_ref, g_ref, y_hbm, o_ref, ybuf, sem):
    t = pl.program_id(0)
    n_tiles = pl.num_programs(0)
    tm = x_ref.shape[0]
    slot = t % 2

    def gather(rows_ref, sl):
        def body(it, carry):
            for u in range(DMA_ROWS_PER_ITER):
                j = it * DMA_ROWS_PER_ITER + u
                for k in range(TOP_K):
                    pltpu.make_async_copy(y_hbm.at[_row_tile(rows_ref[k * tm + j]), :],
                                          ybuf.at[sl, _row_tile(k * tm + j), :], sem.at[sl]).start()
            return carry
        lax.fori_loop(0, tm // DMA_ROWS_PER_ITER, body, 0)

    @pl.when(t == 0)
    def _():
        gather(pos_ref, 0)

    @pl.when(t + 1 < n_tiles)
    def _():
        gather(pos_next_ref, 1 - slot)

    pltpu.make_async_copy(y_hbm.at[pl.ds(0, TOP_K * tm * ROW_TILES), :], ybuf.at[slot], sem.at[slot]).wait()
    gates = gate_ref[...]
    cols = []
    for c in range(ROW_TILES):
        acc = x_ref[:, c * LANES:(c + 1) * LANES]
        for k in range(TOP_K):
            acc = acc + gates[:, k:k + 1] * ybuf[slot, pl.ds(k * tm * ROW_TILES + c, tm, stride=ROW_TILES), :]
        cols.append(acc)
    o_ref[...] = _rms(jnp.concatenate(cols, axis=1), g_ref[...])


def _gather_combine(x, pos, gates, norm_final, y_sorted, tm):
    n, d = x.shape
    assert n % tm == 0 and tm % DMA_ROWS_PER_ITER == 0
    nt = n // tm
    rows = lambda off: pl.BlockSpec((TOP_K * tm,), lambda i: (jnp.minimum(i + off, nt - 1),),
                                    memory_space=pltpu.SMEM)
    return pl.pallas_call(
        _gather_combine_kernel,
        grid=(nt,),
        in_specs=[pl.BlockSpec((tm, d), lambda i: (i, 0)), rows(0), rows(1),
                  pl.BlockSpec((tm, SUBLANES), lambda i: (i, 0)), pl.BlockSpec((1, d), lambda i: (0, 0)),
                  pl.BlockSpec(memory_space=pl.ANY)],
        out_specs=pl.BlockSpec((tm, d), lambda i: (i, 0)),
        out_shape=jax.ShapeDtypeStruct((n, d), F32),
        scratch_shapes=[pltpu.VMEM((2, TOP_K * tm * ROW_TILES, LANES), F32), pltpu.SemaphoreType.DMA((2,))],
        compiler_params=_cparams("arbitrary"),
        name="moe_combine",
    )(x, pos, pos, gates, norm_final, y_sorted)


def _moe_and_final_norm(x_p, x_s, norm_ffn, w_router, b_router, w_gate_up, b_gate_up, w_down, b_down, norm_final):
    n_p, n_s = x_p.shape[0], x_s.shape[0]
    n_rows = (n_p + n_s) * TOP_K
    n_blocks = -(-(n_rows + N_EXPERTS * (BM_EXPERT - 1)) // BM_EXPERT)
    w_router_t = w_router.T.astype(BF16)
    b_r = b_router[:, None]
    g_ffn = norm_ffn[None]
    zero_cnt = jnp.zeros((N_EXPERTS, LANES), I32)
    e_s, r_s, gates_s, cnt_s = _route(x_s, g_ffn, w_router_t, b_r, zero_cnt, n_s)
    e_p, r_p, gates_p, cnt = _route(x_p, g_ffn, w_router_t, b_r, cnt_s, TM_ROUTE)
    plan = _moe_plan(cnt[:, 0], n_blocks)
    pos_p = _positions(plan, e_p, r_p, n_blocks, TM_COMBINE)
    pos_s = _positions(plan, e_s, r_s, n_blocks, n_s)
    h_sorted = _scatter_rows(cnt[:, 0], plan, x_p, pos_p, x_s, pos_s, g_ffn, n_blocks, TM_COMBINE)
    y_sorted = _expert_blocks(plan, h_sorted, w_gate_up.astype(BF16), b_gate_up[:, None, :], w_down.astype(BF16),
                              b_down[:, None, :], n_blocks)
    y_p = _gather_combine(x_p, pos_p, gates_p, norm_final[None], y_sorted, TM_COMBINE)
    y_s = _gather_combine(x_s, pos_s, gates_s, norm_final[None], y_sorted, n_s)
    return y_p, y_s


def _rb(x):
    return x.astype(BF16).astype(F32)


def _sample_proj_kernel(x_ref, g_ref, wm_ref, wg_ref, z_ref):
    hb = _rms(x_ref[...], g_ref[...]).astype(BF16)
    z_ref[:, :MIX_WIDTH] = jnp.dot(hb, wm_ref[...], preferred_element_type=F32)
    z_ref[:, MIX_WIDTH:] = jnp.dot(hb, wg_ref[...], preferred_element_type=F32)


def _sample_proj(x, norm_mix, w_mix, w_gate):
    n, d = x.shape
    args = (x, norm_mix, w_mix, w_gate)
    width = w_mix.shape[1] + w_gate.shape[1]
    return pl.pallas_call(
        _sample_proj_kernel,
        grid=(1,),
        in_specs=[pl.BlockSpec(a.shape, lambda j: (0, 0)) for a in args],
        out_specs=pl.BlockSpec((n, width), lambda j: (0, 0)),
        out_shape=jax.ShapeDtypeStruct((n, width), F32),
        compiler_params=_cparams("arbitrary"),
        name="sample_proj",
    )(*args)


def _sample_mix_kernel(z_ref, c0_ref, c1_ref, c2_ref, mem_ref, sb0_ref, sb1_ref, sb2_ref, nb_ref, lng_ref, lnb_ref,
                       wsp_ref, bsp_ref, bo_ref, co_ref, a_ref, v_ref, n0_ref, n1_ref, n2_ref):
    z = z_ref[0]
    scale = 1.0 / math.sqrt(B_HEAD_DIM)
    row_head = lax.broadcasted_iota(I32, (SUBLANES, B_GROUP_W), 0)
    lane_head = lax.broadcasted_iota(I32, (SUBLANES, B_GROUP_W), 1) // B_HEAD_DIM
    own = (row_head == lane_head).astype(F32)
    kv_cols = jnp.transpose(jnp.broadcast_to(z[:, B_WIDTH:3 * B_WIDTH], (SUBLANES, 2 * B_WIDTH)))[:, 0:1]
    outs, lses = [], []
    for g, (cache_ref, sb_ref, new_ref) in enumerate(((c0_ref, sb0_ref, n0_ref), (c1_ref, sb1_ref, n1_ref),
                                                      (c2_ref, sb2_ref, n2_ref))):
        window = cache_ref.shape[2]
        q, kn, vn = (_rb(z[:, i * B_WIDTH + g * B_GROUP_W:i * B_WIDTH + (g + 1) * B_GROUP_W]) for i in range(3))
        qh = own * q
        kt = cache_ref[0, :B_GROUP_W, :].astype(BF16)
        vt = cache_ref[0, B_GROUP_W:, :].astype(BF16)
        lg = jnp.dot(qh.astype(BF16), kt, preferred_element_type=F32) * scale + sb_ref[...]
        ln = jnp.sum(qh * kn, axis=1, keepdims=True) * scale + nb_ref[g, :, 0:1]
        m = jnp.maximum(jnp.max(lg, axis=1, keepdims=True), ln)
        lse = m + jnp.log(jnp.sum(jnp.exp(lg - m), axis=1, keepdims=True) + jnp.exp(ln - m))
        p = jnp.exp(lg - lse).astype(BF16)
        pn = _rb(jnp.exp(ln - lse))
        pv = lax.dot_general(p, vt, (((1,), (1,)), ((), ())), preferred_element_type=F32) + pn * vn
        outs.append(jnp.sum(own * pv, axis=0, keepdims=True))
        lses.append(jnp.sum(own * lse, axis=0, keepdims=True))
        new_col = jnp.concatenate([kv_cols[i * B_WIDTH + g * B_GROUP_W:i * B_WIDTH + (g + 1) * B_GROUP_W]
                                   for i in range(2)], axis=0)
        lane = lax.broadcasted_iota(I32, (1, window), 1)
        new_ref[0] = jnp.where(lane == window - 1, new_col, pltpu.roll(cache_ref[0], window - 1, axis=1))
    m = jnp.maximum(jnp.maximum(lses[0], lses[1]), lses[2])
    ws = [jnp.exp(l - m) for l in lses]
    den = ws[0] + ws[1] + ws[2]
    bo_ref[0] = _rb(ws[0] / den) * _rb(outs[0]) + _rb(ws[1] / den) * _rb(outs[1]) + _rb(ws[2] / den) * _rb(outs[2])

    cs = []
    for h in range(C_HEADS):
        c0 = MIX_WIDTH - C_WIDTH + h * C_HEAD_DIM
        qh = _rb(z[:, c0:c0 + C_HEAD_DIM])
        kh = _rb(mem_ref[0, pl.ds(h, N_MEM, stride=2 * C_HEADS), :])
        vh = _rb(mem_ref[0, pl.ds(C_HEADS + h, N_MEM, stride=2 * C_HEADS), :])
        s = jnp.sum(kh * qh, axis=-1, keepdims=True) * (1.0 / math.sqrt(C_HEAD_DIM))
        p = jnp.exp(s - jnp.max(s, axis=0, keepdims=True))
        p = _rb(p / jnp.sum(p, axis=0, keepdims=True))
        cs.append(jnp.sum(p * vh, axis=0, keepdims=True))
    co_ref[0] = jnp.concatenate(cs, axis=1)

    u = _gelu(z[:, 3 * B_WIDTH:3 * B_WIDTH + A_WIDTH])
    v = _layernorm(_gelu(z[:, 3 * B_WIDTH + A_WIDTH:3 * B_WIDTH + 2 * A_WIDTH]), lng_ref[...], lnb_ref[...])
    v_ref[0] = v
    a_ref[0] = u * (_rb(wsp_ref[...]) * _rb(v) + bsp_ref[...])


def _sample_mix(z, caches, mem, sbias, nbias, ln_g, ln_b, wsp0, bsp0):
    n = z.shape[0]
    const = lambda arr: pl.BlockSpec(arr.shape, lambda i: (0,) * arr.ndim)
    row = lambda w: pl.BlockSpec((1, 1, w), lambda i: (i, 0, 0))
    per_req = lambda arr: pl.BlockSpec((1,) + arr.shape[1:], lambda i: (i, 0, 0))
    widths = (B_GROUP_W, C_WIDTH, A_WIDTH, A_WIDTH)
    return pl.pallas_call(
        _sample_mix_kernel,
        grid=(n,),
        in_specs=[row(z.shape[2])] + [per_req(c) for c in caches] + [per_req(mem)]
        + [const(a) for a in (*sbias, nbias, ln_g, ln_b, wsp0, bsp0)],
        out_specs=tuple(row(w) for w in widths) + tuple(per_req(c) for c in caches),
        out_shape=tuple(jax.ShapeDtypeStruct((n, 1, w), F32) for w in widths)
        + tuple(jax.ShapeDtypeStruct(c.shape, F32) for c in caches),
        compiler_params=_cparams("arbitrary"),
        name="sample_mix",
    )(z, *caches, mem, *sbias, nbias, ln_g, ln_b, wsp0, bsp0)


def _sample_merge_kernel(x_ref, z_ref, a_ref, b_ref, c_ref, wa_ref, wb_ref, wc_ref, wo_ref, o_ref):
    dot = lambda a, b: jnp.dot(a.astype(BF16), b, preferred_element_type=F32)
    merged = None
    for i, (act_ref, w_ref) in enumerate(((a_ref, wa_ref), (b_ref, wb_ref), (c_ref, wc_ref))):
        gate = jax.nn.sigmoid(z_ref[:, MIX_WIDTH + i * D_MODEL:MIX_WIDTH + (i + 1) * D_MODEL])
        term = gate * dot(act_ref[...], w_ref[...])
        merged = term if merged is None else merged + term
    o_ref[...] = x_ref[...] + dot(merged, wo_ref[...])


def _sample_merge(x, z, a, bo, co, wa, wb, wc, wo):
    args = (x, z, a, bo, co, wa, wb, wc, wo)
    return pl.pallas_call(
        _sample_merge_kernel,
        grid=(1,),
        in_specs=[pl.BlockSpec(arr.shape, lambda i: (0, 0)) for arr in args],
        out_specs=pl.BlockSpec(x.shape, lambda i: (0, 0)),
        out_shape=jax.ShapeDtypeStruct(x.shape, F32),
        compiler_params=_cparams("arbitrary"),
        name="sample_merge",
    )(*args)


def _sample_mixers(x_sample, caches, cache_mem, sbias, nbias, norm_mix, ln_v_g, ln_v_b, w_spatial, b_spatial, wts):
    n = x_sample.shape[0]
    x = x_sample[:, 0]
    z = _sample_proj(x, norm_mix[None], wts["mix"], wts["gate"])
    mem = cache_mem.reshape(n, N_MEM * 2 * C_HEADS, C_HEAD_DIM)
    wsp0 = jnp.repeat(w_spatial[:, 0, 0], LANES)[None]
    bsp0 = jnp.repeat(b_spatial[:, 0], LANES)[None]
    bo, co, a, v, *new_caches = _sample_mix(z[:, None], [_from_window_layout(c) for c in caches], mem, sbias, nbias,
                                            ln_v_g[None], ln_v_b[None], wsp0, bsp0)
    x_new = _sample_merge(x, z, a[:, 0], bo[:, 0], co[:, 0], wts["a"], wts["b"], wts["c"], wts["out"])
    return x_new, [_to_window_layout(c) for c in new_caches], v[:, 0]


def _mixer_weights(w_in, w_branch_a, w_branch_b, w_branch_c, w_out):
    return {"mix": w_in[:, :MIX_WIDTH].astype(BF16), "gate": w_in[:, MIX_WIDTH:].astype(BF16),
            "a": w_branch_a.astype(BF16), "b": w_branch_b.astype(BF16), "c": w_branch_c.astype(BF16),
            "out": w_out.astype(BF16)}


def kernel(x_prompt, x_sample, cache_win0_kv, cache_win1_kv, cache_win2_kv, cache_mem_kv, mem_prompt, rel_bias,
           norm_mix, norm_mem, w_in, ln_v_g, ln_v_b, w_spatial, b_spatial, w_mem_kv, w_branch_a, w_branch_b,
           w_branch_c, w_out, norm_ffn, w_router, b_router, w_gate_up, b_gate_up, w_down, b_down, norm_final):
    assert norm_mix.shape[0] == 1, "one layer"
    b, s, d = x_prompt.shape
    n_s = x_sample.shape[0]
    caches = (cache_win0_kv[0], cache_win1_kv[0], cache_win2_kv[0])
    assert all(c.shape[1] == w for c, (w, _) in zip(caches, B_PAIRS)), "window buffers hold a full window"
    wts = _mixer_weights(w_in[0], w_branch_a[0], w_branch_b[0], w_branch_c[0], w_out[0])
    band_bias, *sbias, nbias = _bias_tables(rel_bias)
    xp, win_p, mem_p = _prompt_mixers(x_prompt, mem_prompt, band_bias, norm_mix[0], norm_mem[0], ln_v_g[0],
                                      ln_v_b[0], w_spatial[0], b_spatial[0], w_mem_kv[0], wts)
    xs, win_s, v_s = _sample_mixers(x_sample, caches, cache_mem_kv[0], sbias, nbias, norm_mix[0], ln_v_g[0],
                                    ln_v_b[0], w_spatial[0], b_spatial[0], wts)
    y_p, y_s = _moe_and_final_norm(xp.reshape(b * s, d), xs, norm_ffn[0], w_router[0], b_router[0], w_gate_up[0],
                                   b_gate_up[0], w_down[0], b_down[0], norm_final)
    chunk_v = v_s.reshape(1, n_s, 1, A_GROUPS, LANES)
    return (y_p.reshape(b, s, d), y_s[:, None], win_p[0], win_p[1], win_p[2], mem_p,
            win_s[0], win_s[1], win_s[2], chunk_v)
```

```python
import functools
import math

import numpy as np
import jax
import jax.numpy as jnp
from jax import lax
from jax.experimental import pallas as pl
from jax.experimental.pallas import tpu as pltpu

F32 = jnp.float32
BF16 = jnp.bfloat16
I32 = jnp.int32

D_MODEL = 1024
N_MEM = 256
CHUNK = 128
A_GROUPS = 4
A_WIDTH = 512
B_PAIRS = ((128, 1), (512, 4), (2048, 16))
B_GROUP_W = 256
B_WIDTH = 768
B_HEAD_DIM = 64
BAND = 128
C_HEADS = 4
C_HEAD_DIM = 128
C_WIDTH = 512
REL_BUCKETS = 32
REL_MAX_DIST = 2048
N_EXPERTS = 32
TOP_K = 4
D_FF = 1024
SWIGLU_LIMIT = 7.0
SWIGLU_ALPHA = 1.702
EPS = 1e-6
NEG_INF = -1e30
MIX_WIDTH = 3 * B_WIDTH + 2 * A_WIDTH + C_WIDTH

LANES = 128
SUBLANES = 8
ROW_TILES = D_MODEL // LANES
VMEM_LIMIT = 56 * 1024 * 1024

TM_PROJ = 512
TM_MERGE = 256
TM_ROUTE = 512
BM_EXPERT = 256
TM_COMBINE = 256


def _cparams(*sem):
    return pltpu.CompilerParams(dimension_semantics=sem, vmem_limit_bytes=VMEM_LIMIT)


def _t5_bucket(dist):
    dist = np.maximum(np.asarray(dist), 0)
    max_exact = REL_BUCKETS // 2
    log_ratio = np.log(np.maximum(dist, max_exact) / max_exact) / math.log(REL_MAX_DIST / max_exact)
    large = np.minimum(max_exact + (log_ratio * (REL_BUCKETS - max_exact)).astype(np.int32), REL_BUCKETS - 1)
    return np.where(dist < max_exact, dist, large).astype(np.int32)


def _gelu(x):
    c = math.sqrt(2.0 / math.pi)
    return x * (0.5 * (1.0 + jnp.tanh(c * (x + 0.044715 * (x * x * x)))))


def _rms(x, g):
    return x * lax.rsqrt(jnp.mean(x * x, axis=-1, keepdims=True) + EPS) * g


def _layernorm(x, g, b):
    xc = x - jnp.mean(x, axis=-1, keepdims=True)
    return xc * lax.rsqrt(jnp.mean(xc * xc, axis=-1, keepdims=True) + EPS) * g + b


def _proj_kernel(x_ref, g_ref, w_ref, lng_ref, lnb_ref, wsp_ref, bsp_ref,
                 qkv0_ref, qkv1_ref, qkv2_ref, win0_ref, win1_ref, win2_ref, a_ref, qc_ref, scr_ref):
    t = pl.program_id(1)
    last = pl.num_programs(1) - 1
    tm = x_ref.shape[1]
    hb = _rms(x_ref[0], g_ref[...]).astype(BF16)

    def proj(c0, width):
        return jnp.dot(hb, w_ref[:, c0:c0 + width], preferred_element_type=F32)

    for g, (_, dil) in enumerate(B_PAIRS):
        q, k, v = (proj(i * B_WIDTH + g * B_GROUP_W, B_GROUP_W) for i in range(3))
        kv = jnp.concatenate([k, v], axis=1)
        if g == 2:
            win2_ref[0] = jnp.transpose(kv)
        elif g == 1:
            @pl.when(t == last)
            def _():
                win1_ref[0] = jnp.transpose(kv)
        else:
            @pl.when(t == last)
            def _():
                win0_ref[0] = jnp.transpose(kv[tm - B_PAIRS[0][0]:])
        if dil == 1:
            qkv0_ref[0] = jnp.concatenate([q, k, v], axis=1).astype(BF16)
            continue
        out_ref = qkv1_ref if g == 1 else qkv2_ref
        for i, arr in enumerate((q, k, v)):
            for ct in range(2):
                scr_ref[2 * i + ct] = arr[:, ct * LANES:(ct + 1) * LANES]
        for r in range(dil):
            rows = [scr_ref[j, pl.ds(r, tm // dil, stride=dil), :] for j in range(6)]
            out_ref[0, r] = jnp.concatenate(rows, axis=1).astype(BF16)

    u = _gelu(proj(3 * B_WIDTH, A_WIDTH))
    v = _layernorm(_gelu(proj(3 * B_WIDTH + A_WIDTH, A_WIDTH)), lng_ref[...], lnb_ref[...]).astype(BF16)
    for c in range(tm // CHUNK):
        rs = slice(c * CHUNK, (c + 1) * CHUNK)
        for g in range(A_GROUPS):
            cs = slice(g * LANES, (g + 1) * LANES)
            s = jnp.dot(wsp_ref[g], v[rs, cs], preferred_element_type=F32) + bsp_ref[:, g:g + 1]
            a_ref[0, rs, cs] = (u[rs, cs] * s).astype(BF16)

    qc_ref[0] = proj(3 * B_WIDTH + 2 * A_WIDTH, C_WIDTH).astype(BF16)


def _prompt_proj(x, norm_mix, w_mix, ln_g, ln_b, wsp, bsp):
    b, s, d = x.shape
    tm = TM_PROJ
    assert s % tm == 0 and s >= B_PAIRS[2][0] and tm == B_PAIRS[1][0]
    nt = s // tm
    const = lambda *shape: pl.BlockSpec(shape, lambda i, j: (0,) * len(shape))
    out_shape = (
        jax.ShapeDtypeStruct((b, s, 3 * B_GROUP_W), BF16),
        jax.ShapeDtypeStruct((b, 4, s // 4, 3 * B_GROUP_W), BF16),
        jax.ShapeDtypeStruct((b, 16, s // 16, 3 * B_GROUP_W), BF16),
        jax.ShapeDtypeStruct((b, 2 * B_GROUP_W, B_PAIRS[0][0]), F32),
        jax.ShapeDtypeStruct((b, 2 * B_GROUP_W, B_PAIRS[1][0]), F32),
        jax.ShapeDtypeStruct((b, 2 * B_GROUP_W, s), F32),
        jax.ShapeDtypeStruct((b, s, A_WIDTH), BF16),
        jax.ShapeDtypeStruct((b, s, C_WIDTH), BF16),
    )
    out_specs = (
        pl.BlockSpec((1, tm, 3 * B_GROUP_W), lambda i, j: (i, j, 0)),
        pl.BlockSpec((1, 4, tm // 4, 3 * B_GROUP_W), lambda i, j: (i, 0, j, 0)),
        pl.BlockSpec((1, 16, tm // 16, 3 * B_GROUP_W), lambda i, j: (i, 0, j, 0)),
        pl.BlockSpec((1, 2 * B_GROUP_W, B_PAIRS[0][0]), lambda i, j: (i, 0, 0)),
        pl.BlockSpec((1, 2 * B_GROUP_W, B_PAIRS[1][0]), lambda i, j: (i, 0, 0)),
        pl.BlockSpec((1, 2 * B_GROUP_W, tm), lambda i, j: (i, 0, j)),
        pl.BlockSpec((1, tm, A_WIDTH), lambda i, j: (i, j, 0)),
        pl.BlockSpec((1, tm, C_WIDTH), lambda i, j: (i, j, 0)),
    )
    return pl.pallas_call(
        _proj_kernel,
        grid=(b, nt),
        in_specs=[pl.BlockSpec((1, tm, d), lambda i, j: (i, j, 0)), const(1, d), const(d, MIX_WIDTH),
                  const(1, A_WIDTH), const(1, A_WIDTH), const(A_GROUPS, CHUNK, CHUNK), const(CHUNK, A_GROUPS)],
        out_specs=out_specs,
        out_shape=out_shape,
        scratch_shapes=[pltpu.VMEM((6, tm, LANES), F32)],
        compiler_params=_cparams("arbitrary", "arbitrary"),
        name="prompt_proj",
    )(x, norm_mix, w_mix, ln_g, ln_b, wsp, bsp)


def _attend(q, k, v, bias_ref, g, k0, hmask):
    kn = k.shape[0]
    qs = jnp.concatenate([q * hm for hm in hmask], axis=0)
    s = lax.dot_general(qs, k, (((1,), (1,)), ((), ())), preferred_element_type=F32)
    s = s * (1.0 / math.sqrt(B_HEAD_DIM)) + bias_ref[g, :, k0:k0 + kn]
    m = jnp.max(s, axis=-1, keepdims=True)
    p = jnp.exp(s - m)
    l = jnp.sum(p, axis=-1, keepdims=True)
    pv = jnp.dot(p.astype(BF16), v, preferred_element_type=F32) / l
    lse_rows = m + jnp.log(l)
    out = jnp.zeros((BAND, B_GROUP_W), F32)
    lse = jnp.zeros((BAND, B_GROUP_W), F32)
    for h, hm in enumerate(hmask):
        rows = slice(h * BAND, (h + 1) * BAND)
        sel = hm > 0
        out = jnp.where(sel, pv[rows], out)
        lse = jnp.where(sel, lse_rows[rows], lse)
    return out, lse


def _attn_kernel(q0_ref, q1_ref, q2_ref, bias_ref, o_ref, out_ref, lse_ref):
    s = q0_ref.shape[1]
    lane_head = lax.broadcasted_iota(I32, (1, B_GROUP_W), 1) // B_HEAD_DIM
    hmask = [(lane_head == h).astype(BF16) for h in range(4)]
    qs, ks, vs = (slice(i * B_GROUP_W, (i + 1) * B_GROUP_W) for i in range(3))

    def store(g, start, dil, out, lse):
        rows = pl.ds(start, BAND) if dil == 1 else pl.ds(start, BAND, stride=dil)
        for ct in range(2):
            out_ref[g, ct, rows, :] = out[:, ct * LANES:(ct + 1) * LANES]
            lse_ref[g, ct, rows, :] = lse[:, ct * LANES:(ct + 1) * LANES]

    for g, (_, dil) in enumerate(B_PAIRS):
        n = s // dil
        nb = n // BAND

        def load(rows, cols, r, g=g):
            if g == 0:
                return q0_ref[0, rows, cols]
            return (q1_ref if g == 1 else q2_ref)[0, r, rows, cols]

        def first_block(r, carry, g=g, dil=dil, load=load):
            rows = pl.ds(0, BAND)
            out, lse = _attend(load(rows, qs, r), load(rows, ks, r), load(rows, vs, r), bias_ref, g, BAND, hmask)
            store(g, r, dil, out, lse)
            return carry

        def later_block(i, carry, g=g, dil=dil, nb=nb, load=load):
            r = i // (nb - 1)
            qb = i % (nb - 1) + 1
            q0 = pl.multiple_of(qb * BAND, BAND)
            rows_q = pl.ds(q0, BAND)
            rows_k = pl.ds(q0 - BAND, 2 * BAND)
            out, lse = _attend(load(rows_q, qs, r), load(rows_k, ks, r), load(rows_k, vs, r), bias_ref, g, 0, hmask)
            store(g, qb * (BAND * dil) + r, dil, out, lse)
            return carry

        lax.fori_loop(0, dil, first_block, 0)
        if nb > 1:
            lax.fori_loop(0, dil * (nb - 1), later_block, 0)

    rc = 256
    def combine(c, carry):
        rows = pl.ds(pl.multiple_of(c * rc, rc), rc)
        for ct in range(2):
            ls = [lse_ref[g, ct, rows, :] for g in range(3)]
            m = jnp.maximum(jnp.maximum(ls[0], ls[1]), ls[2])
            ws = [jnp.exp(l - m) for l in ls]
            den = ws[0] + ws[1] + ws[2]
            num = ws[0] * out_ref[0, ct, rows, :] + ws[1] * out_ref[1, ct, rows, :] + ws[2] * out_ref[2, ct, rows, :]
            o_ref[0, rows, ct * LANES:(ct + 1) * LANES] = (num / den).astype(BF16)
        return carry
    lax.fori_loop(0, s // rc, combine, 0)


def _prompt_attention(qkv0, qkv1, qkv2, bias):
    b, s, w = qkv0.shape
    return pl.pallas_call(
        _attn_kernel,
        grid=(b,),
        in_specs=[pl.BlockSpec((1, s, w), lambda i: (i, 0, 0)),
                  pl.BlockSpec((1, 4, s // 4, w), lambda i: (i, 0, 0, 0)),
                  pl.BlockSpec((1, 16, s // 16, w), lambda i: (i, 0, 0, 0)),
                  pl.BlockSpec(bias.shape, lambda i: (0, 0, 0))],
        out_specs=pl.BlockSpec((1, s, B_GROUP_W), lambda i: (i, 0, 0)),
        out_shape=jax.ShapeDtypeStruct((b, s, B_GROUP_W), BF16),
        scratch_shapes=[pltpu.VMEM((3, 2, s, LANES), F32), pltpu.VMEM((3, 2, s, LANES), F32)],
        compiler_params=_cparams("arbitrary"),
        name="prompt_attention",
    )(qkv0, qkv1, qkv2, bias)


def _bias_kernel(rel_ref, band_idx_ref, s0_ref, s1_ref, s2_ref, band_ref, sb0_ref, sb1_ref, sb2_ref, nb_ref):
    def lookup(idx, col):
        acc = jnp.full(idx.shape, NEG_INF, F32)
        for bucket in range(REL_BUCKETS):
            acc = jnp.where(idx == bucket, rel_ref[bucket, col], acc)
        return acc

    for g, (s_ref, sb_ref) in enumerate(((s0_ref, sb0_ref), (s1_ref, sb1_ref), (s2_ref, sb2_ref))):
        width = s_ref.shape[1]
        for h in range(4):
            band_ref[g, h * BAND:(h + 1) * BAND, :] = lookup(band_idx_ref[g], 4 * g + h)
        sb_ref[...] = jnp.concatenate([lookup(s_ref[...], 4 * g + h) for h in range(4)]
                                      + [jnp.full((SUBLANES - 4, width), NEG_INF, F32)], axis=0)
        nb_ref[g] = jnp.concatenate([jnp.full((1, LANES), rel_ref[0, 4 * g + h], F32) for h in range(4)]
                                    + [jnp.zeros((SUBLANES - 4, LANES), F32)], axis=0)


def _bias_tables(rel_bias):
    steps = np.arange(BAND)[:, None] + BAND - np.arange(2 * BAND)[None, :]
    valid = (steps >= 0) & (steps <= BAND)
    band_idx = np.stack([np.where(valid, _t5_bucket(np.clip(steps, 0, BAND) * dil), -1) for _, dil in B_PAIRS])
    s_idx = []
    for window, dil in B_PAIRS:
        w = np.arange(window)
        s_idx.append(np.where(w % dil == 0, _t5_bucket(window - w), -1)[None].astype(np.int32))
    args = (rel_bias, jnp.asarray(band_idx.astype(np.int32))) + tuple(jnp.asarray(s) for s in s_idx)
    vmem = lambda a: pl.BlockSpec(a.shape, lambda i: (0,) * a.ndim)
    out_shape = (jax.ShapeDtypeStruct((3, 4 * BAND, 2 * BAND), F32),) + tuple(
        jax.ShapeDtypeStruct((SUBLANES, window), F32) for window, _ in B_PAIRS) + (
        jax.ShapeDtypeStruct((3, SUBLANES, LANES), F32),)
    return pl.pallas_call(
        _bias_kernel,
        grid=(1,),
        in_specs=[pl.BlockSpec(memory_space=pltpu.SMEM)] + [vmem(a) for a in args[1:]],
        out_specs=tuple(pl.BlockSpec(s.shape, lambda i, n=len(s.shape): (0,) * n) for s in out_shape),
        out_shape=out_shape,
        compiler_params=_cparams("arbitrary"),
        name="bias_tables",
    )(*args)


def _memkv_kernel(mem_ref, g_ref, w_ref, kv_ref, kvb_ref):
    hb = _rms(mem_ref[0], g_ref[...]).astype(BF16)
    kv = jnp.dot(hb, w_ref[...], preferred_element_type=F32)
    kv_ref[0] = kv
    kvb_ref[0] = kv.astype(BF16)


def _memory_kv(mem, norm_mem, w_mem):
    b, m, d = mem.shape
    w = w_mem.shape[1]
    return pl.pallas_call(
        _memkv_kernel,
        grid=(b,),
        in_specs=[pl.BlockSpec((1, m, d), lambda i: (i, 0, 0)), pl.BlockSpec((1, d), lambda i: (0, 0)),
                  pl.BlockSpec((d, w), lambda i: (0, 0))],
        out_specs=(pl.BlockSpec((1, m, w), lambda i: (i, 0, 0)), pl.BlockSpec((1, m, w), lambda i: (i, 0, 0))),
        out_shape=(jax.ShapeDtypeStruct((b, m, w), F32), jax.ShapeDtypeStruct((b, m, w), BF16)),
        compiler_params=_cparams("arbitrary"),
        name="memory_kv",
    )(mem, norm_mem, w_mem)


def _merge_kernel(x_ref, g_ref, a_ref, b_ref, qc_ref, kv_ref, wg_ref, wa_ref, wb_ref, wc_ref, wo_ref, o_ref):
    x = x_ref[0]
    hb = _rms(x, g_ref[...]).astype(BF16)
    qc = qc_ref[0]
    cs = []
    for h in range(C_HEADS):
        hs = slice(h * C_HEAD_DIM, (h + 1) * C_HEAD_DIM)
        s = lax.dot_general(qc[:, hs], kv_ref[0, :, hs], (((1,), (1,)), ((), ())), preferred_element_type=F32)
        s = s * (1.0 / math.sqrt(C_HEAD_DIM))
        p = jnp.exp(s - jnp.max(s, axis=-1, keepdims=True))
        p = p / jnp.sum(p, axis=-1, keepdims=True)
        vs = slice(C_WIDTH + h * C_HEAD_DIM, C_WIDTH + (h + 1) * C_HEAD_DIM)
        cs.append(jnp.dot(p.astype(BF16), kv_ref[0, :, vs], preferred_element_type=F32))
    c = jnp.concatenate(cs, axis=1).astype(BF16)
    branches = ((a_ref[0], wa_ref), (b_ref[0], wb_ref), (c, wc_ref))
    merged = None
    for i, (act, w_ref) in enumerate(branches):
        gate = jax.nn.sigmoid(jnp.dot(hb, wg_ref[:, i * D_MODEL:(i + 1) * D_MODEL], preferred_element_type=F32))
        term = gate * jnp.dot(act, w_ref[...], preferred_element_type=F32)
        merged = term if merged is None else merged + term
    o_ref[0] = x + jnp.dot(merged.astype(BF16), wo_ref[...], preferred_element_type=F32)


def _prompt_merge(x, norm_mix, a, bo, qc, kvb, wg, wa, wb, wc, wo):
    b, s, d = x.shape
    tm = TM_MERGE
    tile = lambda w: pl.BlockSpec((1, tm, w), lambda i, j: (i, j, 0))
    const = lambda arr: pl.BlockSpec(arr.shape, lambda i, j: (0,) * arr.ndim)
    return pl.pallas_call(
        _merge_kernel,
        grid=(b, s // tm),
        in_specs=[tile(d), const(norm_mix), tile(A_WIDTH), tile(B_GROUP_W), tile(C_WIDTH),
                  pl.BlockSpec((1,) + kvb.shape[1:], lambda i, j: (i, 0, 0)),
                  const(wg), const(wa), const(wb), const(wc), const(wo)],
        out_specs=tile(d),
        out_shape=jax.ShapeDtypeStruct((b, s, d), F32),
        compiler_params=_cparams("arbitrary", "arbitrary"),
        name="prompt_merge",
    )(x, norm_mix, a, bo, qc, kvb, wg, wa, wb, wc, wo)


def _to_window_layout(w):
    b, _, window = w.shape
    return jnp.transpose(w.reshape(b, 2, 4, B_HEAD_DIM, window), (0, 4, 1, 2, 3))[None]


def _from_window_layout(c):
    b, window = c.shape[:2]
    return jnp.transpose(c, (0, 2, 3, 4, 1)).reshape(b, 2 * B_GROUP_W, window)


def _prompt_mixers(x_prompt, mem_prompt, band_bias, norm_mix, norm_mem, ln_v_g, ln_v_b, w_spatial, b_spatial,
                   w_mem_kv, wts):
    b, s, _ = x_prompt.shape
    wsp = jnp.where(np.tril(np.ones((CHUNK, CHUNK), bool)), w_spatial, 0.0).astype(BF16)
    qkv0, qkv1, qkv2, win0, win1, win2, a_out, qc = _prompt_proj(
        x_prompt, norm_mix[None], wts["mix"], ln_v_g[None], ln_v_b[None], wsp, b_spatial.T)
    b_out = _prompt_attention(qkv0, qkv1, qkv2, band_bias)
    mem_kv, mem_kv_b = _memory_kv(mem_prompt, norm_mem[None], w_mem_kv.astype(BF16))
    x_new = _prompt_merge(x_prompt, norm_mix[None], a_out, b_out, qc, mem_kv_b, wts["gate"],
                          wts["a"], wts["b"], wts["c"], wts["out"])
    wins = [_to_window_layout(w) for w in (win0, win1, win2)]
    return x_new, wins, mem_kv.reshape(1, b, N_MEM, 2, C_HEADS, C_HEAD_DIM)


def _route_kernel(x_ref, g_ref, wr_ref, br_ref, cnt_in_ref, e_ref, rank_ref, gate_ref, cnt_ref, carry_ref):
    tm = x_ref.shape[0]

    @pl.when(pl.program_id(0) == 0)
    def _():
        carry_ref[...] = cnt_in_ref[...].astype(F32)

    h = _rms(x_ref[...], g_ref[...])
    logits = lax.dot_general(wr_ref[...], h.astype(BF16), (((1,), (1,)), ((), ())),
                             preferred_element_type=F32) + br_ref[...]
    expert = lax.broadcasted_iota(I32, logits.shape, 0)
    vals, idxs = [], []
    member = jnp.zeros(logits.shape, F32)
    for _ in range(TOP_K):
        m = jnp.max(logits, axis=0, keepdims=True)
        idx = jnp.min(jnp.where(logits == m, expert, N_EXPERTS), axis=0, keepdims=True)
        hit = expert == idx
        vals.append(m)
        idxs.append(idx)
        member = jnp.where(hit, 1.0, member)
        logits = jnp.where(hit, -jnp.inf, logits)
    p = [jnp.exp(v - vals[0]) for v in vals]
    den = p[0] + p[1] + p[2] + p[3]
    gates = jnp.concatenate([pk / den for pk in p] + [jnp.zeros((SUBLANES - TOP_K, tm), F32)], axis=0)
    gate_ref[...] = jnp.transpose(gates)
    e_ref[...] = jnp.concatenate(idxs, axis=0)
    before = (lax.broadcasted_iota(I32, (tm, tm), 0) < lax.broadcasted_iota(I32, (tm, tm), 1)).astype(BF16)
    prefix = jnp.dot(member.astype(BF16), before, preferred_element_type=F32) + carry_ref[:, 0:1]
    rank_ref[...] = jnp.concatenate(
        [jnp.sum(jnp.where(expert == idx, prefix, 0.0), axis=0, keepdims=True) for idx in idxs], axis=0).astype(I32)
    carry_ref[...] = carry_ref[...] + jnp.sum(member, axis=1, keepdims=True)
    cnt_ref[...] = carry_ref[...].astype(I32)


def _route(x, norm_ffn, w_router_t, b_router, cnt_in, tm):
    n, d = x.shape
    assert n % tm == 0
    const = lambda arr: pl.BlockSpec(arr.shape, lambda i: (0,) * arr.ndim)
    return pl.pallas_call(
        _route_kernel,
        grid=(n // tm,),
        in_specs=[pl.BlockSpec((tm, d), lambda i: (i, 0)), const(norm_ffn), const(w_router_t), const(b_router),
                  const(cnt_in)],
        out_specs=(pl.BlockSpec((TOP_K, tm), lambda i: (0, i)), pl.BlockSpec((TOP_K, tm), lambda i: (0, i)),
                   pl.BlockSpec((tm, SUBLANES), lambda i: (i, 0)), pl.BlockSpec((N_EXPERTS, LANES), lambda i: (0, 0))),
        out_shape=(jax.ShapeDtypeStruct((TOP_K, n), I32), jax.ShapeDtypeStruct((TOP_K, n), I32),
                   jax.ShapeDtypeStruct((n, SUBLANES), F32), jax.ShapeDtypeStruct((N_EXPERTS, LANES), I32)),
        scratch_shapes=[pltpu.VMEM((N_EXPERTS, LANES), F32)],
        compiler_params=_cparams("arbitrary"),
        name="moe_route",
    )(x, norm_ffn, w_router_t, b_router, cnt_in)


DMA_ROWS_PER_ITER = 2
N_DMA_THREADS = 2
N_PLAN_TAIL = 1 + N_EXPERTS


def _row_tile(r):
    return pl.ds(pl.multiple_of(r * ROW_TILES, ROW_TILES), ROW_TILES)


def _plan_kernel(n_blocks, cnt_ref, plan_ref):
    shift = BM_EXPERT.bit_length() - 1

    def per_expert(e, carry):
        blk0, last_e = carry
        nb = (cnt_ref[e] + (BM_EXPERT - 1)) >> shift
        plan_ref[n_blocks + 1 + e] = blk0 << shift

        def fill(j, cc):
            plan_ref[blk0 + j] = e
            return cc
        lax.fori_loop(0, nb, fill, 0)
        return blk0 + nb, jnp.where(nb > 0, e, last_e)
    n_used, last_e = lax.fori_loop(0, N_EXPERTS, per_expert, (0, 0))

    def tail(j, cc):
        plan_ref[j] = last_e
        return cc
    lax.fori_loop(n_used, n_blocks, tail, 0)
    plan_ref[n_blocks] = n_used


def _moe_plan(cnt, n_blocks):
    smem = pl.BlockSpec(memory_space=pltpu.SMEM)
    return pl.pallas_call(
        functools.partial(_plan_kernel, n_blocks),
        in_specs=[smem], out_specs=smem,
        out_shape=jax.ShapeDtypeStruct((n_blocks + N_PLAN_TAIL,), I32),
        name="moe_plan",
    )(cnt)


POSITION_TILES_PER_STEP = 16


def _positions_kernel(n_blocks, plan_ref, e_ref, r_ref, pos_ref):
    tiles, _, tm = pos_ref.shape
    e = e_ref[...]
    first = jnp.zeros(e.shape, I32)
    for ex in range(N_EXPERTS):
        first = jnp.where(e == ex, plan_ref[n_blocks + 1 + ex], first)
    pos = first + r_ref[...]
    for i in range(tiles):
        pos_ref[i] = pos[:, i * tm:(i + 1) * tm]


def _positions(plan, e, r, n_blocks, tm):
    n = e.shape[1]
    assert n % tm == 0
    nt = n // tm
    per_step = math.gcd(nt, POSITION_TILES_PER_STEP)
    pos = pl.pallas_call(
        functools.partial(_positions_kernel, n_blocks),
        grid=(nt // per_step,),
        in_specs=[pl.BlockSpec(memory_space=pltpu.SMEM), pl.BlockSpec((TOP_K, per_step * tm), lambda i: (0, i)),
                  pl.BlockSpec((TOP_K, per_step * tm), lambda i: (0, i))],
        out_specs=pl.BlockSpec((per_step, TOP_K, tm), lambda i: (i, 0, 0)),
        out_shape=jax.ShapeDtypeStruct((nt, TOP_K, tm), I32),
        compiler_params=_cparams("arbitrary"),
        name="moe_positions",
    )(plan, e, r)
    return pos.reshape(-1)


def _scatter_rows_kernel(n_blocks, cnt_ref, plan_ref, x_ref, pos_ref, xs_ref, poss_ref, g_ref, h_hbm,
                         hbuf, zbuf, sem, zsem):
    t = pl.program_id(0)
    n_tiles = pl.num_programs(0) - 1
    tm = x_ref.shape[0]
    n_s = xs_ref.shape[0]
    bm = BM_EXPERT
    shift = bm.bit_length() - 1
    slot = t % 2

    def zero_rows(first, count, wait):
        for bit in range(shift):
            n = 1 << bit
            lo = first + (count & (n - 1))

            @pl.when((count >> bit) & 1 == 1)
            def _():
                cp = pltpu.make_async_copy(zbuf.at[pl.ds(0, n * ROW_TILES), :], h_hbm.at[_row_tile_n(lo, n), :],
                                           zsem.at[0])
                cp.wait() if wait else cp.start()

    def zero_block(blk, wait):
        cp = pltpu.make_async_copy(zbuf, h_hbm.at[_row_tile_n(blk * bm, bm), :], zsem.at[0])
        cp.wait() if wait else cp.start()

    @pl.when(t == 0)
    def _():
        zbuf[...] = jnp.zeros(zbuf.shape, F32)
        for wait in (False, True):
            def pad_expert(e, cc, wait=wait):
                cnt = cnt_ref[e]
                zero_rows(plan_ref[n_blocks + 1 + e] + cnt, (-cnt) & (bm - 1), wait)
                return cc
            lax.fori_loop(0, N_EXPERTS, pad_expert, 0)

            def pad_block(blk, cc, wait=wait):
                zero_block(blk, wait)
                return cc
            lax.fori_loop(plan_ref[n_blocks], n_blocks, pad_block, 0)

    def wait_copies(sl, n_tok):
        for _ in range(TOP_K):
            pltpu.make_async_copy(hbuf.at[sl, pl.ds(0, n_tok * ROW_TILES), :],
                                  h_hbm.at[pl.ds(0, n_tok * ROW_TILES), :], sem.at[sl]).wait()

    def copy_rows(src_ref, rows_ref, n_tok):
        h = _rms(src_ref[...], g_ref[...])
        for c in range(ROW_TILES):
            hbuf[slot, pl.ds(c, n_tok, stride=ROW_TILES), :] = h[:, c * LANES:(c + 1) * LANES]

        def body(it, carry):
            for u in range(DMA_ROWS_PER_ITER):
                j = it * DMA_ROWS_PER_ITER + u
                for k in range(TOP_K):
                    pltpu.make_async_copy(hbuf.at[slot, _row_tile(j), :],
                                          h_hbm.at[_row_tile(rows_ref[k * n_tok + j]), :],
                                          sem.at[slot]).start(priority=k % N_DMA_THREADS)
            return carry
        lax.fori_loop(0, n_tok // DMA_ROWS_PER_ITER, body, 0)

    @pl.when(t >= 2)
    def _():
        wait_copies(slot, tm)

    @pl.when(t < n_tiles)
    def _():
        copy_rows(x_ref, pos_ref, tm)

    @pl.when(t == n_tiles)
    def _():
        copy_rows(xs_ref, poss_ref, n_s)
        wait_copies(slot, n_s)

        @pl.when(n_tiles >= 1)
        def _():
            wait_copies(1 - slot, tm)


def _row_tile_n(r, n):
    return pl.ds(pl.multiple_of(r * ROW_TILES, ROW_TILES), n * ROW_TILES)


def _scatter_rows(cnt, plan, x_p, pos_p, x_s, pos_s, norm_ffn, n_blocks, tm):
    n_p, d = x_p.shape
    n_s = x_s.shape[0]
    assert n_p % tm == 0 and tm % DMA_ROWS_PER_ITER == 0 and n_s % DMA_ROWS_PER_ITER == 0 and n_s <= tm
    nt = n_p // tm
    smem = lambda: pl.BlockSpec(memory_space=pltpu.SMEM)
    tile_idx = lambda i: jnp.minimum(i, nt - 1)
    return pl.pallas_call(
        functools.partial(_scatter_rows_kernel, n_blocks),
        grid=(nt + 1,),
        in_specs=[smem(), smem(), pl.BlockSpec((tm, d), lambda i: (tile_idx(i), 0)),
                  pl.BlockSpec((TOP_K * tm,), lambda i: (tile_idx(i),), memory_space=pltpu.SMEM),
                  pl.BlockSpec((n_s, d), lambda i: (0, 0)), smem(), pl.BlockSpec((1, d), lambda i: (0, 0))],
        out_specs=pl.BlockSpec(memory_space=pl.ANY),
        out_shape=jax.ShapeDtypeStruct((n_blocks * BM_EXPERT * ROW_TILES, LANES), F32),
        scratch_shapes=[pltpu.VMEM((2, tm * ROW_TILES, LANES), F32), pltpu.VMEM((BM_EXPERT * ROW_TILES, LANES), F32),
                        pltpu.SemaphoreType.DMA((2,)), pltpu.SemaphoreType.DMA((1,))],
        compiler_params=_cparams("arbitrary"),
        name="moe_scatter_rows",
    )(cnt, plan, x_p, pos_p, x_s, pos_s, norm_ffn)


WEIGHT_CAST_ROWS = 128


def _expert_blocks_kernel(n_blocks, plan_ref, x_ref, wgu_ref, bgu_ref, wd_ref, bd_ref, y_ref, wgu_b, wd_b):
    bm = BM_EXPERT
    i = pl.program_id(0)

    @pl.when((i == 0) | (plan_ref[i] != plan_ref[jnp.maximum(i - 1, 0)]))
    def _():
        for src, dst in ((wgu_ref, wgu_b), (wd_ref, wd_b)):
            for r0 in range(0, src.shape[1], WEIGHT_CAST_ROWS):
                rows = slice(r0, r0 + WEIGHT_CAST_ROWS)
                dst[rows, :] = src[0, rows, :].astype(BF16)

    @pl.when(i < plan_ref[n_blocks])
    def _():
        x = jnp.concatenate([x_ref[pl.ds(c, bm, stride=ROW_TILES), :] for c in range(ROW_TILES)], axis=1)
        gu = jnp.dot(x.astype(BF16), wgu_b[...], preferred_element_type=F32) + bgu_ref[0]
        gate = jnp.minimum(gu[:, :D_FF], SWIGLU_LIMIT)
        up = jnp.clip(gu[:, D_FF:], -SWIGLU_LIMIT, SWIGLU_LIMIT)
        act = gate * jax.nn.sigmoid(SWIGLU_ALPHA * gate) * (up + 1.0)
        y = jnp.dot(act.astype(BF16), wd_b[...], preferred_element_type=F32) + bd_ref[0]
        for c in range(ROW_TILES):
            y_ref[pl.ds(c, bm, stride=ROW_TILES), :] = y[:, c * LANES:(c + 1) * LANES]

    @pl.when(i >= plan_ref[n_blocks])
    def _():
        y_ref[...] = jnp.zeros(y_ref.shape, F32)


def _expert_blocks(plan, h_sorted, wgu, bgu, wd, bd, n_blocks):
    bm = BM_EXPERT
    by_expert = lambda *shape: pl.BlockSpec((1,) + shape, lambda i, plan: (plan[i],) + (0,) * len(shape))
    used = lambda i, plan: (jnp.minimum(i, plan[n_blocks] - 1), 0)
    grid_spec = pltpu.PrefetchScalarGridSpec(
        num_scalar_prefetch=1,
        grid=(n_blocks,),
        in_specs=[pl.BlockSpec((bm * ROW_TILES, LANES), used), by_expert(D_MODEL, 2 * D_FF), by_expert(1, 2 * D_FF),
                  by_expert(D_FF, D_MODEL), by_expert(1, D_MODEL)],
        out_specs=pl.BlockSpec((bm * ROW_TILES, LANES), lambda i, plan: (i, 0)),
        scratch_shapes=[pltpu.VMEM(wgu.shape[1:], BF16), pltpu.VMEM(wd.shape[1:], BF16)],
    )
    return pl.pallas_call(
        functools.partial(_expert_blocks_kernel, n_blocks),
        grid_spec=grid_spec,
        out_shape=jax.ShapeDtypeStruct(h_sorted.shape, F32),
        compiler_params=_cparams("arbitrary"),
        name="moe_experts",
    )(plan, h_sorted, wgu, bgu, wd, bd)


def _gather_combine_kernel(x_ref, pos_ref, pos_next_ref, gate_ref, g_ref, y_hbm, o_ref, ybuf, sem):
    t = pl.program_id(0)
    n_tiles = pl.num_programs(0)
    tm = x_ref.shape[0]
    slot = t % 2

    def gather(rows_ref, sl):
        def body(it, carry):
            for u in range(DMA_ROWS_PER_ITER):
                j = it * DMA_ROWS_PER_ITER + u
                for k in range(TOP_K):
                    pltpu.make_async_copy(y_hbm.at[_row_tile(rows_ref[k * tm + j]), :],
                                          ybuf.at[sl, _row_tile(k * tm + j), :],
                                          sem.at[sl]).start(priority=k % N_DMA_THREADS)
            return carry
        lax.fori_loop(0, tm // DMA_ROWS_PER_ITER, body, 0)

    @pl.when(t == 0)
    def _():
        gather(pos_ref, 0)

    @pl.when(t + 1 < n_tiles)
    def _():
        gather(pos_next_ref, 1 - slot)

    pltpu.make_async_copy(y_hbm.at[pl.ds(0, TOP_K * tm * ROW_TILES), :], ybuf.at[slot], sem.at[slot]).wait()
    gates = gate_ref[...]
    cols = []
    for c in range(ROW_TILES):
        acc = x_ref[:, c * LANES:(c + 1) * LANES]
        for k in range(TOP_K):
            acc = acc + gates[:, k:k + 1] * ybuf[slot, pl.ds(k * tm * ROW_TILES + c, tm, stride=ROW_TILES), :]
        cols.append(acc)
    o_ref[...] = _rms(jnp.concatenate(cols, axis=1), g_ref[...])


def _gather_combine(x, pos, gates, norm_final, y_sorted, tm):
    n, d = x.shape
    assert n % tm == 0 and tm % DMA_ROWS_PER_ITER == 0
    nt = n // tm
    rows = lambda off: pl.BlockSpec((TOP_K * tm,), lambda i: (jnp.minimum(i + off, nt - 1),),
                                    memory_space=pltpu.SMEM)
    return pl.pallas_call(
        _gather_combine_kernel,
        grid=(nt,),
        in_specs=[pl.BlockSpec((tm, d), lambda i: (i, 0)), rows(0), rows(1),
                  pl.BlockSpec((tm, SUBLANES), lambda i: (i, 0)), pl.BlockSpec((1, d), lambda i: (0, 0)),
                  pl.BlockSpec(memory_space=pl.ANY)],
        out_specs=pl.BlockSpec((tm, d), lambda i: (i, 0)),
        out_shape=jax.ShapeDtypeStruct((n, d), F32),
        scratch_shapes=[pltpu.VMEM((2, TOP_K * tm * ROW_TILES, LANES), F32), pltpu.SemaphoreType.DMA((2,))],
        compiler_params=_cparams("arbitrary"),
        name="moe_combine",
    )(x, pos, pos, gates, norm_final, y_sorted)


def _moe_and_final_norm(x_p, x_s, norm_ffn, w_router, b_router, w_gate_up, b_gate_up, w_down, b_down, norm_final):
    n_p, n_s = x_p.shape[0], x_s.shape[0]
    n_rows = (n_p + n_s) * TOP_K
    n_blocks = -(-(n_rows + N_EXPERTS * (BM_EXPERT - 1)) // BM_EXPERT)
    w_router_t = w_router.T.astype(BF16)
    b_r = b_router[:, None]
    g_ffn = norm_ffn[None]
    zero_cnt = jnp.zeros((N_EXPERTS, LANES), I32)
    e_s, r_s, gates_s, cnt_s = _route(x_s, g_ffn, w_router_t, b_r, zero_cnt, n_s)
    e_p, r_p, gates_p, cnt = _route(x_p, g_ffn, w_router_t, b_r, cnt_s, TM_ROUTE)
    plan = _moe_plan(cnt[:, 0], n_blocks)
    pos_p = _positions(plan, e_p, r_p, n_blocks, TM_COMBINE)
    pos_s = _positions(plan, e_s, r_s, n_blocks, n_s)
    h_sorted = _scatter_rows(cnt[:, 0], plan, x_p, pos_p, x_s, pos_s, g_ffn, n_blocks, TM_COMBINE)
    y_sorted = _expert_blocks(plan, h_sorted, w_gate_up, b_gate_up[:, None, :], w_down, b_down[:, None, :], n_blocks)
    y_p = _gather_combine(x_p, pos_p, gates_p, norm_final[None], y_sorted, TM_COMBINE)
    y_s = _gather_combine(x_s, pos_s, gates_s, norm_final[None], y_sorted, n_s)
    return y_p, y_s


def _rb(x):
    return x.astype(BF16).astype(F32)


def _sample_proj_kernel(x_ref, g_ref, wm_ref, wg_ref, z_ref):
    hb = _rms(x_ref[...], g_ref[...]).astype(BF16)
    z_ref[:, :MIX_WIDTH] = jnp.dot(hb, wm_ref[...], preferred_element_type=F32)
    z_ref[:, MIX_WIDTH:] = jnp.dot(hb, wg_ref[...], preferred_element_type=F32)


def _sample_proj(x, norm_mix, w_mix, w_gate):
    n, d = x.shape
    args = (x, norm_mix, w_mix, w_gate)
    width = w_mix.shape[1] + w_gate.shape[1]
    return pl.pallas_call(
        _sample_proj_kernel,
        grid=(1,),
        in_specs=[pl.BlockSpec(a.shape, lambda j: (0, 0)) for a in args],
        out_specs=pl.BlockSpec((n, width), lambda j: (0, 0)),
        out_shape=jax.ShapeDtypeStruct((n, width), F32),
        compiler_params=_cparams("arbitrary"),
        name="sample_proj",
    )(*args)


def _sample_mix_kernel(z_ref, c0_ref, c1_ref, c2_ref, mem_ref, sb0_ref, sb1_ref, sb2_ref, nb_ref, lng_ref, lnb_ref,
                       wsp_ref, bsp_ref, bo_ref, co_ref, a_ref, v_ref, n0_ref, n1_ref, n2_ref):
    z = z_ref[0]
    scale = 1.0 / math.sqrt(B_HEAD_DIM)
    row_head = lax.broadcasted_iota(I32, (SUBLANES, B_GROUP_W), 0)
    lane_head = lax.broadcasted_iota(I32, (SUBLANES, B_GROUP_W), 1) // B_HEAD_DIM
    own = (row_head == lane_head).astype(F32)
    kv_cols = jnp.transpose(jnp.broadcast_to(z[:, B_WIDTH:3 * B_WIDTH], (SUBLANES, 2 * B_WIDTH)))[:, 0:1]
    outs, lses = [], []
    for g, (cache_ref, sb_ref, new_ref) in enumerate(((c0_ref, sb0_ref, n0_ref), (c1_ref, sb1_ref, n1_ref),
                                                      (c2_ref, sb2_ref, n2_ref))):
        window = cache_ref.shape[2]
        q, kn, vn = (_rb(z[:, i * B_WIDTH + g * B_GROUP_W:i * B_WIDTH + (g + 1) * B_GROUP_W]) for i in range(3))
        qh = own * q
        kt = cache_ref[0, :B_GROUP_W, :].astype(BF16)
        vt = cache_ref[0, B_GROUP_W:, :].astype(BF16)
        lg = jnp.dot(qh.astype(BF16), kt, preferred_element_type=F32) * scale + sb_ref[...]
        ln = jnp.sum(qh * kn, axis=1, keepdims=True) * scale + nb_ref[g, :, 0:1]
        m = jnp.maximum(jnp.max(lg, axis=1, keepdims=True), ln)
        lse = m + jnp.log(jnp.sum(jnp.exp(lg - m), axis=1, keepdims=True) + jnp.exp(ln - m))
        p = jnp.exp(lg - lse).astype(BF16)
        pn = _rb(jnp.exp(ln - lse))
        pv = lax.dot_general(p, vt, (((1,), (1,)), ((), ())), preferred_element_type=F32) + pn * vn
        outs.append(jnp.sum(own * pv, axis=0, keepdims=True))
        lses.append(jnp.sum(own * lse, axis=0, keepdims=True))
        new_col = jnp.concatenate([kv_cols[i * B_WIDTH + g * B_GROUP_W:i * B_WIDTH + (g + 1) * B_GROUP_W]
                                   for i in range(2)], axis=0)
        lane = lax.broadcasted_iota(I32, (1, window), 1)
        new_ref[0] = jnp.where(lane == window - 1, new_col, pltpu.roll(cache_ref[0], window - 1, axis=1))
    m = jnp.maximum(jnp.maximum(lses[0], lses[1]), lses[2])
    ws = [jnp.exp(l - m) for l in lses]
    den = ws[0] + ws[1] + ws[2]
    bo_ref[0] = _rb(ws[0] / den) * _rb(outs[0]) + _rb(ws[1] / den) * _rb(outs[1]) + _rb(ws[2] / den) * _rb(outs[2])

    cs = []
    for h in range(C_HEADS):
        c0 = MIX_WIDTH - C_WIDTH + h * C_HEAD_DIM
        qh = _rb(z[:, c0:c0 + C_HEAD_DIM])
        kh = _rb(mem_ref[0, pl.ds(h, N_MEM, stride=2 * C_HEADS), :])
        vh = _rb(mem_ref[0, pl.ds(C_HEADS + h, N_MEM, stride=2 * C_HEADS), :])
        s = jnp.sum(kh * qh, axis=-1, keepdims=True) * (1.0 / math.sqrt(C_HEAD_DIM))
        p = jnp.exp(s - jnp.max(s, axis=0, keepdims=True))
        p = _rb(p / jnp.sum(p, axis=0, keepdims=True))
        cs.append(jnp.sum(p * vh, axis=0, keepdims=True))
    co_ref[0] = jnp.concatenate(cs, axis=1)

    u = _gelu(z[:, 3 * B_WIDTH:3 * B_WIDTH + A_WIDTH])
    v = _layernorm(_gelu(z[:, 3 * B_WIDTH + A_WIDTH:3 * B_WIDTH + 2 * A_WIDTH]), lng_ref[...], lnb_ref[...])
    v_ref[0] = v
    a_ref[0] = u * (_rb(wsp_ref[...]) * _rb(v) + bsp_ref[...])


def _sample_mix(z, caches, mem, sbias, nbias, ln_g, ln_b, wsp0, bsp0):
    n = z.shape[0]
    const = lambda arr: pl.BlockSpec(arr.shape, lambda i: (0,) * arr.ndim)
    row = lambda w: pl.BlockSpec((1, 1, w), lambda i: (i, 0, 0))
    per_req = lambda arr: pl.BlockSpec((1,) + arr.shape[1:], lambda i: (i, 0, 0))
    widths = (B_GROUP_W, C_WIDTH, A_WIDTH, A_WIDTH)
    return pl.pallas_call(
        _sample_mix_kernel,
        grid=(n,),
        in_specs=[row(z.shape[2])] + [per_req(c) for c in caches] + [per_req(mem)]
        + [const(a) for a in (*sbias, nbias, ln_g, ln_b, wsp0, bsp0)],
        out_specs=tuple(row(w) for w in widths) + tuple(per_req(c) for c in caches),
        out_shape=tuple(jax.ShapeDtypeStruct((n, 1, w), F32) for w in widths)
        + tuple(jax.ShapeDtypeStruct(c.shape, F32) for c in caches),
        compiler_params=_cparams("arbitrary"),
        name="sample_mix",
    )(z, *caches, mem, *sbias, nbias, ln_g, ln_b, wsp0, bsp0)


def _sample_merge_kernel(x_ref, z_ref, a_ref, b_ref, c_ref, wa_ref, wb_ref, wc_ref, wo_ref, o_ref):
    dot = lambda a, b: jnp.dot(a.astype(BF16), b, preferred_element_type=F32)
    merged = None
    for i, (act_ref, w_ref) in enumerate(((a_ref, wa_ref), (b_ref, wb_ref), (c_ref, wc_ref))):
        gate = jax.nn.sigmoid(z_ref[:, MIX_WIDTH + i * D_MODEL:MIX_WIDTH + (i + 1) * D_MODEL])
        term = gate * dot(act_ref[...], w_ref[...])
        merged = term if merged is None else merged + term
    o_ref[...] = x_ref[...] + dot(merged, wo_ref[...])


def _sample_merge(x, z, a, bo, co, wa, wb, wc, wo):
    args = (x, z, a, bo, co, wa, wb, wc, wo)
    return pl.pallas_call(
        _sample_merge_kernel,
        grid=(1,),
        in_specs=[pl.BlockSpec(arr.shape, lambda i: (0, 0)) for arr in args],
        out_specs=pl.BlockSpec(x.shape, lambda i: (0, 0)),
        out_shape=jax.ShapeDtypeStruct(x.shape, F32),
        compiler_params=_cparams("arbitrary"),
        name="sample_merge",
    )(*args)


def _sample_mixers(x_sample, caches, cache_mem, sbias, nbias, norm_mix, ln_v_g, ln_v_b, w_spatial, b_spatial, wts):
    n = x_sample.shape[0]
    x = x_sample[:, 0]
    z = _sample_proj(x, norm_mix[None], wts["mix"], wts["gate"])
    mem = cache_mem.reshape(n, N_MEM * 2 * C_HEADS, C_HEAD_DIM)
    wsp0 = jnp.repeat(w_spatial[:, 0, 0], LANES)[None]
    bsp0 = jnp.repeat(b_spatial[:, 0], LANES)[None]
    bo, co, a, v, *new_caches = _sample_mix(z[:, None], [_from_window_layout(c) for c in caches], mem, sbias, nbias,
                                            ln_v_g[None], ln_v_b[None], wsp0, bsp0)
    x_new = _sample_merge(x, z, a[:, 0], bo[:, 0], co[:, 0], wts["a"], wts["b"], wts["c"], wts["out"])
    return x_new, [_to_window_layout(c) for c in new_caches], v[:, 0]


def _mixer_weights(w_in, w_branch_a, w_branch_b, w_branch_c, w_out):
    return {"mix": w_in[:, :MIX_WIDTH].astype(BF16), "gate": w_in[:, MIX_WIDTH:].astype(BF16),
            "a": w_branch_a.astype(BF16), "b": w_branch_b.astype(BF16), "c": w_branch_c.astype(BF16),
            "out": w_out.astype(BF16)}


def kernel(x_prompt, x_sample, cache_win0_kv, cache_win1_kv, cache_win2_kv, cache_mem_kv, mem_prompt, rel_bias,
           norm_mix, norm_mem, w_in, ln_v_g, ln_v_b, w_spatial, b_spatial, w_mem_kv, w_branch_a, w_branch_b,
           w_branch_c, w_out, norm_ffn, w_router, b_router, w_gate_up, b_gate_up, w_down, b_down, norm_final):
    assert norm_mix.shape[0] == 1, "one layer"
    b, s, d = x_prompt.shape
    n_s = x_sample.shape[0]
    caches = (cache_win0_kv[0], cache_win1_kv[0], cache_win2_kv[0])
    assert all(c.shape[1] == w for c, (w, _) in zip(caches, B_PAIRS)), "window buffers hold a full window"
    wts = _mixer_weights(w_in[0], w_branch_a[0], w_branch_b[0], w_branch_c[0], w_out[0])
    band_bias, *sbias, nbias = _bias_tables(rel_bias)
    xp, win_p, mem_p = _prompt_mixers(x_prompt, mem_prompt, band_bias, norm_mix[0], norm_mem[0], ln_v_g[0],
                                      ln_v_b[0], w_spatial[0], b_spatial[0], w_mem_kv[0], wts)
    xs, win_s, v_s = _sample_mixers(x_sample, caches, cache_mem_kv[0], sbias, nbias, norm_mix[0], ln_v_g[0],
                                    ln_v_b[0], w_spatial[0], b_spatial[0], wts)
    y_p, y_s = _moe_and_final_norm(xp.reshape(b * s, d), xs, norm_ffn[0], w_router[0], b_router[0], w_gate_up[0],
                                   b_gate_up[0], w_down[0], b_down[0], norm_final)
    chunk_v = v_s.reshape(1, n_s, 1, A_GROUPS, LANES)
    return (y_p.reshape(b, s, d), y_s[:, None], win_p[0], win_p[1], win_p[2], mem_p,
            win_s[0], win_s[1], win_s[2], chunk_v)
```

```python
import functools
import math

import numpy as np
import jax
import jax.numpy as jnp
from jax import lax
from jax.experimental import pallas as pl
from jax.experimental.pallas import tpu as pltpu

F32 = jnp.float32
BF16 = jnp.bfloat16
I32 = jnp.int32

D_MODEL = 1024
N_MEM = 256
CHUNK = 128
A_GROUPS = 4
A_WIDTH = 512
B_PAIRS = ((128, 1), (512, 4), (2048, 16))
B_GROUP_W = 256
B_WIDTH = 768
B_HEAD_DIM = 64
BAND = 128
C_HEADS = 4
C_HEAD_DIM = 128
C_WIDTH = 512
REL_BUCKETS = 32
REL_MAX_DIST = 2048
N_EXPERTS = 32
TOP_K = 4
D_FF = 1024
SWIGLU_LIMIT = 7.0
SWIGLU_ALPHA = 1.702
EPS = 1e-6
NEG_INF = -1e30
MIX_WIDTH = 3 * B_WIDTH + 2 * A_WIDTH + C_WIDTH

LANES = 128
SUBLANES = 8
ROW_TILES = D_MODEL // LANES
VMEM_LIMIT = 56 * 1024 * 1024

TM_PROJ = 512
TM_MERGE = 256
BM_EXPERT = 512
TM_COMBINE = 512
ATTN_BLOCKS_PER_ITER = 2


def _cparams(*sem):
    return pltpu.CompilerParams(dimension_semantics=sem, vmem_limit_bytes=VMEM_LIMIT)


def _t5_bucket(dist):
    dist = np.maximum(np.asarray(dist), 0)
    max_exact = REL_BUCKETS // 2
    log_ratio = np.log(np.maximum(dist, max_exact) / max_exact) / math.log(REL_MAX_DIST / max_exact)
    large = np.minimum(max_exact + (log_ratio * (REL_BUCKETS - max_exact)).astype(np.int32), REL_BUCKETS - 1)
    return np.where(dist < max_exact, dist, large).astype(np.int32)


def _gelu(x):
    c = math.sqrt(2.0 / math.pi)
    return x * (0.5 * (1.0 + jnp.tanh(c * (x + 0.044715 * (x * x * x)))))


def _rms(x, g):
    return x * lax.rsqrt(jnp.mean(x * x, axis=-1, keepdims=True) + EPS) * g


def _layernorm(x, g, b):
    xc = x - jnp.mean(x, axis=-1, keepdims=True)
    return xc * lax.rsqrt(jnp.mean(xc * xc, axis=-1, keepdims=True) + EPS) * g + b


def _proj_kernel(x_ref, g_ref, w_ref, lng_ref, lnb_ref, wsp_ref, bsp_ref,
                 qkv0_ref, qkv1_ref, qkv2_ref, win0_ref, win1_ref, win2_ref, a_ref, qc_ref, scr_ref):
    t = pl.program_id(1)
    last = pl.num_programs(1) - 1
    tm = x_ref.shape[1]
    hb = _rms(x_ref[0], g_ref[...]).astype(BF16)

    def proj(c0, width):
        return jnp.dot(hb, w_ref[:, c0:c0 + width], preferred_element_type=F32)

    for g, (_, dil) in enumerate(B_PAIRS):
        q, k, v = (proj(i * B_WIDTH + g * B_GROUP_W, B_GROUP_W) for i in range(3))
        kv = jnp.concatenate([k, v], axis=1)
        if g == 2:
            win2_ref[0] = jnp.transpose(kv)
        elif g == 1:
            @pl.when(t == last)
            def _():
                win1_ref[0] = jnp.transpose(kv)
        else:
            @pl.when(t == last)
            def _():
                win0_ref[0] = jnp.transpose(kv[tm - B_PAIRS[0][0]:])
        if dil == 1:
            qkv0_ref[0] = jnp.concatenate([q, k, v], axis=1).astype(BF16)
            continue
        out_ref = qkv1_ref if g == 1 else qkv2_ref
        for i, arr in enumerate((q, k, v)):
            for ct in range(2):
                scr_ref[2 * i + ct] = arr[:, ct * LANES:(ct + 1) * LANES]
        for r in range(dil):
            rows = [scr_ref[j, pl.ds(r, tm // dil, stride=dil), :] for j in range(6)]
            out_ref[0, r] = jnp.concatenate(rows, axis=1).astype(BF16)

    u = _gelu(proj(3 * B_WIDTH, A_WIDTH))
    v = _layernorm(_gelu(proj(3 * B_WIDTH + A_WIDTH, A_WIDTH)), lng_ref[...], lnb_ref[...]).astype(BF16)
    for c in range(tm // CHUNK):
        rs = slice(c * CHUNK, (c + 1) * CHUNK)
        for g in range(A_GROUPS):
            cs = slice(g * LANES, (g + 1) * LANES)
            s = jnp.dot(wsp_ref[g], v[rs, cs], preferred_element_type=F32) + bsp_ref[:, g:g + 1]
            a_ref[0, rs, cs] = (u[rs, cs] * s).astype(BF16)

    qc_ref[0] = proj(3 * B_WIDTH + 2 * A_WIDTH, C_WIDTH).astype(BF16)


def _prompt_proj(x, norm_mix, w_mix, ln_g, ln_b, wsp, bsp):
    b, s, d = x.shape
    tm = TM_PROJ
    assert s % tm == 0 and s >= B_PAIRS[2][0] and tm == B_PAIRS[1][0]
    nt = s // tm
    const = lambda *shape: pl.BlockSpec(shape, lambda i, j: (0,) * len(shape))
    out_shape = (
        jax.ShapeDtypeStruct((b, s, 3 * B_GROUP_W), BF16),
        jax.ShapeDtypeStruct((b, 4, s // 4, 3 * B_GROUP_W), BF16),
        jax.ShapeDtypeStruct((b, 16, s // 16, 3 * B_GROUP_W), BF16),
        jax.ShapeDtypeStruct((b, 2 * B_GROUP_W, B_PAIRS[0][0]), F32),
        jax.ShapeDtypeStruct((b, 2 * B_GROUP_W, B_PAIRS[1][0]), F32),
        jax.ShapeDtypeStruct((b, 2 * B_GROUP_W, s), F32),
        jax.ShapeDtypeStruct((b, s, A_WIDTH), BF16),
        jax.ShapeDtypeStruct((b, s, C_WIDTH), BF16),
    )
    out_specs = (
        pl.BlockSpec((1, tm, 3 * B_GROUP_W), lambda i, j: (i, j, 0)),
        pl.BlockSpec((1, 4, tm // 4, 3 * B_GROUP_W), lambda i, j: (i, 0, j, 0)),
        pl.BlockSpec((1, 16, tm // 16, 3 * B_GROUP_W), lambda i, j: (i, 0, j, 0)),
        pl.BlockSpec((1, 2 * B_GROUP_W, B_PAIRS[0][0]), lambda i, j: (i, 0, 0)),
        pl.BlockSpec((1, 2 * B_GROUP_W, B_PAIRS[1][0]), lambda i, j: (i, 0, 0)),
        pl.BlockSpec((1, 2 * B_GROUP_W, tm), lambda i, j: (i, 0, j)),
        pl.BlockSpec((1, tm, A_WIDTH), lambda i, j: (i, j, 0)),
        pl.BlockSpec((1, tm, C_WIDTH), lambda i, j: (i, j, 0)),
    )
    return pl.pallas_call(
        _proj_kernel,
        grid=(b, nt),
        in_specs=[pl.BlockSpec((1, tm, d), lambda i, j: (i, j, 0)), const(1, d), const(d, MIX_WIDTH),
                  const(1, A_WIDTH), const(1, A_WIDTH), const(A_GROUPS, CHUNK, CHUNK), const(CHUNK, A_GROUPS)],
        out_specs=out_specs,
        out_shape=out_shape,
        scratch_shapes=[pltpu.VMEM((6, tm, LANES), F32)],
        compiler_params=_cparams("arbitrary", "arbitrary"),
        name="prompt_proj",
    )(x, norm_mix, w_mix, ln_g, ln_b, wsp, bsp)


def _attend(q, k, v, bias_ref, g, k0, hmask):
    kn = k.shape[0]
    qs = jnp.concatenate([q * hm for hm in hmask], axis=0)
    s = lax.dot_general(qs, k, (((1,), (1,)), ((), ())), preferred_element_type=F32) + bias_ref[g, :, k0:k0 + kn]
    m = jnp.max(s, axis=-1, keepdims=True)
    p = jnp.exp(s - m)
    l = jnp.sum(p, axis=-1, keepdims=True)
    pv = jnp.dot(p.astype(BF16), v, preferred_element_type=F32) / l
    lse_rows = m + jnp.log(l)
    out = jnp.zeros((BAND, B_GROUP_W), F32)
    lse = jnp.zeros((BAND, B_GROUP_W), F32)
    for h, hm in enumerate(hmask):
        rows = slice(h * BAND, (h + 1) * BAND)
        sel = hm > 0
        out = jnp.where(sel, pv[rows], out)
        lse = jnp.where(sel, lse_rows[rows], lse)
    return out, lse


def _attn_kernel(q0_ref, q1_ref, q2_ref, bias_ref, o_ref, out_ref, lse_ref):
    s = q0_ref.shape[1]
    lane_head = lax.broadcasted_iota(I32, (1, B_GROUP_W), 1) // B_HEAD_DIM
    scale = 1.0 / math.sqrt(B_HEAD_DIM)
    hmask = [jnp.where(lane_head == h, scale, 0.0).astype(BF16) for h in range(4)]
    qs, ks, vs = (slice(i * B_GROUP_W, (i + 1) * B_GROUP_W) for i in range(3))

    def store(g, start, dil, out, lse):
        rows = pl.ds(start, BAND) if dil == 1 else pl.ds(start, BAND, stride=dil)
        for ct in range(2):
            out_ref[g, ct, rows, :] = out[:, ct * LANES:(ct + 1) * LANES]
            lse_ref[g, ct, rows, :] = lse[:, ct * LANES:(ct + 1) * LANES]

    for g, (_, dil) in enumerate(B_PAIRS):
        n = s // dil
        nb = n // BAND

        def load(rows, cols, r, g=g):
            if g == 0:
                return q0_ref[0, rows, cols]
            return (q1_ref if g == 1 else q2_ref)[0, r, rows, cols]

        def first_block(r, g=g, dil=dil, load=load):
            rows = pl.ds(0, BAND)
            out, lse = _attend(load(rows, qs, r), load(rows, ks, r), load(rows, vs, r), bias_ref, g, BAND, hmask)
            store(g, r, dil, out, lse)

        def later_block(i, g=g, dil=dil, nb=nb, load=load):
            r = i // (nb - 1)
            qb = i % (nb - 1) + 1
            q0 = pl.multiple_of(qb * BAND, BAND)
            rows_q = pl.ds(q0, BAND)
            rows_k = pl.ds(q0 - BAND, 2 * BAND)
            out, lse = _attend(load(rows_q, qs, r), load(rows_k, ks, r), load(rows_k, vs, r), bias_ref, g, 0, hmask)
            store(g, qb * (BAND * dil) + r, dil, out, lse)

        def run(block_fn, count):
            per = ATTN_BLOCKS_PER_ITER

            def body(it, carry):
                for u in range(per):
                    block_fn(it * per + u)
                return carry
            lax.fori_loop(0, count // per, body, 0)
            for i in range(count - count % per, count):
                block_fn(i)

        run(first_block, dil)
        run(later_block, dil * (nb - 1))

    rc = 256
    def combine(c, carry):
        rows = pl.ds(pl.multiple_of(c * rc, rc), rc)
        for ct in range(2):
            ls = [lse_ref[g, ct, rows, :] for g in range(3)]
            m = jnp.maximum(jnp.maximum(ls[0], ls[1]), ls[2])
            ws = [jnp.exp(l - m) for l in ls]
            den = ws[0] + ws[1] + ws[2]
            num = ws[0] * out_ref[0, ct, rows, :] + ws[1] * out_ref[1, ct, rows, :] + ws[2] * out_ref[2, ct, rows, :]
            o_ref[0, rows, ct * LANES:(ct + 1) * LANES] = (num / den).astype(BF16)
        return carry
    lax.fori_loop(0, s // rc, combine, 0)


def _prompt_attention(qkv0, qkv1, qkv2, bias):
    b, s, w = qkv0.shape
    return pl.pallas_call(
        _attn_kernel,
        grid=(b,),
        in_specs=[pl.BlockSpec((1, s, w), lambda i: (i, 0, 0)),
                  pl.BlockSpec((1, 4, s // 4, w), lambda i: (i, 0, 0, 0)),
                  pl.BlockSpec((1, 16, s // 16, w), lambda i: (i, 0, 0, 0)),
                  pl.BlockSpec(bias.shape, lambda i: (0, 0, 0))],
        out_specs=pl.BlockSpec((1, s, B_GROUP_W), lambda i: (i, 0, 0)),
        out_shape=jax.ShapeDtypeStruct((b, s, B_GROUP_W), BF16),
        scratch_shapes=[pltpu.VMEM((3, 2, s, LANES), F32), pltpu.VMEM((3, 2, s, LANES), F32)],
        compiler_params=_cparams("arbitrary"),
        name="prompt_attention",
    )(qkv0, qkv1, qkv2, bias)


def _bias_kernel(rel_ref, band_idx_ref, s0_ref, s1_ref, s2_ref, band_ref, sb0_ref, sb1_ref, sb2_ref, nb_ref):
    def lookup(idx, col):
        acc = jnp.full(idx.shape, NEG_INF, F32)
        for bucket in range(REL_BUCKETS):
            acc = jnp.where(idx == bucket, rel_ref[bucket, col], acc)
        return acc

    for g, (s_ref, sb_ref) in enumerate(((s0_ref, sb0_ref), (s1_ref, sb1_ref), (s2_ref, sb2_ref))):
        width = s_ref.shape[1]
        for h in range(4):
            band_ref[g, h * BAND:(h + 1) * BAND, :] = lookup(band_idx_ref[g], 4 * g + h)
        sb_ref[...] = jnp.concatenate([lookup(s_ref[...], 4 * g + h) for h in range(4)]
                                      + [jnp.full((SUBLANES - 4, width), NEG_INF, F32)], axis=0)
        nb_ref[g] = jnp.concatenate([jnp.full((1, LANES), rel_ref[0, 4 * g + h], F32) for h in range(4)]
                                    + [jnp.zeros((SUBLANES - 4, LANES), F32)], axis=0)


def _bias_tables(rel_bias):
    steps = np.arange(BAND)[:, None] + BAND - np.arange(2 * BAND)[None, :]
    valid = (steps >= 0) & (steps <= BAND)
    band_idx = np.stack([np.where(valid, _t5_bucket(np.clip(steps, 0, BAND) * dil), -1) for _, dil in B_PAIRS])
    s_idx = []
    for window, dil in B_PAIRS:
        w = np.arange(window)
        s_idx.append(np.where(w % dil == 0, _t5_bucket(window - w), -1)[None].astype(np.int32))
    args = (rel_bias, jnp.asarray(band_idx.astype(np.int32))) + tuple(jnp.asarray(s) for s in s_idx)
    vmem = lambda a: pl.BlockSpec(a.shape, lambda i: (0,) * a.ndim)
    out_shape = (jax.ShapeDtypeStruct((3, 4 * BAND, 2 * BAND), F32),) + tuple(
        jax.ShapeDtypeStruct((SUBLANES, window), F32) for window, _ in B_PAIRS) + (
        jax.ShapeDtypeStruct((3, SUBLANES, LANES), F32),)
    return pl.pallas_call(
        _bias_kernel,
        grid=(1,),
        in_specs=[pl.BlockSpec(memory_space=pltpu.SMEM)] + [vmem(a) for a in args[1:]],
        out_specs=tuple(pl.BlockSpec(s.shape, lambda i, n=len(s.shape): (0,) * n) for s in out_shape),
        out_shape=out_shape,
        compiler_params=_cparams("arbitrary"),
        name="bias_tables",
    )(*args)


def _memkv_kernel(mem_ref, g_ref, w_ref, kv_ref, kvb_ref):
    hb = _rms(mem_ref[0], g_ref[...]).astype(BF16)
    kv = jnp.dot(hb, w_ref[...], preferred_element_type=F32)
    kv_ref[0] = kv
    kvb_ref[0] = kv.astype(BF16)


def _memory_kv(mem, norm_mem, w_mem):
    b, m, d = mem.shape
    w = w_mem.shape[1]
    return pl.pallas_call(
        _memkv_kernel,
        grid=(b,),
        in_specs=[pl.BlockSpec((1, m, d), lambda i: (i, 0, 0)), pl.BlockSpec((1, d), lambda i: (0, 0)),
                  pl.BlockSpec((d, w), lambda i: (0, 0))],
        out_specs=(pl.BlockSpec((1, m, w), lambda i: (i, 0, 0)), pl.BlockSpec((1, m, w), lambda i: (i, 0, 0))),
        out_shape=(jax.ShapeDtypeStruct((b, m, w), F32), jax.ShapeDtypeStruct((b, m, w), BF16)),
        compiler_params=_cparams("arbitrary"),
        name="memory_kv",
    )(mem, norm_mem, w_mem)


def _merge_kernel(x_ref, g_ref, a_ref, b_ref, qc_ref, kv_ref, wg_ref, wa_ref, wb_ref, wc_ref, wo_ref,
                  gf_ref, wr_ref, br_ref, cnt_in_ref, o_ref, e_ref, rank_ref, gate_ref, cnt_ref, carry_ref):
    x = x_ref[0]
    hb = _rms(x, g_ref[...]).astype(BF16)
    qc = qc_ref[0]
    cs = []
    for h in range(C_HEADS):
        hs = slice(h * C_HEAD_DIM, (h + 1) * C_HEAD_DIM)
        s = lax.dot_general(qc[:, hs], kv_ref[0, :, hs], (((1,), (1,)), ((), ())), preferred_element_type=F32)
        s = s * (1.0 / math.sqrt(C_HEAD_DIM))
        p = jnp.exp(s - jnp.max(s, axis=-1, keepdims=True))
        p = p / jnp.sum(p, axis=-1, keepdims=True)
        vs = slice(C_WIDTH + h * C_HEAD_DIM, C_WIDTH + (h + 1) * C_HEAD_DIM)
        cs.append(jnp.dot(p.astype(BF16), kv_ref[0, :, vs], preferred_element_type=F32))
    c = jnp.concatenate(cs, axis=1).astype(BF16)
    branches = ((a_ref[0], wa_ref), (b_ref[0], wb_ref), (c, wc_ref))
    merged = None
    for i, (act, w_ref) in enumerate(branches):
        gate = jax.nn.sigmoid(jnp.dot(hb, wg_ref[:, i * D_MODEL:(i + 1) * D_MODEL], preferred_element_type=F32))
        term = gate * jnp.dot(act, w_ref[...], preferred_element_type=F32)
        merged = term if merged is None else merged + term
    x_new = x + jnp.dot(merged.astype(BF16), wo_ref[...], preferred_element_type=F32)
    o_ref[0] = x_new
    first_step = (pl.program_id(0) == 0) & (pl.program_id(1) == 0)
    _route_rows(x_new, first_step, gf_ref, wr_ref, br_ref, cnt_in_ref, e_ref, rank_ref, gate_ref, cnt_ref, carry_ref)


def _prompt_merge(x, norm_mix, a, bo, qc, kvb, wg, wa, wb, wc, wo, route_params, cnt_in):
    b, s, d = x.shape
    tm = TM_MERGE
    nt = s // tm
    n = b * s
    tile = lambda w: pl.BlockSpec((1, tm, w), lambda i, j: (i, j, 0))
    const = lambda arr: pl.BlockSpec(arr.shape, lambda i, j: (0,) * arr.ndim)
    ids = pl.BlockSpec((TOP_K, tm), lambda i, j: (0, i * nt + j))
    return pl.pallas_call(
        _merge_kernel,
        grid=(b, nt),
        in_specs=[tile(d), const(norm_mix), tile(A_WIDTH), tile(B_GROUP_W), tile(C_WIDTH),
                  pl.BlockSpec((1,) + kvb.shape[1:], lambda i, j: (i, 0, 0)),
                  const(wg), const(wa), const(wb), const(wc), const(wo)]
        + [const(p) for p in route_params] + [const(cnt_in)],
        out_specs=(tile(d), ids, ids, pl.BlockSpec((tm, SUBLANES), lambda i, j: (i * nt + j, 0)),
                   pl.BlockSpec((N_EXPERTS, LANES), lambda i, j: (0, 0))),
        out_shape=(jax.ShapeDtypeStruct((b, s, d), F32), jax.ShapeDtypeStruct((TOP_K, n), I32),
                   jax.ShapeDtypeStruct((TOP_K, n), I32), jax.ShapeDtypeStruct((n, SUBLANES), F32),
                   jax.ShapeDtypeStruct((N_EXPERTS, LANES), I32)),
        scratch_shapes=[pltpu.VMEM((N_EXPERTS, LANES), F32)],
        compiler_params=_cparams("arbitrary", "arbitrary"),
        name="prompt_merge",
    )(x, norm_mix, a, bo, qc, kvb, wg, wa, wb, wc, wo, *route_params, cnt_in)


def _to_window_layout(w):
    b, _, window = w.shape
    return jnp.transpose(w.reshape(b, 2, 4, B_HEAD_DIM, window), (0, 4, 1, 2, 3))[None]


def _from_window_layout(c):
    b, window = c.shape[:2]
    return jnp.transpose(c, (0, 2, 3, 4, 1)).reshape(b, 2 * B_GROUP_W, window)


def _prompt_mixers(x_prompt, mem_prompt, band_bias, norm_mix, norm_mem, ln_v_g, ln_v_b, w_spatial, b_spatial,
                   w_mem_kv, wts, route_params, cnt_in):
    b, s, _ = x_prompt.shape
    wsp = jnp.where(np.tril(np.ones((CHUNK, CHUNK), bool)), w_spatial, 0.0).astype(BF16)
    qkv0, qkv1, qkv2, win0, win1, win2, a_out, qc = _prompt_proj(
        x_prompt, norm_mix[None], wts["mix"], ln_v_g[None], ln_v_b[None], wsp, b_spatial.T)
    b_out = _prompt_attention(qkv0, qkv1, qkv2, band_bias)
    mem_kv, mem_kv_b = _memory_kv(mem_prompt, norm_mem[None], w_mem_kv.astype(BF16))
    x_new, *routing = _prompt_merge(x_prompt, norm_mix[None], a_out, b_out, qc, mem_kv_b, wts["gate"],
                                    wts["a"], wts["b"], wts["c"], wts["out"], route_params, cnt_in)
    wins = [_to_window_layout(w) for w in (win0, win1, win2)]
    return x_new, routing, wins, mem_kv.reshape(1, b, N_MEM, 2, C_HEADS, C_HEAD_DIM)


def _route_kernel(x_ref, g_ref, wr_ref, br_ref, cnt_in_ref, e_ref, rank_ref, gate_ref, cnt_ref, carry_ref):
    _route_rows(x_ref[...], pl.program_id(0) == 0, g_ref, wr_ref, br_ref, cnt_in_ref, e_ref, rank_ref, gate_ref,
                cnt_ref, carry_ref)


def _route_rows(x, first_step, g_ref, wr_ref, br_ref, cnt_in_ref, e_ref, rank_ref, gate_ref, cnt_ref, carry_ref):
    tm = x.shape[0]

    @pl.when(first_step)
    def _():
        carry_ref[...] = cnt_in_ref[...].astype(F32)

    h = _rms(x, g_ref[...])
    logits = lax.dot_general(wr_ref[...], h.astype(BF16), (((1,), (1,)), ((), ())),
                             preferred_element_type=F32) + br_ref[...]
    expert = lax.broadcasted_iota(I32, logits.shape, 0)
    vals, idxs = [], []
    member = jnp.zeros(logits.shape, F32)
    for _ in range(TOP_K):
        m = jnp.max(logits, axis=0, keepdims=True)
        idx = jnp.min(jnp.where(logits == m, expert, N_EXPERTS), axis=0, keepdims=True)
        hit = expert == idx
        vals.append(m)
        idxs.append(idx)
        member = jnp.where(hit, 1.0, member)
        logits = jnp.where(hit, -jnp.inf, logits)
    p = [jnp.exp(v - vals[0]) for v in vals]
    den = p[0] + p[1] + p[2] + p[3]
    gates = jnp.concatenate([pk / den for pk in p] + [jnp.zeros((SUBLANES - TOP_K, tm), F32)], axis=0)
    gate_ref[...] = jnp.transpose(gates)
    e_ref[...] = jnp.concatenate(idxs, axis=0)
    before = (lax.broadcasted_iota(I32, (tm, tm), 0) < lax.broadcasted_iota(I32, (tm, tm), 1)).astype(BF16)
    prefix = jnp.dot(member.astype(BF16), before, preferred_element_type=F32) + carry_ref[:, 0:1]
    rank_ref[...] = jnp.concatenate(
        [jnp.sum(jnp.where(expert == idx, prefix, 0.0), axis=0, keepdims=True) for idx in idxs], axis=0).astype(I32)
    carry_ref[...] = carry_ref[...] + jnp.sum(member, axis=1, keepdims=True)
    cnt_ref[...] = carry_ref[...].astype(I32)


def _route_params(norm_ffn, w_router, b_router):
    return norm_ffn[None], w_router.T.astype(BF16), b_router[:, None]


def _route(x, route_params, cnt_in, tm):
    n, d = x.shape
    assert n % tm == 0
    norm_ffn, w_router_t, b_router = route_params
    const = lambda arr: pl.BlockSpec(arr.shape, lambda i: (0,) * arr.ndim)
    return pl.pallas_call(
        _route_kernel,
        grid=(n // tm,),
        in_specs=[pl.BlockSpec((tm, d), lambda i: (i, 0)), const(norm_ffn), const(w_router_t), const(b_router),
                  const(cnt_in)],
        out_specs=(pl.BlockSpec((TOP_K, tm), lambda i: (0, i)), pl.BlockSpec((TOP_K, tm), lambda i: (0, i)),
                   pl.BlockSpec((tm, SUBLANES), lambda i: (i, 0)), pl.BlockSpec((N_EXPERTS, LANES), lambda i: (0, 0))),
        out_shape=(jax.ShapeDtypeStruct((TOP_K, n), I32), jax.ShapeDtypeStruct((TOP_K, n), I32),
                   jax.ShapeDtypeStruct((n, SUBLANES), F32), jax.ShapeDtypeStruct((N_EXPERTS, LANES), I32)),
        scratch_shapes=[pltpu.VMEM((N_EXPERTS, LANES), F32)],
        compiler_params=_cparams("arbitrary"),
        name="moe_route",
    )(x, norm_ffn, w_router_t, b_router, cnt_in)


DMA_ROWS_PER_ITER = 2
N_DMA_THREADS = 2
N_PLAN_TAIL = 1 + N_EXPERTS


def _row_tile(r):
    return pl.ds(pl.multiple_of(r * ROW_TILES, ROW_TILES), ROW_TILES)


def _plan_kernel(n_blocks, cnt_ref, plan_ref):
    shift = BM_EXPERT.bit_length() - 1

    def per_expert(e, carry):
        blk0, last_e = carry
        nb = (cnt_ref[e] + (BM_EXPERT - 1)) >> shift
        plan_ref[n_blocks + 1 + e] = blk0 << shift

        def fill(j, cc):
            plan_ref[blk0 + j] = e
            return cc
        lax.fori_loop(0, nb, fill, 0)
        return blk0 + nb, jnp.where(nb > 0, e, last_e)
    n_used, last_e = lax.fori_loop(0, N_EXPERTS, per_expert, (0, 0))

    def tail(j, cc):
        plan_ref[j] = last_e
        return cc
    lax.fori_loop(n_used, n_blocks, tail, 0)
    plan_ref[n_blocks] = n_used


def _moe_plan(cnt, n_blocks):
    smem = pl.BlockSpec(memory_space=pltpu.SMEM)
    return pl.pallas_call(
        functools.partial(_plan_kernel, n_blocks),
        in_specs=[smem], out_specs=smem,
        out_shape=jax.ShapeDtypeStruct((n_blocks + N_PLAN_TAIL,), I32),
        name="moe_plan",
    )(cnt)


POSITION_TILES_PER_STEP = 16


def _positions_kernel(n_blocks, plan_ref, e_ref, r_ref, pos_ref):
    tiles, _, tm = pos_ref.shape
    e = e_ref[...]
    first = jnp.zeros(e.shape, I32)
    for ex in range(N_EXPERTS):
        first = jnp.where(e == ex, plan_ref[n_blocks + 1 + ex], first)
    pos = first + r_ref[...]
    for i in range(tiles):
        pos_ref[i] = pos[:, i * tm:(i + 1) * tm]


def _positions(plan, e, r, n_blocks, tm):
    n = e.shape[1]
    assert n % tm == 0
    nt = n // tm
    per_step = math.gcd(nt, POSITION_TILES_PER_STEP)
    pos = pl.pallas_call(
        functools.partial(_positions_kernel, n_blocks),
        grid=(nt // per_step,),
        in_specs=[pl.BlockSpec(memory_space=pltpu.SMEM), pl.BlockSpec((TOP_K, per_step * tm), lambda i: (0, i)),
                  pl.BlockSpec((TOP_K, per_step * tm), lambda i: (0, i))],
        out_specs=pl.BlockSpec((per_step, TOP_K, tm), lambda i: (i, 0, 0)),
        out_shape=jax.ShapeDtypeStruct((nt, TOP_K, tm), I32),
        compiler_params=_cparams("arbitrary"),
        name="moe_positions",
    )(plan, e, r)
    return pos.reshape(-1)


def _scatter_rows_kernel(n_blocks, cnt_ref, plan_ref, x_ref, pos_ref, xs_ref, poss_ref, g_ref, h_hbm,
                         hbuf, zbuf, sem, zsem):
    t = pl.program_id(0)
    n_tiles = pl.num_programs(0) - 1
    tm = x_ref.shape[0]
    n_s = xs_ref.shape[0]
    bm = BM_EXPERT
    shift = bm.bit_length() - 1
    slot = t % 2

    def zero_rows(first, count, wait):
        for bit in range(shift):
            n = 1 << bit
            lo = first + (count & (n - 1))

            @pl.when((count >> bit) & 1 == 1)
            def _():
                cp = pltpu.make_async_copy(zbuf.at[pl.ds(0, n * ROW_TILES), :], h_hbm.at[_row_tile_n(lo, n), :],
                                           zsem.at[0])
                cp.wait() if wait else cp.start()

    def zero_block(blk, wait):
        cp = pltpu.make_async_copy(zbuf, h_hbm.at[_row_tile_n(blk * bm, bm), :], zsem.at[0])
        cp.wait() if wait else cp.start()

    @pl.when(t == 0)
    def _():
        zbuf[...] = jnp.zeros(zbuf.shape, F32)
        for wait in (False, True):
            def pad_expert(e, cc, wait=wait):
                cnt = cnt_ref[e]
                zero_rows(plan_ref[n_blocks + 1 + e] + cnt, (-cnt) & (bm - 1), wait)
                return cc
            lax.fori_loop(0, N_EXPERTS, pad_expert, 0)

            def pad_block(blk, cc, wait=wait):
                zero_block(blk, wait)
                return cc
            lax.fori_loop(plan_ref[n_blocks], n_blocks, pad_block, 0)

    def wait_copies(sl, n_tok):
        for _ in range(TOP_K):
            pltpu.make_async_copy(hbuf.at[sl, pl.ds(0, n_tok * ROW_TILES), :],
                                  h_hbm.at[pl.ds(0, n_tok * ROW_TILES), :], sem.at[sl]).wait()

    def copy_rows(src_ref, rows_ref, n_tok):
        h = _rms(src_ref[...], g_ref[...])
        for c in range(ROW_TILES):
            hbuf[slot, pl.ds(c, n_tok, stride=ROW_TILES), :] = h[:, c * LANES:(c + 1) * LANES]

        def body(it, carry):
            for u in range(DMA_ROWS_PER_ITER):
                j = it * DMA_ROWS_PER_ITER + u
                for k in range(TOP_K):
                    pltpu.make_async_copy(hbuf.at[slot, _row_tile(j), :],
                                          h_hbm.at[_row_tile(rows_ref[k * n_tok + j]), :],
                                          sem.at[slot]).start(priority=k % N_DMA_THREADS)
            return carry
        lax.fori_loop(0, n_tok // DMA_ROWS_PER_ITER, body, 0)

    @pl.when(t >= 2)
    def _():
        wait_copies(slot, tm)

    @pl.when(t < n_tiles)
    def _():
        copy_rows(x_ref, pos_ref, tm)

    @pl.when(t == n_tiles)
    def _():
        copy_rows(xs_ref, poss_ref, n_s)
        wait_copies(slot, n_s)

        @pl.when(n_tiles >= 1)
        def _():
            wait_copies(1 - slot, tm)


def _row_tile_n(r, n):
    return pl.ds(pl.multiple_of(r * ROW_TILES, ROW_TILES), n * ROW_TILES)


def _scatter_rows(cnt, plan, x_p, pos_p, x_s, pos_s, norm_ffn, n_blocks, tm):
    n_p, d = x_p.shape
    n_s = x_s.shape[0]
    assert n_p % tm == 0 and tm % DMA_ROWS_PER_ITER == 0 and n_s % DMA_ROWS_PER_ITER == 0 and n_s <= tm
    nt = n_p // tm
    smem = lambda: pl.BlockSpec(memory_space=pltpu.SMEM)
    tile_idx = lambda i: jnp.minimum(i, nt - 1)
    return pl.pallas_call(
        functools.partial(_scatter_rows_kernel, n_blocks),
        grid=(nt + 1,),
        in_specs=[smem(), smem(), pl.BlockSpec((tm, d), lambda i: (tile_idx(i), 0)),
                  pl.BlockSpec((TOP_K * tm,), lambda i: (tile_idx(i),), memory_space=pltpu.SMEM),
                  pl.BlockSpec((n_s, d), lambda i: (0, 0)), smem(), pl.BlockSpec((1, d), lambda i: (0, 0))],
        out_specs=pl.BlockSpec(memory_space=pl.ANY),
        out_shape=jax.ShapeDtypeStruct((n_blocks * BM_EXPERT * ROW_TILES, LANES), F32),
        scratch_shapes=[pltpu.VMEM((2, tm * ROW_TILES, LANES), F32), pltpu.VMEM((BM_EXPERT * ROW_TILES, LANES), F32),
                        pltpu.SemaphoreType.DMA((2,)), pltpu.SemaphoreType.DMA((1,))],
        compiler_params=_cparams("arbitrary"),
        name="moe_scatter_rows",
    )(cnt, plan, x_p, pos_p, x_s, pos_s, norm_ffn)


WEIGHT_CAST_ROWS = 128


def _expert_blocks_kernel(n_blocks, plan_ref, x_ref, wgu_ref, bgu_ref, wd_ref, bd_ref, y_ref, wgu_b, wd_b):
    bm = BM_EXPERT
    i = pl.program_id(0)

    @pl.when((i == 0) | (plan_ref[i] != plan_ref[jnp.maximum(i - 1, 0)]))
    def _():
        for src, dst in ((wgu_ref, wgu_b), (wd_ref, wd_b)):
            for r0 in range(0, src.shape[1], WEIGHT_CAST_ROWS):
                rows = slice(r0, r0 + WEIGHT_CAST_ROWS)
                dst[rows, :] = src[0, rows, :].astype(BF16)

    @pl.when(i < plan_ref[n_blocks])
    def _():
        x = jnp.concatenate([x_ref[pl.ds(c, bm, stride=ROW_TILES), :] for c in range(ROW_TILES)], axis=1)
        gu = jnp.dot(x.astype(BF16), wgu_b[...], preferred_element_type=F32) + bgu_ref[0]
        gate = jnp.minimum(gu[:, :D_FF], SWIGLU_LIMIT)
        up = jnp.clip(gu[:, D_FF:], -SWIGLU_LIMIT, SWIGLU_LIMIT)
        act = gate * jax.nn.sigmoid(SWIGLU_ALPHA * gate) * (up + 1.0)
        y = jnp.dot(act.astype(BF16), wd_b[...], preferred_element_type=F32) + bd_ref[0]
        for c in range(ROW_TILES):
            y_ref[pl.ds(c, bm, stride=ROW_TILES), :] = y[:, c * LANES:(c + 1) * LANES]

    @pl.when(i >= plan_ref[n_blocks])
    def _():
        y_ref[...] = jnp.zeros(y_ref.shape, F32)


def _expert_blocks(plan, h_sorted, wgu, bgu, wd, bd, n_blocks):
    bm = BM_EXPERT
    by_expert = lambda *shape: pl.BlockSpec((1,) + shape, lambda i, plan: (plan[i],) + (0,) * len(shape))
    used = lambda i, plan: (jnp.minimum(i, plan[n_blocks] - 1), 0)
    grid_spec = pltpu.PrefetchScalarGridSpec(
        num_scalar_prefetch=1,
        grid=(n_blocks,),
        in_specs=[pl.BlockSpec((bm * ROW_TILES, LANES), used), by_expert(D_MODEL, 2 * D_FF), by_expert(1, 2 * D_FF),
                  by_expert(D_FF, D_MODEL), by_expert(1, D_MODEL)],
        out_specs=pl.BlockSpec((bm * ROW_TILES, LANES), lambda i, plan: (i, 0)),
        scratch_shapes=[pltpu.VMEM(wgu.shape[1:], BF16), pltpu.VMEM(wd.shape[1:], BF16)],
    )
    return pl.pallas_call(
        functools.partial(_expert_blocks_kernel, n_blocks),
        grid_spec=grid_spec,
        out_shape=jax.ShapeDtypeStruct(h_sorted.shape, F32),
        compiler_params=_cparams("arbitrary"),
        name="moe_experts",
    )(plan, h_sorted, wgu, bgu, wd, bd)


def _gather_combine_kernel(x_ref, pos_ref, pos_next_ref, gate_ref, g_ref, y_hbm, o_ref, ybuf, sem):
    t = pl.program_id(0)
    n_tiles = pl.num_programs(0)
    tm = x_ref.shape[0]
    slot = t % 2

    def gather(rows_ref, sl):
        def body(it, carry):
            for u in range(DMA_ROWS_PER_ITER):
                j = it * DMA_ROWS_PER_ITER + u
                for k in range(TOP_K):
                    pltpu.make_async_copy(y_hbm.at[_row_tile(rows_ref[k * tm + j]), :],
                                          ybuf.at[sl, _row_tile(k * tm + j), :],
                                          sem.at[sl]).start(priority=k % N_DMA_THREADS)
            return carry
        lax.fori_loop(0, tm // DMA_ROWS_PER_ITER, body, 0)

    @pl.when(t == 0)
    def _():
        gather(pos_ref, 0)

    @pl.when(t + 1 < n_tiles)
    def _():
        gather(pos_next_ref, 1 - slot)

    pltpu.make_async_copy(y_hbm.at[pl.ds(0, TOP_K * tm * ROW_TILES), :], ybuf.at[slot], sem.at[slot]).wait()
    gates = gate_ref[...]
    cols = []
    for c in range(ROW_TILES):
        acc = x_ref[:, c * LANES:(c + 1) * LANES]
        for k in range(TOP_K):
            acc = acc + gates[:, k:k + 1] * ybuf[slot, pl.ds(k * tm * ROW_TILES + c, tm, stride=ROW_TILES), :]
        cols.append(acc)
    o_ref[...] = _rms(jnp.concatenate(cols, axis=1), g_ref[...])


def _gather_combine(x, pos, gates, norm_final, y_sorted, tm):
    n, d = x.shape
    assert n % tm == 0 and tm % DMA_ROWS_PER_ITER == 0
    nt = n // tm
    rows = lambda off: pl.BlockSpec((TOP_K * tm,), lambda i: (jnp.minimum(i + off, nt - 1),),
                                    memory_space=pltpu.SMEM)
    return pl.pallas_call(
        _gather_combine_kernel,
        grid=(nt,),
        in_specs=[pl.BlockSpec((tm, d), lambda i: (i, 0)), rows(0), rows(1),
                  pl.BlockSpec((tm, SUBLANES), lambda i: (i, 0)), pl.BlockSpec((1, d), lambda i: (0, 0)),
                  pl.BlockSpec(memory_space=pl.ANY)],
        out_specs=pl.BlockSpec((tm, d), lambda i: (i, 0)),
        out_shape=jax.ShapeDtypeStruct((n, d), F32),
        scratch_shapes=[pltpu.VMEM((2, TOP_K * tm * ROW_TILES, LANES), F32), pltpu.SemaphoreType.DMA((2,))],
        compiler_params=_cparams("arbitrary"),
        name="moe_combine",
    )(x, pos, pos, gates, norm_final, y_sorted)


def _moe_and_final_norm(x_p, routing_p, x_s, routing_s, norm_ffn, w_gate_up, b_gate_up, w_down, b_down, norm_final):
    n_p, n_s = x_p.shape[0], x_s.shape[0]
    n_rows = (n_p + n_s) * TOP_K
    n_blocks = -(-(n_rows + N_EXPERTS * (BM_EXPERT - 1)) // BM_EXPERT)
    g_ffn = norm_ffn[None]
    e_s, r_s, gates_s, _ = routing_s
    e_p, r_p, gates_p, cnt = routing_p
    plan = _moe_plan(cnt[:, 0], n_blocks)
    pos_p = _positions(plan, e_p, r_p, n_blocks, TM_COMBINE)
    pos_s = _positions(plan, e_s, r_s, n_blocks, n_s)
    h_sorted = _scatter_rows(cnt[:, 0], plan, x_p, pos_p, x_s, pos_s, g_ffn, n_blocks, TM_COMBINE)
    y_sorted = _expert_blocks(plan, h_sorted, w_gate_up, b_gate_up[:, None, :], w_down, b_down[:, None, :], n_blocks)
    y_p = _gather_combine(x_p, pos_p, gates_p, norm_final[None], y_sorted, TM_COMBINE)
    y_s = _gather_combine(x_s, pos_s, gates_s, norm_final[None], y_sorted, n_s)
    return y_p, y_s


def _rb(x):
    return x.astype(BF16).astype(F32)


def _sample_proj_kernel(x_ref, g_ref, wm_ref, wg_ref, z_ref):
    hb = _rms(x_ref[...], g_ref[...]).astype(BF16)
    z_ref[:, :MIX_WIDTH] = jnp.dot(hb, wm_ref[...], preferred_element_type=F32)
    z_ref[:, MIX_WIDTH:] = jnp.dot(hb, wg_ref[...], preferred_element_type=F32)


def _sample_proj(x, norm_mix, w_mix, w_gate):
    n, d = x.shape
    args = (x, norm_mix, w_mix, w_gate)
    width = w_mix.shape[1] + w_gate.shape[1]
    return pl.pallas_call(
        _sample_proj_kernel,
        grid=(1,),
        in_specs=[pl.BlockSpec(a.shape, lambda j: (0, 0)) for a in args],
        out_specs=pl.BlockSpec((n, width), lambda j: (0, 0)),
        out_shape=jax.ShapeDtypeStruct((n, width), F32),
        compiler_params=_cparams("arbitrary"),
        name="sample_proj",
    )(*args)


def _sample_mix_kernel(z_ref, c0_ref, c1_ref, c2_ref, mem_ref, sb0_ref, sb1_ref, sb2_ref, nb_ref, lng_ref, lnb_ref,
                       wsp_ref, bsp_ref, bo_ref, co_ref, a_ref, v_ref, n0_ref, n1_ref, n2_ref):
    z = z_ref[0]
    scale = 1.0 / math.sqrt(B_HEAD_DIM)
    row_head = lax.broadcasted_iota(I32, (SUBLANES, B_GROUP_W), 0)
    lane_head = lax.broadcasted_iota(I32, (SUBLANES, B_GROUP_W), 1) // B_HEAD_DIM
    own = (row_head == lane_head).astype(F32)
    kv_cols = jnp.transpose(jnp.broadcast_to(z[:, B_WIDTH:3 * B_WIDTH], (SUBLANES, 2 * B_WIDTH)))[:, 0:1]
    outs, lses = [], []
    for g, (cache_ref, sb_ref, new_ref) in enumerate(((c0_ref, sb0_ref, n0_ref), (c1_ref, sb1_ref, n1_ref),
                                                      (c2_ref, sb2_ref, n2_ref))):
        window = cache_ref.shape[2]
        q, kn, vn = (_rb(z[:, i * B_WIDTH + g * B_GROUP_W:i * B_WIDTH + (g + 1) * B_GROUP_W]) for i in range(3))
        qh = own * q
        kt = cache_ref[0, :B_GROUP_W, :].astype(BF16)
        vt = cache_ref[0, B_GROUP_W:, :].astype(BF16)
        lg = jnp.dot(qh.astype(BF16), kt, preferred_element_type=F32) * scale + sb_ref[...]
        ln = jnp.sum(qh * kn, axis=1, keepdims=True) * scale + nb_ref[g, :, 0:1]
        m = jnp.maximum(jnp.max(lg, axis=1, keepdims=True), ln)
        lse = m + jnp.log(jnp.sum(jnp.exp(lg - m), axis=1, keepdims=True) + jnp.exp(ln - m))
        p = jnp.exp(lg - lse).astype(BF16)
        pn = _rb(jnp.exp(ln - lse))
        pv = lax.dot_general(p, vt, (((1,), (1,)), ((), ())), preferred_element_type=F32) + pn * vn
        outs.append(jnp.sum(own * pv, axis=0, keepdims=True))
        lses.append(jnp.sum(own * lse, axis=0, keepdims=True))
        new_col = jnp.concatenate([kv_cols[i * B_WIDTH + g * B_GROUP_W:i * B_WIDTH + (g + 1) * B_GROUP_W]
                                   for i in range(2)], axis=0)
        lane = lax.broadcasted_iota(I32, (1, window), 1)
        new_ref[0] = jnp.where(lane == window - 1, new_col, pltpu.roll(cache_ref[0], window - 1, axis=1))
    m = jnp.maximum(jnp.maximum(lses[0], lses[1]), lses[2])
    ws = [jnp.exp(l - m) for l in lses]
    den = ws[0] + ws[1] + ws[2]
    bo_ref[0] = _rb(ws[0] / den) * _rb(outs[0]) + _rb(ws[1] / den) * _rb(outs[1]) + _rb(ws[2] / den) * _rb(outs[2])

    cs = []
    for h in range(C_HEADS):
        c0 = MIX_WIDTH - C_WIDTH + h * C_HEAD_DIM
        qh = _rb(z[:, c0:c0 + C_HEAD_DIM])
        kh = _rb(mem_ref[0, pl.ds(h, N_MEM, stride=2 * C_HEADS), :])
        vh = _rb(mem_ref[0, pl.ds(C_HEADS + h, N_MEM, stride=2 * C_HEADS), :])
        s = jnp.sum(kh * qh, axis=-1, keepdims=True) * (1.0 / math.sqrt(C_HEAD_DIM))
        p = jnp.exp(s - jnp.max(s, axis=0, keepdims=True))
        p = _rb(p / jnp.sum(p, axis=0, keepdims=True))
        cs.append(jnp.sum(p * vh, axis=0, keepdims=True))
    co_ref[0] = jnp.concatenate(cs, axis=1)

    u = _gelu(z[:, 3 * B_WIDTH:3 * B_WIDTH + A_WIDTH])
    v = _layernorm(_gelu(z[:, 3 * B_WIDTH + A_WIDTH:3 * B_WIDTH + 2 * A_WIDTH]), lng_ref[...], lnb_ref[...])
    v_ref[0] = v
    a_ref[0] = u * (_rb(wsp_ref[...]) * _rb(v) + bsp_ref[...])


def _sample_mix(z, caches, mem, sbias, nbias, ln_g, ln_b, wsp0, bsp0):
    n = z.shape[0]
    const = lambda arr: pl.BlockSpec(arr.shape, lambda i: (0,) * arr.ndim)
    row = lambda w: pl.BlockSpec((1, 1, w), lambda i: (i, 0, 0))
    per_req = lambda arr: pl.BlockSpec((1,) + arr.shape[1:], lambda i: (i, 0, 0))
    widths = (B_GROUP_W, C_WIDTH, A_WIDTH, A_WIDTH)
    return pl.pallas_call(
        _sample_mix_kernel,
        grid=(n,),
        in_specs=[row(z.shape[2])] + [per_req(c) for c in caches] + [per_req(mem)]
        + [const(a) for a in (*sbias, nbias, ln_g, ln_b, wsp0, bsp0)],
        out_specs=tuple(row(w) for w in widths) + tuple(per_req(c) for c in caches),
        out_shape=tuple(jax.ShapeDtypeStruct((n, 1, w), F32) for w in widths)
        + tuple(jax.ShapeDtypeStruct(c.shape, F32) for c in caches),
        compiler_params=_cparams("arbitrary"),
        name="sample_mix",
    )(z, *caches, mem, *sbias, nbias, ln_g, ln_b, wsp0, bsp0)


def _sample_merge_kernel(x_ref, z_ref, a_ref, b_ref, c_ref, wa_ref, wb_ref, wc_ref, wo_ref, o_ref):
    dot = lambda a, b: jnp.dot(a.astype(BF16), b, preferred_element_type=F32)
    merged = None
    for i, (act_ref, w_ref) in enumerate(((a_ref, wa_ref), (b_ref, wb_ref), (c_ref, wc_ref))):
        gate = jax.nn.sigmoid(z_ref[:, MIX_WIDTH + i * D_MODEL:MIX_WIDTH + (i + 1) * D_MODEL])
        term = gate * dot(act_ref[...], w_ref[...])
        merged = term if merged is None else merged + term
    o_ref[...] = x_ref[...] + dot(merged, wo_ref[...])


def _sample_merge(x, z, a, bo, co, wa, wb, wc, wo):
    args = (x, z, a, bo, co, wa, wb, wc, wo)
    return pl.pallas_call(
        _sample_merge_kernel,
        grid=(1,),
        in_specs=[pl.BlockSpec(arr.shape, lambda i: (0, 0)) for arr in args],
        out_specs=pl.BlockSpec(x.shape, lambda i: (0, 0)),
        out_shape=jax.ShapeDtypeStruct(x.shape, F32),
        compiler_params=_cparams("arbitrary"),
        name="sample_merge",
    )(*args)


def _sample_mixers(x_sample, caches, cache_mem, sbias, nbias, norm_mix, ln_v_g, ln_v_b, w_spatial, b_spatial, wts):
    n = x_sample.shape[0]
    x = x_sample[:, 0]
    z = _sample_proj(x, norm_mix[None], wts["mix"], wts["gate"])
    mem = cache_mem.reshape(n, N_MEM * 2 * C_HEADS, C_HEAD_DIM)
    wsp0 = jnp.repeat(w_spatial[:, 0, 0], LANES)[None]
    bsp0 = jnp.repeat(b_spatial[:, 0], LANES)[None]
    bo, co, a, v, *new_caches = _sample_mix(z[:, None], [_from_window_layout(c) for c in caches], mem, sbias, nbias,
                                            ln_v_g[None], ln_v_b[None], wsp0, bsp0)
    x_new = _sample_merge(x, z, a[:, 0], bo[:, 0], co[:, 0], wts["a"], wts["b"], wts["c"], wts["out"])
    return x_new, [_to_window_layout(c) for c in new_caches], v[:, 0]


def _mixer_weights(w_in, w_branch_a, w_branch_b, w_branch_c, w_out):
    return {"mix": w_in[:, :MIX_WIDTH].astype(BF16), "gate": w_in[:, MIX_WIDTH:].astype(BF16),
            "a": w_branch_a.astype(BF16), "b": w_branch_b.astype(BF16), "c": w_branch_c.astype(BF16),
            "out": w_out.astype(BF16)}


def kernel(x_prompt, x_sample, cache_win0_kv, cache_win1_kv, cache_win2_kv, cache_mem_kv, mem_prompt, rel_bias,
           norm_mix, norm_mem, w_in, ln_v_g, ln_v_b, w_spatial, b_spatial, w_mem_kv, w_branch_a, w_branch_b,
           w_branch_c, w_out, norm_ffn, w_router, b_router, w_gate_up, b_gate_up, w_down, b_down, norm_final):
    assert norm_mix.shape[0] == 1, "one layer"
    b, s, d = x_prompt.shape
    n_s = x_sample.shape[0]
    caches = (cache_win0_kv[0], cache_win1_kv[0], cache_win2_kv[0])
    assert all(c.shape[1] == w for c, (w, _) in zip(caches, B_PAIRS)), "window buffers hold a full window"
    wts = _mixer_weights(w_in[0], w_branch_a[0], w_branch_b[0], w_branch_c[0], w_out[0])
    band_bias, *sbias, nbias = _bias_tables(rel_bias)
    route_params = _route_params(norm_ffn[0], w_router[0], b_router[0])
    xs, win_s, v_s = _sample_mixers(x_sample, caches, cache_mem_kv[0], sbias, nbias, norm_mix[0], ln_v_g[0],
                                    ln_v_b[0], w_spatial[0], b_spatial[0], wts)
    routing_s = _route(xs, route_params, jnp.zeros((N_EXPERTS, LANES), I32), n_s)
    xp, routing_p, win_p, mem_p = _prompt_mixers(x_prompt, mem_prompt, band_bias, norm_mix[0], norm_mem[0], ln_v_g[0],
                                                 ln_v_b[0], w_spatial[0], b_spatial[0], w_mem_kv[0], wts,
                                                 route_params, routing_s[3])
    y_p, y_s = _moe_and_final_norm(xp.reshape(b * s, d), routing_p, xs, routing_s, norm_ffn[0], w_gate_up[0],
                                   b_gate_up[0], w_down[0], b_down[0], norm_final)
    chunk_v = v_s.reshape(1, n_s, 1, A_GROUPS, LANES)
    return (y_p.reshape(b, s, d), y_s[:, None], win_p[0], win_p[1], win_p[2], mem_p,
            win_s[0], win_s[1], win_s[2], chunk_v)
```

```python
import functools
import math

import numpy as np
import jax
import jax.numpy as jnp
from jax import lax
from jax.experimental import pallas as pl
from jax.experimental.pallas import tpu as pltpu

F32 = jnp.float32
BF16 = jnp.bfloat16
I32 = jnp.int32

D_MODEL = 1024
N_MEM = 256
CHUNK = 128
A_GROUPS = 4
A_WIDTH = 512
B_PAIRS = ((128, 1), (512, 4), (2048, 16))
B_GROUP_W = 256
B_WIDTH = 768
B_HEAD_DIM = 64
BAND = 128
C_HEADS = 4
C_HEAD_DIM = 128
C_WIDTH = 512
REL_BUCKETS = 32
REL_MAX_DIST = 2048
N_EXPERTS = 32
TOP_K = 4
D_FF = 1024
SWIGLU_LIMIT = 7.0
SWIGLU_ALPHA = 1.702
EPS = 1e-6
NEG_INF = -1e30
MIX_WIDTH = 3 * B_WIDTH + 2 * A_WIDTH + C_WIDTH

LANES = 128
SUBLANES = 8
ROW_TILES = D_MODEL // LANES
VMEM_LIMIT = 56 * 1024 * 1024

TM_PROJ = 1024
TM_MERGE = 512
BM_EXPERT = 512
TM_COMBINE = 512
ATTN_BLOCKS_PER_ITER = 4


def _cparams(*sem):
    return pltpu.CompilerParams(dimension_semantics=sem, vmem_limit_bytes=VMEM_LIMIT)


def _t5_bucket(dist):
    dist = np.maximum(np.asarray(dist), 0)
    max_exact = REL_BUCKETS // 2
    log_ratio = np.log(np.maximum(dist, max_exact) / max_exact) / math.log(REL_MAX_DIST / max_exact)
    large = np.minimum(max_exact + (log_ratio * (REL_BUCKETS - max_exact)).astype(np.int32), REL_BUCKETS - 1)
    return np.where(dist < max_exact, dist, large).astype(np.int32)


def _gelu(x):
    c = math.sqrt(2.0 / math.pi)
    return x * (0.5 * (1.0 + jnp.tanh(c * (x + 0.044715 * (x * x * x)))))


def _rms(x, g):
    return x * lax.rsqrt(jnp.mean(x * x, axis=-1, keepdims=True) + EPS) * g


def _layernorm(x, g, b):
    xc = x - jnp.mean(x, axis=-1, keepdims=True)
    return xc * lax.rsqrt(jnp.mean(xc * xc, axis=-1, keepdims=True) + EPS) * g + b


def _proj_kernel(x_ref, g_ref, w_ref, lng_ref, lnb_ref, wsp_ref, bsp_ref,
                 qkv0_ref, qkv1_ref, qkv2_ref, win0_ref, win1_ref, win2_ref, a_ref, qc_ref, scr_ref):
    t = pl.program_id(1)
    last = pl.num_programs(1) - 1
    tm = x_ref.shape[1]
    hb = _rms(x_ref[0], g_ref[...]).astype(BF16)

    def proj(c0, width):
        return jnp.dot(hb, w_ref[:, c0:c0 + width], preferred_element_type=F32)

    for g, (_, dil) in enumerate(B_PAIRS):
        q, k, v = (proj(i * B_WIDTH + g * B_GROUP_W, B_GROUP_W) for i in range(3))
        kv = jnp.concatenate([k, v], axis=1)
        if g == 2:
            win2_ref[0] = jnp.transpose(kv)
        elif g == 1:
            @pl.when(t == last)
            def _():
                win1_ref[0] = jnp.transpose(kv[tm - B_PAIRS[1][0]:])
        else:
            @pl.when(t == last)
            def _():
                win0_ref[0] = jnp.transpose(kv[tm - B_PAIRS[0][0]:])
        if dil == 1:
            qkv0_ref[0] = jnp.concatenate([q, k, v], axis=1).astype(BF16)
            continue
        out_ref = qkv1_ref if g == 1 else qkv2_ref
        for i, arr in enumerate((q, k, v)):
            for ct in range(2):
                scr_ref[2 * i + ct] = arr[:, ct * LANES:(ct + 1) * LANES]
        for r in range(dil):
            rows = [scr_ref[j, pl.ds(r, tm // dil, stride=dil), :] for j in range(6)]
            out_ref[0, r] = jnp.concatenate(rows, axis=1).astype(BF16)

    u = _gelu(proj(3 * B_WIDTH, A_WIDTH))
    v = _layernorm(_gelu(proj(3 * B_WIDTH + A_WIDTH, A_WIDTH)), lng_ref[...], lnb_ref[...]).astype(BF16)
    for c in range(tm // CHUNK):
        rs = slice(c * CHUNK, (c + 1) * CHUNK)
        for g in range(A_GROUPS):
            cs = slice(g * LANES, (g + 1) * LANES)
            s = jnp.dot(wsp_ref[g], v[rs, cs], preferred_element_type=F32) + bsp_ref[:, g:g + 1]
            a_ref[0, rs, cs] = (u[rs, cs] * s).astype(BF16)

    qc_ref[0] = proj(3 * B_WIDTH + 2 * A_WIDTH, C_WIDTH).astype(BF16)


def _prompt_proj(x, norm_mix, w_mix, ln_g, ln_b, wsp, bsp):
    b, s, d = x.shape
    tm = TM_PROJ
    assert s % tm == 0 and s >= B_PAIRS[2][0] and tm >= B_PAIRS[1][0]
    nt = s // tm
    const = lambda *shape: pl.BlockSpec(shape, lambda i, j: (0,) * len(shape))
    out_shape = (
        jax.ShapeDtypeStruct((b, s, 3 * B_GROUP_W), BF16),
        jax.ShapeDtypeStruct((b, 4, s // 4, 3 * B_GROUP_W), BF16),
        jax.ShapeDtypeStruct((b, 16, s // 16, 3 * B_GROUP_W), BF16),
        jax.ShapeDtypeStruct((b, 2 * B_GROUP_W, B_PAIRS[0][0]), F32),
        jax.ShapeDtypeStruct((b, 2 * B_GROUP_W, B_PAIRS[1][0]), F32),
        jax.ShapeDtypeStruct((b, 2 * B_GROUP_W, s), F32),
        jax.ShapeDtypeStruct((b, s, A_WIDTH), BF16),
        jax.ShapeDtypeStruct((b, s, C_WIDTH), BF16),
    )
    out_specs = (
        pl.BlockSpec((1, tm, 3 * B_GROUP_W), lambda i, j: (i, j, 0)),
        pl.BlockSpec((1, 4, tm // 4, 3 * B_GROUP_W), lambda i, j: (i, 0, j, 0)),
        pl.BlockSpec((1, 16, tm // 16, 3 * B_GROUP_W), lambda i, j: (i, 0, j, 0)),
        pl.BlockSpec((1, 2 * B_GROUP_W, B_PAIRS[0][0]), lambda i, j: (i, 0, 0)),
        pl.BlockSpec((1, 2 * B_GROUP_W, B_PAIRS[1][0]), lambda i, j: (i, 0, 0)),
        pl.BlockSpec((1, 2 * B_GROUP_W, tm), lambda i, j: (i, 0, j)),
        pl.BlockSpec((1, tm, A_WIDTH), lambda i, j: (i, j, 0)),
        pl.BlockSpec((1, tm, C_WIDTH), lambda i, j: (i, j, 0)),
    )
    return pl.pallas_call(
        _proj_kernel,
        grid=(b, nt),
        in_specs=[pl.BlockSpec((1, tm, d), lambda i, j: (i, j, 0)), const(1, d), const(d, MIX_WIDTH),
                  const(1, A_WIDTH), const(1, A_WIDTH), const(A_GROUPS, CHUNK, CHUNK), const(CHUNK, A_GROUPS)],
        out_specs=out_specs,
        out_shape=out_shape,
        scratch_shapes=[pltpu.VMEM((6, tm, LANES), F32)],
        compiler_params=_cparams("arbitrary", "arbitrary"),
        name="prompt_proj",
    )(x, norm_mix, w_mix, ln_g, ln_b, wsp, bsp)


def _attend(q, k, v, bias_ref, g, k0, hmask):
    kn = k.shape[0]
    qs = jnp.concatenate([q * hm for hm in hmask], axis=0)
    s = lax.dot_general(qs, k, (((1,), (1,)), ((), ())), preferred_element_type=F32) + bias_ref[g, :, k0:k0 + kn]
    m = jnp.max(s, axis=-1, keepdims=True)
    p = jnp.exp(s - m)
    l = jnp.sum(p, axis=-1, keepdims=True)
    pv = jnp.dot(p.astype(BF16), v, preferred_element_type=F32) / l
    lse_rows = m + jnp.log(l)
    out = jnp.zeros((BAND, B_GROUP_W), F32)
    lse = jnp.zeros((BAND, B_GROUP_W), F32)
    for h, hm in enumerate(hmask):
        rows = slice(h * BAND, (h + 1) * BAND)
        sel = hm > 0
        out = jnp.where(sel, pv[rows], out)
        lse = jnp.where(sel, lse_rows[rows], lse)
    return out, lse


def _attn_kernel(q0_ref, q1_ref, q2_ref, bias_ref, o_ref, out_ref, lse_ref):
    s = q0_ref.shape[1]
    lane_head = lax.broadcasted_iota(I32, (1, B_GROUP_W), 1) // B_HEAD_DIM
    scale = 1.0 / math.sqrt(B_HEAD_DIM)
    hmask = [jnp.where(lane_head == h, scale, 0.0).astype(BF16) for h in range(4)]
    qs, ks, vs = (slice(i * B_GROUP_W, (i + 1) * B_GROUP_W) for i in range(3))

    def store(g, start, dil, out, lse):
        rows = pl.ds(start, BAND) if dil == 1 else pl.ds(start, BAND, stride=dil)
        for ct in range(2):
            out_ref[g, ct, rows, :] = out[:, ct * LANES:(ct + 1) * LANES]
            lse_ref[g, ct, rows, :] = lse[:, ct * LANES:(ct + 1) * LANES]

    for g, (_, dil) in enumerate(B_PAIRS):
        n = s // dil
        nb = n // BAND

        def load(rows, cols, r, g=g):
            if g == 0:
                return q0_ref[0, rows, cols]
            return (q1_ref if g == 1 else q2_ref)[0, r, rows, cols]

        def first_block(r, g=g, dil=dil, load=load):
            rows = pl.ds(0, BAND)
            out, lse = _attend(load(rows, qs, r), load(rows, ks, r), load(rows, vs, r), bias_ref, g, BAND, hmask)
            store(g, r, dil, out, lse)

        def later_block(i, g=g, dil=dil, nb=nb, load=load):
            r = i // (nb - 1)
            qb = i % (nb - 1) + 1
            q0 = pl.multiple_of(qb * BAND, BAND)
            rows_q = pl.ds(q0, BAND)
            rows_k = pl.ds(q0 - BAND, 2 * BAND)
            out, lse = _attend(load(rows_q, qs, r), load(rows_k, ks, r), load(rows_k, vs, r), bias_ref, g, 0, hmask)
            store(g, qb * (BAND * dil) + r, dil, out, lse)

        def run(block_fn, count):
            per = ATTN_BLOCKS_PER_ITER

            def body(it, carry):
                for u in range(per):
                    block_fn(it * per + u)
                return carry
            lax.fori_loop(0, count // per, body, 0)
            for i in range(count - count % per, count):
                block_fn(i)

        run(first_block, dil)
        run(later_block, dil * (nb - 1))

    rc = 256
    def combine(c, carry):
        rows = pl.ds(pl.multiple_of(c * rc, rc), rc)
        for ct in range(2):
            ls = [lse_ref[g, ct, rows, :] for g in range(3)]
            m = jnp.maximum(jnp.maximum(ls[0], ls[1]), ls[2])
            ws = [jnp.exp(l - m) for l in ls]
            den = ws[0] + ws[1] + ws[2]
            num = ws[0] * out_ref[0, ct, rows, :] + ws[1] * out_ref[1, ct, rows, :] + ws[2] * out_ref[2, ct, rows, :]
            o_ref[0, rows, ct * LANES:(ct + 1) * LANES] = (num / den).astype(BF16)
        return carry
    lax.fori_loop(0, s // rc, combine, 0)


def _prompt_attention(qkv0, qkv1, qkv2, bias):
    b, s, w = qkv0.shape
    return pl.pallas_call(
        _attn_kernel,
        grid=(b,),
        in_specs=[pl.BlockSpec((1, s, w), lambda i: (i, 0, 0)),
                  pl.BlockSpec((1, 4, s // 4, w), lambda i: (i, 0, 0, 0)),
                  pl.BlockSpec((1, 16, s // 16, w), lambda i: (i, 0, 0, 0)),
                  pl.BlockSpec(bias.shape, lambda i: (0, 0, 0))],
        out_specs=pl.BlockSpec((1, s, B_GROUP_W), lambda i: (i, 0, 0)),
        out_shape=jax.ShapeDtypeStruct((b, s, B_GROUP_W), BF16),
        scratch_shapes=[pltpu.VMEM((3, 2, s, LANES), F32), pltpu.VMEM((3, 2, s, LANES), F32)],
        compiler_params=_cparams("arbitrary"),
        name="prompt_attention",
    )(qkv0, qkv1, qkv2, bias)


def _bias_kernel(rel_ref, band_idx_ref, s0_ref, s1_ref, s2_ref, band_ref, sb0_ref, sb1_ref, sb2_ref, nb_ref):
    def lookup(idx, col):
        acc = jnp.full(idx.shape, NEG_INF, F32)
        for bucket in range(REL_BUCKETS):
            acc = jnp.where(idx == bucket, rel_ref[bucket, col], acc)
        return acc

    for g, (s_ref, sb_ref) in enumerate(((s0_ref, sb0_ref), (s1_ref, sb1_ref), (s2_ref, sb2_ref))):
        width = s_ref.shape[1]
        for h in range(4):
            band_ref[g, h * BAND:(h + 1) * BAND, :] = lookup(band_idx_ref[g], 4 * g + h)
        sb_ref[...] = jnp.concatenate([lookup(s_ref[...], 4 * g + h) for h in range(4)]
                                      + [jnp.full((SUBLANES - 4, width), NEG_INF, F32)], axis=0)
        nb_ref[g] = jnp.concatenate([jnp.full((1, LANES), rel_ref[0, 4 * g + h], F32) for h in range(4)]
                                    + [jnp.zeros((SUBLANES - 4, LANES), F32)], axis=0)


def _bias_tables(rel_bias):
    steps = np.arange(BAND)[:, None] + BAND - np.arange(2 * BAND)[None, :]
    valid = (steps >= 0) & (steps <= BAND)
    band_idx = np.stack([np.where(valid, _t5_bucket(np.clip(steps, 0, BAND) * dil), -1) for _, dil in B_PAIRS])
    s_idx = []
    for window, dil in B_PAIRS:
        w = np.arange(window)
        s_idx.append(np.where(w % dil == 0, _t5_bucket(window - w), -1)[None].astype(np.int32))
    args = (rel_bias, jnp.asarray(band_idx.astype(np.int32))) + tuple(jnp.asarray(s) for s in s_idx)
    vmem = lambda a: pl.BlockSpec(a.shape, lambda i: (0,) * a.ndim)
    out_shape = (jax.ShapeDtypeStruct((3, 4 * BAND, 2 * BAND), F32),) + tuple(
        jax.ShapeDtypeStruct((SUBLANES, window), F32) for window, _ in B_PAIRS) + (
        jax.ShapeDtypeStruct((3, SUBLANES, LANES), F32),)
    return pl.pallas_call(
        _bias_kernel,
        grid=(1,),
        in_specs=[pl.BlockSpec(memory_space=pltpu.SMEM)] + [vmem(a) for a in args[1:]],
        out_specs=tuple(pl.BlockSpec(s.shape, lambda i, n=len(s.shape): (0,) * n) for s in out_shape),
        out_shape=out_shape,
        compiler_params=_cparams("arbitrary"),
        name="bias_tables",
    )(*args)


def _memkv_kernel(mem_ref, g_ref, w_ref, kv_ref, kvb_ref):
    hb = _rms(mem_ref[0], g_ref[...]).astype(BF16)
    kv = jnp.dot(hb, w_ref[...], preferred_element_type=F32)
    kv_ref[0] = kv
    kvb_ref[0] = kv.astype(BF16)


def _memory_kv(mem, norm_mem, w_mem):
    b, m, d = mem.shape
    w = w_mem.shape[1]
    return pl.pallas_call(
        _memkv_kernel,
        grid=(b,),
        in_specs=[pl.BlockSpec((1, m, d), lambda i: (i, 0, 0)), pl.BlockSpec((1, d), lambda i: (0, 0)),
                  pl.BlockSpec((d, w), lambda i: (0, 0))],
        out_specs=(pl.BlockSpec((1, m, w), lambda i: (i, 0, 0)), pl.BlockSpec((1, m, w), lambda i: (i, 0, 0))),
        out_shape=(jax.ShapeDtypeStruct((b, m, w), F32), jax.ShapeDtypeStruct((b, m, w), BF16)),
        compiler_params=_cparams("arbitrary"),
        name="memory_kv",
    )(mem, norm_mem, w_mem)


def _merge_kernel(x_ref, g_ref, a_ref, b_ref, qc_ref, kv_ref, wg_ref, wa_ref, wb_ref, wc_ref, wo_ref,
                  gf_ref, wr_ref, br_ref, cnt_in_ref, o_ref, e_ref, rank_ref, gate_ref, cnt_ref, carry_ref):
    x = x_ref[0]
    hb = _rms(x, g_ref[...]).astype(BF16)
    qc = qc_ref[0]
    cs = []
    for h in range(C_HEADS):
        hs = slice(h * C_HEAD_DIM, (h + 1) * C_HEAD_DIM)
        s = lax.dot_general(qc[:, hs], kv_ref[0, :, hs], (((1,), (1,)), ((), ())), preferred_element_type=F32)
        s = s * (1.0 / math.sqrt(C_HEAD_DIM))
        p = jnp.exp(s - jnp.max(s, axis=-1, keepdims=True))
        p = p / jnp.sum(p, axis=-1, keepdims=True)
        vs = slice(C_WIDTH + h * C_HEAD_DIM, C_WIDTH + (h + 1) * C_HEAD_DIM)
        cs.append(jnp.dot(p.astype(BF16), kv_ref[0, :, vs], preferred_element_type=F32))
    c = jnp.concatenate(cs, axis=1).astype(BF16)
    branches = ((a_ref[0], wa_ref), (b_ref[0], wb_ref), (c, wc_ref))
    merged = None
    for i, (act, w_ref) in enumerate(branches):
        gate = jax.nn.sigmoid(jnp.dot(hb, wg_ref[:, i * D_MODEL:(i + 1) * D_MODEL], preferred_element_type=F32))
        term = gate * jnp.dot(act, w_ref[...], preferred_element_type=F32)
        merged = term if merged is None else merged + term
    x_new = x + jnp.dot(merged.astype(BF16), wo_ref[...], preferred_element_type=F32)
    o_ref[0] = x_new
    first_step = (pl.program_id(0) == 0) & (pl.program_id(1) == 0)
    _route_rows(x_new, first_step, gf_ref, wr_ref, br_ref, cnt_in_ref, e_ref, rank_ref, gate_ref, cnt_ref, carry_ref)


def _prompt_merge(x, norm_mix, a, bo, qc, kvb, wg, wa, wb, wc, wo, route_params, cnt_in):
    b, s, d = x.shape
    tm = TM_MERGE
    nt = s // tm
    n = b * s
    tile = lambda w: pl.BlockSpec((1, tm, w), lambda i, j: (i, j, 0))
    const = lambda arr: pl.BlockSpec(arr.shape, lambda i, j: (0,) * arr.ndim)
    ids = pl.BlockSpec((TOP_K, tm), lambda i, j: (0, i * nt + j))
    return pl.pallas_call(
        _merge_kernel,
        grid=(b, nt),
        in_specs=[tile(d), const(norm_mix), tile(A_WIDTH), tile(B_GROUP_W), tile(C_WIDTH),
                  pl.BlockSpec((1,) + kvb.shape[1:], lambda i, j: (i, 0, 0)),
                  const(wg), const(wa), const(wb), const(wc), const(wo)]
        + [const(p) for p in route_params] + [const(cnt_in)],
        out_specs=(tile(d), ids, ids, pl.BlockSpec((tm, SUBLANES), lambda i, j: (i * nt + j, 0)),
                   pl.BlockSpec((N_EXPERTS, LANES), lambda i, j: (0, 0))),
        out_shape=(jax.ShapeDtypeStruct((b, s, d), F32), jax.ShapeDtypeStruct((TOP_K, n), I32),
                   jax.ShapeDtypeStruct((TOP_K, n), I32), jax.ShapeDtypeStruct((n, SUBLANES), F32),
                   jax.ShapeDtypeStruct((N_EXPERTS, LANES), I32)),
        scratch_shapes=[pltpu.VMEM((N_EXPERTS, LANES), F32)],
        compiler_params=_cparams("arbitrary", "arbitrary"),
        name="prompt_merge",
    )(x, norm_mix, a, bo, qc, kvb, wg, wa, wb, wc, wo, *route_params, cnt_in)


def _to_window_layout(w):
    b, _, window = w.shape
    return jnp.transpose(w.reshape(b, 2, 4, B_HEAD_DIM, window), (0, 4, 1, 2, 3))[None]


def _from_window_layout(c):
    b, window = c.shape[:2]
    return jnp.transpose(c, (0, 2, 3, 4, 1)).reshape(b, 2 * B_GROUP_W, window)


def _prompt_mixers(x_prompt, mem_prompt, band_bias, norm_mix, norm_mem, ln_v_g, ln_v_b, w_spatial, b_spatial,
                   w_mem_kv, wts, route_params, cnt_in):
    b, s, _ = x_prompt.shape
    wsp = jnp.where(np.tril(np.ones((CHUNK, CHUNK), bool)), w_spatial, 0.0).astype(BF16)
    qkv0, qkv1, qkv2, win0, win1, win2, a_out, qc = _prompt_proj(
        x_prompt, norm_mix[None], wts["mix"], ln_v_g[None], ln_v_b[None], wsp, b_spatial.T)
    b_out = _prompt_attention(qkv0, qkv1, qkv2, band_bias)
    mem_kv, mem_kv_b = _memory_kv(mem_prompt, norm_mem[None], w_mem_kv.astype(BF16))
    x_new, *routing = _prompt_merge(x_prompt, norm_mix[None], a_out, b_out, qc, mem_kv_b, wts["gate"],
                                    wts["a"], wts["b"], wts["c"], wts["out"], route_params, cnt_in)
    wins = [_to_window_layout(w) for w in (win0, win1, win2)]
    return x_new, routing, wins, mem_kv.reshape(1, b, N_MEM, 2, C_HEADS, C_HEAD_DIM)


def _route_kernel(x_ref, g_ref, wr_ref, br_ref, cnt_in_ref, e_ref, rank_ref, gate_ref, cnt_ref, carry_ref):
    _route_rows(x_ref[...], pl.program_id(0) == 0, g_ref, wr_ref, br_ref, cnt_in_ref, e_ref, rank_ref, gate_ref,
                cnt_ref, carry_ref)


def _route_rows(x, first_step, g_ref, wr_ref, br_ref, cnt_in_ref, e_ref, rank_ref, gate_ref, cnt_ref, carry_ref):
    tm = x.shape[0]

    @pl.when(first_step)
    def _():
        carry_ref[...] = cnt_in_ref[...].astype(F32)

    h = _rms(x, g_ref[...])
    logits = lax.dot_general(wr_ref[...], h.astype(BF16), (((1,), (1,)), ((), ())),
                             preferred_element_type=F32) + br_ref[...]
    expert = lax.broadcasted_iota(I32, logits.shape, 0)
    vals, idxs = [], []
    member = jnp.zeros(logits.shape, F32)
    for _ in range(TOP_K):
        m = jnp.max(logits, axis=0, keepdims=True)
        idx = jnp.min(jnp.where(logits == m, expert, N_EXPERTS), axis=0, keepdims=True)
        hit = expert == idx
        vals.append(m)
        idxs.append(idx)
        member = jnp.where(hit, 1.0, member)
        logits = jnp.where(hit, -jnp.inf, logits)
    p = [jnp.exp(v - vals[0]) for v in vals]
    den = p[0] + p[1] + p[2] + p[3]
    gates = jnp.concatenate([pk / den for pk in p] + [jnp.zeros((SUBLANES - TOP_K, tm), F32)], axis=0)
    gate_ref[...] = jnp.transpose(gates)
    e_ref[...] = jnp.concatenate(idxs, axis=0)
    before = (lax.broadcasted_iota(I32, (tm, tm), 0) < lax.broadcasted_iota(I32, (tm, tm), 1)).astype(BF16)
    prefix = jnp.dot(member.astype(BF16), before, preferred_element_type=F32) + carry_ref[:, 0:1]
    rank_ref[...] = jnp.concatenate(
        [jnp.sum(jnp.where(expert == idx, prefix, 0.0), axis=0, keepdims=True) for idx in idxs], axis=0).astype(I32)
    carry_ref[...] = carry_ref[...] + jnp.sum(member, axis=1, keepdims=True)
    cnt_ref[...] = carry_ref[...].astype(I32)


def _route_params(norm_ffn, w_router, b_router):
    return norm_ffn[None], w_router.T.astype(BF16), b_router[:, None]


def _route(x, route_params, cnt_in, tm):
    n, d = x.shape
    assert n % tm == 0
    norm_ffn, w_router_t, b_router = route_params
    const = lambda arr: pl.BlockSpec(arr.shape, lambda i: (0,) * arr.ndim)
    return pl.pallas_call(
        _route_kernel,
        grid=(n // tm,),
        in_specs=[pl.BlockSpec((tm, d), lambda i: (i, 0)), const(norm_ffn), const(w_router_t), const(b_router),
                  const(cnt_in)],
        out_specs=(pl.BlockSpec((TOP_K, tm), lambda i: (0, i)), pl.BlockSpec((TOP_K, tm), lambda i: (0, i)),
                   pl.BlockSpec((tm, SUBLANES), lambda i: (i, 0)), pl.BlockSpec((N_EXPERTS, LANES), lambda i: (0, 0))),
        out_shape=(jax.ShapeDtypeStruct((TOP_K, n), I32), jax.ShapeDtypeStruct((TOP_K, n), I32),
                   jax.ShapeDtypeStruct((n, SUBLANES), F32), jax.ShapeDtypeStruct((N_EXPERTS, LANES), I32)),
        scratch_shapes=[pltpu.VMEM((N_EXPERTS, LANES), F32)],
        compiler_params=_cparams("arbitrary"),
        name="moe_route",
    )(x, norm_ffn, w_router_t, b_router, cnt_in)


DMA_ROWS_PER_ITER = 2
N_DMA_THREADS = 2
N_PLAN_TAIL = 1 + N_EXPERTS


def _row_tile(r):
    return pl.ds(pl.multiple_of(r * ROW_TILES, ROW_TILES), ROW_TILES)


def _plan_kernel(n_blocks, cnt_ref, plan_ref):
    shift = BM_EXPERT.bit_length() - 1

    def per_expert(e, carry):
        blk0, last_e = carry
        nb = (cnt_ref[e] + (BM_EXPERT - 1)) >> shift
        plan_ref[n_blocks + 1 + e] = blk0 << shift

        def fill(j, cc):
            plan_ref[blk0 + j] = e
            return cc
        lax.fori_loop(0, nb, fill, 0)
        return blk0 + nb, jnp.where(nb > 0, e, last_e)
    n_used, last_e = lax.fori_loop(0, N_EXPERTS, per_expert, (0, 0))

    def tail(j, cc):
        plan_ref[j] = last_e
        return cc
    lax.fori_loop(n_used, n_blocks, tail, 0)
    plan_ref[n_blocks] = n_used


def _moe_plan(cnt, n_blocks):
    smem = pl.BlockSpec(memory_space=pltpu.SMEM)
    return pl.pallas_call(
        functools.partial(_plan_kernel, n_blocks),
        in_specs=[smem], out_specs=smem,
        out_shape=jax.ShapeDtypeStruct((n_blocks + N_PLAN_TAIL,), I32),
        name="moe_plan",
    )(cnt)


POSITION_TILES_PER_STEP = 16


def _positions_kernel(n_blocks, plan_ref, e_ref, r_ref, pos_ref):
    tiles, _, tm = pos_ref.shape
    e = e_ref[...]
    first = jnp.zeros(e.shape, I32)
    for ex in range(N_EXPERTS):
        first = jnp.where(e == ex, plan_ref[n_blocks + 1 + ex], first)
    pos = first + r_ref[...]
    for i in range(tiles):
        pos_ref[i] = pos[:, i * tm:(i + 1) * tm]


def _positions(plan, e, r, n_blocks, tm):
    n = e.shape[1]
    assert n % tm == 0
    nt = n // tm
    per_step = math.gcd(nt, POSITION_TILES_PER_STEP)
    pos = pl.pallas_call(
        functools.partial(_positions_kernel, n_blocks),
        grid=(nt // per_step,),
        in_specs=[pl.BlockSpec(memory_space=pltpu.SMEM), pl.BlockSpec((TOP_K, per_step * tm), lambda i: (0, i)),
                  pl.BlockSpec((TOP_K, per_step * tm), lambda i: (0, i))],
        out_specs=pl.BlockSpec((per_step, TOP_K, tm), lambda i: (i, 0, 0)),
        out_shape=jax.ShapeDtypeStruct((nt, TOP_K, tm), I32),
        compiler_params=_cparams("arbitrary"),
        name="moe_positions",
    )(plan, e, r)
    return pos.reshape(-1)


def _scatter_rows_kernel(n_blocks, cnt_ref, plan_ref, x_ref, pos_ref, xs_ref, poss_ref, g_ref, h_hbm,
                         hbuf, zbuf, sem, zsem):
    t = pl.program_id(0)
    n_tiles = pl.num_programs(0) - 1
    tm = x_ref.shape[0]
    n_s = xs_ref.shape[0]
    bm = BM_EXPERT
    shift = bm.bit_length() - 1
    slot = t % 2

    def zero_rows(first, count, wait):
        for bit in range(shift):
            n = 1 << bit
            lo = first + (count & (n - 1))

            @pl.when((count >> bit) & 1 == 1)
            def _():
                cp = pltpu.make_async_copy(zbuf.at[pl.ds(0, n * ROW_TILES), :], h_hbm.at[_row_tile_n(lo, n), :],
                                           zsem.at[0])
                cp.wait() if wait else cp.start()

    def zero_block(blk, wait):
        cp = pltpu.make_async_copy(zbuf, h_hbm.at[_row_tile_n(blk * bm, bm), :], zsem.at[0])
        cp.wait() if wait else cp.start()

    @pl.when(t == 0)
    def _():
        zbuf[...] = jnp.zeros(zbuf.shape, F32)
        for wait in (False, True):
            def pad_expert(e, cc, wait=wait):
                cnt = cnt_ref[e]
                zero_rows(plan_ref[n_blocks + 1 + e] + cnt, (-cnt) & (bm - 1), wait)
                return cc
            lax.fori_loop(0, N_EXPERTS, pad_expert, 0)

            def pad_block(blk, cc, wait=wait):
                zero_block(blk, wait)
                return cc
            lax.fori_loop(plan_ref[n_blocks], n_blocks, pad_block, 0)

    def wait_copies(sl, n_tok):
        for _ in range(TOP_K):
            pltpu.make_async_copy(hbuf.at[sl, pl.ds(0, n_tok * ROW_TILES), :],
                                  h_hbm.at[pl.ds(0, n_tok * ROW_TILES), :], sem.at[sl]).wait()

    def copy_rows(src_ref, rows_ref, n_tok):
        h = _rms(src_ref[...], g_ref[...])
        for c in range(ROW_TILES):
            hbuf[slot, pl.ds(c, n_tok, stride=ROW_TILES), :] = h[:, c * LANES:(c + 1) * LANES]

        def body(it, carry):
            for u in range(DMA_ROWS_PER_ITER):
                j = it * DMA_ROWS_PER_ITER + u
                for k in range(TOP_K):
                    pltpu.make_async_copy(hbuf.at[slot, _row_tile(j), :],
                                          h_hbm.at[_row_tile(rows_ref[k * n_tok + j]), :],
                                          sem.at[slot]).start(priority=k % N_DMA_THREADS)
            return carry
        lax.fori_loop(0, n_tok // DMA_ROWS_PER_ITER, body, 0)

    @pl.when(t >= 2)
    def _():
        wait_copies(slot, tm)

    @pl.when(t < n_tiles)
    def _():
        copy_rows(x_ref, pos_ref, tm)

    @pl.when(t == n_tiles)
    def _():
        copy_rows(xs_ref, poss_ref, n_s)
        wait_copies(slot, n_s)

        @pl.when(n_tiles >= 1)
        def _():
            wait_copies(1 - slot, tm)


def _row_tile_n(r, n):
    return pl.ds(pl.multiple_of(r * ROW_TILES, ROW_TILES), n * ROW_TILES)


def _scatter_rows(cnt, plan, x_p, pos_p, x_s, pos_s, norm_ffn, n_blocks, tm):
    n_p, d = x_p.shape
    n_s = x_s.shape[0]
    assert n_p % tm == 0 and tm % DMA_ROWS_PER_ITER == 0 and n_s % DMA_ROWS_PER_ITER == 0 and n_s <= tm
    nt = n_p // tm
    smem = lambda: pl.BlockSpec(memory_space=pltpu.SMEM)
    tile_idx = lambda i: jnp.minimum(i, nt - 1)
    return pl.pallas_call(
        functools.partial(_scatter_rows_kernel, n_blocks),
        grid=(nt + 1,),
        in_specs=[smem(), smem(), pl.BlockSpec((tm, d), lambda i: (tile_idx(i), 0)),
                  pl.BlockSpec((TOP_K * tm,), lambda i: (tile_idx(i),), memory_space=pltpu.SMEM),
                  pl.BlockSpec((n_s, d), lambda i: (0, 0)), smem(), pl.BlockSpec((1, d), lambda i: (0, 0))],
        out_specs=pl.BlockSpec(memory_space=pl.ANY),
        out_shape=jax.ShapeDtypeStruct((n_blocks * BM_EXPERT * ROW_TILES, LANES), F32),
        scratch_shapes=[pltpu.VMEM((2, tm * ROW_TILES, LANES), F32), pltpu.VMEM((BM_EXPERT * ROW_TILES, LANES), F32),
                        pltpu.SemaphoreType.DMA((2,)), pltpu.SemaphoreType.DMA((1,))],
        compiler_params=_cparams("arbitrary"),
        name="moe_scatter_rows",
    )(cnt, plan, x_p, pos_p, x_s, pos_s, norm_ffn)


WEIGHT_CAST_ROWS = 128


def _expert_blocks_kernel(n_blocks, plan_ref, x_ref, wgu_ref, bgu_ref, wd_ref, bd_ref, y_ref, wgu_b, wd_b):
    bm = BM_EXPERT
    i = pl.program_id(0)

    @pl.when((i == 0) | (plan_ref[i] != plan_ref[jnp.maximum(i - 1, 0)]))
    def _():
        for src, dst in ((wgu_ref, wgu_b), (wd_ref, wd_b)):
            for r0 in range(0, src.shape[1], WEIGHT_CAST_ROWS):
                rows = slice(r0, r0 + WEIGHT_CAST_ROWS)
                dst[rows, :] = src[0, rows, :].astype(BF16)

    @pl.when(i < plan_ref[n_blocks])
    def _():
        x = jnp.concatenate([x_ref[pl.ds(c, bm, stride=ROW_TILES), :] for c in range(ROW_TILES)], axis=1)
        gu = jnp.dot(x.astype(BF16), wgu_b[...], preferred_element_type=F32) + bgu_ref[0]
        gate = jnp.minimum(gu[:, :D_FF], SWIGLU_LIMIT)
        up = jnp.clip(gu[:, D_FF:], -SWIGLU_LIMIT, SWIGLU_LIMIT)
        act = gate * jax.nn.sigmoid(SWIGLU_ALPHA * gate) * (up + 1.0)
        y = jnp.dot(act.astype(BF16), wd_b[...], preferred_element_type=F32) + bd_ref[0]
        for c in range(ROW_TILES):
            y_ref[pl.ds(c, bm, stride=ROW_TILES), :] = y[:, c * LANES:(c + 1) * LANES]

    @pl.when(i >= plan_ref[n_blocks])
    def _():
        y_ref[...] = jnp.zeros(y_ref.shape, F32)


def _expert_blocks(plan, h_sorted, wgu, bgu, wd, bd, n_blocks):
    bm = BM_EXPERT
    by_expert = lambda *shape: pl.BlockSpec((1,) + shape, lambda i, plan: (plan[i],) + (0,) * len(shape))
    used = lambda i, plan: (jnp.minimum(i, plan[n_blocks] - 1), 0)
    grid_spec = pltpu.PrefetchScalarGridSpec(
        num_scalar_prefetch=1,
        grid=(n_blocks,),
        in_specs=[pl.BlockSpec((bm * ROW_TILES, LANES), used), by_expert(D_MODEL, 2 * D_FF), by_expert(1, 2 * D_FF),
                  by_expert(D_FF, D_MODEL), by_expert(1, D_MODEL)],
        out_specs=pl.BlockSpec((bm * ROW_TILES, LANES), lambda i, plan: (i, 0)),
        scratch_shapes=[pltpu.VMEM(wgu.shape[1:], BF16), pltpu.VMEM(wd.shape[1:], BF16)],
    )
    return pl.pallas_call(
        functools.partial(_expert_blocks_kernel, n_blocks),
        grid_spec=grid_spec,
        out_shape=jax.ShapeDtypeStruct(h_sorted.shape, F32),
        compiler_params=_cparams("arbitrary"),
        name="moe_experts",
    )(plan, h_sorted, wgu, bgu, wd, bd)


def _gather_combine_kernel(x_ref, pos_ref, pos_next_ref, gate_ref, g_ref, y_hbm, o_ref, ybuf, sem):
    t = pl.program_id(0)
    n_tiles = pl.num_programs(0)
    tm = x_ref.shape[0]
    slot = t % 2

    def gather(rows_ref, sl):
        def body(it, carry):
            for u in range(DMA_ROWS_PER_ITER):
                j = it * DMA_ROWS_PER_ITER + u
                for k in range(TOP_K):
                    pltpu.make_async_copy(y_hbm.at[_row_tile(rows_ref[k * tm + j]), :],
                                          ybuf.at[sl, _row_tile(k * tm + j), :],
                                          sem.at[sl]).start(priority=k % N_DMA_THREADS)
            return carry
        lax.fori_loop(0, tm // DMA_ROWS_PER_ITER, body, 0)

    @pl.when(t == 0)
    def _():
        gather(pos_ref, 0)

    @pl.when(t + 1 < n_tiles)
    def _():
        gather(pos_next_ref, 1 - slot)

    pltpu.make_async_copy(y_hbm.at[pl.ds(0, TOP_K * tm * ROW_TILES), :], ybuf.at[slot], sem.at[slot]).wait()
    gates = gate_ref[...]
    cols = []
    for c in range(ROW_TILES):
        acc = x_ref[:, c * LANES:(c + 1) * LANES]
        for k in range(TOP_K):
            acc = acc + gates[:, k:k + 1] * ybuf[slot, pl.ds(k * tm * ROW_TILES + c, tm, stride=ROW_TILES), :]
        cols.append(acc)
    o_ref[...] = _rms(jnp.concatenate(cols, axis=1), g_ref[...])


def _gather_combine(x, pos, gates, norm_final, y_sorted, tm):
    n, d = x.shape
    assert n % tm == 0 and tm % DMA_ROWS_PER_ITER == 0
    nt = n // tm
    rows = lambda off: pl.BlockSpec((TOP_K * tm,), lambda i: (jnp.minimum(i + off, nt - 1),),
                                    memory_space=pltpu.SMEM)
    return pl.pallas_call(
        _gather_combine_kernel,
        grid=(nt,),
        in_specs=[pl.BlockSpec((tm, d), lambda i: (i, 0)), rows(0), rows(1),
                  pl.BlockSpec((tm, SUBLANES), lambda i: (i, 0)), pl.BlockSpec((1, d), lambda i: (0, 0)),
                  pl.BlockSpec(memory_space=pl.ANY)],
        out_specs=pl.BlockSpec((tm, d), lambda i: (i, 0)),
        out_shape=jax.ShapeDtypeStruct((n, d), F32),
        scratch_shapes=[pltpu.VMEM((2, TOP_K * tm * ROW_TILES, LANES), F32), pltpu.SemaphoreType.DMA((2,))],
        compiler_params=_cparams("arbitrary"),
        name="moe_combine",
    )(x, pos, pos, gates, norm_final, y_sorted)


def _moe_and_final_norm(x_p, routing_p, x_s, routing_s, norm_ffn, w_gate_up, b_gate_up, w_down, b_down, norm_final):
    n_p, n_s = x_p.shape[0], x_s.shape[0]
    n_rows = (n_p + n_s) * TOP_K
    n_blocks = -(-(n_rows + N_EXPERTS * (BM_EXPERT - 1)) // BM_EXPERT)
    g_ffn = norm_ffn[None]
    e_s, r_s, gates_s, _ = routing_s
    e_p, r_p, gates_p, cnt = routing_p
    plan = _moe_plan(cnt[:, 0], n_blocks)
    pos_p = _positions(plan, e_p, r_p, n_blocks, TM_COMBINE)
    pos_s = _positions(plan, e_s, r_s, n_blocks, n_s)
    h_sorted = _scatter_rows(cnt[:, 0], plan, x_p, pos_p, x_s, pos_s, g_ffn, n_blocks, TM_COMBINE)
    y_sorted = _expert_blocks(plan, h_sorted, w_gate_up, b_gate_up[:, None, :], w_down, b_down[:, None, :], n_blocks)
    y_p = _gather_combine(x_p, pos_p, gates_p, norm_final[None], y_sorted, TM_COMBINE)
    y_s = _gather_combine(x_s, pos_s, gates_s, norm_final[None], y_sorted, n_s)
    return y_p, y_s


def _rb(x):
    return x.astype(BF16).astype(F32)


def _sample_proj_kernel(x_ref, g_ref, wm_ref, wg_ref, z_ref):
    hb = _rms(x_ref[...], g_ref[...]).astype(BF16)
    z_ref[:, :MIX_WIDTH] = jnp.dot(hb, wm_ref[...], preferred_element_type=F32)
    z_ref[:, MIX_WIDTH:] = jnp.dot(hb, wg_ref[...], preferred_element_type=F32)


def _sample_proj(x, norm_mix, w_mix, w_gate):
    n, d = x.shape
    args = (x, norm_mix, w_mix, w_gate)
    width = w_mix.shape[1] + w_gate.shape[1]
    return pl.pallas_call(
        _sample_proj_kernel,
        grid=(1,),
        in_specs=[pl.BlockSpec(a.shape, lambda j: (0, 0)) for a in args],
        out_specs=pl.BlockSpec((n, width), lambda j: (0, 0)),
        out_shape=jax.ShapeDtypeStruct((n, width), F32),
        compiler_params=_cparams("arbitrary"),
        name="sample_proj",
    )(*args)


def _sample_mix_kernel(z_ref, c0_ref, c1_ref, c2_ref, mem_ref, sb0_ref, sb1_ref, sb2_ref, nb_ref, lng_ref, lnb_ref,
                       wsp_ref, bsp_ref, bo_ref, co_ref, a_ref, v_ref, n0_ref, n1_ref, n2_ref):
    z = z_ref[0]
    scale = 1.0 / math.sqrt(B_HEAD_DIM)
    row_head = lax.broadcasted_iota(I32, (SUBLANES, B_GROUP_W), 0)
    lane_head = lax.broadcasted_iota(I32, (SUBLANES, B_GROUP_W), 1) // B_HEAD_DIM
    own = (row_head == lane_head).astype(F32)
    kv_cols = jnp.transpose(jnp.broadcast_to(z[:, B_WIDTH:3 * B_WIDTH], (SUBLANES, 2 * B_WIDTH)))[:, 0:1]
    outs, lses = [], []
    for g, (cache_ref, sb_ref, new_ref) in enumerate(((c0_ref, sb0_ref, n0_ref), (c1_ref, sb1_ref, n1_ref),
                                                      (c2_ref, sb2_ref, n2_ref))):
        window = cache_ref.shape[2]
        q, kn, vn = (_rb(z[:, i * B_WIDTH + g * B_GROUP_W:i * B_WIDTH + (g + 1) * B_GROUP_W]) for i in range(3))
        qh = own * q
        kt = cache_ref[0, :B_GROUP_W, :].astype(BF16)
        vt = cache_ref[0, B_GROUP_W:, :].astype(BF16)
        lg = jnp.dot(qh.astype(BF16), kt, preferred_element_type=F32) * scale + sb_ref[...]
        ln = jnp.sum(qh * kn, axis=1, keepdims=True) * scale + nb_ref[g, :, 0:1]
        m = jnp.maximum(jnp.max(lg, axis=1, keepdims=True), ln)
        lse = m + jnp.log(jnp.sum(jnp.exp(lg - m), axis=1, keepdims=True) + jnp.exp(ln - m))
        p = jnp.exp(lg - lse).astype(BF16)
        pn = _rb(jnp.exp(ln - lse))
        pv = lax.dot_general(p, vt, (((1,), (1,)), ((), ())), preferred_element_type=F32) + pn * vn
        outs.append(jnp.sum(own * pv, axis=0, keepdims=True))
        lses.append(jnp.sum(own * lse, axis=0, keepdims=True))
        new_col = jnp.concatenate([kv_cols[i * B_WIDTH + g * B_GROUP_W:i * B_WIDTH + (g + 1) * B_GROUP_W]
                                   for i in range(2)], axis=0)
        lane = lax.broadcasted_iota(I32, (1, window), 1)
        new_ref[0] = jnp.where(lane == window - 1, new_col, pltpu.roll(cache_ref[0], window - 1, axis=1))
    m = jnp.maximum(jnp.maximum(lses[0], lses[1]), lses[2])
    ws = [jnp.exp(l - m) for l in lses]
    den = ws[0] + ws[1] + ws[2]
    bo_ref[0] = _rb(ws[0] / den) * _rb(outs[0]) + _rb(ws[1] / den) * _rb(outs[1]) + _rb(ws[2] / den) * _rb(outs[2])

    cs = []
    for h in range(C_HEADS):
        c0 = MIX_WIDTH - C_WIDTH + h * C_HEAD_DIM
        qh = _rb(z[:, c0:c0 + C_HEAD_DIM])
        kh = _rb(mem_ref[0, pl.ds(h, N_MEM, stride=2 * C_HEADS), :])
        vh = _rb(mem_ref[0, pl.ds(C_HEADS + h, N_MEM, stride=2 * C_HEADS), :])
        s = jnp.sum(kh * qh, axis=-1, keepdims=True) * (1.0 / math.sqrt(C_HEAD_DIM))
        p = jnp.exp(s - jnp.max(s, axis=0, keepdims=True))
        p = _rb(p / jnp.sum(p, axis=0, keepdims=True))
        cs.append(jnp.sum(p * vh, axis=0, keepdims=True))
    co_ref[0] = jnp.concatenate(cs, axis=1)

    u = _gelu(z[:, 3 * B_WIDTH:3 * B_WIDTH + A_WIDTH])
    v = _layernorm(_gelu(z[:, 3 * B_WIDTH + A_WIDTH:3 * B_WIDTH + 2 * A_WIDTH]), lng_ref[...], lnb_ref[...])
    v_ref[0] = v
    a_ref[0] = u * (_rb(wsp_ref[...]) * _rb(v) + bsp_ref[...])


def _sample_mix(z, caches, mem, sbias, nbias, ln_g, ln_b, wsp0, bsp0):
    n = z.shape[0]
    const = lambda arr: pl.BlockSpec(arr.shape, lambda i: (0,) * arr.ndim)
    row = lambda w: pl.BlockSpec((1, 1, w), lambda i: (i, 0, 0))
    per_req = lambda arr: pl.BlockSpec((1,) + arr.shape[1:], lambda i: (i, 0, 0))
    widths = (B_GROUP_W, C_WIDTH, A_WIDTH, A_WIDTH)
    return pl.pallas_call(
        _sample_mix_kernel,
        grid=(n,),
        in_specs=[row(z.shape[2])] + [per_req(c) for c in caches] + [per_req(mem)]
        + [const(a) for a in (*sbias, nbias, ln_g, ln_b, wsp0, bsp0)],
        out_specs=tuple(row(w) for w in widths) + tuple(per_req(c) for c in caches),
        out_shape=tuple(jax.ShapeDtypeStruct((n, 1, w), F32) for w in widths)
        + tuple(jax.ShapeDtypeStruct(c.shape, F32) for c in caches),
        compiler_params=_cparams("arbitrary"),
        name="sample_mix",
    )(z, *caches, mem, *sbias, nbias, ln_g, ln_b, wsp0, bsp0)


def _sample_merge_kernel(x_ref, z_ref, a_ref, b_ref, c_ref, wa_ref, wb_ref, wc_ref, wo_ref, o_ref):
    dot = lambda a, b: jnp.dot(a.astype(BF16), b, preferred_element_type=F32)
    merged = None
    for i, (act_ref, w_ref) in enumerate(((a_ref, wa_ref), (b_ref, wb_ref), (c_ref, wc_ref))):
        gate = jax.nn.sigmoid(z_ref[:, MIX_WIDTH + i * D_MODEL:MIX_WIDTH + (i + 1) * D_MODEL])
        term = gate * dot(act_ref[...], w_ref[...])
        merged = term if merged is None else merged + term
    o_ref[...] = x_ref[...] + dot(merged, wo_ref[...])


def _sample_merge(x, z, a, bo, co, wa, wb, wc, wo):
    args = (x, z, a, bo, co, wa, wb, wc, wo)
    return pl.pallas_call(
        _sample_merge_kernel,
        grid=(1,),
        in_specs=[pl.BlockSpec(arr.shape, lambda i: (0, 0)) for arr in args],
        out_specs=pl.BlockSpec(x.shape, lambda i: (0, 0)),
        out_shape=jax.ShapeDtypeStruct(x.shape, F32),
        compiler_params=_cparams("arbitrary"),
        name="sample_merge",
    )(*args)


def _sample_mixers(x_sample, caches, cache_mem, sbias, nbias, norm_mix, ln_v_g, ln_v_b, w_spatial, b_spatial, wts):
    n = x_sample.shape[0]
    x = x_sample[:, 0]
    z = _sample_proj(x, norm_mix[None], wts["mix"], wts["gate"])
    mem = cache_mem.reshape(n, N_MEM * 2 * C_HEADS, C_HEAD_DIM)
    wsp0 = jnp.repeat(w_spatial[:, 0, 0], LANES)[None]
    bsp0 = jnp.repeat(b_spatial[:, 0], LANES)[None]
    bo, co, a, v, *new_caches = _sample_mix(z[:, None], [_from_window_layout(c) for c in caches], mem, sbias, nbias,
                                            ln_v_g[None], ln_v_b[None], wsp0, bsp0)
    x_new = _sample_merge(x, z, a[:, 0], bo[:, 0], co[:, 0], wts["a"], wts["b"], wts["c"], wts["out"])
    return x_new, [_to_window_layout(c) for c in new_caches], v[:, 0]


def _mixer_weights(w_in, w_branch_a, w_branch_b, w_branch_c, w_out):
    return {"mix": w_in[:, :MIX_WIDTH].astype(BF16), "gate": w_in[:, MIX_WIDTH:].astype(BF16),
            "a": w_branch_a.astype(BF16), "b": w_branch_b.astype(BF16), "c": w_branch_c.astype(BF16),
            "out": w_out.astype(BF16)}


def kernel(x_prompt, x_sample, cache_win0_kv, cache_win1_kv, cache_win2_kv, cache_mem_kv, mem_prompt, rel_bias,
           norm_mix, norm_mem, w_in, ln_v_g, ln_v_b, w_spatial, b_spatial, w_mem_kv, w_branch_a, w_branch_b,
           w_branch_c, w_out, norm_ffn, w_router, b_router, w_gate_up, b_gate_up, w_down, b_down, norm_final):
    assert norm_mix.shape[0] == 1, "one layer"
    b, s, d = x_prompt.shape
    n_s = x_sample.shape[0]
    caches = (cache_win0_kv[0], cache_win1_kv[0], cache_win2_kv[0])
    assert all(c.shape[1] == w for c, (w, _) in zip(caches, B_PAIRS)), "window buffers hold a full window"
    wts = _mixer_weights(w_in[0], w_branch_a[0], w_branch_b[0], w_branch_c[0], w_out[0])
    band_bias, *sbias, nbias = _bias_tables(rel_bias)
    route_params = _route_params(norm_ffn[0], w_router[0], b_router[0])
    xs, win_s, v_s = _sample_mixers(x_sample, caches, cache_mem_kv[0], sbias, nbias, norm_mix[0], ln_v_g[0],
                                    ln_v_b[0], w_spatial[0], b_spatial[0], wts)
    routing_s = _route(xs, route_params, jnp.zeros((N_EXPERTS, LANES), I32), n_s)
    xp, routing_p, win_p, mem_p = _prompt_mixers(x_prompt, mem_prompt, band_bias, norm_mix[0], norm_mem[0], ln_v_g[0],
                                                 ln_v_b[0], w_spatial[0], b_spatial[0], w_mem_kv[0], wts,
                                                 route_params, routing_s[3])
    y_p, y_s = _moe_and_final_norm(xp.reshape(b * s, d), routing_p, xs, routing_s, norm_ffn[0], w_gate_up[0],
                                   b_gate_up[0], w_down[0], b_down[0], norm_final)
    chunk_v = v_s.reshape(1, n_s, 1, A_GROUPS, LANES)
    return (y_p.reshape(b, s, d), y_s[:, None], win_p[0], win_p[1], win_p[2], mem_p,
            win_s[0], win_s[1], win_s[2], chunk_v)
```

```python
import functools
import math

import numpy as np
import jax
import jax.numpy as jnp
from jax import lax
from jax.experimental import pallas as pl
from jax.experimental.pallas import tpu as pltpu

F32 = jnp.float32
BF16 = jnp.bfloat16
I32 = jnp.int32

D_MODEL = 1024
N_MEM = 256
CHUNK = 128
A_GROUPS = 4
A_WIDTH = 512
B_PAIRS = ((128, 1), (512, 4), (2048, 16))
B_GROUP_W = 256
B_WIDTH = 768
B_HEAD_DIM = 64
BAND = 128
C_HEADS = 4
C_HEAD_DIM = 128
C_WIDTH = 512
REL_BUCKETS = 32
REL_MAX_DIST = 2048
N_EXPERTS = 32
TOP_K = 4
D_FF = 1024
SWIGLU_LIMIT = 7.0
SWIGLU_ALPHA = 1.702
EPS = 1e-6
NEG_INF = -1e30
MIX_WIDTH = 3 * B_WIDTH + 2 * A_WIDTH + C_WIDTH

LANES = 128
SUBLANES = 8
ROW_TILES = D_MODEL // LANES
VMEM_LIMIT = 56 * 1024 * 1024

TM_PROJ = 1024
TM_MERGE = 512
BM_EXPERT = 512
TM_COMBINE = 512
ATTN_BLOCKS_PER_ITER = 4


def _cparams(*sem):
    return pltpu.CompilerParams(dimension_semantics=sem, vmem_limit_bytes=VMEM_LIMIT)


def _t5_bucket(dist):
    dist = np.maximum(np.asarray(dist), 0)
    max_exact = REL_BUCKETS // 2
    log_ratio = np.log(np.maximum(dist, max_exact) / max_exact) / math.log(REL_MAX_DIST / max_exact)
    large = np.minimum(max_exact + (log_ratio * (REL_BUCKETS - max_exact)).astype(np.int32), REL_BUCKETS - 1)
    return np.where(dist < max_exact, dist, large).astype(np.int32)


def _gelu(x):
    c = math.sqrt(2.0 / math.pi)
    return x * (0.5 * (1.0 + jnp.tanh(c * (x + 0.044715 * (x * x * x)))))


def _rms(x, g):
    return x * lax.rsqrt(jnp.mean(x * x, axis=-1, keepdims=True) + EPS) * g


def _layernorm(x, g, b):
    xc = x - jnp.mean(x, axis=-1, keepdims=True)
    return xc * lax.rsqrt(jnp.mean(xc * xc, axis=-1, keepdims=True) + EPS) * g + b


def _proj_kernel(x_ref, g_ref, w_ref, lng_ref, lnb_ref, wsp_ref, bsp_ref,
                 qkv0_ref, qkv1_ref, qkv2_ref, win0_ref, win1_ref, win2_ref, a_ref, qc_ref, scr_ref):
    t = pl.program_id(1)
    last = pl.num_programs(1) - 1
    tm = x_ref.shape[1]
    hb = _rms(x_ref[0], g_ref[...]).astype(BF16)

    def proj(c0, width):
        return jnp.dot(hb, w_ref[:, c0:c0 + width], preferred_element_type=F32)

    for g, (_, dil) in enumerate(B_PAIRS):
        q, k, v = (proj(i * B_WIDTH + g * B_GROUP_W, B_GROUP_W) for i in range(3))
        kv = jnp.concatenate([k, v], axis=1)
        if g == 2:
            win2_ref[0] = jnp.transpose(kv)
        elif g == 1:
            @pl.when(t == last)
            def _():
                win1_ref[0] = jnp.transpose(kv[tm - B_PAIRS[1][0]:])
        else:
            @pl.when(t == last)
            def _():
                win0_ref[0] = jnp.transpose(kv[tm - B_PAIRS[0][0]:])
        if dil == 1:
            qkv0_ref[0] = jnp.concatenate([q, k, v], axis=1).astype(BF16)
            continue
        out_ref = qkv1_ref if g == 1 else qkv2_ref
        for i, arr in enumerate((q, k, v)):
            for ct in range(2):
                scr_ref[2 * i + ct] = arr[:, ct * LANES:(ct + 1) * LANES]
        for r in range(dil):
            rows = [scr_ref[j, pl.ds(r, tm // dil, stride=dil), :] for j in range(6)]
            out_ref[0, r] = jnp.concatenate(rows, axis=1).astype(BF16)

    u = _gelu(proj(3 * B_WIDTH, A_WIDTH))
    v = _layernorm(_gelu(proj(3 * B_WIDTH + A_WIDTH, A_WIDTH)), lng_ref[...], lnb_ref[...]).astype(BF16)
    for c in range(tm // CHUNK):
        rs = slice(c * CHUNK, (c + 1) * CHUNK)
        for g in range(A_GROUPS):
            cs = slice(g * LANES, (g + 1) * LANES)
            s = jnp.dot(wsp_ref[g], v[rs, cs], preferred_element_type=F32) + bsp_ref[:, g:g + 1]
            a_ref[0, rs, cs] = (u[rs, cs] * s).astype(BF16)

    qc_ref[0] = proj(3 * B_WIDTH + 2 * A_WIDTH, C_WIDTH).astype(BF16)


def _prompt_proj(x, norm_mix, w_mix, ln_g, ln_b, wsp, bsp):
    b, s, d = x.shape
    tm = TM_PROJ
    assert s % tm == 0 and s >= B_PAIRS[2][0] and tm >= B_PAIRS[1][0]
    nt = s // tm
    const = lambda *shape: pl.BlockSpec(shape, lambda i, j: (0,) * len(shape))
    out_shape = (
        jax.ShapeDtypeStruct((b, s, 3 * B_GROUP_W), BF16),
        jax.ShapeDtypeStruct((b, 4, s // 4, 3 * B_GROUP_W), BF16),
        jax.ShapeDtypeStruct((b, 16, s // 16, 3 * B_GROUP_W), BF16),
        jax.ShapeDtypeStruct((b, 2 * B_GROUP_W, B_PAIRS[0][0]), F32),
        jax.ShapeDtypeStruct((b, 2 * B_GROUP_W, B_PAIRS[1][0]), F32),
        jax.ShapeDtypeStruct((b, 2 * B_GROUP_W, s), F32),
        jax.ShapeDtypeStruct((b, s, A_WIDTH), BF16),
        jax.ShapeDtypeStruct((b, s, C_WIDTH), BF16),
    )
    out_specs = (
        pl.BlockSpec((1, tm, 3 * B_GROUP_W), lambda i, j: (i, j, 0)),
        pl.BlockSpec((1, 4, tm // 4, 3 * B_GROUP_W), lambda i, j: (i, 0, j, 0)),
        pl.BlockSpec((1, 16, tm // 16, 3 * B_GROUP_W), lambda i, j: (i, 0, j, 0)),
        pl.BlockSpec((1, 2 * B_GROUP_W, B_PAIRS[0][0]), lambda i, j: (i, 0, 0)),
        pl.BlockSpec((1, 2 * B_GROUP_W, B_PAIRS[1][0]), lambda i, j: (i, 0, 0)),
        pl.BlockSpec((1, 2 * B_GROUP_W, tm), lambda i, j: (i, 0, j)),
        pl.BlockSpec((1, tm, A_WIDTH), lambda i, j: (i, j, 0)),
        pl.BlockSpec((1, tm, C_WIDTH), lambda i, j: (i, j, 0)),
    )
    return pl.pallas_call(
        _proj_kernel,
        grid=(b, nt),
        in_specs=[pl.BlockSpec((1, tm, d), lambda i, j: (i, j, 0)), const(1, d), const(d, MIX_WIDTH),
                  const(1, A_WIDTH), const(1, A_WIDTH), const(A_GROUPS, CHUNK, CHUNK), const(CHUNK, A_GROUPS)],
        out_specs=out_specs,
        out_shape=out_shape,
        scratch_shapes=[pltpu.VMEM((6, tm, LANES), F32)],
        compiler_params=_cparams("arbitrary", "arbitrary"),
        name="prompt_proj",
    )(x, norm_mix, w_mix, ln_g, ln_b, wsp, bsp)


def _attend(q, k, v, bias_ref, g, k0, hmask):
    kn = k.shape[0]
    qs = jnp.concatenate([q * hm for hm in hmask], axis=0)
    s = lax.dot_general(qs, k, (((1,), (1,)), ((), ())), preferred_element_type=F32) + bias_ref[g, :, k0:k0 + kn]
    m = jnp.max(s, axis=-1, keepdims=True)
    p = jnp.exp(s - m)
    l = jnp.sum(p, axis=-1, keepdims=True)
    pv = jnp.dot(p.astype(BF16), v, preferred_element_type=F32) / l
    lse_rows = m + jnp.log(l)
    out = jnp.zeros((BAND, B_GROUP_W), F32)
    lse = jnp.zeros((BAND, B_GROUP_W), F32)
    for h, hm in enumerate(hmask):
        rows = slice(h * BAND, (h + 1) * BAND)
        sel = hm > 0
        out = jnp.where(sel, pv[rows], out)
        lse = jnp.where(sel, lse_rows[rows], lse)
    return out, lse


def _attn_kernel(q0_ref, q1_ref, q2_ref, bias_ref, o_ref, out_ref, lse_ref):
    s = q0_ref.shape[1]
    lane_head = lax.broadcasted_iota(I32, (1, B_GROUP_W), 1) // B_HEAD_DIM
    scale = 1.0 / math.sqrt(B_HEAD_DIM)
    hmask = [jnp.where(lane_head == h, scale, 0.0).astype(BF16) for h in range(4)]
    qs, ks, vs = (slice(i * B_GROUP_W, (i + 1) * B_GROUP_W) for i in range(3))

    def store(g, start, dil, out, lse):
        rows = pl.ds(start, BAND) if dil == 1 else pl.ds(start, BAND, stride=dil)
        for ct in range(2):
            out_ref[g, ct, rows, :] = out[:, ct * LANES:(ct + 1) * LANES]
            lse_ref[g, ct, rows, :] = lse[:, ct * LANES:(ct + 1) * LANES]

    for g, (_, dil) in enumerate(B_PAIRS):
        n = s // dil
        nb = n // BAND

        def load(rows, cols, r, g=g):
            if g == 0:
                return q0_ref[0, rows, cols]
            return (q1_ref if g == 1 else q2_ref)[0, r, rows, cols]

        def first_block(r, g=g, dil=dil, load=load):
            rows = pl.ds(0, BAND)
            out, lse = _attend(load(rows, qs, r), load(rows, ks, r), load(rows, vs, r), bias_ref, g, BAND, hmask)
            store(g, r, dil, out, lse)

        def later_block(i, g=g, dil=dil, nb=nb, load=load):
            r = i // (nb - 1)
            qb = i % (nb - 1) + 1
            q0 = pl.multiple_of(qb * BAND, BAND)
            rows_q = pl.ds(q0, BAND)
            rows_k = pl.ds(q0 - BAND, 2 * BAND)
            out, lse = _attend(load(rows_q, qs, r), load(rows_k, ks, r), load(rows_k, vs, r), bias_ref, g, 0, hmask)
            store(g, qb * (BAND * dil) + r, dil, out, lse)

        def run(block_fn, count):
            per = ATTN_BLOCKS_PER_ITER

            def body(it, carry):
                for u in range(per):
                    block_fn(it * per + u)
                return carry
            lax.fori_loop(0, count // per, body, 0)
            for i in range(count - count % per, count):
                block_fn(i)

        run(first_block, dil)
        run(later_block, dil * (nb - 1))

    rc = 256
    def combine(c, carry):
        rows = pl.ds(pl.multiple_of(c * rc, rc), rc)
        for ct in range(2):
            ls = [lse_ref[g, ct, rows, :] for g in range(3)]
            m = jnp.maximum(jnp.maximum(ls[0], ls[1]), ls[2])
            ws = [jnp.exp(l - m) for l in ls]
            den = ws[0] + ws[1] + ws[2]
            num = ws[0] * out_ref[0, ct, rows, :] + ws[1] * out_ref[1, ct, rows, :] + ws[2] * out_ref[2, ct, rows, :]
            o_ref[0, rows, ct * LANES:(ct + 1) * LANES] = (num / den).astype(BF16)
        return carry
    lax.fori_loop(0, s // rc, combine, 0)


def _prompt_attention(qkv0, qkv1, qkv2, bias):
    b, s, w = qkv0.shape
    return pl.pallas_call(
        _attn_kernel,
        grid=(b,),
        in_specs=[pl.BlockSpec((1, s, w), lambda i: (i, 0, 0)),
                  pl.BlockSpec((1, 4, s // 4, w), lambda i: (i, 0, 0, 0)),
                  pl.BlockSpec((1, 16, s // 16, w), lambda i: (i, 0, 0, 0)),
                  pl.BlockSpec(bias.shape, lambda i: (0, 0, 0))],
        out_specs=pl.BlockSpec((1, s, B_GROUP_W), lambda i: (i, 0, 0)),
        out_shape=jax.ShapeDtypeStruct((b, s, B_GROUP_W), BF16),
        scratch_shapes=[pltpu.VMEM((3, 2, s, LANES), F32), pltpu.VMEM((3, 2, s, LANES), F32)],
        compiler_params=_cparams("arbitrary"),
        name="prompt_attention",
    )(qkv0, qkv1, qkv2, bias)


def _bias_kernel(rel_ref, band_idx_ref, s0_ref, s1_ref, s2_ref, band_ref, sb0_ref, sb1_ref, sb2_ref, nb_ref):
    def lookup(idx, col):
        acc = jnp.full(idx.shape, NEG_INF, F32)
        for bucket in range(REL_BUCKETS):
            acc = jnp.where(idx == bucket, rel_ref[bucket, col], acc)
        return acc

    for g, (s_ref, sb_ref) in enumerate(((s0_ref, sb0_ref), (s1_ref, sb1_ref), (s2_ref, sb2_ref))):
        width = s_ref.shape[1]
        for h in range(4):
            band_ref[g, h * BAND:(h + 1) * BAND, :] = lookup(band_idx_ref[g], 4 * g + h)
        sb_ref[...] = jnp.concatenate([lookup(s_ref[...], 4 * g + h) for h in range(4)]
                                      + [jnp.full((SUBLANES - 4, width), NEG_INF, F32)], axis=0)
        nb_ref[g] = jnp.concatenate([jnp.full((1, LANES), rel_ref[0, 4 * g + h], F32) for h in range(4)]
                                    + [jnp.zeros((SUBLANES - 4, LANES), F32)], axis=0)


def _bias_tables(rel_bias):
    steps = np.arange(BAND)[:, None] + BAND - np.arange(2 * BAND)[None, :]
    valid = (steps >= 0) & (steps <= BAND)
    band_idx = np.stack([np.where(valid, _t5_bucket(np.clip(steps, 0, BAND) * dil), -1) for _, dil in B_PAIRS])
    s_idx = []
    for window, dil in B_PAIRS:
        w = np.arange(window)
        s_idx.append(np.where(w % dil == 0, _t5_bucket(window - w), -1)[None].astype(np.int32))
    args = (rel_bias, jnp.asarray(band_idx.astype(np.int32))) + tuple(jnp.asarray(s) for s in s_idx)
    vmem = lambda a: pl.BlockSpec(a.shape, lambda i: (0,) * a.ndim)
    out_shape = (jax.ShapeDtypeStruct((3, 4 * BAND, 2 * BAND), F32),) + tuple(
        jax.ShapeDtypeStruct((SUBLANES, window), F32) for window, _ in B_PAIRS) + (
        jax.ShapeDtypeStruct((3, SUBLANES, LANES), F32),)
    return pl.pallas_call(
        _bias_kernel,
        grid=(1,),
        in_specs=[pl.BlockSpec(memory_space=pltpu.SMEM)] + [vmem(a) for a in args[1:]],
        out_specs=tuple(pl.BlockSpec(s.shape, lambda i, n=len(s.shape): (0,) * n) for s in out_shape),
        out_shape=out_shape,
        compiler_params=_cparams("arbitrary"),
        name="bias_tables",
    )(*args)


def _memkv_kernel(mem_ref, g_ref, w_ref, kv_ref, kvb_ref):
    hb = _rms(mem_ref[0], g_ref[...]).astype(BF16)
    kv = jnp.dot(hb, w_ref[...], preferred_element_type=F32)
    m = kv.shape[0]
    n_heads = kv.shape[1] // C_HEAD_DIM
    for j in range(n_heads):
        kv_ref[0, pl.ds(j, m, stride=n_heads), :] = kv[:, j * C_HEAD_DIM:(j + 1) * C_HEAD_DIM]
    kvb_ref[0] = kv.astype(BF16)


def _memory_kv(mem, norm_mem, w_mem):
    b, m, d = mem.shape
    w = w_mem.shape[1]
    rows = m * w // C_HEAD_DIM
    return pl.pallas_call(
        _memkv_kernel,
        grid=(b,),
        in_specs=[pl.BlockSpec((1, m, d), lambda i: (i, 0, 0)), pl.BlockSpec((1, d), lambda i: (0, 0)),
                  pl.BlockSpec((d, w), lambda i: (0, 0))],
        out_specs=(pl.BlockSpec((1, rows, C_HEAD_DIM), lambda i: (i, 0, 0)),
                   pl.BlockSpec((1, m, w), lambda i: (i, 0, 0))),
        out_shape=(jax.ShapeDtypeStruct((b, rows, C_HEAD_DIM), F32), jax.ShapeDtypeStruct((b, m, w), BF16)),
        compiler_params=_cparams("arbitrary"),
        name="memory_kv",
    )(mem, norm_mem, w_mem)


def _merge_kernel(x_ref, g_ref, a_ref, b_ref, qc_ref, kv_ref, wg_ref, wa_ref, wb_ref, wc_ref, wo_ref,
                  gf_ref, wr_ref, br_ref, cnt_in_ref, o_ref, e_ref, rank_ref, gate_ref, cnt_ref, carry_ref):
    x = x_ref[0]
    hb = _rms(x, g_ref[...]).astype(BF16)
    qc = qc_ref[0]
    cs = []
    for h in range(C_HEADS):
        hs = slice(h * C_HEAD_DIM, (h + 1) * C_HEAD_DIM)
        s = lax.dot_general(qc[:, hs], kv_ref[0, :, hs], (((1,), (1,)), ((), ())), preferred_element_type=F32)
        s = s * (1.0 / math.sqrt(C_HEAD_DIM))
        p = jnp.exp(s - jnp.max(s, axis=-1, keepdims=True))
        p = p / jnp.sum(p, axis=-1, keepdims=True)
        vs = slice(C_WIDTH + h * C_HEAD_DIM, C_WIDTH + (h + 1) * C_HEAD_DIM)
        cs.append(jnp.dot(p.astype(BF16), kv_ref[0, :, vs], preferred_element_type=F32))
    c = jnp.concatenate(cs, axis=1).astype(BF16)
    branches = ((a_ref[0], wa_ref), (b_ref[0], wb_ref), (c, wc_ref))
    merged = None
    for i, (act, w_ref) in enumerate(branches):
        gate = jax.nn.sigmoid(jnp.dot(hb, wg_ref[:, i * D_MODEL:(i + 1) * D_MODEL], preferred_element_type=F32))
        term = gate * jnp.dot(act, w_ref[...], preferred_element_type=F32)
        merged = term if merged is None else merged + term
    x_new = x + jnp.dot(merged.astype(BF16), wo_ref[...], preferred_element_type=F32)
    o_ref[0] = x_new
    first_step = (pl.program_id(0) == 0) & (pl.program_id(1) == 0)
    _route_rows(x_new, first_step, gf_ref, wr_ref, br_ref, cnt_in_ref, e_ref, rank_ref, gate_ref, cnt_ref, carry_ref)


def _prompt_merge(x, norm_mix, a, bo, qc, kvb, wg, wa, wb, wc, wo, route_params, cnt_in):
    b, s, d = x.shape
    tm = TM_MERGE
    nt = s // tm
    n = b * s
    tile = lambda w: pl.BlockSpec((1, tm, w), lambda i, j: (i, j, 0))
    const = lambda arr: pl.BlockSpec(arr.shape, lambda i, j: (0,) * arr.ndim)
    ids = pl.BlockSpec((TOP_K, tm), lambda i, j: (0, i * nt + j))
    return pl.pallas_call(
        _merge_kernel,
        grid=(b, nt),
        in_specs=[tile(d), const(norm_mix), tile(A_WIDTH), tile(B_GROUP_W), tile(C_WIDTH),
                  pl.BlockSpec((1,) + kvb.shape[1:], lambda i, j: (i, 0, 0)),
                  const(wg), const(wa), const(wb), const(wc), const(wo)]
        + [const(p) for p in route_params] + [const(cnt_in)],
        out_specs=(tile(d), ids, ids, pl.BlockSpec((tm, SUBLANES), lambda i, j: (i * nt + j, 0)),
                   pl.BlockSpec((N_EXPERTS, LANES), lambda i, j: (0, 0))),
        out_shape=(jax.ShapeDtypeStruct((b, s, d), F32), jax.ShapeDtypeStruct((TOP_K, n), I32),
                   jax.ShapeDtypeStruct((TOP_K, n), I32), jax.ShapeDtypeStruct((n, SUBLANES), F32),
                   jax.ShapeDtypeStruct((N_EXPERTS, LANES), I32)),
        scratch_shapes=[pltpu.VMEM((N_EXPERTS, LANES), F32)],
        compiler_params=_cparams("arbitrary", "arbitrary"),
        name="prompt_merge",
    )(x, norm_mix, a, bo, qc, kvb, wg, wa, wb, wc, wo, *route_params, cnt_in)


def _to_window_layout(w):
    b, _, window = w.shape
    return jnp.transpose(w.reshape(b, 2, 4, B_HEAD_DIM, window), (0, 4, 1, 2, 3))[None]


def _from_window_layout(c):
    b, window = c.shape[:2]
    return jnp.transpose(c, (0, 2, 3, 4, 1)).reshape(b, 2 * B_GROUP_W, window)


def _prompt_mixers(x_prompt, mem_prompt, band_bias, norm_mix, norm_mem, ln_v_g, ln_v_b, w_spatial, b_spatial,
                   w_mem_kv, wts, route_params, cnt_in):
    b, s, _ = x_prompt.shape
    wsp = jnp.where(np.tril(np.ones((CHUNK, CHUNK), bool)), w_spatial, 0.0).astype(BF16)
    qkv0, qkv1, qkv2, win0, win1, win2, a_out, qc = _prompt_proj(
        x_prompt, norm_mix[None], wts["mix"], ln_v_g[None], ln_v_b[None], wsp, b_spatial.T)
    b_out = _prompt_attention(qkv0, qkv1, qkv2, band_bias)
    mem_kv, mem_kv_b = _memory_kv(mem_prompt, norm_mem[None], w_mem_kv.astype(BF16))
    x_new, *routing = _prompt_merge(x_prompt, norm_mix[None], a_out, b_out, qc, mem_kv_b, wts["gate"],
                                    wts["a"], wts["b"], wts["c"], wts["out"], route_params, cnt_in)
    wins = [_to_window_layout(w) for w in (win0, win1, win2)]
    return x_new, routing, wins, mem_kv.reshape(1, b, N_MEM, 2, C_HEADS, C_HEAD_DIM)


def _route_kernel(x_ref, g_ref, wr_ref, br_ref, cnt_in_ref, e_ref, rank_ref, gate_ref, cnt_ref, carry_ref):
    _route_rows(x_ref[...], pl.program_id(0) == 0, g_ref, wr_ref, br_ref, cnt_in_ref, e_ref, rank_ref, gate_ref,
                cnt_ref, carry_ref)


def _route_rows(x, first_step, g_ref, wr_ref, br_ref, cnt_in_ref, e_ref, rank_ref, gate_ref, cnt_ref, carry_ref):
    tm = x.shape[0]

    @pl.when(first_step)
    def _():
        carry_ref[...] = cnt_in_ref[...].astype(F32)

    h = _rms(x, g_ref[...])
    logits = lax.dot_general(wr_ref[...], h.astype(BF16), (((1,), (1,)), ((), ())),
                             preferred_element_type=F32) + br_ref[...]
    expert = lax.broadcasted_iota(I32, logits.shape, 0)
    vals, idxs = [], []
    member = jnp.zeros(logits.shape, F32)
    for _ in range(TOP_K):
        m = jnp.max(logits, axis=0, keepdims=True)
        idx = jnp.min(jnp.where(logits == m, expert, N_EXPERTS), axis=0, keepdims=True)
        hit = expert == idx
        vals.append(m)
        idxs.append(idx)
        member = jnp.where(hit, 1.0, member)
        logits = jnp.where(hit, -jnp.inf, logits)
    p = [jnp.exp(v - vals[0]) for v in vals]
    den = p[0] + p[1] + p[2] + p[3]
    gates = jnp.concatenate([pk / den for pk in p] + [jnp.zeros((SUBLANES - TOP_K, tm), F32)], axis=0)
    gate_ref[...] = jnp.transpose(gates)
    e_ref[...] = jnp.concatenate(idxs, axis=0)
    before = (lax.broadcasted_iota(I32, (tm, tm), 0) < lax.broadcasted_iota(I32, (tm, tm), 1)).astype(BF16)
    prefix = jnp.dot(member.astype(BF16), before, preferred_element_type=F32) + carry_ref[:, 0:1]
    rank_ref[...] = jnp.concatenate(
        [jnp.sum(jnp.where(expert == idx, prefix, 0.0), axis=0, keepdims=True) for idx in idxs], axis=0).astype(I32)
    carry_ref[...] = carry_ref[...] + jnp.sum(member, axis=1, keepdims=True)
    cnt_ref[...] = carry_ref[...].astype(I32)


def _route_params(norm_ffn, w_router, b_router):
    return norm_ffn[None], w_router.T.astype(BF16), b_router[:, None]


def _route(x, route_params, cnt_in, tm):
    n, d = x.shape
    assert n % tm == 0
    norm_ffn, w_router_t, b_router = route_params
    const = lambda arr: pl.BlockSpec(arr.shape, lambda i: (0,) * arr.ndim)
    return pl.pallas_call(
        _route_kernel,
        grid=(n // tm,),
        in_specs=[pl.BlockSpec((tm, d), lambda i: (i, 0)), const(norm_ffn), const(w_router_t), const(b_router),
                  const(cnt_in)],
        out_specs=(pl.BlockSpec((TOP_K, tm), lambda i: (0, i)), pl.BlockSpec((TOP_K, tm), lambda i: (0, i)),
                   pl.BlockSpec((tm, SUBLANES), lambda i: (i, 0)), pl.BlockSpec((N_EXPERTS, LANES), lambda i: (0, 0))),
        out_shape=(jax.ShapeDtypeStruct((TOP_K, n), I32), jax.ShapeDtypeStruct((TOP_K, n), I32),
                   jax.ShapeDtypeStruct((n, SUBLANES), F32), jax.ShapeDtypeStruct((N_EXPERTS, LANES), I32)),
        scratch_shapes=[pltpu.VMEM((N_EXPERTS, LANES), F32)],
        compiler_params=_cparams("arbitrary"),
        name="moe_route",
    )(x, norm_ffn, w_router_t, b_router, cnt_in)


DMA_ROWS_PER_ITER = 8
N_DMA_THREADS = 2
N_PLAN_TAIL = 1 + N_EXPERTS


def _row_tile(r):
    return pl.ds(pl.multiple_of(r * ROW_TILES, ROW_TILES), ROW_TILES)


def _plan_kernel(n_blocks, cnt_ref, plan_ref):
    shift = BM_EXPERT.bit_length() - 1

    def per_expert(e, carry):
        blk0, last_e = carry
        nb = (cnt_ref[e] + (BM_EXPERT - 1)) >> shift
        plan_ref[n_blocks + 1 + e] = blk0 << shift

        def fill(j, cc):
            plan_ref[blk0 + j] = e
            return cc
        lax.fori_loop(0, nb, fill, 0)
        return blk0 + nb, jnp.where(nb > 0, e, last_e)
    n_used, last_e = lax.fori_loop(0, N_EXPERTS, per_expert, (0, 0))

    def tail(j, cc):
        plan_ref[j] = last_e
        return cc
    lax.fori_loop(n_used, n_blocks, tail, 0)
    plan_ref[n_blocks] = n_used


def _moe_plan(cnt, n_blocks):
    smem = pl.BlockSpec(memory_space=pltpu.SMEM)
    return pl.pallas_call(
        functools.partial(_plan_kernel, n_blocks),
        in_specs=[smem], out_specs=smem,
        out_shape=jax.ShapeDtypeStruct((n_blocks + N_PLAN_TAIL,), I32),
        name="moe_plan",
    )(cnt)


POSITION_TILES_PER_STEP = 16


def _positions_kernel(n_blocks, plan_ref, e_ref, r_ref, pos_ref):
    tiles, _, tm = pos_ref.shape
    e = e_ref[...]
    first = jnp.zeros(e.shape, I32)
    for ex in range(N_EXPERTS):
        first = jnp.where(e == ex, plan_ref[n_blocks + 1 + ex], first)
    pos = first + r_ref[...]
    for i in range(tiles):
        pos_ref[i] = pos[:, i * tm:(i + 1) * tm]


def _positions(plan, e, r, n_blocks, tm):
    n = e.shape[1]
    assert n % tm == 0
    nt = n // tm
    per_step = math.gcd(nt, POSITION_TILES_PER_STEP)
    pos = pl.pallas_call(
        functools.partial(_positions_kernel, n_blocks),
        grid=(nt // per_step,),
        in_specs=[pl.BlockSpec(memory_space=pltpu.SMEM), pl.BlockSpec((TOP_K, per_step * tm), lambda i: (0, i)),
                  pl.BlockSpec((TOP_K, per_step * tm), lambda i: (0, i))],
        out_specs=pl.BlockSpec((per_step, TOP_K, tm), lambda i: (i, 0, 0)),
        out_shape=jax.ShapeDtypeStruct((nt, TOP_K, tm), I32),
        compiler_params=_cparams("arbitrary"),
        name="moe_positions",
    )(plan, e, r)
    return pos.reshape(-1)


def _scatter_rows_kernel(n_blocks, cnt_ref, plan_ref, x_ref, pos_ref, xs_ref, poss_ref, g_ref, h_hbm,
                         hbuf, zbuf, sem, zsem):
    t = pl.program_id(0)
    n_tiles = pl.num_programs(0) - 1
    tm = x_ref.shape[0]
    n_s = xs_ref.shape[0]
    bm = BM_EXPERT
    shift = bm.bit_length() - 1
    slot = t % 2

    def zero_rows(first, count, wait):
        for bit in range(shift):
            n = 1 << bit
            lo = first + (count & (n - 1))

            @pl.when((count >> bit) & 1 == 1)
            def _():
                cp = pltpu.make_async_copy(zbuf.at[pl.ds(0, n * ROW_TILES), :], h_hbm.at[_row_tile_n(lo, n), :],
                                           zsem.at[0])
                cp.wait() if wait else cp.start()

    def zero_block(blk, wait):
        cp = pltpu.make_async_copy(zbuf, h_hbm.at[_row_tile_n(blk * bm, bm), :], zsem.at[0])
        cp.wait() if wait else cp.start()

    @pl.when(t == 0)
    def _():
        zbuf[...] = jnp.zeros(zbuf.shape, F32)
        for wait in (False, True):
            def pad_expert(e, cc, wait=wait):
                cnt = cnt_ref[e]
                zero_rows(plan_ref[n_blocks + 1 + e] + cnt, (-cnt) & (bm - 1), wait)
                return cc
            lax.fori_loop(0, N_EXPERTS, pad_expert, 0)

            def pad_block(blk, cc, wait=wait):
                zero_block(blk, wait)
                return cc
            lax.fori_loop(plan_ref[n_blocks], n_blocks, pad_block, 0)

    def wait_copies(sl, n_tok):
        for _ in range(TOP_K):
            pltpu.make_async_copy(hbuf.at[sl, pl.ds(0, n_tok * ROW_TILES), :],
                                  h_hbm.at[pl.ds(0, n_tok * ROW_TILES), :], sem.at[sl]).wait()

    def copy_rows(src_ref, rows_ref, n_tok):
        h = _rms(src_ref[...], g_ref[...])
        for c in range(ROW_TILES):
            hbuf[slot, pl.ds(c, n_tok, stride=ROW_TILES), :] = h[:, c * LANES:(c + 1) * LANES]

        def body(it, carry):
            for u in range(DMA_ROWS_PER_ITER):
                j = it * DMA_ROWS_PER_ITER + u
                for k in range(TOP_K):
                    pltpu.make_async_copy(hbuf.at[slot, _row_tile(j), :],
                                          h_hbm.at[_row_tile(rows_ref[k * n_tok + j]), :],
                                          sem.at[slot]).start(priority=k % N_DMA_THREADS)
            return carry
        lax.fori_loop(0, n_tok // DMA_ROWS_PER_ITER, body, 0)

    @pl.when(t >= 2)
    def _():
        wait_copies(slot, tm)

    @pl.when(t < n_tiles)
    def _():
        copy_rows(x_ref, pos_ref, tm)

    @pl.when(t == n_tiles)
    def _():
        copy_rows(xs_ref, poss_ref, n_s)
        wait_copies(slot, n_s)

        @pl.when(n_tiles >= 1)
        def _():
            wait_copies(1 - slot, tm)


def _row_tile_n(r, n):
    return pl.ds(pl.multiple_of(r * ROW_TILES, ROW_TILES), n * ROW_TILES)


def _scatter_rows(cnt, plan, x_p, pos_p, x_s, pos_s, norm_ffn, n_blocks, tm):
    n_p, d = x_p.shape
    n_s = x_s.shape[0]
    assert n_p % tm == 0 and tm % DMA_ROWS_PER_ITER == 0 and n_s % DMA_ROWS_PER_ITER == 0 and n_s <= tm
    nt = n_p // tm
    smem = lambda: pl.BlockSpec(memory_space=pltpu.SMEM)
    tile_idx = lambda i: jnp.minimum(i, nt - 1)
    return pl.pallas_call(
        functools.partial(_scatter_rows_kernel, n_blocks),
        grid=(nt + 1,),
        in_specs=[smem(), smem(), pl.BlockSpec((tm, d), lambda i: (tile_idx(i), 0)),
                  pl.BlockSpec((TOP_K * tm,), lambda i: (tile_idx(i),), memory_space=pltpu.SMEM),
                  pl.BlockSpec((n_s, d), lambda i: (0, 0)), smem(), pl.BlockSpec((1, d), lambda i: (0, 0))],
        out_specs=pl.BlockSpec(memory_space=pl.ANY),
        out_shape=jax.ShapeDtypeStruct((n_blocks * BM_EXPERT * ROW_TILES, LANES), F32),
        scratch_shapes=[pltpu.VMEM((2, tm * ROW_TILES, LANES), F32), pltpu.VMEM((BM_EXPERT * ROW_TILES, LANES), F32),
                        pltpu.SemaphoreType.DMA((2,)), pltpu.SemaphoreType.DMA((1,))],
        compiler_params=_cparams("arbitrary"),
        name="moe_scatter_rows",
    )(cnt, plan, x_p, pos_p, x_s, pos_s, norm_ffn)


WEIGHT_CAST_ROWS = 128


def _expert_blocks_kernel(n_blocks, plan_ref, x_ref, wgu_ref, bgu_ref, wd_ref, bd_ref, y_ref, wgu_b, wd_b):
    bm = BM_EXPERT
    i = pl.program_id(0)

    @pl.when((i == 0) | (plan_ref[i] != plan_ref[jnp.maximum(i - 1, 0)]))
    def _():
        for src, dst in ((wgu_ref, wgu_b), (wd_ref, wd_b)):
            for r0 in range(0, src.shape[1], WEIGHT_CAST_ROWS):
                rows = slice(r0, r0 + WEIGHT_CAST_ROWS)
                dst[rows, :] = src[0, rows, :].astype(BF16)

    @pl.when(i < plan_ref[n_blocks])
    def _():
        x = jnp.concatenate([x_ref[pl.ds(c, bm, stride=ROW_TILES), :] for c in range(ROW_TILES)], axis=1)
        gu = jnp.dot(x.astype(BF16), wgu_b[...], preferred_element_type=F32) + bgu_ref[0]
        gate = jnp.minimum(gu[:, :D_FF], SWIGLU_LIMIT)
        up = jnp.clip(gu[:, D_FF:], -SWIGLU_LIMIT, SWIGLU_LIMIT)
        act = gate * jax.nn.sigmoid(SWIGLU_ALPHA * gate) * (up + 1.0)
        y = jnp.dot(act.astype(BF16), wd_b[...], preferred_element_type=F32) + bd_ref[0]
        for c in range(ROW_TILES):
            y_ref[pl.ds(c, bm, stride=ROW_TILES), :] = y[:, c * LANES:(c + 1) * LANES]

    @pl.when(i >= plan_ref[n_blocks])
    def _():
        y_ref[...] = jnp.zeros(y_ref.shape, F32)


def _expert_blocks(plan, h_sorted, wgu, bgu, wd, bd, n_blocks):
    bm = BM_EXPERT
    by_expert = lambda *shape: pl.BlockSpec((1,) + shape, lambda i, plan: (plan[i],) + (0,) * len(shape))
    used = lambda i, plan: (jnp.minimum(i, plan[n_blocks] - 1), 0)
    grid_spec = pltpu.PrefetchScalarGridSpec(
        num_scalar_prefetch=1,
        grid=(n_blocks,),
        in_specs=[pl.BlockSpec((bm * ROW_TILES, LANES), used), by_expert(D_MODEL, 2 * D_FF), by_expert(1, 2 * D_FF),
                  by_expert(D_FF, D_MODEL), by_expert(1, D_MODEL)],
        out_specs=pl.BlockSpec((bm * ROW_TILES, LANES), lambda i, plan: (i, 0)),
        scratch_shapes=[pltpu.VMEM(wgu.shape[1:], BF16), pltpu.VMEM(wd.shape[1:], BF16)],
    )
    return pl.pallas_call(
        functools.partial(_expert_blocks_kernel, n_blocks),
        grid_spec=grid_spec,
        out_shape=jax.ShapeDtypeStruct(h_sorted.shape, F32),
        compiler_params=_cparams("arbitrary"),
        name="moe_experts",
    )(plan, h_sorted, wgu, bgu, wd, bd)


def _gather_combine_kernel(x_ref, pos_ref, pos_next_ref, gate_ref, g_ref, y_hbm, o_ref, ybuf, sem):
    t = pl.program_id(0)
    n_tiles = pl.num_programs(0)
    tm = x_ref.shape[0]
    slot = t % 2

    def gather(rows_ref, sl):
        def body(it, carry):
            for u in range(DMA_ROWS_PER_ITER):
                j = it * DMA_ROWS_PER_ITER + u
                for k in range(TOP_K):
                    pltpu.make_async_copy(y_hbm.at[_row_tile(rows_ref[k * tm + j]), :],
                                          ybuf.at[sl, _row_tile(k * tm + j), :],
                                          sem.at[sl]).start(priority=k % N_DMA_THREADS)
            return carry
        lax.fori_loop(0, tm // DMA_ROWS_PER_ITER, body, 0)

    @pl.when(t == 0)
    def _():
        gather(pos_ref, 0)

    @pl.when(t + 1 < n_tiles)
    def _():
        gather(pos_next_ref, 1 - slot)

    pltpu.make_async_copy(y_hbm.at[pl.ds(0, TOP_K * tm * ROW_TILES), :], ybuf.at[slot], sem.at[slot]).wait()
    gates = gate_ref[...]
    cols = []
    for c in range(ROW_TILES):
        acc = x_ref[:, c * LANES:(c + 1) * LANES]
        for k in range(TOP_K):
            acc = acc + gates[:, k:k + 1] * ybuf[slot, pl.ds(k * tm * ROW_TILES + c, tm, stride=ROW_TILES), :]
        cols.append(acc)
    o_ref[...] = _rms(jnp.concatenate(cols, axis=1), g_ref[...])


def _gather_combine(x, pos, gates, norm_final, y_sorted, tm):
    n, d = x.shape
    assert n % tm == 0 and tm % DMA_ROWS_PER_ITER == 0
    nt = n // tm
    rows = lambda off: pl.BlockSpec((TOP_K * tm,), lambda i: (jnp.minimum(i + off, nt - 1),),
                                    memory_space=pltpu.SMEM)
    return pl.pallas_call(
        _gather_combine_kernel,
        grid=(nt,),
        in_specs=[pl.BlockSpec((tm, d), lambda i: (i, 0)), rows(0), rows(1),
                  pl.BlockSpec((tm, SUBLANES), lambda i: (i, 0)), pl.BlockSpec((1, d), lambda i: (0, 0)),
                  pl.BlockSpec(memory_space=pl.ANY)],
        out_specs=pl.BlockSpec((tm, d), lambda i: (i, 0)),
        out_shape=jax.ShapeDtypeStruct((n, d), F32),
        scratch_shapes=[pltpu.VMEM((2, TOP_K * tm * ROW_TILES, LANES), F32), pltpu.SemaphoreType.DMA((2,))],
        compiler_params=_cparams("arbitrary"),
        name="moe_combine",
    )(x, pos, pos, gates, norm_final, y_sorted)


def _moe_and_final_norm(x_p, routing_p, x_s, routing_s, norm_ffn, w_gate_up, b_gate_up, w_down, b_down, norm_final):
    n_p, n_s = x_p.shape[0], x_s.shape[0]
    n_rows = (n_p + n_s) * TOP_K
    n_blocks = -(-(n_rows + N_EXPERTS * (BM_EXPERT - 1)) // BM_EXPERT)
    g_ffn = norm_ffn[None]
    e_s, r_s, gates_s, _ = routing_s
    e_p, r_p, gates_p, cnt = routing_p
    plan = _moe_plan(cnt[:, 0], n_blocks)
    pos_p = _positions(plan, e_p, r_p, n_blocks, TM_COMBINE)
    pos_s = _positions(plan, e_s, r_s, n_blocks, n_s)
    h_sorted = _scatter_rows(cnt[:, 0], plan, x_p, pos_p, x_s, pos_s, g_ffn, n_blocks, TM_COMBINE)
    y_sorted = _expert_blocks(plan, h_sorted, w_gate_up, b_gate_up[:, None, :], w_down, b_down[:, None, :], n_blocks)
    y_p = _gather_combine(x_p, pos_p, gates_p, norm_final[None], y_sorted, TM_COMBINE)
    y_s = _gather_combine(x_s, pos_s, gates_s, norm_final[None], y_sorted, n_s)
    return y_p, y_s


def _rb(x):
    return x.astype(BF16).astype(F32)


def _sample_proj_kernel(x_ref, g_ref, wm_ref, wg_ref, z_ref):
    hb = _rms(x_ref[...], g_ref[...]).astype(BF16)
    z_ref[:, :MIX_WIDTH] = jnp.dot(hb, wm_ref[...], preferred_element_type=F32)
    z_ref[:, MIX_WIDTH:] = jnp.dot(hb, wg_ref[...], preferred_element_type=F32)


def _sample_proj(x, norm_mix, w_mix, w_gate):
    n, d = x.shape
    args = (x, norm_mix, w_mix, w_gate)
    width = w_mix.shape[1] + w_gate.shape[1]
    return pl.pallas_call(
        _sample_proj_kernel,
        grid=(1,),
        in_specs=[pl.BlockSpec(a.shape, lambda j: (0, 0)) for a in args],
        out_specs=pl.BlockSpec((n, width), lambda j: (0, 0)),
        out_shape=jax.ShapeDtypeStruct((n, width), F32),
        compiler_params=_cparams("arbitrary"),
        name="sample_proj",
    )(*args)


def _sample_mix_kernel(z_ref, c0_ref, c1_ref, c2_ref, mem_ref, sb0_ref, sb1_ref, sb2_ref, nb_ref, lng_ref, lnb_ref,
                       wsp_ref, bsp_ref, bo_ref, co_ref, a_ref, v_ref, n0_ref, n1_ref, n2_ref):
    z = z_ref[0]
    scale = 1.0 / math.sqrt(B_HEAD_DIM)
    row_head = lax.broadcasted_iota(I32, (SUBLANES, B_GROUP_W), 0)
    lane_head = lax.broadcasted_iota(I32, (SUBLANES, B_GROUP_W), 1) // B_HEAD_DIM
    own = (row_head == lane_head).astype(F32)
    kv_cols = jnp.transpose(jnp.broadcast_to(z[:, B_WIDTH:3 * B_WIDTH], (SUBLANES, 2 * B_WIDTH)))[:, 0:1]
    outs, lses = [], []
    for g, (cache_ref, sb_ref, new_ref) in enumerate(((c0_ref, sb0_ref, n0_ref), (c1_ref, sb1_ref, n1_ref),
                                                      (c2_ref, sb2_ref, n2_ref))):
        window = cache_ref.shape[2]
        q, kn, vn = (_rb(z[:, i * B_WIDTH + g * B_GROUP_W:i * B_WIDTH + (g + 1) * B_GROUP_W]) for i in range(3))
        qh = own * q
        kt = cache_ref[0, :B_GROUP_W, :].astype(BF16)
        vt = cache_ref[0, B_GROUP_W:, :].astype(BF16)
        lg = jnp.dot(qh.astype(BF16), kt, preferred_element_type=F32) * scale + sb_ref[...]
        ln = jnp.sum(qh * kn, axis=1, keepdims=True) * scale + nb_ref[g, :, 0:1]
        m = jnp.maximum(jnp.max(lg, axis=1, keepdims=True), ln)
        lse = m + jnp.log(jnp.sum(jnp.exp(lg - m), axis=1, keepdims=True) + jnp.exp(ln - m))
        p = jnp.exp(lg - lse).astype(BF16)
        pn = _rb(jnp.exp(ln - lse))
        pv = lax.dot_general(p, vt, (((1,), (1,)), ((), ())), preferred_element_type=F32) + pn * vn
        outs.append(jnp.sum(own * pv, axis=0, keepdims=True))
        lses.append(jnp.sum(own * lse, axis=0, keepdims=True))
        new_col = jnp.concatenate([kv_cols[i * B_WIDTH + g * B_GROUP_W:i * B_WIDTH + (g + 1) * B_GROUP_W]
                                   for i in range(2)], axis=0)
        lane = lax.broadcasted_iota(I32, (1, window), 1)
        new_ref[0] = jnp.where(lane == window - 1, new_col, pltpu.roll(cache_ref[0], window - 1, axis=1))
    m = jnp.maximum(jnp.maximum(lses[0], lses[1]), lses[2])
    ws = [jnp.exp(l - m) for l in lses]
    den = ws[0] + ws[1] + ws[2]
    bo_ref[0] = _rb(ws[0] / den) * _rb(outs[0]) + _rb(ws[1] / den) * _rb(outs[1]) + _rb(ws[2] / den) * _rb(outs[2])

    cs = []
    for h in range(C_HEADS):
        c0 = MIX_WIDTH - C_WIDTH + h * C_HEAD_DIM
        qh = _rb(z[:, c0:c0 + C_HEAD_DIM])
        kh = _rb(mem_ref[0, pl.ds(h, N_MEM, stride=2 * C_HEADS), :])
        vh = _rb(mem_ref[0, pl.ds(C_HEADS + h, N_MEM, stride=2 * C_HEADS), :])
        s = jnp.sum(kh * qh, axis=-1, keepdims=True) * (1.0 / math.sqrt(C_HEAD_DIM))
        p = jnp.exp(s - jnp.max(s, axis=0, keepdims=True))
        p = _rb(p / jnp.sum(p, axis=0, keepdims=True))
        cs.append(jnp.sum(p * vh, axis=0, keepdims=True))
    co_ref[0] = jnp.concatenate(cs, axis=1)

    u = _gelu(z[:, 3 * B_WIDTH:3 * B_WIDTH + A_WIDTH])
    v = _layernorm(_gelu(z[:, 3 * B_WIDTH + A_WIDTH:3 * B_WIDTH + 2 * A_WIDTH]), lng_ref[...], lnb_ref[...])
    v_ref[0] = v
    a_ref[0] = u * (_rb(wsp_ref[...]) * _rb(v) + bsp_ref[...])


def _sample_mix(z, caches, mem, sbias, nbias, ln_g, ln_b, wsp0, bsp0):
    n = z.shape[0]
    const = lambda arr: pl.BlockSpec(arr.shape, lambda i: (0,) * arr.ndim)
    row = lambda w: pl.BlockSpec((1, 1, w), lambda i: (i, 0, 0))
    per_req = lambda arr: pl.BlockSpec((1,) + arr.shape[1:], lambda i: (i, 0, 0))
    widths = (B_GROUP_W, C_WIDTH, A_WIDTH, A_WIDTH)
    return pl.pallas_call(
        _sample_mix_kernel,
        grid=(n,),
        in_specs=[row(z.shape[2])] + [per_req(c) for c in caches] + [per_req(mem)]
        + [const(a) for a in (*sbias, nbias, ln_g, ln_b, wsp0, bsp0)],
        out_specs=tuple(row(w) for w in widths) + tuple(per_req(c) for c in caches),
        out_shape=tuple(jax.ShapeDtypeStruct((n, 1, w), F32) for w in widths)
        + tuple(jax.ShapeDtypeStruct(c.shape, F32) for c in caches),
        compiler_params=_cparams("arbitrary"),
        name="sample_mix",
    )(z, *caches, mem, *sbias, nbias, ln_g, ln_b, wsp0, bsp0)


def _sample_merge_kernel(x_ref, z_ref, a_ref, b_ref, c_ref, wa_ref, wb_ref, wc_ref, wo_ref, o_ref):
    dot = lambda a, b: jnp.dot(a.astype(BF16), b, preferred_element_type=F32)
    merged = None
    for i, (act_ref, w_ref) in enumerate(((a_ref, wa_ref), (b_ref, wb_ref), (c_ref, wc_ref))):
        gate = jax.nn.sigmoid(z_ref[:, MIX_WIDTH + i * D_MODEL:MIX_WIDTH + (i + 1) * D_MODEL])
        term = gate * dot(act_ref[...], w_ref[...])
        merged = term if merged is None else merged + term
    o_ref[...] = x_ref[...] + dot(merged, wo_ref[...])


def _sample_merge(x, z, a, bo, co, wa, wb, wc, wo):
    args = (x, z, a, bo, co, wa, wb, wc, wo)
    return pl.pallas_call(
        _sample_merge_kernel,
        grid=(1,),
        in_specs=[pl.BlockSpec(arr.shape, lambda i: (0, 0)) for arr in args],
        out_specs=pl.BlockSpec(x.shape, lambda i: (0, 0)),
        out_shape=jax.ShapeDtypeStruct(x.shape, F32),
        compiler_params=_cparams("arbitrary"),
        name="sample_merge",
    )(*args)


def _sample_mixers(x_sample, caches, cache_mem, sbias, nbias, norm_mix, ln_v_g, ln_v_b, w_spatial, b_spatial, wts):
    n = x_sample.shape[0]
    x = x_sample[:, 0]
    z = _sample_proj(x, norm_mix[None], wts["mix"], wts["gate"])
    mem = cache_mem.reshape(n, N_MEM * 2 * C_HEADS, C_HEAD_DIM)
    wsp0 = jnp.repeat(w_spatial[:, 0, 0], LANES)[None]
    bsp0 = jnp.repeat(b_spatial[:, 0], LANES)[None]
    bo, co, a, v, *new_caches = _sample_mix(z[:, None], [_from_window_layout(c) for c in caches], mem, sbias, nbias,
                                            ln_v_g[None], ln_v_b[None], wsp0, bsp0)
    x_new = _sample_merge(x, z, a[:, 0], bo[:, 0], co[:, 0], wts["a"], wts["b"], wts["c"], wts["out"])
    return x_new, [_to_window_layout(c) for c in new_caches], v[:, 0]


def _mixer_weights(w_in, w_branch_a, w_branch_b, w_branch_c, w_out):
    return {"mix": w_in[:, :MIX_WIDTH].astype(BF16), "gate": w_in[:, MIX_WIDTH:].astype(BF16),
            "a": w_branch_a.astype(BF16), "b": w_branch_b.astype(BF16), "c": w_branch_c.astype(BF16),
            "out": w_out.astype(BF16)}


def kernel(x_prompt, x_sample, cache_win0_kv, cache_win1_kv, cache_win2_kv, cache_mem_kv, mem_prompt, rel_bias,
           norm_mix, norm_mem, w_in, ln_v_g, ln_v_b, w_spatial, b_spatial, w_mem_kv, w_branch_a, w_branch_b,
           w_branch_c, w_out, norm_ffn, w_router, b_router, w_gate_up, b_gate_up, w_down, b_down, norm_final):
    assert norm_mix.shape[0] == 1, "one layer"
    b, s, d = x_prompt.shape
    n_s = x_sample.shape[0]
    caches = (cache_win0_kv[0], cache_win1_kv[0], cache_win2_kv[0])
    assert all(c.shape[1] == w for c, (w, _) in zip(caches, B_PAIRS)), "window buffers hold a full window"
    wts = _mixer_weights(w_in[0], w_branch_a[0], w_branch_b[0], w_branch_c[0], w_out[0])
    band_bias, *sbias, nbias = _bias_tables(rel_bias)
    route_params = _route_params(norm_ffn[0], w_router[0], b_router[0])
    xs, win_s, v_s = _sample_mixers(x_sample, caches, cache_mem_kv[0], sbias, nbias, norm_mix[0], ln_v_g[0],
                                    ln_v_b[0], w_spatial[0], b_spatial[0], wts)
    routing_s = _route(xs, route_params, jnp.zeros((N_EXPERTS, LANES), I32), n_s)
    xp, routing_p, win_p, mem_p = _prompt_mixers(x_prompt, mem_prompt, band_bias, norm_mix[0], norm_mem[0], ln_v_g[0],
                                                 ln_v_b[0], w_spatial[0], b_spatial[0], w_mem_kv[0], wts,
                                                 route_params, routing_s[3])
    y_p, y_s = _moe_and_final_norm(xp.reshape(b * s, d), routing_p, xs, routing_s, norm_ffn[0], w_gate_up[0],
                                   b_gate_up[0], w_down[0], b_down[0], norm_final)
    chunk_v = v_s.reshape(1, n_s, 1, A_GROUPS, LANES)
    return (y_p.reshape(b, s, d), y_s[:, None], win_p[0], win_p[1], win_p[2], mem_p,
            win_s[0], win_s[1], win_s[2], chunk_v)
```

```python
import functools
import math

import numpy as np
import jax
import jax.numpy as jnp
from jax import lax
from jax.experimental import pallas as pl
from jax.experimental.pallas import tpu as pltpu

F32 = jnp.float32
BF16 = jnp.bfloat16
I32 = jnp.int32

D_MODEL = 1024
N_MEM = 256
CHUNK = 128
A_GROUPS = 4
A_WIDTH = 512
B_PAIRS = ((128, 1), (512, 4), (2048, 16))
B_GROUP_W = 256
B_WIDTH = 768
B_HEAD_DIM = 64
BAND = 128
C_HEADS = 4
C_HEAD_DIM = 128
C_WIDTH = 512
REL_BUCKETS = 32
REL_MAX_DIST = 2048
N_EXPERTS = 32
TOP_K = 4
D_FF = 1024
SWIGLU_LIMIT = 7.0
SWIGLU_ALPHA = 1.702
EPS = 1e-6
NEG_INF = -1e30
MIX_WIDTH = 3 * B_WIDTH + 2 * A_WIDTH + C_WIDTH

LANES = 128
SUBLANES = 8
ROW_TILES = D_MODEL // LANES
VMEM_LIMIT = 56 * 1024 * 1024

TM_PROJ = 1024
TM_MERGE = 512
BM_EXPERT = 512
TM_COMBINE = 512
ATTN_BLOCKS_PER_ITER = 6


def _cparams(*sem):
    return pltpu.CompilerParams(dimension_semantics=sem, vmem_limit_bytes=VMEM_LIMIT)


def _t5_bucket(dist):
    dist = np.maximum(np.asarray(dist), 0)
    max_exact = REL_BUCKETS // 2
    log_ratio = np.log(np.maximum(dist, max_exact) / max_exact) / math.log(REL_MAX_DIST / max_exact)
    large = np.minimum(max_exact + (log_ratio * (REL_BUCKETS - max_exact)).astype(np.int32), REL_BUCKETS - 1)
    return np.where(dist < max_exact, dist, large).astype(np.int32)


def _gelu(x):
    c = math.sqrt(2.0 / math.pi)
    return x * (0.5 * (1.0 + jnp.tanh(c * (x + 0.044715 * (x * x * x)))))


def _rms(x, g):
    return x * lax.rsqrt(jnp.mean(x * x, axis=-1, keepdims=True) + EPS) * g


def _layernorm(x, g, b):
    xc = x - jnp.mean(x, axis=-1, keepdims=True)
    return xc * lax.rsqrt(jnp.mean(xc * xc, axis=-1, keepdims=True) + EPS) * g + b


def _proj_kernel(x_ref, g_ref, w_ref, lng_ref, lnb_ref, wsp_ref, bsp_ref,
                 qkv0_ref, qkv1_ref, qkv2_ref, win0_ref, win1_ref, win2_ref, a_ref, qc_ref, scr_ref):
    t = pl.program_id(1)
    last = pl.num_programs(1) - 1
    tm = x_ref.shape[1]
    hb = _rms(x_ref[0], g_ref[...]).astype(BF16)

    def proj(c0, width):
        return jnp.dot(hb, w_ref[:, c0:c0 + width], preferred_element_type=F32)

    for g, (_, dil) in enumerate(B_PAIRS):
        q, k, v = (proj(i * B_WIDTH + g * B_GROUP_W, B_GROUP_W) for i in range(3))
        kv = jnp.concatenate([k, v], axis=1)
        if g == 2:
            win2_ref[0] = jnp.transpose(kv)
        elif g == 1:
            @pl.when(t == last)
            def _():
                win1_ref[0] = jnp.transpose(kv[tm - B_PAIRS[1][0]:])
        else:
            @pl.when(t == last)
            def _():
                win0_ref[0] = jnp.transpose(kv[tm - B_PAIRS[0][0]:])
        if dil == 1:
            qkv0_ref[0] = jnp.concatenate([q, k, v], axis=1).astype(BF16)
            continue
        out_ref = qkv1_ref if g == 1 else qkv2_ref
        for i, arr in enumerate((q, k, v)):
            for ct in range(2):
                scr_ref[2 * i + ct] = arr[:, ct * LANES:(ct + 1) * LANES]
        for r in range(dil):
            rows = [scr_ref[j, pl.ds(r, tm // dil, stride=dil), :] for j in range(6)]
            out_ref[0, r] = jnp.concatenate(rows, axis=1).astype(BF16)

    u = _gelu(proj(3 * B_WIDTH, A_WIDTH))
    v = _layernorm(_gelu(proj(3 * B_WIDTH + A_WIDTH, A_WIDTH)), lng_ref[...], lnb_ref[...]).astype(BF16)
    for c in range(tm // CHUNK):
        rs = slice(c * CHUNK, (c + 1) * CHUNK)
        for g in range(A_GROUPS):
            cs = slice(g * LANES, (g + 1) * LANES)
            s = jnp.dot(wsp_ref[g], v[rs, cs], preferred_element_type=F32) + bsp_ref[:, g:g + 1]
            a_ref[0, rs, cs] = (u[rs, cs] * s).astype(BF16)

    qc_ref[0] = proj(3 * B_WIDTH + 2 * A_WIDTH, C_WIDTH).astype(BF16)


def _prompt_proj(x, norm_mix, w_mix, ln_g, ln_b, wsp, bsp):
    b, s, d = x.shape
    tm = TM_PROJ
    assert s % tm == 0 and s >= B_PAIRS[2][0] and tm >= B_PAIRS[1][0]
    nt = s // tm
    const = lambda *shape: pl.BlockSpec(shape, lambda i, j: (0,) * len(shape))
    out_shape = (
        jax.ShapeDtypeStruct((b, s, 3 * B_GROUP_W), BF16),
        jax.ShapeDtypeStruct((b, 4, s // 4, 3 * B_GROUP_W), BF16),
        jax.ShapeDtypeStruct((b, 16, s // 16, 3 * B_GROUP_W), BF16),
        jax.ShapeDtypeStruct((b, 2 * B_GROUP_W, B_PAIRS[0][0]), F32),
        jax.ShapeDtypeStruct((b, 2 * B_GROUP_W, B_PAIRS[1][0]), F32),
        jax.ShapeDtypeStruct((b, 2 * B_GROUP_W, s), F32),
        jax.ShapeDtypeStruct((b, s, A_WIDTH), BF16),
        jax.ShapeDtypeStruct((b, s, C_WIDTH), BF16),
    )
    out_specs = (
        pl.BlockSpec((1, tm, 3 * B_GROUP_W), lambda i, j: (i, j, 0)),
        pl.BlockSpec((1, 4, tm // 4, 3 * B_GROUP_W), lambda i, j: (i, 0, j, 0)),
        pl.BlockSpec((1, 16, tm // 16, 3 * B_GROUP_W), lambda i, j: (i, 0, j, 0)),
        pl.BlockSpec((1, 2 * B_GROUP_W, B_PAIRS[0][0]), lambda i, j: (i, 0, 0)),
        pl.BlockSpec((1, 2 * B_GROUP_W, B_PAIRS[1][0]), lambda i, j: (i, 0, 0)),
        pl.BlockSpec((1, 2 * B_GROUP_W, tm), lambda i, j: (i, 0, j)),
        pl.BlockSpec((1, tm, A_WIDTH), lambda i, j: (i, j, 0)),
        pl.BlockSpec((1, tm, C_WIDTH), lambda i, j: (i, j, 0)),
    )
    return pl.pallas_call(
        _proj_kernel,
        grid=(b, nt),
        in_specs=[pl.BlockSpec((1, tm, d), lambda i, j: (i, j, 0)), const(1, d), const(d, MIX_WIDTH),
                  const(1, A_WIDTH), const(1, A_WIDTH), const(A_GROUPS, CHUNK, CHUNK), const(CHUNK, A_GROUPS)],
        out_specs=out_specs,
        out_shape=out_shape,
        scratch_shapes=[pltpu.VMEM((6, tm, LANES), F32)],
        compiler_params=_cparams("arbitrary", "arbitrary"),
        name="prompt_proj",
    )(x, norm_mix, w_mix, ln_g, ln_b, wsp, bsp)


def _attend(q, k, v, bias_ref, g, k0, hmask):
    kn = k.shape[0]
    qs = jnp.concatenate([q * hm for hm in hmask], axis=0)
    s = lax.dot_general(qs, k, (((1,), (1,)), ((), ())), preferred_element_type=F32) + bias_ref[g, :, k0:k0 + kn]
    m = jnp.max(s, axis=-1, keepdims=True)
    p = jnp.exp(s - m)
    l = jnp.sum(p, axis=-1, keepdims=True)
    pv = jnp.dot(p.astype(BF16), v, preferred_element_type=F32) / l
    lse_rows = m + jnp.log(l)
    out = jnp.zeros((BAND, B_GROUP_W), F32)
    lse = jnp.zeros((BAND, B_GROUP_W), F32)
    for h, hm in enumerate(hmask):
        rows = slice(h * BAND, (h + 1) * BAND)
        sel = hm > 0
        out = jnp.where(sel, pv[rows], out)
        lse = jnp.where(sel, lse_rows[rows], lse)
    return out, lse


def _attn_kernel(q0_ref, q1_ref, q2_ref, bias_ref, o_ref, out_ref, lse_ref):
    s = q0_ref.shape[1]
    lane_head = lax.broadcasted_iota(I32, (1, B_GROUP_W), 1) // B_HEAD_DIM
    scale = 1.0 / math.sqrt(B_HEAD_DIM)
    hmask = [jnp.where(lane_head == h, scale, 0.0).astype(BF16) for h in range(4)]
    qs, ks, vs = (slice(i * B_GROUP_W, (i + 1) * B_GROUP_W) for i in range(3))

    def store(g, start, dil, out, lse):
        rows = pl.ds(start, BAND) if dil == 1 else pl.ds(start, BAND, stride=dil)
        for ct in range(2):
            out_ref[g, ct, rows, :] = out[:, ct * LANES:(ct + 1) * LANES]
            lse_ref[g, ct, rows, :] = lse[:, ct * LANES:(ct + 1) * LANES]

    for g, (_, dil) in enumerate(B_PAIRS):
        n = s // dil
        nb = n // BAND

        def load(rows, cols, r, g=g):
            if g == 0:
                return q0_ref[0, rows, cols]
            return (q1_ref if g == 1 else q2_ref)[0, r, rows, cols]

        def first_block(r, g=g, dil=dil, load=load):
            rows = pl.ds(0, BAND)
            out, lse = _attend(load(rows, qs, r), load(rows, ks, r), load(rows, vs, r), bias_ref, g, BAND, hmask)
            store(g, r, dil, out, lse)

        def later_block(i, g=g, dil=dil, nb=nb, load=load):
            r = i // (nb - 1)
            qb = i % (nb - 1) + 1
            q0 = pl.multiple_of(qb * BAND, BAND)
            rows_q = pl.ds(q0, BAND)
            rows_k = pl.ds(q0 - BAND, 2 * BAND)
            out, lse = _attend(load(rows_q, qs, r), load(rows_k, ks, r), load(rows_k, vs, r), bias_ref, g, 0, hmask)
            store(g, qb * (BAND * dil) + r, dil, out, lse)

        def run(block_fn, count):
            per = ATTN_BLOCKS_PER_ITER

            def body(it, carry):
                for u in range(per):
                    block_fn(it * per + u)
                return carry
            lax.fori_loop(0, count // per, body, 0)
            for i in range(count - count % per, count):
                block_fn(i)

        run(first_block, dil)
        run(later_block, dil * (nb - 1))

    rc = 256
    def combine(c, carry):
        rows = pl.ds(pl.multiple_of(c * rc, rc), rc)
        for ct in range(2):
            ls = [lse_ref[g, ct, rows, :] for g in range(3)]
            m = jnp.maximum(jnp.maximum(ls[0], ls[1]), ls[2])
            ws = [jnp.exp(l - m) for l in ls]
            den = ws[0] + ws[1] + ws[2]
            num = ws[0] * out_ref[0, ct, rows, :] + ws[1] * out_ref[1, ct, rows, :] + ws[2] * out_ref[2, ct, rows, :]
            o_ref[0, rows, ct * LANES:(ct + 1) * LANES] = (num / den).astype(BF16)
        return carry
    lax.fori_loop(0, s // rc, combine, 0)


def _prompt_attention(qkv0, qkv1, qkv2, bias):
    b, s, w = qkv0.shape
    return pl.pallas_call(
        _attn_kernel,
        grid=(b,),
        in_specs=[pl.BlockSpec((1, s, w), lambda i: (i, 0, 0)),
                  pl.BlockSpec((1, 4, s // 4, w), lambda i: (i, 0, 0, 0)),
                  pl.BlockSpec((1, 16, s // 16, w), lambda i: (i, 0, 0, 0)),
                  pl.BlockSpec(bias.shape, lambda i: (0, 0, 0))],
        out_specs=pl.BlockSpec((1, s, B_GROUP_W), lambda i: (i, 0, 0)),
        out_shape=jax.ShapeDtypeStruct((b, s, B_GROUP_W), BF16),
        scratch_shapes=[pltpu.VMEM((3, 2, s, LANES), F32), pltpu.VMEM((3, 2, s, LANES), F32)],
        compiler_params=_cparams("arbitrary"),
        name="prompt_attention",
    )(qkv0, qkv1, qkv2, bias)


def _bias_kernel(rel_ref, band_idx_ref, s0_ref, s1_ref, s2_ref, band_ref, sb0_ref, sb1_ref, sb2_ref, nb_ref):
    def lookup(idx, col):
        acc = jnp.full(idx.shape, NEG_INF, F32)
        for bucket in range(REL_BUCKETS):
            acc = jnp.where(idx == bucket, rel_ref[bucket, col], acc)
        return acc

    for g, (s_ref, sb_ref) in enumerate(((s0_ref, sb0_ref), (s1_ref, sb1_ref), (s2_ref, sb2_ref))):
        width = s_ref.shape[1]
        for h in range(4):
            band_ref[g, h * BAND:(h + 1) * BAND, :] = lookup(band_idx_ref[g], 4 * g + h)
        sb_ref[...] = jnp.concatenate([lookup(s_ref[...], 4 * g + h) for h in range(4)]
                                      + [jnp.full((SUBLANES - 4, width), NEG_INF, F32)], axis=0)
        nb_ref[g] = jnp.concatenate([jnp.full((1, LANES), rel_ref[0, 4 * g + h], F32) for h in range(4)]
                                    + [jnp.zeros((SUBLANES - 4, LANES), F32)], axis=0)


def _bias_tables(rel_bias):
    steps = np.arange(BAND)[:, None] + BAND - np.arange(2 * BAND)[None, :]
    valid = (steps >= 0) & (steps <= BAND)
    band_idx = np.stack([np.where(valid, _t5_bucket(np.clip(steps, 0, BAND) * dil), -1) for _, dil in B_PAIRS])
    s_idx = []
    for window, dil in B_PAIRS:
        w = np.arange(window)
        s_idx.append(np.where(w % dil == 0, _t5_bucket(window - w), -1)[None].astype(np.int32))
    args = (rel_bias, jnp.asarray(band_idx.astype(np.int32))) + tuple(jnp.asarray(s) for s in s_idx)
    vmem = lambda a: pl.BlockSpec(a.shape, lambda i: (0,) * a.ndim)
    out_shape = (jax.ShapeDtypeStruct((3, 4 * BAND, 2 * BAND), F32),) + tuple(
        jax.ShapeDtypeStruct((SUBLANES, window), F32) for window, _ in B_PAIRS) + (
        jax.ShapeDtypeStruct((3, SUBLANES, LANES), F32),)
    return pl.pallas_call(
        _bias_kernel,
        grid=(1,),
        in_specs=[pl.BlockSpec(memory_space=pltpu.SMEM)] + [vmem(a) for a in args[1:]],
        out_specs=tuple(pl.BlockSpec(s.shape, lambda i, n=len(s.shape): (0,) * n) for s in out_shape),
        out_shape=out_shape,
        compiler_params=_cparams("arbitrary"),
        name="bias_tables",
    )(*args)


def _memkv_kernel(mem_ref, g_ref, w_ref, kv_ref, kvb_ref):
    hb = _rms(mem_ref[0], g_ref[...]).astype(BF16)
    kv = jnp.dot(hb, w_ref[...], preferred_element_type=F32)
    m = kv.shape[0]
    n_heads = kv.shape[1] // C_HEAD_DIM
    for j in range(n_heads):
        kv_ref[0, pl.ds(j, m, stride=n_heads), :] = kv[:, j * C_HEAD_DIM:(j + 1) * C_HEAD_DIM]
    kvb_ref[0] = kv.astype(BF16)


def _memory_kv(mem, norm_mem, w_mem):
    b, m, d = mem.shape
    w = w_mem.shape[1]
    rows = m * w // C_HEAD_DIM
    return pl.pallas_call(
        _memkv_kernel,
        grid=(b,),
        in_specs=[pl.BlockSpec((1, m, d), lambda i: (i, 0, 0)), pl.BlockSpec((1, d), lambda i: (0, 0)),
                  pl.BlockSpec((d, w), lambda i: (0, 0))],
        out_specs=(pl.BlockSpec((1, rows, C_HEAD_DIM), lambda i: (i, 0, 0)),
                   pl.BlockSpec((1, m, w), lambda i: (i, 0, 0))),
        out_shape=(jax.ShapeDtypeStruct((b, rows, C_HEAD_DIM), F32), jax.ShapeDtypeStruct((b, m, w), BF16)),
        compiler_params=_cparams("arbitrary"),
        name="memory_kv",
    )(mem, norm_mem, w_mem)


def _merge_kernel(x_ref, g_ref, a_ref, b_ref, qc_ref, kv_ref, wg_ref, wa_ref, wb_ref, wc_ref, wo_ref,
                  gf_ref, wr_ref, br_ref, cnt_in_ref, o_ref, e_ref, rank_ref, gate_ref, cnt_ref, carry_ref):
    x = x_ref[0]
    hb = _rms(x, g_ref[...]).astype(BF16)
    qc = qc_ref[0]
    cs = []
    for h in range(C_HEADS):
        hs = slice(h * C_HEAD_DIM, (h + 1) * C_HEAD_DIM)
        s = lax.dot_general(qc[:, hs], kv_ref[0, :, hs], (((1,), (1,)), ((), ())), preferred_element_type=F32)
        s = s * (1.0 / math.sqrt(C_HEAD_DIM))
        p = jnp.exp(s - jnp.max(s, axis=-1, keepdims=True))
        p = p / jnp.sum(p, axis=-1, keepdims=True)
        vs = slice(C_WIDTH + h * C_HEAD_DIM, C_WIDTH + (h + 1) * C_HEAD_DIM)
        cs.append(jnp.dot(p.astype(BF16), kv_ref[0, :, vs], preferred_element_type=F32))
    c = jnp.concatenate(cs, axis=1).astype(BF16)
    branches = ((a_ref[0], wa_ref), (b_ref[0], wb_ref), (c, wc_ref))
    merged = None
    for i, (act, w_ref) in enumerate(branches):
        gate = jax.nn.sigmoid(jnp.dot(hb, wg_ref[:, i * D_MODEL:(i + 1) * D_MODEL], preferred_element_type=F32))
        term = gate * jnp.dot(act, w_ref[...], preferred_element_type=F32)
        merged = term if merged is None else merged + term
    x_new = x + jnp.dot(merged.astype(BF16), wo_ref[...], preferred_element_type=F32)
    o_ref[0] = x_new
    first_step = (pl.program_id(0) == 0) & (pl.program_id(1) == 0)
    _route_rows(x_new, first_step, gf_ref, wr_ref, br_ref, cnt_in_ref, e_ref, rank_ref, gate_ref, cnt_ref, carry_ref)


def _prompt_merge(x, norm_mix, a, bo, qc, kvb, wg, wa, wb, wc, wo, route_params, cnt_in):
    b, s, d = x.shape
    tm = TM_MERGE
    nt = s // tm
    n = b * s
    tile = lambda w: pl.BlockSpec((1, tm, w), lambda i, j: (i, j, 0))
    const = lambda arr: pl.BlockSpec(arr.shape, lambda i, j: (0,) * arr.ndim)
    ids = pl.BlockSpec((TOP_K, tm), lambda i, j: (0, i * nt + j))
    return pl.pallas_call(
        _merge_kernel,
        grid=(b, nt),
        in_specs=[tile(d), const(norm_mix), tile(A_WIDTH), tile(B_GROUP_W), tile(C_WIDTH),
                  pl.BlockSpec((1,) + kvb.shape[1:], lambda i, j: (i, 0, 0)),
                  const(wg), const(wa), const(wb), const(wc), const(wo)]
        + [const(p) for p in route_params] + [const(cnt_in)],
        out_specs=(tile(d), ids, ids, pl.BlockSpec((tm, SUBLANES), lambda i, j: (i * nt + j, 0)),
                   pl.BlockSpec((N_EXPERTS, LANES), lambda i, j: (0, 0))),
        out_shape=(jax.ShapeDtypeStruct((b, s, d), F32), jax.ShapeDtypeStruct((TOP_K, n), I32),
                   jax.ShapeDtypeStruct((TOP_K, n), I32), jax.ShapeDtypeStruct((n, SUBLANES), F32),
                   jax.ShapeDtypeStruct((N_EXPERTS, LANES), I32)),
        scratch_shapes=[pltpu.VMEM((N_EXPERTS, LANES), F32)],
        compiler_params=_cparams("arbitrary", "arbitrary"),
        name="prompt_merge",
    )(x, norm_mix, a, bo, qc, kvb, wg, wa, wb, wc, wo, *route_params, cnt_in)


def _to_window_layout(w):
    b, _, window = w.shape
    return jnp.transpose(w.reshape(b, 2, 4, B_HEAD_DIM, window), (0, 4, 1, 2, 3))[None]


def _from_window_layout(c):
    b, window = c.shape[:2]
    return jnp.transpose(c, (0, 2, 3, 4, 1)).reshape(b, 2 * B_GROUP_W, window)


def _prompt_mixers(x_prompt, mem_prompt, band_bias, norm_mix, norm_mem, ln_v_g, ln_v_b, w_spatial, b_spatial,
                   w_mem_kv, wts, route_params, cnt_in):
    b, s, _ = x_prompt.shape
    wsp = jnp.where(np.tril(np.ones((CHUNK, CHUNK), bool)), w_spatial, 0.0).astype(BF16)
    qkv0, qkv1, qkv2, win0, win1, win2, a_out, qc = _prompt_proj(
        x_prompt, norm_mix[None], wts["mix"], ln_v_g[None], ln_v_b[None], wsp, b_spatial.T)
    b_out = _prompt_attention(qkv0, qkv1, qkv2, band_bias)
    mem_kv, mem_kv_b = _memory_kv(mem_prompt, norm_mem[None], w_mem_kv.astype(BF16))
    x_new, *routing = _prompt_merge(x_prompt, norm_mix[None], a_out, b_out, qc, mem_kv_b, wts["gate"],
                                    wts["a"], wts["b"], wts["c"], wts["out"], route_params, cnt_in)
    wins = [_to_window_layout(w) for w in (win0, win1, win2)]
    return x_new, routing, wins, mem_kv.reshape(1, b, N_MEM, 2, C_HEADS, C_HEAD_DIM)


def _route_kernel(x_ref, g_ref, wr_ref, br_ref, cnt_in_ref, e_ref, rank_ref, gate_ref, cnt_ref, carry_ref):
    _route_rows(x_ref[...], pl.program_id(0) == 0, g_ref, wr_ref, br_ref, cnt_in_ref, e_ref, rank_ref, gate_ref,
                cnt_ref, carry_ref)


def _route_rows(x, first_step, g_ref, wr_ref, br_ref, cnt_in_ref, e_ref, rank_ref, gate_ref, cnt_ref, carry_ref):
    tm = x.shape[0]

    @pl.when(first_step)
    def _():
        carry_ref[...] = cnt_in_ref[...].astype(F32)

    h = _rms(x, g_ref[...])
    logits = lax.dot_general(wr_ref[...], h.astype(BF16), (((1,), (1,)), ((), ())),
                             preferred_element_type=F32) + br_ref[...]
    expert = lax.broadcasted_iota(I32, logits.shape, 0)
    vals, idxs = [], []
    member = jnp.zeros(logits.shape, F32)
    for _ in range(TOP_K):
        m = jnp.max(logits, axis=0, keepdims=True)
        idx = jnp.min(jnp.where(logits == m, expert, N_EXPERTS), axis=0, keepdims=True)
        hit = expert == idx
        vals.append(m)
        idxs.append(idx)
        member = jnp.where(hit, 1.0, member)
        logits = jnp.where(hit, -jnp.inf, logits)
    p = [jnp.exp(v - vals[0]) for v in vals]
    den = p[0] + p[1] + p[2] + p[3]
    gates = jnp.concatenate([pk / den for pk in p] + [jnp.zeros((SUBLANES - TOP_K, tm), F32)], axis=0)
    gate_ref[...] = jnp.transpose(gates)
    e_ref[...] = jnp.concatenate(idxs, axis=0)
    before = (lax.broadcasted_iota(I32, (tm, tm), 0) < lax.broadcasted_iota(I32, (tm, tm), 1)).astype(BF16)
    prefix = jnp.dot(member.astype(BF16), before, preferred_element_type=F32) + carry_ref[:, 0:1]
    rank_ref[...] = jnp.concatenate(
        [jnp.sum(jnp.where(expert == idx, prefix, 0.0), axis=0, keepdims=True) for idx in idxs], axis=0).astype(I32)
    carry_ref[...] = carry_ref[...] + jnp.sum(member, axis=1, keepdims=True)
    cnt_ref[...] = carry_ref[...].astype(I32)


def _route_params(norm_ffn, w_router, b_router):
    return norm_ffn[None], w_router.T.astype(BF16), b_router[:, None]


def _route(x, route_params, cnt_in, tm):
    n, d = x.shape
    assert n % tm == 0
    norm_ffn, w_router_t, b_router = route_params
    const = lambda arr: pl.BlockSpec(arr.shape, lambda i: (0,) * arr.ndim)
    return pl.pallas_call(
        _route_kernel,
        grid=(n // tm,),
        in_specs=[pl.BlockSpec((tm, d), lambda i: (i, 0)), const(norm_ffn), const(w_router_t), const(b_router),
                  const(cnt_in)],
        out_specs=(pl.BlockSpec((TOP_K, tm), lambda i: (0, i)), pl.BlockSpec((TOP_K, tm), lambda i: (0, i)),
                   pl.BlockSpec((tm, SUBLANES), lambda i: (i, 0)), pl.BlockSpec((N_EXPERTS, LANES), lambda i: (0, 0))),
        out_shape=(jax.ShapeDtypeStruct((TOP_K, n), I32), jax.ShapeDtypeStruct((TOP_K, n), I32),
                   jax.ShapeDtypeStruct((n, SUBLANES), F32), jax.ShapeDtypeStruct((N_EXPERTS, LANES), I32)),
        scratch_shapes=[pltpu.VMEM((N_EXPERTS, LANES), F32)],
        compiler_params=_cparams("arbitrary"),
        name="moe_route",
    )(x, norm_ffn, w_router_t, b_router, cnt_in)


DMA_ROWS_PER_ITER = 8
COMBINE_ROWS = 16
N_DMA_THREADS = 2
N_PLAN_TAIL = 1 + N_EXPERTS


def _row_tile(r):
    return pl.ds(pl.multiple_of(r * ROW_TILES, ROW_TILES), ROW_TILES)


def _plan_kernel(n_blocks, cnt_ref, plan_ref):
    shift = BM_EXPERT.bit_length() - 1

    def per_expert(e, carry):
        blk0, last_e = carry
        nb = (cnt_ref[e] + (BM_EXPERT - 1)) >> shift
        plan_ref[n_blocks + 1 + e] = blk0 << shift

        def fill(j, cc):
            plan_ref[blk0 + j] = e
            return cc
        lax.fori_loop(0, nb, fill, 0)
        return blk0 + nb, jnp.where(nb > 0, e, last_e)
    n_used, last_e = lax.fori_loop(0, N_EXPERTS, per_expert, (0, 0))

    def tail(j, cc):
        plan_ref[j] = last_e
        return cc
    lax.fori_loop(n_used, n_blocks, tail, 0)
    plan_ref[n_blocks] = n_used


def _moe_plan(cnt, n_blocks):
    smem = pl.BlockSpec(memory_space=pltpu.SMEM)
    return pl.pallas_call(
        functools.partial(_plan_kernel, n_blocks),
        in_specs=[smem], out_specs=smem,
        out_shape=jax.ShapeDtypeStruct((n_blocks + N_PLAN_TAIL,), I32),
        name="moe_plan",
    )(cnt)


POSITION_TILES_PER_STEP = 16


def _positions_kernel(n_blocks, plan_ref, e_ref, r_ref, pos_ref):
    tiles, _, tm = pos_ref.shape
    e = e_ref[...]
    first = jnp.zeros(e.shape, I32)
    for ex in range(N_EXPERTS):
        first = jnp.where(e == ex, plan_ref[n_blocks + 1 + ex], first)
    pos = first + r_ref[...]
    for i in range(tiles):
        pos_ref[i] = pos[:, i * tm:(i + 1) * tm]


def _positions(plan, e, r, n_blocks, tm):
    n = e.shape[1]
    assert n % tm == 0
    nt = n // tm
    per_step = math.gcd(nt, POSITION_TILES_PER_STEP)
    pos = pl.pallas_call(
        functools.partial(_positions_kernel, n_blocks),
        grid=(nt // per_step,),
        in_specs=[pl.BlockSpec(memory_space=pltpu.SMEM), pl.BlockSpec((TOP_K, per_step * tm), lambda i: (0, i)),
                  pl.BlockSpec((TOP_K, per_step * tm), lambda i: (0, i))],
        out_specs=pl.BlockSpec((per_step, TOP_K, tm), lambda i: (i, 0, 0)),
        out_shape=jax.ShapeDtypeStruct((nt, TOP_K, tm), I32),
        compiler_params=_cparams("arbitrary"),
        name="moe_positions",
    )(plan, e, r)
    return pos.reshape(-1)


def _scatter_rows_kernel(n_blocks, cnt_ref, plan_ref, x_ref, pos_ref, xs_ref, poss_ref, g_ref, h_hbm,
                         hbuf, zbuf, sem, zsem):
    t = pl.program_id(0)
    n_tiles = pl.num_programs(0) - 1
    tm = x_ref.shape[0]
    n_s = xs_ref.shape[0]
    bm = BM_EXPERT
    shift = bm.bit_length() - 1
    slot = t % 2

    def zero_rows(first, count, wait):
        for bit in range(shift):
            n = 1 << bit
            lo = first + (count & (n - 1))

            @pl.when((count >> bit) & 1 == 1)
            def _():
                cp = pltpu.make_async_copy(zbuf.at[pl.ds(0, n * ROW_TILES), :], h_hbm.at[_row_tile_n(lo, n), :],
                                           zsem.at[0])
                cp.wait() if wait else cp.start()

    def zero_block(blk, wait):
        cp = pltpu.make_async_copy(zbuf, h_hbm.at[_row_tile_n(blk * bm, bm), :], zsem.at[0])
        cp.wait() if wait else cp.start()

    @pl.when(t == 0)
    def _():
        zbuf[...] = jnp.zeros(zbuf.shape, F32)
        for wait in (False, True):
            def pad_expert(e, cc, wait=wait):
                cnt = cnt_ref[e]
                zero_rows(plan_ref[n_blocks + 1 + e] + cnt, (-cnt) & (bm - 1), wait)
                return cc
            lax.fori_loop(0, N_EXPERTS, pad_expert, 0)

            def pad_block(blk, cc, wait=wait):
                zero_block(blk, wait)
                return cc
            lax.fori_loop(plan_ref[n_blocks], n_blocks, pad_block, 0)

    def wait_copies(sl, n_tok):
        for _ in range(TOP_K):
            pltpu.make_async_copy(hbuf.at[sl, pl.ds(0, n_tok * ROW_TILES), :],
                                  h_hbm.at[pl.ds(0, n_tok * ROW_TILES), :], sem.at[sl]).wait()

    def copy_rows(src_ref, rows_ref, n_tok):
        h = _rms(src_ref[...], g_ref[...])
        for c in range(ROW_TILES):
            hbuf[slot, pl.ds(c, n_tok, stride=ROW_TILES), :] = h[:, c * LANES:(c + 1) * LANES]

        def body(it, carry):
            for u in range(DMA_ROWS_PER_ITER):
                j = it * DMA_ROWS_PER_ITER + u
                for k in range(TOP_K):
                    pltpu.make_async_copy(hbuf.at[slot, _row_tile(j), :],
                                          h_hbm.at[_row_tile(rows_ref[k * n_tok + j]), :],
                                          sem.at[slot]).start(priority=k % N_DMA_THREADS)
            return carry
        lax.fori_loop(0, n_tok // DMA_ROWS_PER_ITER, body, 0)

    @pl.when(t >= 2)
    def _():
        wait_copies(slot, tm)

    @pl.when(t < n_tiles)
    def _():
        copy_rows(x_ref, pos_ref, tm)

    @pl.when(t == n_tiles)
    def _():
        copy_rows(xs_ref, poss_ref, n_s)
        wait_copies(slot, n_s)

        @pl.when(n_tiles >= 1)
        def _():
            wait_copies(1 - slot, tm)


def _row_tile_n(r, n):
    return pl.ds(pl.multiple_of(r * ROW_TILES, ROW_TILES), n * ROW_TILES)


def _scatter_rows(cnt, plan, x_p, pos_p, x_s, pos_s, norm_ffn, n_blocks, tm):
    n_p, d = x_p.shape
    n_s = x_s.shape[0]
    assert n_p % tm == 0 and tm % DMA_ROWS_PER_ITER == 0 and n_s % DMA_ROWS_PER_ITER == 0 and n_s <= tm
    nt = n_p // tm
    smem = lambda: pl.BlockSpec(memory_space=pltpu.SMEM)
    tile_idx = lambda i: jnp.minimum(i, nt - 1)
    return pl.pallas_call(
        functools.partial(_scatter_rows_kernel, n_blocks),
        grid=(nt + 1,),
        in_specs=[smem(), smem(), pl.BlockSpec((tm, d), lambda i: (tile_idx(i), 0)),
                  pl.BlockSpec((TOP_K * tm,), lambda i: (tile_idx(i),), memory_space=pltpu.SMEM),
                  pl.BlockSpec((n_s, d), lambda i: (0, 0)), smem(), pl.BlockSpec((1, d), lambda i: (0, 0))],
        out_specs=pl.BlockSpec(memory_space=pl.ANY),
        out_shape=jax.ShapeDtypeStruct((n_blocks * BM_EXPERT * ROW_TILES, LANES), F32),
        scratch_shapes=[pltpu.VMEM((2, tm * ROW_TILES, LANES), F32), pltpu.VMEM((BM_EXPERT * ROW_TILES, LANES), F32),
                        pltpu.SemaphoreType.DMA((2,)), pltpu.SemaphoreType.DMA((1,))],
        compiler_params=_cparams("arbitrary"),
        name="moe_scatter_rows",
    )(cnt, plan, x_p, pos_p, x_s, pos_s, norm_ffn)


WEIGHT_CAST_ROWS = 128


def _expert_blocks_kernel(n_blocks, plan_ref, x_ref, wgu_ref, bgu_ref, wd_ref, bd_ref, y_ref, wgu_b, wd_b):
    bm = BM_EXPERT
    i = pl.program_id(0)

    @pl.when((i == 0) | (plan_ref[i] != plan_ref[jnp.maximum(i - 1, 0)]))
    def _():
        for src, dst in ((wgu_ref, wgu_b), (wd_ref, wd_b)):
            for r0 in range(0, src.shape[1], WEIGHT_CAST_ROWS):
                rows = slice(r0, r0 + WEIGHT_CAST_ROWS)
                dst[rows, :] = src[0, rows, :].astype(BF16)

    @pl.when(i < plan_ref[n_blocks])
    def _():
        x = jnp.concatenate([x_ref[pl.ds(c, bm, stride=ROW_TILES), :] for c in range(ROW_TILES)], axis=1)
        gu = jnp.dot(x.astype(BF16), wgu_b[...], preferred_element_type=F32) + bgu_ref[0]
        gate = jnp.minimum(gu[:, :D_FF], SWIGLU_LIMIT)
        up = jnp.clip(gu[:, D_FF:], -SWIGLU_LIMIT, SWIGLU_LIMIT)
        act = gate * jax.nn.sigmoid(SWIGLU_ALPHA * gate) * (up + 1.0)
        y = jnp.dot(act.astype(BF16), wd_b[...], preferred_element_type=F32) + bd_ref[0]
        for c in range(ROW_TILES):
            y_ref[pl.ds(c, bm, stride=ROW_TILES), :] = y[:, c * LANES:(c + 1) * LANES]

    @pl.when(i >= plan_ref[n_blocks])
    def _():
        y_ref[...] = jnp.zeros(y_ref.shape, F32)


def _expert_blocks(plan, h_sorted, wgu, bgu, wd, bd, n_blocks):
    bm = BM_EXPERT
    by_expert = lambda *shape: pl.BlockSpec((1,) + shape, lambda i, plan: (plan[i],) + (0,) * len(shape))
    used = lambda i, plan: (jnp.minimum(i, plan[n_blocks] - 1), 0)
    grid_spec = pltpu.PrefetchScalarGridSpec(
        num_scalar_prefetch=1,
        grid=(n_blocks,),
        in_specs=[pl.BlockSpec((bm * ROW_TILES, LANES), used), by_expert(D_MODEL, 2 * D_FF), by_expert(1, 2 * D_FF),
                  by_expert(D_FF, D_MODEL), by_expert(1, D_MODEL)],
        out_specs=pl.BlockSpec((bm * ROW_TILES, LANES), lambda i, plan: (i, 0)),
        scratch_shapes=[pltpu.VMEM(wgu.shape[1:], BF16), pltpu.VMEM(wd.shape[1:], BF16)],
    )
    return pl.pallas_call(
        functools.partial(_expert_blocks_kernel, n_blocks),
        grid_spec=grid_spec,
        out_shape=jax.ShapeDtypeStruct(h_sorted.shape, F32),
        compiler_params=_cparams("arbitrary"),
        name="moe_experts",
    )(plan, h_sorted, wgu, bgu, wd, bd)


def _gather_combine_kernel(x_ref, pos_ref, pos_next_ref, gate_ref, g_ref, y_hbm, o_ref, ybuf, sem):
    t = pl.program_id(0)
    n_tiles = pl.num_programs(0)
    tm = x_ref.shape[0]
    slot = t % 2

    def gather(rows_ref, sl):
        def body(it, carry):
            for u in range(DMA_ROWS_PER_ITER):
                j = it * DMA_ROWS_PER_ITER + u
                for k in range(TOP_K):
                    pltpu.make_async_copy(y_hbm.at[_row_tile(rows_ref[k * tm + j]), :],
                                          ybuf.at[sl, _row_tile(k * tm + j), :],
                                          sem.at[sl]).start(priority=k % N_DMA_THREADS)
            return carry
        lax.fori_loop(0, tm // DMA_ROWS_PER_ITER, body, 0)

    @pl.when(t == 0)
    def _():
        gather(pos_ref, 0)

    pltpu.make_async_copy(y_hbm.at[pl.ds(0, TOP_K * tm * ROW_TILES), :], ybuf.at[slot], sem.at[slot]).wait()

    def combine_rows(issue_next):
        rb = min(tm, COMBINE_ROWS)

        def body(it, carry):
            r0 = pl.multiple_of(it * rb, rb)
            gates = gate_ref[pl.ds(r0, rb), :]
            xs = [x_ref[pl.ds(r0, rb), c * LANES:(c + 1) * LANES] for c in range(ROW_TILES)]
            ys = [[ybuf[slot, pl.ds((k * tm + r0) * ROW_TILES + c, rb, stride=ROW_TILES), :]
                   for c in range(ROW_TILES)] for k in range(TOP_K)]
            if issue_next:
                for u in range(rb):
                    for k in range(TOP_K):
                        j = r0 + u
                        pltpu.make_async_copy(y_hbm.at[_row_tile(pos_next_ref[k * tm + j]), :],
                                              ybuf.at[1 - slot, _row_tile(k * tm + j), :],
                                              sem.at[1 - slot]).start(priority=k % N_DMA_THREADS)
            cols = []
            for c in range(ROW_TILES):
                acc = xs[c]
                for k in range(TOP_K):
                    acc = acc + gates[:, k:k + 1] * ys[k][c]
                cols.append(acc)
            o_ref[pl.ds(r0, rb), :] = _rms(jnp.concatenate(cols, axis=1), g_ref[...])
            return carry
        lax.fori_loop(0, tm // rb, body, 0)

    @pl.when(t + 1 < n_tiles)
    def _():
        combine_rows(True)

    @pl.when(t + 1 >= n_tiles)
    def _():
        combine_rows(False)


def _gather_combine(x, pos, gates, norm_final, y_sorted, tm):
    n, d = x.shape
    assert n % tm == 0 and tm % DMA_ROWS_PER_ITER == 0
    nt = n // tm
    rows = lambda off: pl.BlockSpec((TOP_K * tm,), lambda i: (jnp.minimum(i + off, nt - 1),),
                                    memory_space=pltpu.SMEM)
    return pl.pallas_call(
        _gather_combine_kernel,
        grid=(nt,),
        in_specs=[pl.BlockSpec((tm, d), lambda i: (i, 0)), rows(0), rows(1),
                  pl.BlockSpec((tm, SUBLANES), lambda i: (i, 0)), pl.BlockSpec((1, d), lambda i: (0, 0)),
                  pl.BlockSpec(memory_space=pl.ANY)],
        out_specs=pl.BlockSpec((tm, d), lambda i: (i, 0)),
        out_shape=jax.ShapeDtypeStruct((n, d), F32),
        scratch_shapes=[pltpu.VMEM((2, TOP_K * tm * ROW_TILES, LANES), F32), pltpu.SemaphoreType.DMA((2,))],
        compiler_params=_cparams("arbitrary"),
        name="moe_combine",
    )(x, pos, pos, gates, norm_final, y_sorted)


def _moe_and_final_norm(x_p, routing_p, x_s, routing_s, norm_ffn, w_gate_up, b_gate_up, w_down, b_down, norm_final):
    n_p, n_s = x_p.shape[0], x_s.shape[0]
    n_rows = (n_p + n_s) * TOP_K
    n_blocks = -(-(n_rows + N_EXPERTS * (BM_EXPERT - 1)) // BM_EXPERT)
    g_ffn = norm_ffn[None]
    e_s, r_s, gates_s, _ = routing_s
    e_p, r_p, gates_p, cnt = routing_p
    plan = _moe_plan(cnt[:, 0], n_blocks)
    pos_p = _positions(plan, e_p, r_p, n_blocks, TM_COMBINE)
    pos_s = _positions(plan, e_s, r_s, n_blocks, n_s)
    h_sorted = _scatter_rows(cnt[:, 0], plan, x_p, pos_p, x_s, pos_s, g_ffn, n_blocks, TM_COMBINE)
    y_sorted = _expert_blocks(plan, h_sorted, w_gate_up, b_gate_up[:, None, :], w_down, b_down[:, None, :], n_blocks)
    y_p = _gather_combine(x_p, pos_p, gates_p, norm_final[None], y_sorted, TM_COMBINE)
    y_s = _gather_combine(x_s, pos_s, gates_s, norm_final[None], y_sorted, n_s)
    return y_p, y_s


def _rb(x):
    return x.astype(BF16).astype(F32)


def _sample_proj_kernel(x_ref, g_ref, wm_ref, wg_ref, z_ref):
    hb = _rms(x_ref[...], g_ref[...]).astype(BF16)
    z_ref[:, :MIX_WIDTH] = jnp.dot(hb, wm_ref[...], preferred_element_type=F32)
    z_ref[:, MIX_WIDTH:] = jnp.dot(hb, wg_ref[...], preferred_element_type=F32)


def _sample_proj(x, norm_mix, w_mix, w_gate):
    n, d = x.shape
    args = (x, norm_mix, w_mix, w_gate)
    width = w_mix.shape[1] + w_gate.shape[1]
    return pl.pallas_call(
        _sample_proj_kernel,
        grid=(1,),
        in_specs=[pl.BlockSpec(a.shape, lambda j: (0, 0)) for a in args],
        out_specs=pl.BlockSpec((n, width), lambda j: (0, 0)),
        out_shape=jax.ShapeDtypeStruct((n, width), F32),
        compiler_params=_cparams("arbitrary"),
        name="sample_proj",
    )(*args)


def _sample_mix_kernel(z_ref, c0_ref, c1_ref, c2_ref, mem_ref, sb0_ref, sb1_ref, sb2_ref, nb_ref, lng_ref, lnb_ref,
                       wsp_ref, bsp_ref, bo_ref, co_ref, a_ref, v_ref, n0_ref, n1_ref, n2_ref):
    z = z_ref[0]
    scale = 1.0 / math.sqrt(B_HEAD_DIM)
    row_head = lax.broadcasted_iota(I32, (SUBLANES, B_GROUP_W), 0)
    lane_head = lax.broadcasted_iota(I32, (SUBLANES, B_GROUP_W), 1) // B_HEAD_DIM
    own = (row_head == lane_head).astype(F32)
    kv_cols = jnp.transpose(jnp.broadcast_to(z[:, B_WIDTH:3 * B_WIDTH], (SUBLANES, 2 * B_WIDTH)))[:, 0:1]
    outs, lses = [], []
    for g, (cache_ref, sb_ref, new_ref) in enumerate(((c0_ref, sb0_ref, n0_ref), (c1_ref, sb1_ref, n1_ref),
                                                      (c2_ref, sb2_ref, n2_ref))):
        window = cache_ref.shape[2]
        q, kn, vn = (_rb(z[:, i * B_WIDTH + g * B_GROUP_W:i * B_WIDTH + (g + 1) * B_GROUP_W]) for i in range(3))
        qh = own * q
        kt = cache_ref[0, :B_GROUP_W, :].astype(BF16)
        vt = cache_ref[0, B_GROUP_W:, :].astype(BF16)
        lg = jnp.dot(qh.astype(BF16), kt, preferred_element_type=F32) * scale + sb_ref[...]
        ln = jnp.sum(qh * kn, axis=1, keepdims=True) * scale + nb_ref[g, :, 0:1]
        m = jnp.maximum(jnp.max(lg, axis=1, keepdims=True), ln)
        lse = m + jnp.log(jnp.sum(jnp.exp(lg - m), axis=1, keepdims=True) + jnp.exp(ln - m))
        p = jnp.exp(lg - lse).astype(BF16)
        pn = _rb(jnp.exp(ln - lse))
        pv = lax.dot_general(p, vt, (((1,), (1,)), ((), ())), preferred_element_type=F32) + pn * vn
        outs.append(jnp.sum(own * pv, axis=0, keepdims=True))
        lses.append(jnp.sum(own * lse, axis=0, keepdims=True))
        new_col = jnp.concatenate([kv_cols[i * B_WIDTH + g * B_GROUP_W:i * B_WIDTH + (g + 1) * B_GROUP_W]
                                   for i in range(2)], axis=0)
        lane = lax.broadcasted_iota(I32, (1, window), 1)
        new_ref[0] = jnp.where(lane == window - 1, new_col, pltpu.roll(cache_ref[0], window - 1, axis=1))
    m = jnp.maximum(jnp.maximum(lses[0], lses[1]), lses[2])
    ws = [jnp.exp(l - m) for l in lses]
    den = ws[0] + ws[1] + ws[2]
    bo_ref[0] = _rb(ws[0] / den) * _rb(outs[0]) + _rb(ws[1] / den) * _rb(outs[1]) + _rb(ws[2] / den) * _rb(outs[2])

    cs = []
    for h in range(C_HEADS):
        c0 = MIX_WIDTH - C_WIDTH + h * C_HEAD_DIM
        qh = _rb(z[:, c0:c0 + C_HEAD_DIM])
        kh = _rb(mem_ref[0, pl.ds(h, N_MEM, stride=2 * C_HEADS), :])
        vh = _rb(mem_ref[0, pl.ds(C_HEADS + h, N_MEM, stride=2 * C_HEADS), :])
        s = jnp.sum(kh * qh, axis=-1, keepdims=True) * (1.0 / math.sqrt(C_HEAD_DIM))
        p = jnp.exp(s - jnp.max(s, axis=0, keepdims=True))
        p = _rb(p / jnp.sum(p, axis=0, keepdims=True))
        cs.append(jnp.sum(p * vh, axis=0, keepdims=True))
    co_ref[0] = jnp.concatenate(cs, axis=1)

    u = _gelu(z[:, 3 * B_WIDTH:3 * B_WIDTH + A_WIDTH])
    v = _layernorm(_gelu(z[:, 3 * B_WIDTH + A_WIDTH:3 * B_WIDTH + 2 * A_WIDTH]), lng_ref[...], lnb_ref[...])
    v_ref[0] = v
    a_ref[0] = u * (_rb(wsp_ref[...]) * _rb(v) + bsp_ref[...])


def _sample_mix(z, caches, mem, sbias, nbias, ln_g, ln_b, wsp0, bsp0):
    n = z.shape[0]
    const = lambda arr: pl.BlockSpec(arr.shape, lambda i: (0,) * arr.ndim)
    row = lambda w: pl.BlockSpec((1, 1, w), lambda i: (i, 0, 0))
    per_req = lambda arr: pl.BlockSpec((1,) + arr.shape[1:], lambda i: (i, 0, 0))
    widths = (B_GROUP_W, C_WIDTH, A_WIDTH, A_WIDTH)
    return pl.pallas_call(
        _sample_mix_kernel,
        grid=(n,),
        in_specs=[row(z.shape[2])] + [per_req(c) for c in caches] + [per_req(mem)]
        + [const(a) for a in (*sbias, nbias, ln_g, ln_b, wsp0, bsp0)],
        out_specs=tuple(row(w) for w in widths) + tuple(per_req(c) for c in caches),
        out_shape=tuple(jax.ShapeDtypeStruct((n, 1, w), F32) for w in widths)
        + tuple(jax.ShapeDtypeStruct(c.shape, F32) for c in caches),
        compiler_params=_cparams("arbitrary"),
        name="sample_mix",
    )(z, *caches, mem, *sbias, nbias, ln_g, ln_b, wsp0, bsp0)


def _sample_merge_kernel(x_ref, z_ref, a_ref, b_ref, c_ref, wa_ref, wb_ref, wc_ref, wo_ref, o_ref):
    dot = lambda a, b: jnp.dot(a.astype(BF16), b, preferred_element_type=F32)
    merged = None
    for i, (act_ref, w_ref) in enumerate(((a_ref, wa_ref), (b_ref, wb_ref), (c_ref, wc_ref))):
        gate = jax.nn.sigmoid(z_ref[:, MIX_WIDTH + i * D_MODEL:MIX_WIDTH + (i + 1) * D_MODEL])
        term = gate * dot(act_ref[...], w_ref[...])
        merged = term if merged is None else merged + term
    o_ref[...] = x_ref[...] + dot(merged, wo_ref[...])


def _sample_merge(x, z, a, bo, co, wa, wb, wc, wo):
    args = (x, z, a, bo, co, wa, wb, wc, wo)
    return pl.pallas_call(
        _sample_merge_kernel,
        grid=(1,),
        in_specs=[pl.BlockSpec(arr.shape, lambda i: (0, 0)) for arr in args],
        out_specs=pl.BlockSpec(x.shape, lambda i: (0, 0)),
        out_shape=jax.ShapeDtypeStruct(x.shape, F32),
        compiler_params=_cparams("arbitrary"),
        name="sample_merge",
    )(*args)


def _sample_mixers(x_sample, caches, cache_mem, sbias, nbias, norm_mix, ln_v_g, ln_v_b, w_spatial, b_spatial, wts):
    n = x_sample.shape[0]
    x = x_sample[:, 0]
    z = _sample_proj(x, norm_mix[None], wts["mix"], wts["gate"])
    mem = cache_mem.reshape(n, N_MEM * 2 * C_HEADS, C_HEAD_DIM)
    wsp0 = jnp.repeat(w_spatial[:, 0, 0], LANES)[None]
    bsp0 = jnp.repeat(b_spatial[:, 0], LANES)[None]
    bo, co, a, v, *new_caches = _sample_mix(z[:, None], [_from_window_layout(c) for c in caches], mem, sbias, nbias,
                                            ln_v_g[None], ln_v_b[None], wsp0, bsp0)
    x_new = _sample_merge(x, z, a[:, 0], bo[:, 0], co[:, 0], wts["a"], wts["b"], wts["c"], wts["out"])
    return x_new, [_to_window_layout(c) for c in new_caches], v[:, 0]


def _mixer_weights(w_in, w_branch_a, w_branch_b, w_branch_c, w_out):
    return {"mix": w_in[:, :MIX_WIDTH].astype(BF16), "gate": w_in[:, MIX_WIDTH:].astype(BF16),
            "a": w_branch_a.astype(BF16), "b": w_branch_b.astype(BF16), "c": w_branch_c.astype(BF16),
            "out": w_out.astype(BF16)}


def kernel(x_prompt, x_sample, cache_win0_kv, cache_win1_kv, cache_win2_kv, cache_mem_kv, mem_prompt, rel_bias,
           norm_mix, norm_mem, w_in, ln_v_g, ln_v_b, w_spatial, b_spatial, w_mem_kv, w_branch_a, w_branch_b,
           w_branch_c, w_out, norm_ffn, w_router, b_router, w_gate_up, b_gate_up, w_down, b_down, norm_final):
    assert norm_mix.shape[0] == 1, "one layer"
    b, s, d = x_prompt.shape
    n_s = x_sample.shape[0]
    caches = (cache_win0_kv[0], cache_win1_kv[0], cache_win2_kv[0])
    assert all(c.shape[1] == w for c, (w, _) in zip(caches, B_PAIRS)), "window buffers hold a full window"
    wts = _mixer_weights(w_in[0], w_branch_a[0], w_branch_b[0], w_branch_c[0], w_out[0])
    band_bias, *sbias, nbias = _bias_tables(rel_bias)
    route_params = _route_params(norm_ffn[0], w_router[0], b_router[0])
    xs, win_s, v_s = _sample_mixers(x_sample, caches, cache_mem_kv[0], sbias, nbias, norm_mix[0], ln_v_g[0],
                                    ln_v_b[0], w_spatial[0], b_spatial[0], wts)
    routing_s = _route(xs, route_params, jnp.zeros((N_EXPERTS, LANES), I32), n_s)
    xp, routing_p, win_p, mem_p = _prompt_mixers(x_prompt, mem_prompt, band_bias, norm_mix[0], norm_mem[0], ln_v_g[0],
                                                 ln_v_b[0], w_spatial[0], b_spatial[0], w_mem_kv[0], wts,
                                                 route_params, routing_s[3])
    y_p, y_s = _moe_and_final_norm(xp.reshape(b * s, d), routing_p, xs, routing_s, norm_ffn[0], w_gate_up[0],
                                   b_gate_up[0], w_down[0], b_down[0], norm_final)
    chunk_v = v_s.reshape(1, n_s, 1, A_GROUPS, LANES)
    return (y_p.reshape(b, s, d), y_s[:, None], win_p[0], win_p[1], win_p[2], mem_p,
            win_s[0], win_s[1], win_s[2], chunk_v)
```

```python
import functools
import math

import numpy as np
import jax
import jax.numpy as jnp
from jax import lax
from jax.experimental import pallas as pl
from jax.experimental.pallas import tpu as pltpu

F32 = jnp.float32
BF16 = jnp.bfloat16
I32 = jnp.int32

D_MODEL = 1024
N_MEM = 256
CHUNK = 128
A_GROUPS = 4
A_WIDTH = 512
B_PAIRS = ((128, 1), (512, 4), (2048, 16))
B_GROUP_W = 256
B_WIDTH = 768
B_HEAD_DIM = 64
BAND = 128
C_HEADS = 4
C_HEAD_DIM = 128
C_WIDTH = 512
REL_BUCKETS = 32
REL_MAX_DIST = 2048
N_EXPERTS = 32
TOP_K = 4
D_FF = 1024
SWIGLU_LIMIT = 7.0
SWIGLU_ALPHA = 1.702
EPS = 1e-6
NEG_INF = -1e30
MIX_WIDTH = 3 * B_WIDTH + 2 * A_WIDTH + C_WIDTH

LANES = 128
SUBLANES = 8
ROW_TILES = D_MODEL // LANES
VMEM_LIMIT = 56 * 1024 * 1024

TM_PROJ = 1024
TM_MERGE = 512
BM_EXPERT = 512
TM_COMBINE = 512
ATTN_BLOCKS_PER_ITER = 6


def _cparams(*sem):
    return pltpu.CompilerParams(dimension_semantics=sem, vmem_limit_bytes=VMEM_LIMIT)


def _t5_bucket(dist):
    dist = np.maximum(np.asarray(dist), 0)
    max_exact = REL_BUCKETS // 2
    log_ratio = np.log(np.maximum(dist, max_exact) / max_exact) / math.log(REL_MAX_DIST / max_exact)
    large = np.minimum(max_exact + (log_ratio * (REL_BUCKETS - max_exact)).astype(np.int32), REL_BUCKETS - 1)
    return np.where(dist < max_exact, dist, large).astype(np.int32)


def _gelu(x):
    c = math.sqrt(2.0 / math.pi)
    return x * (0.5 * (1.0 + jnp.tanh(c * (x + 0.044715 * (x * x * x)))))


def _rms(x, g):
    return x * lax.rsqrt(jnp.mean(x * x, axis=-1, keepdims=True) + EPS) * g


def _layernorm(x, g, b):
    xc = x - jnp.mean(x, axis=-1, keepdims=True)
    return xc * lax.rsqrt(jnp.mean(xc * xc, axis=-1, keepdims=True) + EPS) * g + b


def _proj_kernel(x_ref, g_ref, w_ref, lng_ref, lnb_ref, wsp_ref, bsp_ref,
                 qkv0_ref, qkv1_ref, qkv2_ref, win0_ref, win1_ref, win2_ref, a_ref, qc_ref, scr_ref):
    tm = x_ref.shape[1]
    hb = _rms(x_ref[0], g_ref[...]).astype(BF16)

    def proj(c0, width):
        return jnp.dot(hb, w_ref[:, c0:c0 + width], preferred_element_type=F32)

    for g, (_, dil) in enumerate(B_PAIRS):
        q, k, v = (proj(i * B_WIDTH + g * B_GROUP_W, B_GROUP_W) for i in range(3))
        kv = jnp.concatenate([k, v], axis=1)
        window = min(B_PAIRS[g][0], tm)
        (win0_ref, win1_ref, win2_ref)[g][0] = jnp.transpose(kv[tm - window:])
        if dil == 1:
            qkv0_ref[0] = jnp.concatenate([q, k, v], axis=1).astype(BF16)
            continue
        out_ref = qkv1_ref if g == 1 else qkv2_ref
        for i, arr in enumerate((q, k, v)):
            for ct in range(2):
                scr_ref[2 * i + ct] = arr[:, ct * LANES:(ct + 1) * LANES]
        for r in range(dil):
            rows = [scr_ref[j, pl.ds(r, tm // dil, stride=dil), :] for j in range(6)]
            out_ref[0, r] = jnp.concatenate(rows, axis=1).astype(BF16)

    u = _gelu(proj(3 * B_WIDTH, A_WIDTH))
    v = _layernorm(_gelu(proj(3 * B_WIDTH + A_WIDTH, A_WIDTH)), lng_ref[...], lnb_ref[...]).astype(BF16)
    for c in range(tm // CHUNK):
        rs = slice(c * CHUNK, (c + 1) * CHUNK)
        for g in range(A_GROUPS):
            cs = slice(g * LANES, (g + 1) * LANES)
            s = jnp.dot(wsp_ref[g], v[rs, cs], preferred_element_type=F32) + bsp_ref[:, g:g + 1]
            a_ref[0, rs, cs] = (u[rs, cs] * s).astype(BF16)

    qc_ref[0] = proj(3 * B_WIDTH + 2 * A_WIDTH, C_WIDTH).astype(BF16)


def _prompt_proj(x, norm_mix, w_mix, ln_g, ln_b, wsp, bsp):
    b, s, d = x.shape
    tm = TM_PROJ
    assert s % tm == 0 and s >= B_PAIRS[2][0] and tm >= B_PAIRS[1][0]
    nt = s // tm
    const = lambda *shape: pl.BlockSpec(shape, lambda i, j: (0,) * len(shape))
    out_shape = (
        jax.ShapeDtypeStruct((b, s, 3 * B_GROUP_W), BF16),
        jax.ShapeDtypeStruct((b, 4, s // 4, 3 * B_GROUP_W), BF16),
        jax.ShapeDtypeStruct((b, 16, s // 16, 3 * B_GROUP_W), BF16),
        jax.ShapeDtypeStruct((b, 2 * B_GROUP_W, B_PAIRS[0][0]), F32),
        jax.ShapeDtypeStruct((b, 2 * B_GROUP_W, B_PAIRS[1][0]), F32),
        jax.ShapeDtypeStruct((b, 2 * B_GROUP_W, s), F32),
        jax.ShapeDtypeStruct((b, s, A_WIDTH), BF16),
        jax.ShapeDtypeStruct((b, s, C_WIDTH), BF16),
    )
    out_specs = (
        pl.BlockSpec((1, tm, 3 * B_GROUP_W), lambda i, j: (i, j, 0)),
        pl.BlockSpec((1, 4, tm // 4, 3 * B_GROUP_W), lambda i, j: (i, 0, j, 0)),
        pl.BlockSpec((1, 16, tm // 16, 3 * B_GROUP_W), lambda i, j: (i, 0, j, 0)),
        pl.BlockSpec((1, 2 * B_GROUP_W, B_PAIRS[0][0]), lambda i, j: (i, 0, 0)),
        pl.BlockSpec((1, 2 * B_GROUP_W, B_PAIRS[1][0]), lambda i, j: (i, 0, 0)),
        pl.BlockSpec((1, 2 * B_GROUP_W, tm), lambda i, j: (i, 0, j)),
        pl.BlockSpec((1, tm, A_WIDTH), lambda i, j: (i, j, 0)),
        pl.BlockSpec((1, tm, C_WIDTH), lambda i, j: (i, j, 0)),
    )
    return pl.pallas_call(
        _proj_kernel,
        grid=(b, nt),
        in_specs=[pl.BlockSpec((1, tm, d), lambda i, j: (i, j, 0)), const(1, d), const(d, MIX_WIDTH),
                  const(1, A_WIDTH), const(1, A_WIDTH), const(A_GROUPS, CHUNK, CHUNK), const(CHUNK, A_GROUPS)],
        out_specs=out_specs,
        out_shape=out_shape,
        scratch_shapes=[pltpu.VMEM((6, tm, LANES), F32)],
        compiler_params=_cparams("arbitrary", "arbitrary"),
        name="prompt_proj",
    )(x, norm_mix, w_mix, ln_g, ln_b, wsp, bsp)


def _attend(q, k, v, bias_ref, g, k0, hmask):
    kn = k.shape[0]
    qs = jnp.concatenate([q * hm for hm in hmask], axis=0)
    s = lax.dot_general(qs, k, (((1,), (1,)), ((), ())), preferred_element_type=F32) + bias_ref[g, :, k0:k0 + kn]
    m = jnp.max(s, axis=-1, keepdims=True)
    p = jnp.exp(s - m)
    l = jnp.sum(p, axis=-1, keepdims=True)
    pv = jnp.dot(p.astype(BF16), v, preferred_element_type=F32) / l
    lse_rows = m + jnp.log(l)
    out = jnp.zeros((BAND, B_GROUP_W), F32)
    lse = jnp.zeros((BAND, B_GROUP_W), F32)
    for h, hm in enumerate(hmask):
        rows = slice(h * BAND, (h + 1) * BAND)
        sel = hm > 0
        out = jnp.where(sel, pv[rows], out)
        lse = jnp.where(sel, lse_rows[rows], lse)
    return out, lse


def _attn_kernel(q0_ref, q1_ref, q2_ref, bias_ref, o_ref, out_ref, lse_ref):
    s = q0_ref.shape[1]
    lane_head = lax.broadcasted_iota(I32, (1, B_GROUP_W), 1) // B_HEAD_DIM
    scale = 1.0 / math.sqrt(B_HEAD_DIM)
    hmask = [jnp.where(lane_head == h, scale, 0.0).astype(BF16) for h in range(4)]
    qs, ks, vs = (slice(i * B_GROUP_W, (i + 1) * B_GROUP_W) for i in range(3))

    def store(g, start, dil, out, lse):
        rows = pl.ds(start, BAND) if dil == 1 else pl.ds(start, BAND, stride=dil)
        for ct in range(2):
            out_ref[g, ct, rows, :] = out[:, ct * LANES:(ct + 1) * LANES]
            lse_ref[g, ct, rows, :] = lse[:, ct * LANES:(ct + 1) * LANES]

    for g, (_, dil) in enumerate(B_PAIRS):
        n = s // dil
        nb = n // BAND

        def load(rows, cols, r, g=g):
            if g == 0:
                return q0_ref[0, rows, cols]
            return (q1_ref if g == 1 else q2_ref)[0, r, rows, cols]

        def first_block(r, g=g, dil=dil, load=load):
            rows = pl.ds(0, BAND)
            out, lse = _attend(load(rows, qs, r), load(rows, ks, r), load(rows, vs, r), bias_ref, g, BAND, hmask)
            store(g, r, dil, out, lse)

        def later_block(i, g=g, dil=dil, nb=nb, load=load):
            r = i // (nb - 1)
            qb = i % (nb - 1) + 1
            q0 = pl.multiple_of(qb * BAND, BAND)
            rows_q = pl.ds(q0, BAND)
            rows_k = pl.ds(q0 - BAND, 2 * BAND)
            out, lse = _attend(load(rows_q, qs, r), load(rows_k, ks, r), load(rows_k, vs, r), bias_ref, g, 0, hmask)
            store(g, qb * (BAND * dil) + r, dil, out, lse)

        def run(block_fn, count):
            per = ATTN_BLOCKS_PER_ITER

            def body(it, carry):
                for u in range(per):
                    block_fn(it * per + u)
                return carry
            lax.fori_loop(0, count // per, body, 0)
            for i in range(count - count % per, count):
                block_fn(i)

        run(first_block, dil)
        run(later_block, dil * (nb - 1))

    rc = 256
    def combine(c, carry):
        rows = pl.ds(pl.multiple_of(c * rc, rc), rc)
        for ct in range(2):
            ls = [lse_ref[g, ct, rows, :] for g in range(3)]
            m = jnp.maximum(jnp.maximum(ls[0], ls[1]), ls[2])
            ws = [jnp.exp(l - m) for l in ls]
            den = ws[0] + ws[1] + ws[2]
            num = ws[0] * out_ref[0, ct, rows, :] + ws[1] * out_ref[1, ct, rows, :] + ws[2] * out_ref[2, ct, rows, :]
            o_ref[0, rows, ct * LANES:(ct + 1) * LANES] = (num / den).astype(BF16)
        return carry
    lax.fori_loop(0, s // rc, combine, 0)


def _prompt_attention(qkv0, qkv1, qkv2, bias):
    b, s, w = qkv0.shape
    return pl.pallas_call(
        _attn_kernel,
        grid=(b,),
        in_specs=[pl.BlockSpec((1, s, w), lambda i: (i, 0, 0)),
                  pl.BlockSpec((1, 4, s // 4, w), lambda i: (i, 0, 0, 0)),
                  pl.BlockSpec((1, 16, s // 16, w), lambda i: (i, 0, 0, 0)),
                  pl.BlockSpec(bias.shape, lambda i: (0, 0, 0))],
        out_specs=pl.BlockSpec((1, s, B_GROUP_W), lambda i: (i, 0, 0)),
        out_shape=jax.ShapeDtypeStruct((b, s, B_GROUP_W), BF16),
        scratch_shapes=[pltpu.VMEM((3, 2, s, LANES), F32), pltpu.VMEM((3, 2, s, LANES), F32)],
        compiler_params=_cparams("arbitrary"),
        name="prompt_attention",
    )(qkv0, qkv1, qkv2, bias)


def _bias_kernel(rel_ref, band_idx_ref, s0_ref, s1_ref, s2_ref, band_ref, sb0_ref, sb1_ref, sb2_ref, nb_ref):
    def lookup(idx, col):
        acc = jnp.full(idx.shape, NEG_INF, F32)
        for bucket in range(REL_BUCKETS):
            acc = jnp.where(idx == bucket, rel_ref[bucket, col], acc)
        return acc

    for g, (s_ref, sb_ref) in enumerate(((s0_ref, sb0_ref), (s1_ref, sb1_ref), (s2_ref, sb2_ref))):
        width = s_ref.shape[1]
        for h in range(4):
            band_ref[g, h * BAND:(h + 1) * BAND, :] = lookup(band_idx_ref[g], 4 * g + h)
        sb_ref[...] = jnp.concatenate([lookup(s_ref[...], 4 * g + h) for h in range(4)]
                                      + [jnp.full((SUBLANES - 4, width), NEG_INF, F32)], axis=0)
        nb_ref[g] = jnp.concatenate([jnp.full((1, LANES), rel_ref[0, 4 * g + h], F32) for h in range(4)]
                                    + [jnp.zeros((SUBLANES - 4, LANES), F32)], axis=0)


def _bias_tables(rel_bias):
    steps = np.arange(BAND)[:, None] + BAND - np.arange(2 * BAND)[None, :]
    valid = (steps >= 0) & (steps <= BAND)
    band_idx = np.stack([np.where(valid, _t5_bucket(np.clip(steps, 0, BAND) * dil), -1) for _, dil in B_PAIRS])
    s_idx = []
    for window, dil in B_PAIRS:
        w = np.arange(window)
        s_idx.append(np.where(w % dil == 0, _t5_bucket(window - w), -1)[None].astype(np.int32))
    args = (rel_bias, jnp.asarray(band_idx.astype(np.int32))) + tuple(jnp.asarray(s) for s in s_idx)
    vmem = lambda a: pl.BlockSpec(a.shape, lambda i: (0,) * a.ndim)
    out_shape = (jax.ShapeDtypeStruct((3, 4 * BAND, 2 * BAND), F32),) + tuple(
        jax.ShapeDtypeStruct((SUBLANES, window), F32) for window, _ in B_PAIRS) + (
        jax.ShapeDtypeStruct((3, SUBLANES, LANES), F32),)
    return pl.pallas_call(
        _bias_kernel,
        grid=(1,),
        in_specs=[pl.BlockSpec(memory_space=pltpu.SMEM)] + [vmem(a) for a in args[1:]],
        out_specs=tuple(pl.BlockSpec(s.shape, lambda i, n=len(s.shape): (0,) * n) for s in out_shape),
        out_shape=out_shape,
        compiler_params=_cparams("arbitrary"),
        name="bias_tables",
    )(*args)


def _memkv_kernel(mem_ref, g_ref, w_ref, kv_ref, kvb_ref):
    hb = _rms(mem_ref[0], g_ref[...]).astype(BF16)
    kv = jnp.dot(hb, w_ref[...], preferred_element_type=F32)
    m = kv.shape[0]
    n_heads = kv.shape[1] // C_HEAD_DIM
    for j in range(n_heads):
        kv_ref[0, pl.ds(j, m, stride=n_heads), :] = kv[:, j * C_HEAD_DIM:(j + 1) * C_HEAD_DIM]
    kvb_ref[0] = kv.astype(BF16)


def _memory_kv(mem, norm_mem, w_mem):
    b, m, d = mem.shape
    w = w_mem.shape[1]
    rows = m * w // C_HEAD_DIM
    return pl.pallas_call(
        _memkv_kernel,
        grid=(b,),
        in_specs=[pl.BlockSpec((1, m, d), lambda i: (i, 0, 0)), pl.BlockSpec((1, d), lambda i: (0, 0)),
                  pl.BlockSpec((d, w), lambda i: (0, 0))],
        out_specs=(pl.BlockSpec((1, rows, C_HEAD_DIM), lambda i: (i, 0, 0)),
                   pl.BlockSpec((1, m, w), lambda i: (i, 0, 0))),
        out_shape=(jax.ShapeDtypeStruct((b, rows, C_HEAD_DIM), F32), jax.ShapeDtypeStruct((b, m, w), BF16)),
        compiler_params=_cparams("arbitrary"),
        name="memory_kv",
    )(mem, norm_mem, w_mem)


def _merge_kernel(x_ref, g_ref, a_ref, b_ref, qc_ref, kv_ref, wg_ref, wa_ref, wb_ref, wc_ref, wo_ref,
                  gf_ref, wr_ref, br_ref, cnt_in_ref, o_ref, e_ref, rank_ref, gate_ref, cnt_ref, carry_ref):
    @pl.when((pl.program_id(0) == 0) & (pl.program_id(1) == 0))
    def _():
        carry_ref[...] = cnt_in_ref[...].astype(F32)

    x = x_ref[0]
    hb = _rms(x, g_ref[...]).astype(BF16)
    qc = qc_ref[0]
    cs = []
    for h in range(C_HEADS):
        hs = slice(h * C_HEAD_DIM, (h + 1) * C_HEAD_DIM)
        s = lax.dot_general(qc[:, hs], kv_ref[0, :, hs], (((1,), (1,)), ((), ())), preferred_element_type=F32)
        s = s * (1.0 / math.sqrt(C_HEAD_DIM))
        p = jnp.exp(s - jnp.max(s, axis=-1, keepdims=True))
        p = p / jnp.sum(p, axis=-1, keepdims=True)
        vs = slice(C_WIDTH + h * C_HEAD_DIM, C_WIDTH + (h + 1) * C_HEAD_DIM)
        cs.append(jnp.dot(p.astype(BF16), kv_ref[0, :, vs], preferred_element_type=F32))
    c = jnp.concatenate(cs, axis=1).astype(BF16)
    branches = ((a_ref[0], wa_ref), (b_ref[0], wb_ref), (c, wc_ref))
    merged = None
    for i, (act, w_ref) in enumerate(branches):
        gate = jax.nn.sigmoid(jnp.dot(hb, wg_ref[:, i * D_MODEL:(i + 1) * D_MODEL], preferred_element_type=F32))
        term = gate * jnp.dot(act, w_ref[...], preferred_element_type=F32)
        merged = term if merged is None else merged + term
    x_new = x + jnp.dot(merged.astype(BF16), wo_ref[...], preferred_element_type=F32)
    o_ref[0] = x_new
    _route_rows(x_new, gf_ref, wr_ref, br_ref, e_ref, rank_ref, gate_ref, cnt_ref, carry_ref)


def _prompt_merge(x, norm_mix, a, bo, qc, kvb, wg, wa, wb, wc, wo, route_params, cnt_in):
    b, s, d = x.shape
    tm = TM_MERGE
    nt = s // tm
    n = b * s
    tile = lambda w: pl.BlockSpec((1, tm, w), lambda i, j: (i, j, 0))
    const = lambda arr: pl.BlockSpec(arr.shape, lambda i, j: (0,) * arr.ndim)
    ids = pl.BlockSpec((TOP_K, tm), lambda i, j: (0, i * nt + j))
    return pl.pallas_call(
        _merge_kernel,
        grid=(b, nt),
        in_specs=[tile(d), const(norm_mix), tile(A_WIDTH), tile(B_GROUP_W), tile(C_WIDTH),
                  pl.BlockSpec((1,) + kvb.shape[1:], lambda i, j: (i, 0, 0)),
                  const(wg), const(wa), const(wb), const(wc), const(wo)]
        + [const(p) for p in route_params] + [const(cnt_in)],
        out_specs=(tile(d), ids, ids, pl.BlockSpec((tm, SUBLANES), lambda i, j: (i * nt + j, 0)),
                   pl.BlockSpec((N_EXPERTS, LANES), lambda i, j: (0, 0))),
        out_shape=(jax.ShapeDtypeStruct((b, s, d), F32), jax.ShapeDtypeStruct((TOP_K, n), I32),
                   jax.ShapeDtypeStruct((TOP_K, n), I32), jax.ShapeDtypeStruct((n, SUBLANES), F32),
                   jax.ShapeDtypeStruct((N_EXPERTS, LANES), I32)),
        scratch_shapes=[pltpu.VMEM((N_EXPERTS, LANES), F32)],
        compiler_params=_cparams("arbitrary", "arbitrary"),
        name="prompt_merge",
    )(x, norm_mix, a, bo, qc, kvb, wg, wa, wb, wc, wo, *route_params, cnt_in)


def _to_window_layout(w):
    b, _, window = w.shape
    return jnp.transpose(w.reshape(b, 2, 4, B_HEAD_DIM, window), (0, 4, 1, 2, 3))[None]


def _from_window_layout(c):
    b, window = c.shape[:2]
    return jnp.transpose(c, (0, 2, 3, 4, 1)).reshape(b, 2 * B_GROUP_W, window)


def _prompt_mixers(x_prompt, mem_prompt, band_bias, norm_mix, norm_mem, ln_v_g, ln_v_b, w_spatial, b_spatial,
                   w_mem_kv, wts, route_params, cnt_in):
    b, s, _ = x_prompt.shape
    wsp = jnp.where(np.tril(np.ones((CHUNK, CHUNK), bool)), w_spatial, 0.0).astype(BF16)
    qkv0, qkv1, qkv2, win0, win1, win2, a_out, qc = _prompt_proj(
        x_prompt, norm_mix[None], wts["mix"], ln_v_g[None], ln_v_b[None], wsp, b_spatial.T)
    b_out = _prompt_attention(qkv0, qkv1, qkv2, band_bias)
    mem_kv, mem_kv_b = _memory_kv(mem_prompt, norm_mem[None], w_mem_kv.astype(BF16))
    x_new, *routing = _prompt_merge(x_prompt, norm_mix[None], a_out, b_out, qc, mem_kv_b, wts["gate"],
                                    wts["a"], wts["b"], wts["c"], wts["out"], route_params, cnt_in)
    wins = [_to_window_layout(w) for w in (win0, win1, win2)]
    return x_new, routing, wins, mem_kv.reshape(1, b, N_MEM, 2, C_HEADS, C_HEAD_DIM)


def _route_kernel(x_ref, g_ref, wr_ref, br_ref, cnt_in_ref, e_ref, rank_ref, gate_ref, cnt_ref, carry_ref):
    @pl.when(pl.program_id(0) == 0)
    def _():
        carry_ref[...] = cnt_in_ref[...].astype(F32)

    _route_rows(x_ref[...], g_ref, wr_ref, br_ref, e_ref, rank_ref, gate_ref, cnt_ref, carry_ref)


def _route_rows(x, g_ref, wr_ref, br_ref, e_ref, rank_ref, gate_ref, cnt_ref, carry_ref):
    tm = x.shape[0]
    h = _rms(x, g_ref[...])
    logits = lax.dot_general(wr_ref[...], h.astype(BF16), (((1,), (1,)), ((), ())),
                             preferred_element_type=F32) + br_ref[...]
    expert = lax.broadcasted_iota(I32, logits.shape, 0)
    vals, idxs = [], []
    member = jnp.zeros(logits.shape, F32)
    for _ in range(TOP_K):
        m = jnp.max(logits, axis=0, keepdims=True)
        idx = jnp.min(jnp.where(logits == m, expert, N_EXPERTS), axis=0, keepdims=True)
        hit = expert == idx
        vals.append(m)
        idxs.append(idx)
        member = jnp.where(hit, 1.0, member)
        logits = jnp.where(hit, -jnp.inf, logits)
    p = [jnp.exp(v - vals[0]) for v in vals]
    den = p[0] + p[1] + p[2] + p[3]
    gates = jnp.concatenate([pk / den for pk in p] + [jnp.zeros((SUBLANES - TOP_K, tm), F32)], axis=0)
    gate_ref[...] = jnp.transpose(gates)
    e_ref[...] = jnp.concatenate(idxs, axis=0)
    before = (lax.broadcasted_iota(I32, (tm, tm), 0) < lax.broadcasted_iota(I32, (tm, tm), 1)).astype(BF16)
    prefix = jnp.dot(member.astype(BF16), before, preferred_element_type=F32) + carry_ref[:, 0:1]
    rank_ref[...] = jnp.concatenate(
        [jnp.sum(jnp.where(expert == idx, prefix, 0.0), axis=0, keepdims=True) for idx in idxs], axis=0).astype(I32)
    carry_ref[...] = carry_ref[...] + jnp.sum(member, axis=1, keepdims=True)
    cnt_ref[...] = carry_ref[...].astype(I32)


def _route_params(norm_ffn, w_router, b_router):
    return norm_ffn[None], w_router.T.astype(BF16), b_router[:, None]


def _route(x, route_params, cnt_in, tm):
    n, d = x.shape
    assert n % tm == 0
    norm_ffn, w_router_t, b_router = route_params
    const = lambda arr: pl.BlockSpec(arr.shape, lambda i: (0,) * arr.ndim)
    return pl.pallas_call(
        _route_kernel,
        grid=(n // tm,),
        in_specs=[pl.BlockSpec((tm, d), lambda i: (i, 0)), const(norm_ffn), const(w_router_t), const(b_router),
                  const(cnt_in)],
        out_specs=(pl.BlockSpec((TOP_K, tm), lambda i: (0, i)), pl.BlockSpec((TOP_K, tm), lambda i: (0, i)),
                   pl.BlockSpec((tm, SUBLANES), lambda i: (i, 0)), pl.BlockSpec((N_EXPERTS, LANES), lambda i: (0, 0))),
        out_shape=(jax.ShapeDtypeStruct((TOP_K, n), I32), jax.ShapeDtypeStruct((TOP_K, n), I32),
                   jax.ShapeDtypeStruct((n, SUBLANES), F32), jax.ShapeDtypeStruct((N_EXPERTS, LANES), I32)),
        scratch_shapes=[pltpu.VMEM((N_EXPERTS, LANES), F32)],
        compiler_params=_cparams("arbitrary"),
        name="moe_route",
    )(x, norm_ffn, w_router_t, b_router, cnt_in)


DMA_ROWS_PER_ITER = 8
COMBINE_ROWS = 16
N_DMA_THREADS = 2
N_PLAN_TAIL = 1 + N_EXPERTS


def _row_tile(r):
    return pl.ds(pl.multiple_of(r * ROW_TILES, ROW_TILES), ROW_TILES)


def _plan_kernel(n_blocks, cnt_ref, plan_ref):
    shift = BM_EXPERT.bit_length() - 1

    def per_expert(e, carry):
        blk0, last_e = carry
        nb = (cnt_ref[e] + (BM_EXPERT - 1)) >> shift
        plan_ref[n_blocks + 1 + e] = blk0 << shift

        def fill(j, cc):
            plan_ref[blk0 + j] = e
            return cc
        lax.fori_loop(0, nb, fill, 0)
        return blk0 + nb, jnp.where(nb > 0, e, last_e)
    n_used, last_e = lax.fori_loop(0, N_EXPERTS, per_expert, (0, 0))

    def tail(j, cc):
        plan_ref[j] = last_e
        return cc
    lax.fori_loop(n_used, n_blocks, tail, 0)
    plan_ref[n_blocks] = n_used


def _moe_plan(cnt, n_blocks):
    smem = pl.BlockSpec(memory_space=pltpu.SMEM)
    return pl.pallas_call(
        functools.partial(_plan_kernel, n_blocks),
        in_specs=[smem], out_specs=smem,
        out_shape=jax.ShapeDtypeStruct((n_blocks + N_PLAN_TAIL,), I32),
        name="moe_plan",
    )(cnt)


POSITION_TILES_PER_STEP = 16


def _positions_kernel(n_blocks, plan_ref, e_ref, r_ref, pos_ref):
    tiles, _, tm = pos_ref.shape
    e = e_ref[...]
    first = jnp.zeros(e.shape, I32)
    for ex in range(N_EXPERTS):
        first = jnp.where(e == ex, plan_ref[n_blocks + 1 + ex], first)
    pos = first + r_ref[...]
    for i in range(tiles):
        pos_ref[i] = pos[:, i * tm:(i + 1) * tm]


def _positions(plan, e, r, n_blocks, tm):
    n = e.shape[1]
    assert n % tm == 0
    nt = n // tm
    per_step = math.gcd(nt, POSITION_TILES_PER_STEP)
    pos = pl.pallas_call(
        functools.partial(_positions_kernel, n_blocks),
        grid=(nt // per_step,),
        in_specs=[pl.BlockSpec(memory_space=pltpu.SMEM), pl.BlockSpec((TOP_K, per_step * tm), lambda i: (0, i)),
                  pl.BlockSpec((TOP_K, per_step * tm), lambda i: (0, i))],
        out_specs=pl.BlockSpec((per_step, TOP_K, tm), lambda i: (i, 0, 0)),
        out_shape=jax.ShapeDtypeStruct((nt, TOP_K, tm), I32),
        compiler_params=_cparams("arbitrary"),
        name="moe_positions",
    )(plan, e, r)
    return pos.reshape(-1)


def _scatter_rows_kernel(n_blocks, cnt_ref, plan_ref, x_ref, pos_ref, xs_ref, poss_ref, g_ref, h_hbm,
                         hbuf, zbuf, sem, zsem):
    t = pl.program_id(0)
    n_tiles = pl.num_programs(0) - 1
    tm = x_ref.shape[0]
    n_s = xs_ref.shape[0]
    bm = BM_EXPERT
    shift = bm.bit_length() - 1
    slot = t % 2

    def zero_rows(first, count, wait):
        for bit in range(shift):
            n = 1 << bit
            lo = first + (count & (n - 1))

            @pl.when((count >> bit) & 1 == 1)
            def _():
                cp = pltpu.make_async_copy(zbuf.at[pl.ds(0, n * ROW_TILES), :], h_hbm.at[_row_tile_n(lo, n), :],
                                           zsem.at[0])
                cp.wait() if wait else cp.start()

    def zero_block(blk, wait):
        cp = pltpu.make_async_copy(zbuf, h_hbm.at[_row_tile_n(blk * bm, bm), :], zsem.at[0])
        cp.wait() if wait else cp.start()

    @pl.when(t == 0)
    def _():
        zbuf[...] = jnp.zeros(zbuf.shape, F32)
        for wait in (False, True):
            def pad_expert(e, cc, wait=wait):
                cnt = cnt_ref[e]
                zero_rows(plan_ref[n_blocks + 1 + e] + cnt, (-cnt) & (bm - 1), wait)
                return cc
            lax.fori_loop(0, N_EXPERTS, pad_expert, 0)

            def pad_block(blk, cc, wait=wait):
                zero_block(blk, wait)
                return cc
            lax.fori_loop(plan_ref[n_blocks], n_blocks, pad_block, 0)

    def wait_copies(sl, n_tok):
        for _ in range(TOP_K):
            pltpu.make_async_copy(hbuf.at[sl, pl.ds(0, n_tok * ROW_TILES), :],
                                  h_hbm.at[pl.ds(0, n_tok * ROW_TILES), :], sem.at[sl]).wait()

    def copy_rows(src_ref, rows_ref, n_tok):
        h = _rms(src_ref[...], g_ref[...])
        for c in range(ROW_TILES):
            hbuf[slot, pl.ds(c, n_tok, stride=ROW_TILES), :] = h[:, c * LANES:(c + 1) * LANES]

        def body(it, carry):
            for u in range(DMA_ROWS_PER_ITER):
                j = it * DMA_ROWS_PER_ITER + u
                for k in range(TOP_K):
                    pltpu.make_async_copy(hbuf.at[slot, _row_tile(j), :],
                                          h_hbm.at[_row_tile(rows_ref[k * n_tok + j]), :],
                                          sem.at[slot]).start(priority=k % N_DMA_THREADS)
            return carry
        lax.fori_loop(0, n_tok // DMA_ROWS_PER_ITER, body, 0)

    @pl.when(t >= 2)
    def _():
        wait_copies(slot, tm)

    @pl.when(t < n_tiles)
    def _():
        copy_rows(x_ref, pos_ref, tm)

    @pl.when(t == n_tiles)
    def _():
        copy_rows(xs_ref, poss_ref, n_s)
        wait_copies(slot, n_s)

        @pl.when(n_tiles >= 1)
        def _():
            wait_copies(1 - slot, tm)


def _row_tile_n(r, n):
    return pl.ds(pl.multiple_of(r * ROW_TILES, ROW_TILES), n * ROW_TILES)


def _scatter_rows(cnt, plan, x_p, pos_p, x_s, pos_s, norm_ffn, n_blocks, tm):
    n_p, d = x_p.shape
    n_s = x_s.shape[0]
    assert n_p % tm == 0 and tm % DMA_ROWS_PER_ITER == 0 and n_s % DMA_ROWS_PER_ITER == 0 and n_s <= tm
    nt = n_p // tm
    smem = lambda: pl.BlockSpec(memory_space=pltpu.SMEM)
    tile_idx = lambda i: jnp.minimum(i, nt - 1)
    return pl.pallas_call(
        functools.partial(_scatter_rows_kernel, n_blocks),
        grid=(nt + 1,),
        in_specs=[smem(), smem(), pl.BlockSpec((tm, d), lambda i: (tile_idx(i), 0)),
                  pl.BlockSpec((TOP_K * tm,), lambda i: (tile_idx(i),), memory_space=pltpu.SMEM),
                  pl.BlockSpec((n_s, d), lambda i: (0, 0)), smem(), pl.BlockSpec((1, d), lambda i: (0, 0))],
        out_specs=pl.BlockSpec(memory_space=pl.ANY),
        out_shape=jax.ShapeDtypeStruct((n_blocks * BM_EXPERT * ROW_TILES, LANES), F32),
        scratch_shapes=[pltpu.VMEM((2, tm * ROW_TILES, LANES), F32), pltpu.VMEM((BM_EXPERT * ROW_TILES, LANES), F32),
                        pltpu.SemaphoreType.DMA((2,)), pltpu.SemaphoreType.DMA((1,))],
        compiler_params=_cparams("arbitrary"),
        name="moe_scatter_rows",
    )(cnt, plan, x_p, pos_p, x_s, pos_s, norm_ffn)


WEIGHT_CAST_ROWS = 128


def _expert_blocks_kernel(n_blocks, plan_ref, x_ref, wgu_ref, bgu_ref, wd_ref, bd_ref, y_ref, wgu_b, wd_b):
    bm = BM_EXPERT
    i = pl.program_id(0)

    @pl.when((i == 0) | (plan_ref[i] != plan_ref[jnp.maximum(i - 1, 0)]))
    def _():
        for src, dst in ((wgu_ref, wgu_b), (wd_ref, wd_b)):
            for r0 in range(0, src.shape[1], WEIGHT_CAST_ROWS):
                rows = slice(r0, r0 + WEIGHT_CAST_ROWS)
                dst[rows, :] = src[0, rows, :].astype(BF16)

    @pl.when(i < plan_ref[n_blocks])
    def _():
        x = jnp.concatenate([x_ref[pl.ds(c, bm, stride=ROW_TILES), :] for c in range(ROW_TILES)], axis=1)
        gu = jnp.dot(x.astype(BF16), wgu_b[...], preferred_element_type=F32) + bgu_ref[0]
        gate = jnp.minimum(gu[:, :D_FF], SWIGLU_LIMIT)
        up = jnp.clip(gu[:, D_FF:], -SWIGLU_LIMIT, SWIGLU_LIMIT)
        act = gate * jax.nn.sigmoid(SWIGLU_ALPHA * gate) * (up + 1.0)
        y = jnp.dot(act.astype(BF16), wd_b[...], preferred_element_type=F32) + bd_ref[0]
        for c in range(ROW_TILES):
            y_ref[pl.ds(c, bm, stride=ROW_TILES), :] = y[:, c * LANES:(c + 1) * LANES]

    @pl.when(i >= plan_ref[n_blocks])
    def _():
        y_ref[...] = jnp.zeros(y_ref.shape, F32)


def _expert_blocks(plan, h_sorted, wgu, bgu, wd, bd, n_blocks):
    bm = BM_EXPERT
    by_expert = lambda *shape: pl.BlockSpec((1,) + shape, lambda i, plan: (plan[i],) + (0,) * len(shape))
    used = lambda i, plan: (jnp.minimum(i, plan[n_blocks] - 1), 0)
    grid_spec = pltpu.PrefetchScalarGridSpec(
        num_scalar_prefetch=1,
        grid=(n_blocks,),
        in_specs=[pl.BlockSpec((bm * ROW_TILES, LANES), used), by_expert(D_MODEL, 2 * D_FF), by_expert(1, 2 * D_FF),
                  by_expert(D_FF, D_MODEL), by_expert(1, D_MODEL)],
        out_specs=pl.BlockSpec((bm * ROW_TILES, LANES), lambda i, plan: (i, 0)),
        scratch_shapes=[pltpu.VMEM(wgu.shape[1:], BF16), pltpu.VMEM(wd.shape[1:], BF16)],
    )
    return pl.pallas_call(
        functools.partial(_expert_blocks_kernel, n_blocks),
        grid_spec=grid_spec,
        out_shape=jax.ShapeDtypeStruct(h_sorted.shape, F32),
        compiler_params=_cparams("arbitrary"),
        name="moe_experts",
    )(plan, h_sorted, wgu, bgu, wd, bd)


def _gather_combine_kernel(x_ref, pos_ref, pos_next_ref, gate_ref, g_ref, y_hbm, o_ref, ybuf, sem):
    t = pl.program_id(0)
    n_tiles = pl.num_programs(0)
    tm = x_ref.shape[0]
    slot = t % 2

    def gather(rows_ref, sl):
        def body(it, carry):
            for u in range(DMA_ROWS_PER_ITER):
                j = it * DMA_ROWS_PER_ITER + u
                for k in range(TOP_K):
                    pltpu.make_async_copy(y_hbm.at[_row_tile(rows_ref[k * tm + j]), :],
                                          ybuf.at[sl, _row_tile(k * tm + j), :],
                                          sem.at[sl]).start(priority=k % N_DMA_THREADS)
            return carry
        lax.fori_loop(0, tm // DMA_ROWS_PER_ITER, body, 0)

    @pl.when(t == 0)
    def _():
        gather(pos_ref, 0)

    pltpu.make_async_copy(y_hbm.at[pl.ds(0, TOP_K * tm * ROW_TILES), :], ybuf.at[slot], sem.at[slot]).wait()

    def combine_rows(issue_next):
        rb = min(tm, COMBINE_ROWS)

        def body(it, carry):
            r0 = pl.multiple_of(it * rb, rb)
            gates = gate_ref[pl.ds(r0, rb), :]
            xs = [x_ref[pl.ds(r0, rb), c * LANES:(c + 1) * LANES] for c in range(ROW_TILES)]
            ys = [[ybuf[slot, pl.ds((k * tm + r0) * ROW_TILES + c, rb, stride=ROW_TILES), :]
                   for c in range(ROW_TILES)] for k in range(TOP_K)]
            if issue_next:
                for u in range(rb):
                    for k in range(TOP_K):
                        j = r0 + u
                        pltpu.make_async_copy(y_hbm.at[_row_tile(pos_next_ref[k * tm + j]), :],
                                              ybuf.at[1 - slot, _row_tile(k * tm + j), :],
                                              sem.at[1 - slot]).start(priority=k % N_DMA_THREADS)
            cols = []
            for c in range(ROW_TILES):
                acc = xs[c]
                for k in range(TOP_K):
                    acc = acc + gates[:, k:k + 1] * ys[k][c]
                cols.append(acc)
            o_ref[pl.ds(r0, rb), :] = _rms(jnp.concatenate(cols, axis=1), g_ref[...])
            return carry
        lax.fori_loop(0, tm // rb, body, 0)

    @pl.when(t + 1 < n_tiles)
    def _():
        combine_rows(True)

    @pl.when(t + 1 >= n_tiles)
    def _():
        combine_rows(False)


def _gather_combine(x, pos, gates, norm_final, y_sorted, tm):
    n, d = x.shape
    assert n % tm == 0 and tm % DMA_ROWS_PER_ITER == 0
    nt = n // tm
    rows = lambda off: pl.BlockSpec((TOP_K * tm,), lambda i: (jnp.minimum(i + off, nt - 1),),
                                    memory_space=pltpu.SMEM)
    return pl.pallas_call(
        _gather_combine_kernel,
        grid=(nt,),
        in_specs=[pl.BlockSpec((tm, d), lambda i: (i, 0)), rows(0), rows(1),
                  pl.BlockSpec((tm, SUBLANES), lambda i: (i, 0)), pl.BlockSpec((1, d), lambda i: (0, 0)),
                  pl.BlockSpec(memory_space=pl.ANY)],
        out_specs=pl.BlockSpec((tm, d), lambda i: (i, 0)),
        out_shape=jax.ShapeDtypeStruct((n, d), F32),
        scratch_shapes=[pltpu.VMEM((2, TOP_K * tm * ROW_TILES, LANES), F32), pltpu.SemaphoreType.DMA((2,))],
        compiler_params=_cparams("arbitrary"),
        name="moe_combine",
    )(x, pos, pos, gates, norm_final, y_sorted)


def _moe_and_final_norm(x_p, routing_p, x_s, routing_s, norm_ffn, w_gate_up, b_gate_up, w_down, b_down, norm_final):
    n_p, n_s = x_p.shape[0], x_s.shape[0]
    n_rows = (n_p + n_s) * TOP_K
    n_blocks = -(-(n_rows + N_EXPERTS * (BM_EXPERT - 1)) // BM_EXPERT)
    g_ffn = norm_ffn[None]
    e_s, r_s, gates_s, _ = routing_s
    e_p, r_p, gates_p, cnt = routing_p
    plan = _moe_plan(cnt[:, 0], n_blocks)
    pos_p = _positions(plan, e_p, r_p, n_blocks, TM_COMBINE)
    pos_s = _positions(plan, e_s, r_s, n_blocks, n_s)
    h_sorted = _scatter_rows(cnt[:, 0], plan, x_p, pos_p, x_s, pos_s, g_ffn, n_blocks, TM_COMBINE)
    y_sorted = _expert_blocks(plan, h_sorted, w_gate_up, b_gate_up[:, None, :], w_down, b_down[:, None, :], n_blocks)
    y_p = _gather_combine(x_p, pos_p, gates_p, norm_final[None], y_sorted, TM_COMBINE)
    y_s = _gather_combine(x_s, pos_s, gates_s, norm_final[None], y_sorted, n_s)
    return y_p, y_s


def _rb(x):
    return x.astype(BF16).astype(F32)


def _sample_proj_kernel(x_ref, g_ref, wm_ref, wg_ref, z_ref):
    hb = _rms(x_ref[...], g_ref[...]).astype(BF16)
    z_ref[:, :MIX_WIDTH] = jnp.dot(hb, wm_ref[...], preferred_element_type=F32)
    z_ref[:, MIX_WIDTH:] = jnp.dot(hb, wg_ref[...], preferred_element_type=F32)


def _sample_proj(x, norm_mix, w_mix, w_gate):
    n, d = x.shape
    args = (x, norm_mix, w_mix, w_gate)
    width = w_mix.shape[1] + w_gate.shape[1]
    return pl.pallas_call(
        _sample_proj_kernel,
        grid=(1,),
        in_specs=[pl.BlockSpec(a.shape, lambda j: (0, 0)) for a in args],
        out_specs=pl.BlockSpec((n, width), lambda j: (0, 0)),
        out_shape=jax.ShapeDtypeStruct((n, width), F32),
        compiler_params=_cparams("arbitrary"),
        name="sample_proj",
    )(*args)


def _sample_mix_kernel(z_ref, c0_ref, c1_ref, c2_ref, mem_ref, sb0_ref, sb1_ref, sb2_ref, nb_ref, lng_ref, lnb_ref,
                       wsp_ref, bsp_ref, bo_ref, co_ref, a_ref, v_ref, n0_ref, n1_ref, n2_ref):
    z = z_ref[0]
    scale = 1.0 / math.sqrt(B_HEAD_DIM)
    row_head = lax.broadcasted_iota(I32, (SUBLANES, B_GROUP_W), 0)
    lane_head = lax.broadcasted_iota(I32, (SUBLANES, B_GROUP_W), 1) // B_HEAD_DIM
    own = (row_head == lane_head).astype(F32)
    kv_cols = jnp.transpose(jnp.broadcast_to(z[:, B_WIDTH:3 * B_WIDTH], (SUBLANES, 2 * B_WIDTH)))[:, 0:1]
    outs, lses = [], []
    for g, (cache_ref, sb_ref, new_ref) in enumerate(((c0_ref, sb0_ref, n0_ref), (c1_ref, sb1_ref, n1_ref),
                                                      (c2_ref, sb2_ref, n2_ref))):
        window = cache_ref.shape[2]
        q, kn, vn = (_rb(z[:, i * B_WIDTH + g * B_GROUP_W:i * B_WIDTH + (g + 1) * B_GROUP_W]) for i in range(3))
        qh = own * q
        kt = cache_ref[0, :B_GROUP_W, :].astype(BF16)
        vt = cache_ref[0, B_GROUP_W:, :].astype(BF16)
        lg = jnp.dot(qh.astype(BF16), kt, preferred_element_type=F32) * scale + sb_ref[...]
        ln = jnp.sum(qh * kn, axis=1, keepdims=True) * scale + nb_ref[g, :, 0:1]
        m = jnp.maximum(jnp.max(lg, axis=1, keepdims=True), ln)
        lse = m + jnp.log(jnp.sum(jnp.exp(lg - m), axis=1, keepdims=True) + jnp.exp(ln - m))
        p = jnp.exp(lg - lse).astype(BF16)
        pn = _rb(jnp.exp(ln - lse))
        pv = lax.dot_general(p, vt, (((1,), (1,)), ((), ())), preferred_element_type=F32) + pn * vn
        outs.append(jnp.sum(own * pv, axis=0, keepdims=True))
        lses.append(jnp.sum(own * lse, axis=0, keepdims=True))
        new_col = jnp.concatenate([kv_cols[i * B_WIDTH + g * B_GROUP_W:i * B_WIDTH + (g + 1) * B_GROUP_W]
                                   for i in range(2)], axis=0)
        lane = lax.broadcasted_iota(I32, (1, window), 1)
        new_ref[0] = jnp.where(lane == window - 1, new_col, pltpu.roll(cache_ref[0], window - 1, axis=1))
    m = jnp.maximum(jnp.maximum(lses[0], lses[1]), lses[2])
    ws = [jnp.exp(l - m) for l in lses]
    den = ws[0] + ws[1] + ws[2]
    bo_ref[0] = _rb(ws[0] / den) * _rb(outs[0]) + _rb(ws[1] / den) * _rb(outs[1]) + _rb(ws[2] / den) * _rb(outs[2])

    cs = []
    for h in range(C_HEADS):
        c0 = MIX_WIDTH - C_WIDTH + h * C_HEAD_DIM
        qh = _rb(z[:, c0:c0 + C_HEAD_DIM])
        kh = _rb(mem_ref[0, pl.ds(h, N_MEM, stride=2 * C_HEADS), :])
        vh = _rb(mem_ref[0, pl.ds(C_HEADS + h, N_MEM, stride=2 * C_HEADS), :])
        s = jnp.sum(kh * qh, axis=-1, keepdims=True) * (1.0 / math.sqrt(C_HEAD_DIM))
        p = jnp.exp(s - jnp.max(s, axis=0, keepdims=True))
        p = _rb(p / jnp.sum(p, axis=0, keepdims=True))
        cs.append(jnp.sum(p * vh, axis=0, keepdims=True))
    co_ref[0] = jnp.concatenate(cs, axis=1)

    u = _gelu(z[:, 3 * B_WIDTH:3 * B_WIDTH + A_WIDTH])
    v = _layernorm(_gelu(z[:, 3 * B_WIDTH + A_WIDTH:3 * B_WIDTH + 2 * A_WIDTH]), lng_ref[...], lnb_ref[...])
    v_ref[0] = v
    a_ref[0] = u * (_rb(wsp_ref[...]) * _rb(v) + bsp_ref[...])


def _sample_mix(z, caches, mem, sbias, nbias, ln_g, ln_b, wsp0, bsp0):
    n = z.shape[0]
    const = lambda arr: pl.BlockSpec(arr.shape, lambda i: (0,) * arr.ndim)
    row = lambda w: pl.BlockSpec((1, 1, w), lambda i: (i, 0, 0))
    per_req = lambda arr: pl.BlockSpec((1,) + arr.shape[1:], lambda i: (i, 0, 0))
    widths = (B_GROUP_W, C_WIDTH, A_WIDTH, A_WIDTH)
    return pl.pallas_call(
        _sample_mix_kernel,
        grid=(n,),
        in_specs=[row(z.shape[2])] + [per_req(c) for c in caches] + [per_req(mem)]
        + [const(a) for a in (*sbias, nbias, ln_g, ln_b, wsp0, bsp0)],
        out_specs=tuple(row(w) for w in widths) + tuple(per_req(c) for c in caches),
        out_shape=tuple(jax.ShapeDtypeStruct((n, 1, w), F32) for w in widths)
        + tuple(jax.ShapeDtypeStruct(c.shape, F32) for c in caches),
        compiler_params=_cparams("arbitrary"),
        name="sample_mix",
    )(z, *caches, mem, *sbias, nbias, ln_g, ln_b, wsp0, bsp0)


def _sample_merge_kernel(x_ref, z_ref, a_ref, b_ref, c_ref, wa_ref, wb_ref, wc_ref, wo_ref, o_ref):
    dot = lambda a, b: jnp.dot(a.astype(BF16), b, preferred_element_type=F32)
    merged = None
    for i, (act_ref, w_ref) in enumerate(((a_ref, wa_ref), (b_ref, wb_ref), (c_ref, wc_ref))):
        gate = jax.nn.sigmoid(z_ref[:, MIX_WIDTH + i * D_MODEL:MIX_WIDTH + (i + 1) * D_MODEL])
        term = gate * dot(act_ref[...], w_ref[...])
        merged = term if merged is None else merged + term
    o_ref[...] = x_ref[...] + dot(merged, wo_ref[...])


def _sample_merge(x, z, a, bo, co, wa, wb, wc, wo):
    args = (x, z, a, bo, co, wa, wb, wc, wo)
    return pl.pallas_call(
        _sample_merge_kernel,
        grid=(1,),
        in_specs=[pl.BlockSpec(arr.shape, lambda i: (0, 0)) for arr in args],
        out_specs=pl.BlockSpec(x.shape, lambda i: (0, 0)),
        out_shape=jax.ShapeDtypeStruct(x.shape, F32),
        compiler_params=_cparams("arbitrary"),
        name="sample_merge",
    )(*args)


def _sample_mixers(x_sample, caches, cache_mem, sbias, nbias, norm_mix, ln_v_g, ln_v_b, w_spatial, b_spatial, wts):
    n = x_sample.shape[0]
    x = x_sample[:, 0]
    z = _sample_proj(x, norm_mix[None], wts["mix"], wts["gate"])
    mem = cache_mem.reshape(n, N_MEM * 2 * C_HEADS, C_HEAD_DIM)
    wsp0 = jnp.repeat(w_spatial[:, 0, 0], LANES)[None]
    bsp0 = jnp.repeat(b_spatial[:, 0], LANES)[None]
    bo, co, a, v, *new_caches = _sample_mix(z[:, None], [_from_window_layout(c) for c in caches], mem, sbias, nbias,
                                            ln_v_g[None], ln_v_b[None], wsp0, bsp0)
    x_new = _sample_merge(x, z, a[:, 0], bo[:, 0], co[:, 0], wts["a"], wts["b"], wts["c"], wts["out"])
    return x_new, [_to_window_layout(c) for c in new_caches], v[:, 0]


def _mixer_weights(w_in, w_branch_a, w_branch_b, w_branch_c, w_out):
    return {"mix": w_in[:, :MIX_WIDTH].astype(BF16), "gate": w_in[:, MIX_WIDTH:].astype(BF16),
            "a": w_branch_a.astype(BF16), "b": w_branch_b.astype(BF16), "c": w_branch_c.astype(BF16),
            "out": w_out.astype(BF16)}


def kernel(x_prompt, x_sample, cache_win0_kv, cache_win1_kv, cache_win2_kv, cache_mem_kv, mem_prompt, rel_bias,
           norm_mix, norm_mem, w_in, ln_v_g, ln_v_b, w_spatial, b_spatial, w_mem_kv, w_branch_a, w_branch_b,
           w_branch_c, w_out, norm_ffn, w_router, b_router, w_gate_up, b_gate_up, w_down, b_down, norm_final):
    assert norm_mix.shape[0] == 1, "one layer"
    b, s, d = x_prompt.shape
    n_s = x_sample.shape[0]
    caches = (cache_win0_kv[0], cache_win1_kv[0], cache_win2_kv[0])
    assert all(c.shape[1] == w for c, (w, _) in zip(caches, B_PAIRS)), "window buffers hold a full window"
    wts = _mixer_weights(w_in[0], w_branch_a[0], w_branch_b[0], w_branch_c[0], w_out[0])
    band_bias, *sbias, nbias = _bias_tables(rel_bias)
    route_params = _route_params(norm_ffn[0], w_router[0], b_router[0])
    xs, win_s, v_s = _sample_mixers(x_sample, caches, cache_mem_kv[0], sbias, nbias, norm_mix[0], ln_v_g[0],
                                    ln_v_b[0], w_spatial[0], b_spatial[0], wts)
    routing_s = _route(xs, route_params, jnp.zeros((N_EXPERTS, LANES), I32), n_s)
    xp, routing_p, win_p, mem_p = _prompt_mixers(x_prompt, mem_prompt, band_bias, norm_mix[0], norm_mem[0], ln_v_g[0],
                                                 ln_v_b[0], w_spatial[0], b_spatial[0], w_mem_kv[0], wts,
                                                 route_params, routing_s[3])
    y_p, y_s = _moe_and_final_norm(xp.reshape(b * s, d), routing_p, xs, routing_s, norm_ffn[0], w_gate_up[0],
                                   b_gate_up[0], w_down[0], b_down[0], norm_final)
    chunk_v = v_s.reshape(1, n_s, 1, A_GROUPS, LANES)
    return (y_p.reshape(b, s, d), y_s[:, None], win_p[0], win_p[1], win_p[2], mem_p,
            win_s[0], win_s[1], win_s[2], chunk_v)
```

```python
import functools
import math

import numpy as np
import jax
import jax.numpy as jnp
from jax import lax
from jax.experimental import pallas as pl
from jax.experimental.pallas import tpu as pltpu

F32 = jnp.float32
BF16 = jnp.bfloat16
I32 = jnp.int32

D_MODEL = 1024
N_MEM = 256
CHUNK = 128
A_GROUPS = 4
A_WIDTH = 512
B_PAIRS = ((128, 1), (512, 4), (2048, 16))
B_GROUP_W = 256
B_WIDTH = 768
B_HEAD_DIM = 64
BAND = 128
C_HEADS = 4
C_HEAD_DIM = 128
C_WIDTH = 512
REL_BUCKETS = 32
REL_MAX_DIST = 2048
N_EXPERTS = 32
TOP_K = 4
D_FF = 1024
SWIGLU_LIMIT = 7.0
SWIGLU_ALPHA = 1.702
EPS = 1e-6
NEG_INF = -1e30
MIX_WIDTH = 3 * B_WIDTH + 2 * A_WIDTH + C_WIDTH

LANES = 128
SUBLANES = 8
ROW_TILES = D_MODEL // LANES
VMEM_LIMIT = 56 * 1024 * 1024

TM_PROJ = 1024
TM_MERGE = 1024
BM_EXPERT = 512
TM_COMBINE = 512
ATTN_BLOCKS_PER_ITER = 6


def _cparams(*sem):
    return pltpu.CompilerParams(dimension_semantics=sem, vmem_limit_bytes=VMEM_LIMIT)


def _t5_bucket(dist):
    dist = np.maximum(np.asarray(dist), 0)
    max_exact = REL_BUCKETS // 2
    log_ratio = np.log(np.maximum(dist, max_exact) / max_exact) / math.log(REL_MAX_DIST / max_exact)
    large = np.minimum(max_exact + (log_ratio * (REL_BUCKETS - max_exact)).astype(np.int32), REL_BUCKETS - 1)
    return np.where(dist < max_exact, dist, large).astype(np.int32)


def _gelu(x):
    c = math.sqrt(2.0 / math.pi)
    return x * (0.5 * (1.0 + jnp.tanh(c * (x + 0.044715 * (x * x * x)))))


def _rms(x, g):
    return x * lax.rsqrt(jnp.mean(x * x, axis=-1, keepdims=True) + EPS) * g


def _layernorm(x, g, b):
    xc = x - jnp.mean(x, axis=-1, keepdims=True)
    return xc * lax.rsqrt(jnp.mean(xc * xc, axis=-1, keepdims=True) + EPS) * g + b


def _proj_kernel(x_ref, g_ref, w_ref, lng_ref, lnb_ref, wsp_ref, bsp_ref,
                 qkv0_ref, qkv1_ref, qkv2_ref, win0_ref, win1_ref, win2_ref, a_ref, qc_ref, scr_ref):
    tm = x_ref.shape[1]
    hb = _rms(x_ref[0], g_ref[...]).astype(BF16)

    def proj(c0, width):
        return jnp.dot(hb, w_ref[:, c0:c0 + width], preferred_element_type=F32)

    for g, (_, dil) in enumerate(B_PAIRS):
        q, k, v = (proj(i * B_WIDTH + g * B_GROUP_W, B_GROUP_W) for i in range(3))
        kv = jnp.concatenate([k, v], axis=1)
        window = min(B_PAIRS[g][0], tm)
        (win0_ref, win1_ref, win2_ref)[g][0] = jnp.transpose(kv[tm - window:])
        if dil == 1:
            qkv0_ref[0] = jnp.concatenate([q, k, v], axis=1).astype(BF16)
            continue
        out_ref = qkv1_ref if g == 1 else qkv2_ref
        for i, arr in enumerate((q, k, v)):
            for ct in range(2):
                scr_ref[2 * i + ct] = arr[:, ct * LANES:(ct + 1) * LANES]
        for r in range(dil):
            rows = [scr_ref[j, pl.ds(r, tm // dil, stride=dil), :] for j in range(6)]
            out_ref[0, r] = jnp.concatenate(rows, axis=1).astype(BF16)

    u = _gelu(proj(3 * B_WIDTH, A_WIDTH))
    v = _layernorm(_gelu(proj(3 * B_WIDTH + A_WIDTH, A_WIDTH)), lng_ref[...], lnb_ref[...]).astype(BF16)
    for c in range(tm // CHUNK):
        rs = slice(c * CHUNK, (c + 1) * CHUNK)
        for g in range(A_GROUPS):
            cs = slice(g * LANES, (g + 1) * LANES)
            s = jnp.dot(wsp_ref[g], v[rs, cs], preferred_element_type=F32) + bsp_ref[:, g:g + 1]
            a_ref[0, rs, cs] = (u[rs, cs] * s).astype(BF16)

    qc_ref[0] = proj(3 * B_WIDTH + 2 * A_WIDTH, C_WIDTH).astype(BF16)


def _prompt_proj(x, norm_mix, w_mix, ln_g, ln_b, wsp, bsp):
    b, s, d = x.shape
    tm = TM_PROJ
    assert s % tm == 0 and s >= B_PAIRS[2][0] and tm >= B_PAIRS[1][0]
    nt = s // tm
    const = lambda *shape: pl.BlockSpec(shape, lambda i, j: (0,) * len(shape))
    out_shape = (
        jax.ShapeDtypeStruct((b, s, 3 * B_GROUP_W), BF16),
        jax.ShapeDtypeStruct((b, 4, s // 4, 3 * B_GROUP_W), BF16),
        jax.ShapeDtypeStruct((b, 16, s // 16, 3 * B_GROUP_W), BF16),
        jax.ShapeDtypeStruct((b, 2 * B_GROUP_W, B_PAIRS[0][0]), F32),
        jax.ShapeDtypeStruct((b, 2 * B_GROUP_W, B_PAIRS[1][0]), F32),
        jax.ShapeDtypeStruct((b, 2 * B_GROUP_W, s), F32),
        jax.ShapeDtypeStruct((b, s, A_WIDTH), BF16),
        jax.ShapeDtypeStruct((b, s, C_WIDTH), BF16),
    )
    out_specs = (
        pl.BlockSpec((1, tm, 3 * B_GROUP_W), lambda i, j: (i, j, 0)),
        pl.BlockSpec((1, 4, tm // 4, 3 * B_GROUP_W), lambda i, j: (i, 0, j, 0)),
        pl.BlockSpec((1, 16, tm // 16, 3 * B_GROUP_W), lambda i, j: (i, 0, j, 0)),
        pl.BlockSpec((1, 2 * B_GROUP_W, B_PAIRS[0][0]), lambda i, j: (i, 0, 0)),
        pl.BlockSpec((1, 2 * B_GROUP_W, B_PAIRS[1][0]), lambda i, j: (i, 0, 0)),
        pl.BlockSpec((1, 2 * B_GROUP_W, tm), lambda i, j: (i, 0, j)),
        pl.BlockSpec((1, tm, A_WIDTH), lambda i, j: (i, j, 0)),
        pl.BlockSpec((1, tm, C_WIDTH), lambda i, j: (i, j, 0)),
    )
    return pl.pallas_call(
        _proj_kernel,
        grid=(b, nt),
        in_specs=[pl.BlockSpec((1, tm, d), lambda i, j: (i, j, 0)), const(1, d), const(d, MIX_WIDTH),
                  const(1, A_WIDTH), const(1, A_WIDTH), const(A_GROUPS, CHUNK, CHUNK), const(CHUNK, A_GROUPS)],
        out_specs=out_specs,
        out_shape=out_shape,
        scratch_shapes=[pltpu.VMEM((6, tm, LANES), F32)],
        compiler_params=_cparams("arbitrary", "arbitrary"),
        name="prompt_proj",
    )(x, norm_mix, w_mix, ln_g, ln_b, wsp, bsp)


def _attend(q, k, v, bias_ref, g, k0, hmask):
    kn = k.shape[0]
    qs = jnp.concatenate([q * hm for hm in hmask], axis=0)
    s = lax.dot_general(qs, k, (((1,), (1,)), ((), ())), preferred_element_type=F32) + bias_ref[g, :, k0:k0 + kn]
    m = jnp.max(s, axis=-1, keepdims=True)
    p = jnp.exp(s - m)
    l = jnp.sum(p, axis=-1, keepdims=True)
    pv = jnp.dot(p.astype(BF16), v, preferred_element_type=F32) / l
    lse_rows = m + jnp.log(l)
    out = jnp.zeros((BAND, B_GROUP_W), F32)
    lse = jnp.zeros((BAND, B_GROUP_W), F32)
    for h, hm in enumerate(hmask):
        rows = slice(h * BAND, (h + 1) * BAND)
        sel = hm > 0
        out = jnp.where(sel, pv[rows], out)
        lse = jnp.where(sel, lse_rows[rows], lse)
    return out, lse


def _attn_kernel(q0_ref, q1_ref, q2_ref, bias_ref, o_ref, out_ref, lse_ref):
    s = q0_ref.shape[1]
    lane_head = lax.broadcasted_iota(I32, (1, B_GROUP_W), 1) // B_HEAD_DIM
    scale = 1.0 / math.sqrt(B_HEAD_DIM)
    hmask = [jnp.where(lane_head == h, scale, 0.0).astype(BF16) for h in range(4)]
    qs, ks, vs = (slice(i * B_GROUP_W, (i + 1) * B_GROUP_W) for i in range(3))

    def store(g, start, dil, out, lse):
        rows = pl.ds(start, BAND) if dil == 1 else pl.ds(start, BAND, stride=dil)
        for ct in range(2):
            out_ref[g, ct, rows, :] = out[:, ct * LANES:(ct + 1) * LANES]
            lse_ref[g, ct, rows, :] = lse[:, ct * LANES:(ct + 1) * LANES]

    for g, (_, dil) in enumerate(B_PAIRS):
        n = s // dil
        nb = n // BAND

        def load(rows, cols, r, g=g):
            if g == 0:
                return q0_ref[0, rows, cols]
            return (q1_ref if g == 1 else q2_ref)[0, r, rows, cols]

        def first_block(r, g=g, dil=dil, load=load):
            rows = pl.ds(0, BAND)
            out, lse = _attend(load(rows, qs, r), load(rows, ks, r), load(rows, vs, r), bias_ref, g, BAND, hmask)
            store(g, r, dil, out, lse)

        def later_block(i, g=g, dil=dil, nb=nb, load=load):
            r = i // (nb - 1)
            qb = i % (nb - 1) + 1
            q0 = pl.multiple_of(qb * BAND, BAND)
            rows_q = pl.ds(q0, BAND)
            rows_k = pl.ds(q0 - BAND, 2 * BAND)
            out, lse = _attend(load(rows_q, qs, r), load(rows_k, ks, r), load(rows_k, vs, r), bias_ref, g, 0, hmask)
            store(g, qb * (BAND * dil) + r, dil, out, lse)

        def run(block_fn, count):
            per = ATTN_BLOCKS_PER_ITER

            def body(it, carry):
                for u in range(per):
                    block_fn(it * per + u)
                return carry
            lax.fori_loop(0, count // per, body, 0)
            for i in range(count - count % per, count):
                block_fn(i)

        run(first_block, dil)
        run(later_block, dil * (nb - 1))

    rc = 256
    def combine(c, carry):
        rows = pl.ds(pl.multiple_of(c * rc, rc), rc)
        for ct in range(2):
            ls = [lse_ref[g, ct, rows, :] for g in range(3)]
            m = jnp.maximum(jnp.maximum(ls[0], ls[1]), ls[2])
            ws = [jnp.exp(l - m) for l in ls]
            den = ws[0] + ws[1] + ws[2]
            num = ws[0] * out_ref[0, ct, rows, :] + ws[1] * out_ref[1, ct, rows, :] + ws[2] * out_ref[2, ct, rows, :]
            o_ref[0, rows, ct * LANES:(ct + 1) * LANES] = (num / den).astype(BF16)
        return carry
    lax.fori_loop(0, s // rc, combine, 0)


def _prompt_attention(qkv0, qkv1, qkv2, bias):
    b, s, w = qkv0.shape
    return pl.pallas_call(
        _attn_kernel,
        grid=(b,),
        in_specs=[pl.BlockSpec((1, s, w), lambda i: (i, 0, 0)),
                  pl.BlockSpec((1, 4, s // 4, w), lambda i: (i, 0, 0, 0)),
                  pl.BlockSpec((1, 16, s // 16, w), lambda i: (i, 0, 0, 0)),
                  pl.BlockSpec(bias.shape, lambda i: (0, 0, 0))],
        out_specs=pl.BlockSpec((1, s, B_GROUP_W), lambda i: (i, 0, 0)),
        out_shape=jax.ShapeDtypeStruct((b, s, B_GROUP_W), BF16),
        scratch_shapes=[pltpu.VMEM((3, 2, s, LANES), F32), pltpu.VMEM((3, 2, s, LANES), F32)],
        compiler_params=_cparams("arbitrary"),
        name="prompt_attention",
    )(qkv0, qkv1, qkv2, bias)


def _bias_kernel(rel_ref, band_idx_ref, s0_ref, s1_ref, s2_ref, band_ref, sb0_ref, sb1_ref, sb2_ref, nb_ref):
    def lookup(idx, col):
        acc = jnp.full(idx.shape, NEG_INF, F32)
        for bucket in range(REL_BUCKETS):
            acc = jnp.where(idx == bucket, rel_ref[bucket, col], acc)
        return acc

    for g, (s_ref, sb_ref) in enumerate(((s0_ref, sb0_ref), (s1_ref, sb1_ref), (s2_ref, sb2_ref))):
        width = s_ref.shape[1]
        for h in range(4):
            band_ref[g, h * BAND:(h + 1) * BAND, :] = lookup(band_idx_ref[g], 4 * g + h)
        sb_ref[...] = jnp.concatenate([lookup(s_ref[...], 4 * g + h) for h in range(4)]
                                      + [jnp.full((SUBLANES - 4, width), NEG_INF, F32)], axis=0)
        nb_ref[g] = jnp.concatenate([jnp.full((1, LANES), rel_ref[0, 4 * g + h], F32) for h in range(4)]
                                    + [jnp.zeros((SUBLANES - 4, LANES), F32)], axis=0)


def _bias_tables(rel_bias):
    steps = np.arange(BAND)[:, None] + BAND - np.arange(2 * BAND)[None, :]
    valid = (steps >= 0) & (steps <= BAND)
    band_idx = np.stack([np.where(valid, _t5_bucket(np.clip(steps, 0, BAND) * dil), -1) for _, dil in B_PAIRS])
    s_idx = []
    for window, dil in B_PAIRS:
        w = np.arange(window)
        s_idx.append(np.where(w % dil == 0, _t5_bucket(window - w), -1)[None].astype(np.int32))
    args = (rel_bias, jnp.asarray(band_idx.astype(np.int32))) + tuple(jnp.asarray(s) for s in s_idx)
    vmem = lambda a: pl.BlockSpec(a.shape, lambda i: (0,) * a.ndim)
    out_shape = (jax.ShapeDtypeStruct((3, 4 * BAND, 2 * BAND), F32),) + tuple(
        jax.ShapeDtypeStruct((SUBLANES, window), F32) for window, _ in B_PAIRS) + (
        jax.ShapeDtypeStruct((3, SUBLANES, LANES), F32),)
    return pl.pallas_call(
        _bias_kernel,
        grid=(1,),
        in_specs=[pl.BlockSpec(memory_space=pltpu.SMEM)] + [vmem(a) for a in args[1:]],
        out_specs=tuple(pl.BlockSpec(s.shape, lambda i, n=len(s.shape): (0,) * n) for s in out_shape),
        out_shape=out_shape,
        compiler_params=_cparams("arbitrary"),
        name="bias_tables",
    )(*args)


def _memkv_kernel(mem_ref, g_ref, w_ref, kv_ref, kvb_ref):
    hb = _rms(mem_ref[0], g_ref[...]).astype(BF16)
    kv = jnp.dot(hb, w_ref[...], preferred_element_type=F32)
    m = kv.shape[0]
    n_heads = kv.shape[1] // C_HEAD_DIM
    for j in range(n_heads):
        kv_ref[0, pl.ds(j, m, stride=n_heads), :] = kv[:, j * C_HEAD_DIM:(j + 1) * C_HEAD_DIM]
    kvb_ref[0] = kv.astype(BF16)


def _memory_kv(mem, norm_mem, w_mem):
    b, m, d = mem.shape
    w = w_mem.shape[1]
    rows = m * w // C_HEAD_DIM
    return pl.pallas_call(
        _memkv_kernel,
        grid=(b,),
        in_specs=[pl.BlockSpec((1, m, d), lambda i: (i, 0, 0)), pl.BlockSpec((1, d), lambda i: (0, 0)),
                  pl.BlockSpec((d, w), lambda i: (0, 0))],
        out_specs=(pl.BlockSpec((1, rows, C_HEAD_DIM), lambda i: (i, 0, 0)),
                   pl.BlockSpec((1, m, w), lambda i: (i, 0, 0))),
        out_shape=(jax.ShapeDtypeStruct((b, rows, C_HEAD_DIM), F32), jax.ShapeDtypeStruct((b, m, w), BF16)),
        compiler_params=_cparams("arbitrary"),
        name="memory_kv",
    )(mem, norm_mem, w_mem)


def _merge_kernel(x_ref, g_ref, a_ref, b_ref, qc_ref, kv_ref, wg_ref, wa_ref, wb_ref, wc_ref, wo_ref,
                  gf_ref, wr_ref, br_ref, cnt_in_ref, o_ref, e_ref, rank_ref, gate_ref, cnt_ref, carry_ref):
    @pl.when((pl.program_id(0) == 0) & (pl.program_id(1) == 0))
    def _():
        carry_ref[...] = cnt_in_ref[...].astype(F32)

    x = x_ref[0]
    hb = _rms(x, g_ref[...]).astype(BF16)
    qc = qc_ref[0]
    cs = []
    for h in range(C_HEADS):
        hs = slice(h * C_HEAD_DIM, (h + 1) * C_HEAD_DIM)
        s = lax.dot_general(qc[:, hs], kv_ref[0, :, hs], (((1,), (1,)), ((), ())), preferred_element_type=F32)
        s = s * (1.0 / math.sqrt(C_HEAD_DIM))
        p = jnp.exp(s - jnp.max(s, axis=-1, keepdims=True))
        p = p / jnp.sum(p, axis=-1, keepdims=True)
        vs = slice(C_WIDTH + h * C_HEAD_DIM, C_WIDTH + (h + 1) * C_HEAD_DIM)
        cs.append(jnp.dot(p.astype(BF16), kv_ref[0, :, vs], preferred_element_type=F32))
    c = jnp.concatenate(cs, axis=1).astype(BF16)
    branches = ((a_ref[0], wa_ref), (b_ref[0], wb_ref), (c, wc_ref))
    merged = None
    for i, (act, w_ref) in enumerate(branches):
        gate = jax.nn.sigmoid(jnp.dot(hb, wg_ref[:, i * D_MODEL:(i + 1) * D_MODEL], preferred_element_type=F32))
        term = gate * jnp.dot(act, w_ref[...], preferred_element_type=F32)
        merged = term if merged is None else merged + term
    x_new = x + jnp.dot(merged.astype(BF16), wo_ref[...], preferred_element_type=F32)
    o_ref[0] = x_new
    _route_rows(x_new, gf_ref, wr_ref, br_ref, e_ref, rank_ref, gate_ref, cnt_ref, carry_ref)


def _prompt_merge(x, norm_mix, a, bo, qc, kvb, wg, wa, wb, wc, wo, route_params, cnt_in):
    b, s, d = x.shape
    tm = TM_MERGE
    nt = s // tm
    n = b * s
    tile = lambda w: pl.BlockSpec((1, tm, w), lambda i, j: (i, j, 0))
    const = lambda arr: pl.BlockSpec(arr.shape, lambda i, j: (0,) * arr.ndim)
    ids = pl.BlockSpec((TOP_K, tm), lambda i, j: (0, i * nt + j))
    return pl.pallas_call(
        _merge_kernel,
        grid=(b, nt),
        in_specs=[tile(d), const(norm_mix), tile(A_WIDTH), tile(B_GROUP_W), tile(C_WIDTH),
                  pl.BlockSpec((1,) + kvb.shape[1:], lambda i, j: (i, 0, 0)),
                  const(wg), const(wa), const(wb), const(wc), const(wo)]
        + [const(p) for p in route_params] + [const(cnt_in)],
        out_specs=(tile(d), ids, ids, pl.BlockSpec((tm, SUBLANES), lambda i, j: (i * nt + j, 0)),
                   pl.BlockSpec((N_EXPERTS, LANES), lambda i, j: (0, 0))),
        out_shape=(jax.ShapeDtypeStruct((b, s, d), F32), jax.ShapeDtypeStruct((TOP_K, n), I32),
                   jax.ShapeDtypeStruct((TOP_K, n), I32), jax.ShapeDtypeStruct((n, SUBLANES), F32),
                   jax.ShapeDtypeStruct((N_EXPERTS, LANES), I32)),
        scratch_shapes=[pltpu.VMEM((N_EXPERTS, LANES), F32)],
        compiler_params=_cparams("arbitrary", "arbitrary"),
        name="prompt_merge",
    )(x, norm_mix, a, bo, qc, kvb, wg, wa, wb, wc, wo, *route_params, cnt_in)


def _to_window_layout(w):
    b, _, window = w.shape
    return jnp.transpose(w.reshape(b, 2, 4, B_HEAD_DIM, window), (0, 4, 1, 2, 3))[None]


def _from_window_layout(c):
    b, window = c.shape[:2]
    return jnp.transpose(c, (0, 2, 3, 4, 1)).reshape(b, 2 * B_GROUP_W, window)


def _prompt_mixers(x_prompt, mem_prompt, band_bias, norm_mix, norm_mem, ln_v_g, ln_v_b, w_spatial, b_spatial,
                   w_mem_kv, wts, route_params, cnt_in):
    b, s, _ = x_prompt.shape
    wsp = jnp.where(np.tril(np.ones((CHUNK, CHUNK), bool)), w_spatial, 0.0).astype(BF16)
    qkv0, qkv1, qkv2, win0, win1, win2, a_out, qc = _prompt_proj(
        x_prompt, norm_mix[None], wts["mix"], ln_v_g[None], ln_v_b[None], wsp, b_spatial.T)
    b_out = _prompt_attention(qkv0, qkv1, qkv2, band_bias)
    mem_kv, mem_kv_b = _memory_kv(mem_prompt, norm_mem[None], w_mem_kv.astype(BF16))
    x_new, *routing = _prompt_merge(x_prompt, norm_mix[None], a_out, b_out, qc, mem_kv_b, wts["gate"],
                                    wts["a"], wts["b"], wts["c"], wts["out"], route_params, cnt_in)
    wins = [_to_window_layout(w) for w in (win0, win1, win2)]
    return x_new, routing, wins, mem_kv.reshape(1, b, N_MEM, 2, C_HEADS, C_HEAD_DIM)


def _route_kernel(x_ref, g_ref, wr_ref, br_ref, cnt_in_ref, e_ref, rank_ref, gate_ref, cnt_ref, carry_ref):
    @pl.when(pl.program_id(0) == 0)
    def _():
        carry_ref[...] = cnt_in_ref[...].astype(F32)

    _route_rows(x_ref[...], g_ref, wr_ref, br_ref, e_ref, rank_ref, gate_ref, cnt_ref, carry_ref)


def _route_rows(x, g_ref, wr_ref, br_ref, e_ref, rank_ref, gate_ref, cnt_ref, carry_ref):
    tm = x.shape[0]
    h = _rms(x, g_ref[...])
    logits = lax.dot_general(wr_ref[...], h.astype(BF16), (((1,), (1,)), ((), ())),
                             preferred_element_type=F32) + br_ref[...]
    expert = lax.broadcasted_iota(I32, logits.shape, 0)
    vals, idxs = [], []
    member = jnp.zeros(logits.shape, F32)
    for _ in range(TOP_K):
        m = jnp.max(logits, axis=0, keepdims=True)
        idx = jnp.min(jnp.where(logits == m, expert, N_EXPERTS), axis=0, keepdims=True)
        hit = expert == idx
        vals.append(m)
        idxs.append(idx)
        member = jnp.where(hit, 1.0, member)
        logits = jnp.where(hit, -jnp.inf, logits)
    p = [jnp.exp(v - vals[0]) for v in vals]
    den = p[0] + p[1] + p[2] + p[3]
    gates = jnp.concatenate([pk / den for pk in p] + [jnp.zeros((SUBLANES - TOP_K, tm), F32)], axis=0)
    gate_ref[...] = jnp.transpose(gates)
    e_ref[...] = jnp.concatenate(idxs, axis=0)
    before = (lax.broadcasted_iota(I32, (tm, tm), 0) < lax.broadcasted_iota(I32, (tm, tm), 1)).astype(BF16)
    prefix = jnp.dot(member.astype(BF16), before, preferred_element_type=F32) + carry_ref[:, 0:1]
    rank_ref[...] = jnp.concatenate(
        [jnp.sum(jnp.where(expert == idx, prefix, 0.0), axis=0, keepdims=True) for idx in idxs], axis=0).astype(I32)
    carry_ref[...] = carry_ref[...] + jnp.sum(member, axis=1, keepdims=True)
    cnt_ref[...] = carry_ref[...].astype(I32)


def _route_params(norm_ffn, w_router, b_router):
    return norm_ffn[None], w_router.T.astype(BF16), b_router[:, None]


def _route(x, route_params, cnt_in, tm):
    n, d = x.shape
    assert n % tm == 0
    norm_ffn, w_router_t, b_router = route_params
    const = lambda arr: pl.BlockSpec(arr.shape, lambda i: (0,) * arr.ndim)
    return pl.pallas_call(
        _route_kernel,
        grid=(n // tm,),
        in_specs=[pl.BlockSpec((tm, d), lambda i: (i, 0)), const(norm_ffn), const(w_router_t), const(b_router),
                  const(cnt_in)],
        out_specs=(pl.BlockSpec((TOP_K, tm), lambda i: (0, i)), pl.BlockSpec((TOP_K, tm), lambda i: (0, i)),
                   pl.BlockSpec((tm, SUBLANES), lambda i: (i, 0)), pl.BlockSpec((N_EXPERTS, LANES), lambda i: (0, 0))),
        out_shape=(jax.ShapeDtypeStruct((TOP_K, n), I32), jax.ShapeDtypeStruct((TOP_K, n), I32),
                   jax.ShapeDtypeStruct((n, SUBLANES), F32), jax.ShapeDtypeStruct((N_EXPERTS, LANES), I32)),
        scratch_shapes=[pltpu.VMEM((N_EXPERTS, LANES), F32)],
        compiler_params=_cparams("arbitrary"),
        name="moe_route",
    )(x, norm_ffn, w_router_t, b_router, cnt_in)


DMA_ROWS_PER_ITER = 8
N_DMA_THREADS = 2
N_PLAN_TAIL = 1 + N_EXPERTS


def _row_tile(r):
    return pl.ds(pl.multiple_of(r * ROW_TILES, ROW_TILES), ROW_TILES)


def _plan_kernel(n_blocks, cnt_ref, plan_ref):
    shift = BM_EXPERT.bit_length() - 1

    def per_expert(e, carry):
        blk0, last_e = carry
        nb = (cnt_ref[e] + (BM_EXPERT - 1)) >> shift
        plan_ref[n_blocks + 1 + e] = blk0 << shift

        def fill(j, cc):
            plan_ref[blk0 + j] = e
            return cc
        lax.fori_loop(0, nb, fill, 0)
        return blk0 + nb, jnp.where(nb > 0, e, last_e)
    n_used, last_e = lax.fori_loop(0, N_EXPERTS, per_expert, (0, 0))

    def tail(j, cc):
        plan_ref[j] = last_e
        return cc
    lax.fori_loop(n_used, n_blocks, tail, 0)
    plan_ref[n_blocks] = n_used


def _moe_plan(cnt, n_blocks):
    smem = pl.BlockSpec(memory_space=pltpu.SMEM)
    return pl.pallas_call(
        functools.partial(_plan_kernel, n_blocks),
        in_specs=[smem], out_specs=smem,
        out_shape=jax.ShapeDtypeStruct((n_blocks + N_PLAN_TAIL,), I32),
        name="moe_plan",
    )(cnt)


POSITION_TILES_PER_STEP = 16


def _positions_kernel(n_blocks, plan_ref, e_ref, r_ref, pos_ref):
    tiles, _, tm = pos_ref.shape
    e = e_ref[...]
    first = jnp.zeros(e.shape, I32)
    for ex in range(N_EXPERTS):
        first = jnp.where(e == ex, plan_ref[n_blocks + 1 + ex], first)
    pos = first + r_ref[...]
    for i in range(tiles):
        pos_ref[i] = pos[:, i * tm:(i + 1) * tm]


def _positions(plan, e, r, n_blocks, tm):
    n = e.shape[1]
    assert n % tm == 0
    nt = n // tm
    per_step = math.gcd(nt, POSITION_TILES_PER_STEP)
    pos = pl.pallas_call(
        functools.partial(_positions_kernel, n_blocks),
        grid=(nt // per_step,),
        in_specs=[pl.BlockSpec(memory_space=pltpu.SMEM), pl.BlockSpec((TOP_K, per_step * tm), lambda i: (0, i)),
                  pl.BlockSpec((TOP_K, per_step * tm), lambda i: (0, i))],
        out_specs=pl.BlockSpec((per_step, TOP_K, tm), lambda i: (i, 0, 0)),
        out_shape=jax.ShapeDtypeStruct((nt, TOP_K, tm), I32),
        compiler_params=_cparams("arbitrary"),
        name="moe_positions",
    )(plan, e, r)
    return pos.reshape(-1)


def _scatter_rows_kernel(n_blocks, cnt_ref, plan_ref, x_ref, pos_ref, xs_ref, poss_ref, g_ref, h_hbm,
                         hbuf, zbuf, sem, zsem):
    t = pl.program_id(0)
    n_tiles = pl.num_programs(0) - 1
    tm = x_ref.shape[0]
    n_s = xs_ref.shape[0]
    bm = BM_EXPERT
    shift = bm.bit_length() - 1
    slot = t % 2

    def zero_rows(first, count, wait):
        for bit in range(shift):
            n = 1 << bit
            lo = first + (count & (n - 1))

            @pl.when((count >> bit) & 1 == 1)
            def _():
                cp = pltpu.make_async_copy(zbuf.at[pl.ds(0, n * ROW_TILES), :], h_hbm.at[_row_tile_n(lo, n), :],
                                           zsem.at[0])
                cp.wait() if wait else cp.start()

    def zero_block(blk, wait):
        cp = pltpu.make_async_copy(zbuf, h_hbm.at[_row_tile_n(blk * bm, bm), :], zsem.at[0])
        cp.wait() if wait else cp.start()

    @pl.when(t == 0)
    def _():
        zbuf[...] = jnp.zeros(zbuf.shape, F32)
        for wait in (False, True):
            def pad_expert(e, cc, wait=wait):
                cnt = cnt_ref[e]
                zero_rows(plan_ref[n_blocks + 1 + e] + cnt, (-cnt) & (bm - 1), wait)
                return cc
            lax.fori_loop(0, N_EXPERTS, pad_expert, 0)

            def pad_block(blk, cc, wait=wait):
                zero_block(blk, wait)
                return cc
            lax.fori_loop(plan_ref[n_blocks], n_blocks, pad_block, 0)

    def wait_copies(sl, n_tok):
        for _ in range(TOP_K):
            pltpu.make_async_copy(hbuf.at[sl, pl.ds(0, n_tok * ROW_TILES), :],
                                  h_hbm.at[pl.ds(0, n_tok * ROW_TILES), :], sem.at[sl]).wait()

    def copy_rows(src_ref, rows_ref, n_tok):
        h = _rms(src_ref[...], g_ref[...])
        for c in range(ROW_TILES):
            hbuf[slot, pl.ds(c, n_tok, stride=ROW_TILES), :] = h[:, c * LANES:(c + 1) * LANES]

        def body(it, carry):
            for u in range(DMA_ROWS_PER_ITER):
                j = it * DMA_ROWS_PER_ITER + u
                for k in range(TOP_K):
                    pltpu.make_async_copy(hbuf.at[slot, _row_tile(j), :],
                                          h_hbm.at[_row_tile(rows_ref[k * n_tok + j]), :],
                                          sem.at[slot]).start(priority=k % N_DMA_THREADS)
            return carry
        lax.fori_loop(0, n_tok // DMA_ROWS_PER_ITER, body, 0)

    @pl.when(t >= 2)
    def _():
        wait_copies(slot, tm)

    @pl.when(t < n_tiles)
    def _():
        copy_rows(x_ref, pos_ref, tm)

    @pl.when(t == n_tiles)
    def _():
        copy_rows(xs_ref, poss_ref, n_s)
        wait_copies(slot, n_s)

        @pl.when(n_tiles >= 1)
        def _():
            wait_copies(1 - slot, tm)


def _row_tile_n(r, n):
    return pl.ds(pl.multiple_of(r * ROW_TILES, ROW_TILES), n * ROW_TILES)


def _scatter_rows(cnt, plan, x_p, pos_p, x_s, pos_s, norm_ffn, n_blocks, tm):
    n_p, d = x_p.shape
    n_s = x_s.shape[0]
    assert n_p % tm == 0 and tm % DMA_ROWS_PER_ITER == 0 and n_s % DMA_ROWS_PER_ITER == 0 and n_s <= tm
    nt = n_p // tm
    smem = lambda: pl.BlockSpec(memory_space=pltpu.SMEM)
    tile_idx = lambda i: jnp.minimum(i, nt - 1)
    return pl.pallas_call(
        functools.partial(_scatter_rows_kernel, n_blocks),
        grid=(nt + 1,),
        in_specs=[smem(), smem(), pl.BlockSpec((tm, d), lambda i: (tile_idx(i), 0)),
                  pl.BlockSpec((TOP_K * tm,), lambda i: (tile_idx(i),), memory_space=pltpu.SMEM),
                  pl.BlockSpec((n_s, d), lambda i: (0, 0)), smem(), pl.BlockSpec((1, d), lambda i: (0, 0))],
        out_specs=pl.BlockSpec(memory_space=pl.ANY),
        out_shape=jax.ShapeDtypeStruct((n_blocks * BM_EXPERT * ROW_TILES, LANES), F32),
        scratch_shapes=[pltpu.VMEM((2, tm * ROW_TILES, LANES), F32), pltpu.VMEM((BM_EXPERT * ROW_TILES, LANES), F32),
                        pltpu.SemaphoreType.DMA((2,)), pltpu.SemaphoreType.DMA((1,))],
        compiler_params=_cparams("arbitrary"),
        name="moe_scatter_rows",
    )(cnt, plan, x_p, pos_p, x_s, pos_s, norm_ffn)


WEIGHT_CAST_ROWS = 128


def _expert_blocks_kernel(n_blocks, plan_ref, x_ref, wgu_ref, bgu_ref, wd_ref, bd_ref, y_ref, wgu_b, wd_b):
    bm = BM_EXPERT
    i = pl.program_id(0)

    @pl.when((i == 0) | (plan_ref[i] != plan_ref[jnp.maximum(i - 1, 0)]))
    def _():
        for src, dst in ((wgu_ref, wgu_b), (wd_ref, wd_b)):
            for r0 in range(0, src.shape[1], WEIGHT_CAST_ROWS):
                rows = slice(r0, r0 + WEIGHT_CAST_ROWS)
                dst[rows, :] = src[0, rows, :].astype(BF16)

    @pl.when(i < plan_ref[n_blocks])
    def _():
        x = jnp.concatenate([x_ref[pl.ds(c, bm, stride=ROW_TILES), :] for c in range(ROW_TILES)], axis=1)
        gu = jnp.dot(x.astype(BF16), wgu_b[...], preferred_element_type=F32) + bgu_ref[0]
        gate = jnp.minimum(gu[:, :D_FF], SWIGLU_LIMIT)
        up = jnp.clip(gu[:, D_FF:], -SWIGLU_LIMIT, SWIGLU_LIMIT)
        act = gate * jax.nn.sigmoid(SWIGLU_ALPHA * gate) * (up + 1.0)
        y = jnp.dot(act.astype(BF16), wd_b[...], preferred_element_type=F32) + bd_ref[0]
        for c in range(ROW_TILES):
            y_ref[pl.ds(c, bm, stride=ROW_TILES), :] = y[:, c * LANES:(c + 1) * LANES]

    @pl.when(i >= plan_ref[n_blocks])
    def _():
        y_ref[...] = jnp.zeros(y_ref.shape, F32)


def _expert_blocks(plan, h_sorted, wgu, bgu, wd, bd, n_blocks):
    bm = BM_EXPERT
    by_expert = lambda *shape: pl.BlockSpec((1,) + shape, lambda i, plan: (plan[i],) + (0,) * len(shape))
    used = lambda i, plan: (jnp.minimum(i, plan[n_blocks] - 1), 0)
    grid_spec = pltpu.PrefetchScalarGridSpec(
        num_scalar_prefetch=1,
        grid=(n_blocks,),
        in_specs=[pl.BlockSpec((bm * ROW_TILES, LANES), used), by_expert(D_MODEL, 2 * D_FF), by_expert(1, 2 * D_FF),
                  by_expert(D_FF, D_MODEL), by_expert(1, D_MODEL)],
        out_specs=pl.BlockSpec((bm * ROW_TILES, LANES), lambda i, plan: (i, 0)),
        scratch_shapes=[pltpu.VMEM(wgu.shape[1:], BF16), pltpu.VMEM(wd.shape[1:], BF16)],
    )
    return pl.pallas_call(
        functools.partial(_expert_blocks_kernel, n_blocks),
        grid_spec=grid_spec,
        out_shape=jax.ShapeDtypeStruct(h_sorted.shape, F32),
        compiler_params=_cparams("arbitrary"),
        name="moe_experts",
    )(plan, h_sorted, wgu, bgu, wd, bd)


def _gather_combine_kernel(x_ref, pos_ref, pos_next_ref, gate_ref, g_ref, y_hbm, o_ref, ybuf, sem):
    t = pl.program_id(0)
    n_tiles = pl.num_programs(0)
    tm = x_ref.shape[0]
    slot = t % 2

    def gather(rows_ref, sl):
        def body(it, carry):
            for u in range(DMA_ROWS_PER_ITER):
                j = it * DMA_ROWS_PER_ITER + u
                for k in range(TOP_K):
                    pltpu.make_async_copy(y_hbm.at[_row_tile(rows_ref[k * tm + j]), :],
                                          ybuf.at[sl, _row_tile(k * tm + j), :],
                                          sem.at[sl]).start(priority=k % N_DMA_THREADS)
            return carry
        lax.fori_loop(0, tm // DMA_ROWS_PER_ITER, body, 0)

    @pl.when(t == 0)
    def _():
        gather(pos_ref, 0)

    @pl.when(t + 1 < n_tiles)
    def _():
        gather(pos_next_ref, 1 - slot)

    pltpu.make_async_copy(y_hbm.at[pl.ds(0, TOP_K * tm * ROW_TILES), :], ybuf.at[slot], sem.at[slot]).wait()
    gates = gate_ref[...]
    cols = []
    for c in range(ROW_TILES):
        acc = x_ref[:, c * LANES:(c + 1) * LANES]
        for k in range(TOP_K):
            acc = acc + gates[:, k:k + 1] * ybuf[slot, pl.ds(k * tm * ROW_TILES + c, tm, stride=ROW_TILES), :]
        cols.append(acc)
    o_ref[...] = _rms(jnp.concatenate(cols, axis=1), g_ref[...])


def _gather_combine(x, pos, gates, norm_final, y_sorted, tm):
    n, d = x.shape
    assert n % tm == 0 and tm % DMA_ROWS_PER_ITER == 0
    nt = n // tm
    rows = lambda off: pl.BlockSpec((TOP_K * tm,), lambda i: (jnp.minimum(i + off, nt - 1),),
                                    memory_space=pltpu.SMEM)
    return pl.pallas_call(
        _gather_combine_kernel,
        grid=(nt,),
        in_specs=[pl.BlockSpec((tm, d), lambda i: (i, 0)), rows(0), rows(1),
                  pl.BlockSpec((tm, SUBLANES), lambda i: (i, 0)), pl.BlockSpec((1, d), lambda i: (0, 0)),
                  pl.BlockSpec(memory_space=pl.ANY)],
        out_specs=pl.BlockSpec((tm, d), lambda i: (i, 0)),
        out_shape=jax.ShapeDtypeStruct((n, d), F32),
        scratch_shapes=[pltpu.VMEM((2, TOP_K * tm * ROW_TILES, LANES), F32), pltpu.SemaphoreType.DMA((2,))],
        compiler_params=_cparams("arbitrary"),
        name="moe_combine",
    )(x, pos, pos, gates, norm_final, y_sorted)


def _moe_and_final_norm(x_p, routing_p, x_s, routing_s, norm_ffn, w_gate_up, b_gate_up, w_down, b_down, norm_final):
    n_p, n_s = x_p.shape[0], x_s.shape[0]
    n_rows = (n_p + n_s) * TOP_K
    n_blocks = -(-(n_rows + N_EXPERTS * (BM_EXPERT - 1)) // BM_EXPERT)
    g_ffn = norm_ffn[None]
    e_s, r_s, gates_s, _ = routing_s
    e_p, r_p, gates_p, cnt = routing_p
    plan = _moe_plan(cnt[:, 0], n_blocks)
    pos_p = _positions(plan, e_p, r_p, n_blocks, TM_COMBINE)
    pos_s = _positions(plan, e_s, r_s, n_blocks, n_s)
    h_sorted = _scatter_rows(cnt[:, 0], plan, x_p, pos_p, x_s, pos_s, g_ffn, n_blocks, TM_COMBINE)
    y_sorted = _expert_blocks(plan, h_sorted, w_gate_up, b_gate_up[:, None, :], w_down, b_down[:, None, :], n_blocks)
    y_p = _gather_combine(x_p, pos_p, gates_p, norm_final[None], y_sorted, TM_COMBINE)
    y_s = _gather_combine(x_s, pos_s, gates_s, norm_final[None], y_sorted, n_s)
    return y_p, y_s


def _rb(x):
    return x.astype(BF16).astype(F32)


def _sample_proj_kernel(x_ref, g_ref, wm_ref, wg_ref, z_ref):
    hb = _rms(x_ref[...], g_ref[...]).astype(BF16)
    z_ref[:, :MIX_WIDTH] = jnp.dot(hb, wm_ref[...], preferred_element_type=F32)
    z_ref[:, MIX_WIDTH:] = jnp.dot(hb, wg_ref[...], preferred_element_type=F32)


def _sample_proj(x, norm_mix, w_mix, w_gate):
    n, d = x.shape
    args = (x, norm_mix, w_mix, w_gate)
    width = w_mix.shape[1] + w_gate.shape[1]
    return pl.pallas_call(
        _sample_proj_kernel,
        grid=(1,),
        in_specs=[pl.BlockSpec(a.shape, lambda j: (0, 0)) for a in args],
        out_specs=pl.BlockSpec((n, width), lambda j: (0, 0)),
        out_shape=jax.ShapeDtypeStruct((n, width), F32),
        compiler_params=_cparams("arbitrary"),
        name="sample_proj",
    )(*args)


def _sample_mix_kernel(z_ref, c0_ref, c1_ref, c2_ref, mem_ref, sb0_ref, sb1_ref, sb2_ref, nb_ref, lng_ref, lnb_ref,
                       wsp_ref, bsp_ref, bo_ref, co_ref, a_ref, v_ref, n0_ref, n1_ref, n2_ref):
    z = z_ref[0]
    scale = 1.0 / math.sqrt(B_HEAD_DIM)
    row_head = lax.broadcasted_iota(I32, (SUBLANES, B_GROUP_W), 0)
    lane_head = lax.broadcasted_iota(I32, (SUBLANES, B_GROUP_W), 1) // B_HEAD_DIM
    own = (row_head == lane_head).astype(F32)
    kv_cols = jnp.transpose(jnp.broadcast_to(z[:, B_WIDTH:3 * B_WIDTH], (SUBLANES, 2 * B_WIDTH)))[:, 0:1]
    outs, lses = [], []
    for g, (cache_ref, sb_ref, new_ref) in enumerate(((c0_ref, sb0_ref, n0_ref), (c1_ref, sb1_ref, n1_ref),
                                                      (c2_ref, sb2_ref, n2_ref))):
        window = cache_ref.shape[2]
        q, kn, vn = (_rb(z[:, i * B_WIDTH + g * B_GROUP_W:i * B_WIDTH + (g + 1) * B_GROUP_W]) for i in range(3))
        qh = own * q
        kt = cache_ref[0, :B_GROUP_W, :].astype(BF16)
        vt = cache_ref[0, B_GROUP_W:, :].astype(BF16)
        lg = jnp.dot(qh.astype(BF16), kt, preferred_element_type=F32) * scale + sb_ref[...]
        ln = jnp.sum(qh * kn, axis=1, keepdims=True) * scale + nb_ref[g, :, 0:1]
        m = jnp.maximum(jnp.max(lg, axis=1, keepdims=True), ln)
        lse = m + jnp.log(jnp.sum(jnp.exp(lg - m), axis=1, keepdims=True) + jnp.exp(ln - m))
        p = jnp.exp(lg - lse).astype(BF16)
        pn = _rb(jnp.exp(ln - lse))
        pv = lax.dot_general(p, vt, (((1,), (1,)), ((), ())), preferred_element_type=F32) + pn * vn
        outs.append(jnp.sum(own * pv, axis=0, keepdims=True))
        lses.append(jnp.sum(own * lse, axis=0, keepdims=True))
        new_col = jnp.concatenate([kv_cols[i * B_WIDTH + g * B_GROUP_W:i * B_WIDTH + (g + 1) * B_GROUP_W]
                                   for i in range(2)], axis=0)
        lane = lax.broadcasted_iota(I32, (1, window), 1)
        new_ref[0] = jnp.where(lane == window - 1, new_col, pltpu.roll(cache_ref[0], window - 1, axis=1))
    m = jnp.maximum(jnp.maximum(lses[0], lses[1]), lses[2])
    ws = [jnp.exp(l - m) for l in lses]
    den = ws[0] + ws[1] + ws[2]
    bo_ref[0] = _rb(ws[0] / den) * _rb(outs[0]) + _rb(ws[1] / den) * _rb(outs[1]) + _rb(ws[2] / den) * _rb(outs[2])

    cs = []
    for h in range(C_HEADS):
        c0 = MIX_WIDTH - C_WIDTH + h * C_HEAD_DIM
        qh = _rb(z[:, c0:c0 + C_HEAD_DIM])
        kh = _rb(mem_ref[0, pl.ds(h, N_MEM, stride=2 * C_HEADS), :])
        vh = _rb(mem_ref[0, pl.ds(C_HEADS + h, N_MEM, stride=2 * C_HEADS), :])
        s = jnp.sum(kh * qh, axis=-1, keepdims=True) * (1.0 / math.sqrt(C_HEAD_DIM))
        p = jnp.exp(s - jnp.max(s, axis=0, keepdims=True))
        p = _rb(p / jnp.sum(p, axis=0, keepdims=True))
        cs.append(jnp.sum(p * vh, axis=0, keepdims=True))
    co_ref[0] = jnp.concatenate(cs, axis=1)

    u = _gelu(z[:, 3 * B_WIDTH:3 * B_WIDTH + A_WIDTH])
    v = _layernorm(_gelu(z[:, 3 * B_WIDTH + A_WIDTH:3 * B_WIDTH + 2 * A_WIDTH]), lng_ref[...], lnb_ref[...])
    v_ref[0] = v
    a_ref[0] = u * (_rb(wsp_ref[...]) * _rb(v) + bsp_ref[...])


def _sample_mix(z, caches, mem, sbias, nbias, ln_g, ln_b, wsp0, bsp0):
    n = z.shape[0]
    const = lambda arr: pl.BlockSpec(arr.shape, lambda i: (0,) * arr.ndim)
    row = lambda w: pl.BlockSpec((1, 1, w), lambda i: (i, 0, 0))
    per_req = lambda arr: pl.BlockSpec((1,) + arr.shape[1:], lambda i: (i, 0, 0))
    widths = (B_GROUP_W, C_WIDTH, A_WIDTH, A_WIDTH)
    return pl.pallas_call(
        _sample_mix_kernel,
        grid=(n,),
        in_specs=[row(z.shape[2])] + [per_req(c) for c in caches] + [per_req(mem)]
        + [const(a) for a in (*sbias, nbias, ln_g, ln_b, wsp0, bsp0)],
        out_specs=tuple(row(w) for w in widths) + tuple(per_req(c) for c in caches),
        out_shape=tuple(jax.ShapeDtypeStruct((n, 1, w), F32) for w in widths)
        + tuple(jax.ShapeDtypeStruct(c.shape, F32) for c in caches),
        compiler_params=_cparams("arbitrary"),
        name="sample_mix",
    )(z, *caches, mem, *sbias, nbias, ln_g, ln_b, wsp0, bsp0)


def _sample_merge_kernel(x_ref, z_ref, a_ref, b_ref, c_ref, wa_ref, wb_ref, wc_ref, wo_ref, o_ref):
    dot = lambda a, b: jnp.dot(a.astype(BF16), b, preferred_element_type=F32)
    merged = None
    for i, (act_ref, w_ref) in enumerate(((a_ref, wa_ref), (b_ref, wb_ref), (c_ref, wc_ref))):
        gate = jax.nn.sigmoid(z_ref[:, MIX_WIDTH + i * D_MODEL:MIX_WIDTH + (i + 1) * D_MODEL])
        term = gate * dot(act_ref[...], w_ref[...])
        merged = term if merged is None else merged + term
    o_ref[...] = x_ref[...] + dot(merged, wo_ref[...])


def _sample_merge(x, z, a, bo, co, wa, wb, wc, wo):
    args = (x, z, a, bo, co, wa, wb, wc, wo)
    return pl.pallas_call(
        _sample_merge_kernel,
        grid=(1,),
        in_specs=[pl.BlockSpec(arr.shape, lambda i: (0, 0)) for arr in args],
        out_specs=pl.BlockSpec(x.shape, lambda i: (0, 0)),
        out_shape=jax.ShapeDtypeStruct(x.shape, F32),
        compiler_params=_cparams("arbitrary"),
        name="sample_merge",
    )(*args)


def _sample_mixers(x_sample, caches, cache_mem, sbias, nbias, norm_mix, ln_v_g, ln_v_b, w_spatial, b_spatial, wts):
    n = x_sample.shape[0]
    x = x_sample[:, 0]
    z = _sample_proj(x, norm_mix[None], wts["mix"], wts["gate"])
    mem = cache_mem.reshape(n, N_MEM * 2 * C_HEADS, C_HEAD_DIM)
    wsp0 = jnp.repeat(w_spatial[:, 0, 0], LANES)[None]
    bsp0 = jnp.repeat(b_spatial[:, 0], LANES)[None]
    bo, co, a, v, *new_caches = _sample_mix(z[:, None], [_from_window_layout(c) for c in caches], mem, sbias, nbias,
                                            ln_v_g[None], ln_v_b[None], wsp0, bsp0)
    x_new = _sample_merge(x, z, a[:, 0], bo[:, 0], co[:, 0], wts["a"], wts["b"], wts["c"], wts["out"])
    return x_new, [_to_window_layout(c) for c in new_caches], v[:, 0]


def _mixer_weights(w_in, w_branch_a, w_branch_b, w_branch_c, w_out):
    return {"mix": w_in[:, :MIX_WIDTH].astype(BF16), "gate": w_in[:, MIX_WIDTH:].astype(BF16),
            "a": w_branch_a.astype(BF16), "b": w_branch_b.astype(BF16), "c": w_branch_c.astype(BF16),
            "out": w_out.astype(BF16)}


def kernel(x_prompt, x_sample, cache_win0_kv, cache_win1_kv, cache_win2_kv, cache_mem_kv, mem_prompt, rel_bias,
           norm_mix, norm_mem, w_in, ln_v_g, ln_v_b, w_spatial, b_spatial, w_mem_kv, w_branch_a, w_branch_b,
           w_branch_c, w_out, norm_ffn, w_router, b_router, w_gate_up, b_gate_up, w_down, b_down, norm_final):
    assert norm_mix.shape[0] == 1, "one layer"
    b, s, d = x_prompt.shape
    n_s = x_sample.shape[0]
    caches = (cache_win0_kv[0], cache_win1_kv[0], cache_win2_kv[0])
    assert all(c.shape[1] == w for c, (w, _) in zip(caches, B_PAIRS)), "window buffers hold a full window"
    wts = _mixer_weights(w_in[0], w_branch_a[0], w_branch_b[0], w_branch_c[0], w_out[0])
    band_bias, *sbias, nbias = _bias_tables(rel_bias)
    route_params = _route_params(norm_ffn[0], w_router[0], b_router[0])
    xs, win_s, v_s = _sample_mixers(x_sample, caches, cache_mem_kv[0], sbias, nbias, norm_mix[0], ln_v_g[0],
                                    ln_v_b[0], w_spatial[0], b_spatial[0], wts)
    routing_s = _route(xs, route_params, jnp.zeros((N_EXPERTS, LANES), I32), n_s)
    xp, routing_p, win_p, mem_p = _prompt_mixers(x_prompt, mem_prompt, band_bias, norm_mix[0], norm_mem[0], ln_v_g[0],
                                                 ln_v_b[0], w_spatial[0], b_spatial[0], w_mem_kv[0], wts,
                                                 route_params, routing_s[3])
    y_p, y_s = _moe_and_final_norm(xp.reshape(b * s, d), routing_p, xs, routing_s, norm_ffn[0], w_gate_up[0],
                                   b_gate_up[0], w_down[0], b_down[0], norm_final)
    chunk_v = v_s.reshape(1, n_s, 1, A_GROUPS, LANES)
    return (y_p.reshape(b, s, d), y_s[:, None], win_p[0], win_p[1], win_p[2], mem_p,
            win_s[0], win_s[1], win_s[2], chunk_v)
```

```python
import functools
import math

import numpy as np
import jax
import jax.numpy as jnp
from jax import lax
from jax.experimental import pallas as pl
from jax.experimental.pallas import tpu as pltpu

F32 = jnp.float32
BF16 = jnp.bfloat16
I32 = jnp.int32

D_MODEL = 1024
N_MEM = 256
CHUNK = 128
A_GROUPS = 4
A_WIDTH = 512
B_PAIRS = ((128, 1), (512, 4), (2048, 16))
B_GROUP_W = 256
B_WIDTH = 768
B_HEAD_DIM = 64
BAND = 128
C_HEADS = 4
C_HEAD_DIM = 128
C_WIDTH = 512
REL_BUCKETS = 32
REL_MAX_DIST = 2048
N_EXPERTS = 32
TOP_K = 4
D_FF = 1024
SWIGLU_LIMIT = 7.0
SWIGLU_ALPHA = 1.702
EPS = 1e-6
NEG_INF = -1e30
MIX_WIDTH = 3 * B_WIDTH + 2 * A_WIDTH + C_WIDTH

LANES = 128
SUBLANES = 8
ROW_TILES = D_MODEL // LANES
VMEM_LIMIT = 56 * 1024 * 1024

TM_PROJ = 1024
TM_MERGE = 1024
BM_EXPERT = 512
TM_COMBINE = 1024
ATTN_BLOCKS_PER_ITER = 6


def _cparams(*sem):
    return pltpu.CompilerParams(dimension_semantics=sem, vmem_limit_bytes=VMEM_LIMIT)


def _t5_bucket(dist):
    dist = np.maximum(np.asarray(dist), 0)
    max_exact = REL_BUCKETS // 2
    log_ratio = np.log(np.maximum(dist, max_exact) / max_exact) / math.log(REL_MAX_DIST / max_exact)
    large = np.minimum(max_exact + (log_ratio * (REL_BUCKETS - max_exact)).astype(np.int32), REL_BUCKETS - 1)
    return np.where(dist < max_exact, dist, large).astype(np.int32)


def _gelu(x):
    c = math.sqrt(2.0 / math.pi)
    return x * (0.5 * (1.0 + jnp.tanh(c * (x + 0.044715 * (x * x * x)))))


def _rms(x, g):
    return x * lax.rsqrt(jnp.mean(x * x, axis=-1, keepdims=True) + EPS) * g


def _layernorm(x, g, b):
    xc = x - jnp.mean(x, axis=-1, keepdims=True)
    return xc * lax.rsqrt(jnp.mean(xc * xc, axis=-1, keepdims=True) + EPS) * g + b


def _proj_kernel(x_ref, g_ref, w_ref, lng_ref, lnb_ref, wsp_ref, bsp_ref,
                 qkv0_ref, qkv1_ref, qkv2_ref, win0_ref, win1_ref, win2_ref, a_ref, qc_ref, scr_ref):
    tm = x_ref.shape[1]
    hb = _rms(x_ref[0], g_ref[...]).astype(BF16)

    def proj(c0, width):
        return jnp.dot(hb, w_ref[:, c0:c0 + width], preferred_element_type=F32)

    for g, (_, dil) in enumerate(B_PAIRS):
        q, k, v = (proj(i * B_WIDTH + g * B_GROUP_W, B_GROUP_W) for i in range(3))
        kv = jnp.concatenate([k, v], axis=1)
        window = min(B_PAIRS[g][0], tm)
        (win0_ref, win1_ref, win2_ref)[g][0] = jnp.transpose(kv[tm - window:])
        if dil == 1:
            qkv0_ref[0] = jnp.concatenate([q, k, v], axis=1).astype(BF16)
            continue
        out_ref = qkv1_ref if g == 1 else qkv2_ref
        for i, arr in enumerate((q, k, v)):
            for ct in range(2):
                scr_ref[2 * i + ct] = arr[:, ct * LANES:(ct + 1) * LANES]
        for r in range(dil):
            rows = [scr_ref[j, pl.ds(r, tm // dil, stride=dil), :] for j in range(6)]
            out_ref[0, r] = jnp.concatenate(rows, axis=1).astype(BF16)

    u = _gelu(proj(3 * B_WIDTH, A_WIDTH))
    v = _layernorm(_gelu(proj(3 * B_WIDTH + A_WIDTH, A_WIDTH)), lng_ref[...], lnb_ref[...]).astype(BF16)
    for c in range(tm // CHUNK):
        rs = slice(c * CHUNK, (c + 1) * CHUNK)
        for g in range(A_GROUPS):
            cs = slice(g * LANES, (g + 1) * LANES)
            s = jnp.dot(wsp_ref[g], v[rs, cs], preferred_element_type=F32) + bsp_ref[:, g:g + 1]
            a_ref[0, rs, cs] = (u[rs, cs] * s).astype(BF16)

    qc_ref[0] = proj(3 * B_WIDTH + 2 * A_WIDTH, C_WIDTH).astype(BF16)


def _prompt_proj(x, norm_mix, w_mix, ln_g, ln_b, wsp, bsp):
    b, s, d = x.shape
    tm = TM_PROJ
    assert s % tm == 0 and s >= B_PAIRS[2][0] and tm >= B_PAIRS[1][0]
    nt = s // tm
    const = lambda *shape: pl.BlockSpec(shape, lambda i, j: (0,) * len(shape))
    out_shape = (
        jax.ShapeDtypeStruct((b, s, 3 * B_GROUP_W), BF16),
        jax.ShapeDtypeStruct((b, 4, s // 4, 3 * B_GROUP_W), BF16),
        jax.ShapeDtypeStruct((b, 16, s // 16, 3 * B_GROUP_W), BF16),
        jax.ShapeDtypeStruct((b, 2 * B_GROUP_W, B_PAIRS[0][0]), F32),
        jax.ShapeDtypeStruct((b, 2 * B_GROUP_W, B_PAIRS[1][0]), F32),
        jax.ShapeDtypeStruct((b, 2 * B_GROUP_W, s), F32),
        jax.ShapeDtypeStruct((b, s, A_WIDTH), BF16),
        jax.ShapeDtypeStruct((b, s, C_WIDTH), BF16),
    )
    out_specs = (
        pl.BlockSpec((1, tm, 3 * B_GROUP_W), lambda i, j: (i, j, 0)),
        pl.BlockSpec((1, 4, tm // 4, 3 * B_GROUP_W), lambda i, j: (i, 0, j, 0)),
        pl.BlockSpec((1, 16, tm // 16, 3 * B_GROUP_W), lambda i, j: (i, 0, j, 0)),
        pl.BlockSpec((1, 2 * B_GROUP_W, B_PAIRS[0][0]), lambda i, j: (i, 0, 0)),
        pl.BlockSpec((1, 2 * B_GROUP_W, B_PAIRS[1][0]), lambda i, j: (i, 0, 0)),
        pl.BlockSpec((1, 2 * B_GROUP_W, tm), lambda i, j: (i, 0, j)),
        pl.BlockSpec((1, tm, A_WIDTH), lambda i, j: (i, j, 0)),
        pl.BlockSpec((1, tm, C_WIDTH), lambda i, j: (i, j, 0)),
    )
    return pl.pallas_call(
        _proj_kernel,
        grid=(b, nt),
        in_specs=[pl.BlockSpec((1, tm, d), lambda i, j: (i, j, 0)), const(1, d), const(d, MIX_WIDTH),
                  const(1, A_WIDTH), const(1, A_WIDTH), const(A_GROUPS, CHUNK, CHUNK), const(CHUNK, A_GROUPS)],
        out_specs=out_specs,
        out_shape=out_shape,
        scratch_shapes=[pltpu.VMEM((6, tm, LANES), F32)],
        compiler_params=_cparams("arbitrary", "arbitrary"),
        name="prompt_proj",
    )(x, norm_mix, w_mix, ln_g, ln_b, wsp, bsp)


def _attend(q, k, v, bias_ref, g, k0, hmask):
    kn = k.shape[0]
    qs = jnp.concatenate([q * hm for hm in hmask], axis=0)
    s = lax.dot_general(qs, k, (((1,), (1,)), ((), ())), preferred_element_type=F32) + bias_ref[g, :, k0:k0 + kn]
    m = jnp.max(s, axis=-1, keepdims=True)
    p = jnp.exp(s - m)
    l = jnp.sum(p, axis=-1, keepdims=True)
    pv = jnp.dot(p.astype(BF16), v, preferred_element_type=F32) / l
    lse_rows = m + jnp.log(l)
    out = jnp.zeros((BAND, B_GROUP_W), F32)
    lse = jnp.zeros((BAND, B_GROUP_W), F32)
    for h, hm in enumerate(hmask):
        rows = slice(h * BAND, (h + 1) * BAND)
        sel = hm > 0
        out = jnp.where(sel, pv[rows], out)
        lse = jnp.where(sel, lse_rows[rows], lse)
    return out, lse


def _attn_kernel(q0_ref, q1_ref, q2_ref, bias_ref, o_ref, out_ref, lse_ref):
    s = q0_ref.shape[1]
    lane_head = lax.broadcasted_iota(I32, (1, B_GROUP_W), 1) // B_HEAD_DIM
    scale = 1.0 / math.sqrt(B_HEAD_DIM)
    hmask = [jnp.where(lane_head == h, scale, 0.0).astype(BF16) for h in range(4)]
    qs, ks, vs = (slice(i * B_GROUP_W, (i + 1) * B_GROUP_W) for i in range(3))

    def store(g, start, dil, out, lse):
        rows = pl.ds(start, BAND) if dil == 1 else pl.ds(start, BAND, stride=dil)
        for ct in range(2):
            out_ref[g, ct, rows, :] = out[:, ct * LANES:(ct + 1) * LANES]
            lse_ref[g, ct, rows, :] = lse[:, ct * LANES:(ct + 1) * LANES]

    for g, (_, dil) in enumerate(B_PAIRS):
        n = s // dil
        nb = n // BAND

        def load(rows, cols, r, g=g):
            if g == 0:
                return q0_ref[0, rows, cols]
            return (q1_ref if g == 1 else q2_ref)[0, r, rows, cols]

        def first_block(r, g=g, dil=dil, load=load):
            rows = pl.ds(0, BAND)
            out, lse = _attend(load(rows, qs, r), load(rows, ks, r), load(rows, vs, r), bias_ref, g, BAND, hmask)
            store(g, r, dil, out, lse)

        def later_block(i, g=g, dil=dil, nb=nb, load=load):
            r = i // (nb - 1)
            qb = i % (nb - 1) + 1
            q0 = pl.multiple_of(qb * BAND, BAND)
            rows_q = pl.ds(q0, BAND)
            rows_k = pl.ds(q0 - BAND, 2 * BAND)
            out, lse = _attend(load(rows_q, qs, r), load(rows_k, ks, r), load(rows_k, vs, r), bias_ref, g, 0, hmask)
            store(g, qb * (BAND * dil) + r, dil, out, lse)

        def run(block_fn, count):
            per = ATTN_BLOCKS_PER_ITER

            def body(it, carry):
                for u in range(per):
                    block_fn(it * per + u)
                return carry
            lax.fori_loop(0, count // per, body, 0)
            for i in range(count - count % per, count):
                block_fn(i)

        run(first_block, dil)
        run(later_block, dil * (nb - 1))

    rc = 256
    def combine(c, carry):
        rows = pl.ds(pl.multiple_of(c * rc, rc), rc)
        for ct in range(2):
            ls = [lse_ref[g, ct, rows, :] for g in range(3)]
            m = jnp.maximum(jnp.maximum(ls[0], ls[1]), ls[2])
            ws = [jnp.exp(l - m) for l in ls]
            den = ws[0] + ws[1] + ws[2]
            num = ws[0] * out_ref[0, ct, rows, :] + ws[1] * out_ref[1, ct, rows, :] + ws[2] * out_ref[2, ct, rows, :]
            o_ref[0, rows, ct * LANES:(ct + 1) * LANES] = (num / den).astype(BF16)
        return carry
    lax.fori_loop(0, s // rc, combine, 0)


def _prompt_attention(qkv0, qkv1, qkv2, bias):
    b, s, w = qkv0.shape
    return pl.pallas_call(
        _attn_kernel,
        grid=(b,),
        in_specs=[pl.BlockSpec((1, s, w), lambda i: (i, 0, 0)),
                  pl.BlockSpec((1, 4, s // 4, w), lambda i: (i, 0, 0, 0)),
                  pl.BlockSpec((1, 16, s // 16, w), lambda i: (i, 0, 0, 0)),
                  pl.BlockSpec(bias.shape, lambda i: (0, 0, 0))],
        out_specs=pl.BlockSpec((1, s, B_GROUP_W), lambda i: (i, 0, 0)),
        out_shape=jax.ShapeDtypeStruct((b, s, B_GROUP_W), BF16),
        scratch_shapes=[pltpu.VMEM((3, 2, s, LANES), F32), pltpu.VMEM((3, 2, s, LANES), F32)],
        compiler_params=_cparams("arbitrary"),
        name="prompt_attention",
    )(qkv0, qkv1, qkv2, bias)


def _bias_kernel(rel_ref, band_idx_ref, s0_ref, s1_ref, s2_ref, band_ref, sb0_ref, sb1_ref, sb2_ref, nb_ref):
    def lookup(idx, col):
        acc = jnp.full(idx.shape, NEG_INF, F32)
        for bucket in range(REL_BUCKETS):
            acc = jnp.where(idx == bucket, rel_ref[bucket, col], acc)
        return acc

    for g, (s_ref, sb_ref) in enumerate(((s0_ref, sb0_ref), (s1_ref, sb1_ref), (s2_ref, sb2_ref))):
        width = s_ref.shape[1]
        for h in range(4):
            band_ref[g, h * BAND:(h + 1) * BAND, :] = lookup(band_idx_ref[g], 4 * g + h)
        sb_ref[...] = jnp.concatenate([lookup(s_ref[...], 4 * g + h) for h in range(4)]
                                      + [jnp.full((SUBLANES - 4, width), NEG_INF, F32)], axis=0)
        nb_ref[g] = jnp.concatenate([jnp.full((1, LANES), rel_ref[0, 4 * g + h], F32) for h in range(4)]
                                    + [jnp.zeros((SUBLANES - 4, LANES), F32)], axis=0)


def _bias_tables(rel_bias):
    steps = np.arange(BAND)[:, None] + BAND - np.arange(2 * BAND)[None, :]
    valid = (steps >= 0) & (steps <= BAND)
    band_idx = np.stack([np.where(valid, _t5_bucket(np.clip(steps, 0, BAND) * dil), -1) for _, dil in B_PAIRS])
    s_idx = []
    for window, dil in B_PAIRS:
        w = np.arange(window)
        s_idx.append(np.where(w % dil == 0, _t5_bucket(window - w), -1)[None].astype(np.int32))
    args = (rel_bias, jnp.asarray(band_idx.astype(np.int32))) + tuple(jnp.asarray(s) for s in s_idx)
    vmem = lambda a: pl.BlockSpec(a.shape, lambda i: (0,) * a.ndim)
    out_shape = (jax.ShapeDtypeStruct((3, 4 * BAND, 2 * BAND), F32),) + tuple(
        jax.ShapeDtypeStruct((SUBLANES, window), F32) for window, _ in B_PAIRS) + (
        jax.ShapeDtypeStruct((3, SUBLANES, LANES), F32),)
    return pl.pallas_call(
        _bias_kernel,
        grid=(1,),
        in_specs=[pl.BlockSpec(memory_space=pltpu.SMEM)] + [vmem(a) for a in args[1:]],
        out_specs=tuple(pl.BlockSpec(s.shape, lambda i, n=len(s.shape): (0,) * n) for s in out_shape),
        out_shape=out_shape,
        compiler_params=_cparams("arbitrary"),
        name="bias_tables",
    )(*args)


def _memkv_kernel(mem_ref, g_ref, w_ref, kv_ref, kvb_ref):
    hb = _rms(mem_ref[0], g_ref[...]).astype(BF16)
    kv = jnp.dot(hb, w_ref[...], preferred_element_type=F32)
    m = kv.shape[0]
    n_heads = kv.shape[1] // C_HEAD_DIM
    for j in range(n_heads):
        kv_ref[0, pl.ds(j, m, stride=n_heads), :] = kv[:, j * C_HEAD_DIM:(j + 1) * C_HEAD_DIM]
    kvb_ref[0] = kv.astype(BF16)


def _memory_kv(mem, norm_mem, w_mem):
    b, m, d = mem.shape
    w = w_mem.shape[1]
    rows = m * w // C_HEAD_DIM
    return pl.pallas_call(
        _memkv_kernel,
        grid=(b,),
        in_specs=[pl.BlockSpec((1, m, d), lambda i: (i, 0, 0)), pl.BlockSpec((1, d), lambda i: (0, 0)),
                  pl.BlockSpec((d, w), lambda i: (0, 0))],
        out_specs=(pl.BlockSpec((1, rows, C_HEAD_DIM), lambda i: (i, 0, 0)),
                   pl.BlockSpec((1, m, w), lambda i: (i, 0, 0))),
        out_shape=(jax.ShapeDtypeStruct((b, rows, C_HEAD_DIM), F32), jax.ShapeDtypeStruct((b, m, w), BF16)),
        compiler_params=_cparams("arbitrary"),
        name="memory_kv",
    )(mem, norm_mem, w_mem)


def _merge_kernel(x_ref, g_ref, a_ref, b_ref, qc_ref, kv_ref, wg_ref, wa_ref, wb_ref, wc_ref, wo_ref,
                  gf_ref, wr_ref, br_ref, cnt_in_ref, o_ref, e_ref, rank_ref, gate_ref, cnt_ref, carry_ref):
    @pl.when((pl.program_id(0) == 0) & (pl.program_id(1) == 0))
    def _():
        carry_ref[...] = cnt_in_ref[...].astype(F32)

    x = x_ref[0]
    hb = _rms(x, g_ref[...]).astype(BF16)
    qc = qc_ref[0]
    cs = []
    for h in range(C_HEADS):
        hs = slice(h * C_HEAD_DIM, (h + 1) * C_HEAD_DIM)
        s = lax.dot_general(qc[:, hs], kv_ref[0, :, hs], (((1,), (1,)), ((), ())), preferred_element_type=F32)
        s = s * (1.0 / math.sqrt(C_HEAD_DIM))
        p = jnp.exp(s - jnp.max(s, axis=-1, keepdims=True))
        p = p / jnp.sum(p, axis=-1, keepdims=True)
        vs = slice(C_WIDTH + h * C_HEAD_DIM, C_WIDTH + (h + 1) * C_HEAD_DIM)
        cs.append(jnp.dot(p.astype(BF16), kv_ref[0, :, vs], preferred_element_type=F32))
    c = jnp.concatenate(cs, axis=1).astype(BF16)
    branches = ((a_ref[0], wa_ref), (b_ref[0], wb_ref), (c, wc_ref))
    merged = None
    for i, (act, w_ref) in enumerate(branches):
        gate = jax.nn.sigmoid(jnp.dot(hb, wg_ref[:, i * D_MODEL:(i + 1) * D_MODEL], preferred_element_type=F32))
        term = gate * jnp.dot(act, w_ref[...], preferred_element_type=F32)
        merged = term if merged is None else merged + term
    x_new = x + jnp.dot(merged.astype(BF16), wo_ref[...], preferred_element_type=F32)
    o_ref[0] = x_new
    _route_rows(x_new, gf_ref, wr_ref, br_ref, e_ref, rank_ref, gate_ref, cnt_ref, carry_ref)


def _prompt_merge(x, norm_mix, a, bo, qc, kvb, wg, wa, wb, wc, wo, route_params, cnt_in):
    b, s, d = x.shape
    tm = TM_MERGE
    nt = s // tm
    n = b * s
    tile = lambda w: pl.BlockSpec((1, tm, w), lambda i, j: (i, j, 0))
    const = lambda arr: pl.BlockSpec(arr.shape, lambda i, j: (0,) * arr.ndim)
    ids = pl.BlockSpec((TOP_K, tm), lambda i, j: (0, i * nt + j))
    return pl.pallas_call(
        _merge_kernel,
        grid=(b, nt),
        in_specs=[tile(d), const(norm_mix), tile(A_WIDTH), tile(B_GROUP_W), tile(C_WIDTH),
                  pl.BlockSpec((1,) + kvb.shape[1:], lambda i, j: (i, 0, 0)),
                  const(wg), const(wa), const(wb), const(wc), const(wo)]
        + [const(p) for p in route_params] + [const(cnt_in)],
        out_specs=(tile(d), ids, ids, pl.BlockSpec((tm, SUBLANES), lambda i, j: (i * nt + j, 0)),
                   pl.BlockSpec((N_EXPERTS, LANES), lambda i, j: (0, 0))),
        out_shape=(jax.ShapeDtypeStruct((b, s, d), F32), jax.ShapeDtypeStruct((TOP_K, n), I32),
                   jax.ShapeDtypeStruct((TOP_K, n), I32), jax.ShapeDtypeStruct((n, SUBLANES), F32),
                   jax.ShapeDtypeStruct((N_EXPERTS, LANES), I32)),
        scratch_shapes=[pltpu.VMEM((N_EXPERTS, LANES), F32)],
        compiler_params=_cparams("arbitrary", "arbitrary"),
        name="prompt_merge",
    )(x, norm_mix, a, bo, qc, kvb, wg, wa, wb, wc, wo, *route_params, cnt_in)


def _to_window_layout(w):
    b, _, window = w.shape
    return jnp.transpose(w.reshape(b, 2, 4, B_HEAD_DIM, window), (0, 4, 1, 2, 3))[None]


def _from_window_layout(c):
    b, window = c.shape[:2]
    return jnp.transpose(c, (0, 2, 3, 4, 1)).reshape(b, 2 * B_GROUP_W, window)


def _prompt_mixers(x_prompt, mem_prompt, band_bias, norm_mix, norm_mem, ln_v_g, ln_v_b, w_spatial, b_spatial,
                   w_mem_kv, wts, route_params, cnt_in):
    b, s, _ = x_prompt.shape
    wsp = jnp.where(np.tril(np.ones((CHUNK, CHUNK), bool)), w_spatial, 0.0).astype(BF16)
    qkv0, qkv1, qkv2, win0, win1, win2, a_out, qc = _prompt_proj(
        x_prompt, norm_mix[None], wts["mix"], ln_v_g[None], ln_v_b[None], wsp, b_spatial.T)
    b_out = _prompt_attention(qkv0, qkv1, qkv2, band_bias)
    mem_kv, mem_kv_b = _memory_kv(mem_prompt, norm_mem[None], w_mem_kv.astype(BF16))
    x_new, *routing = _prompt_merge(x_prompt, norm_mix[None], a_out, b_out, qc, mem_kv_b, wts["gate"],
                                    wts["a"], wts["b"], wts["c"], wts["out"], route_params, cnt_in)
    wins = [_to_window_layout(w) for w in (win0, win1, win2)]
    return x_new, routing, wins, mem_kv.reshape(1, b, N_MEM, 2, C_HEADS, C_HEAD_DIM)


def _route_kernel(x_ref, g_ref, wr_ref, br_ref, cnt_in_ref, e_ref, rank_ref, gate_ref, cnt_ref, carry_ref):
    @pl.when(pl.program_id(0) == 0)
    def _():
        carry_ref[...] = cnt_in_ref[...].astype(F32)

    _route_rows(x_ref[...], g_ref, wr_ref, br_ref, e_ref, rank_ref, gate_ref, cnt_ref, carry_ref)


def _route_rows(x, g_ref, wr_ref, br_ref, e_ref, rank_ref, gate_ref, cnt_ref, carry_ref):
    tm = x.shape[0]
    h = _rms(x, g_ref[...])
    logits = lax.dot_general(wr_ref[...], h.astype(BF16), (((1,), (1,)), ((), ())),
                             preferred_element_type=F32) + br_ref[...]
    expert = lax.broadcasted_iota(I32, logits.shape, 0)
    vals, idxs = [], []
    member = jnp.zeros(logits.shape, F32)
    for _ in range(TOP_K):
        m = jnp.max(logits, axis=0, keepdims=True)
        idx = jnp.min(jnp.where(logits == m, expert, N_EXPERTS), axis=0, keepdims=True)
        hit = expert == idx
        vals.append(m)
        idxs.append(idx)
        member = jnp.where(hit, 1.0, member)
        logits = jnp.where(hit, -jnp.inf, logits)
    p = [jnp.exp(v - vals[0]) for v in vals]
    den = p[0] + p[1] + p[2] + p[3]
    gates = jnp.concatenate([pk / den for pk in p] + [jnp.zeros((SUBLANES - TOP_K, tm), F32)], axis=0)
    gate_ref[...] = jnp.transpose(gates)
    e_ref[...] = jnp.concatenate(idxs, axis=0)
    before = (lax.broadcasted_iota(I32, (tm, tm), 0) < lax.broadcasted_iota(I32, (tm, tm), 1)).astype(BF16)
    prefix = jnp.dot(member.astype(BF16), before, preferred_element_type=F32) + carry_ref[:, 0:1]
    rank_ref[...] = jnp.concatenate(
        [jnp.sum(jnp.where(expert == idx, prefix, 0.0), axis=0, keepdims=True) for idx in idxs], axis=0).astype(I32)
    carry_ref[...] = carry_ref[...] + jnp.sum(member, axis=1, keepdims=True)
    cnt_ref[...] = carry_ref[...].astype(I32)


def _route_params(norm_ffn, w_router, b_router):
    return norm_ffn[None], w_router.T.astype(BF16), b_router[:, None]


def _route(x, route_params, cnt_in, tm):
    n, d = x.shape
    assert n % tm == 0
    norm_ffn, w_router_t, b_router = route_params
    const = lambda arr: pl.BlockSpec(arr.shape, lambda i: (0,) * arr.ndim)
    return pl.pallas_call(
        _route_kernel,
        grid=(n // tm,),
        in_specs=[pl.BlockSpec((tm, d), lambda i: (i, 0)), const(norm_ffn), const(w_router_t), const(b_router),
                  const(cnt_in)],
        out_specs=(pl.BlockSpec((TOP_K, tm), lambda i: (0, i)), pl.BlockSpec((TOP_K, tm), lambda i: (0, i)),
                   pl.BlockSpec((tm, SUBLANES), lambda i: (i, 0)), pl.BlockSpec((N_EXPERTS, LANES), lambda i: (0, 0))),
        out_shape=(jax.ShapeDtypeStruct((TOP_K, n), I32), jax.ShapeDtypeStruct((TOP_K, n), I32),
                   jax.ShapeDtypeStruct((n, SUBLANES), F32), jax.ShapeDtypeStruct((N_EXPERTS, LANES), I32)),
        scratch_shapes=[pltpu.VMEM((N_EXPERTS, LANES), F32)],
        compiler_params=_cparams("arbitrary"),
        name="moe_route",
    )(x, norm_ffn, w_router_t, b_router, cnt_in)


DMA_ROWS_PER_ITER = 8
N_DMA_THREADS = 2
N_PLAN_TAIL = 1 + N_EXPERTS


def _row_tile(r):
    return pl.ds(pl.multiple_of(r * ROW_TILES, ROW_TILES), ROW_TILES)


def _plan_kernel(n_blocks, cnt_ref, plan_ref):
    shift = BM_EXPERT.bit_length() - 1

    def per_expert(e, carry):
        blk0, last_e = carry
        nb = (cnt_ref[e] + (BM_EXPERT - 1)) >> shift
        plan_ref[n_blocks + 1 + e] = blk0 << shift

        def fill(j, cc):
            plan_ref[blk0 + j] = e
            return cc
        lax.fori_loop(0, nb, fill, 0)
        return blk0 + nb, jnp.where(nb > 0, e, last_e)
    n_used, last_e = lax.fori_loop(0, N_EXPERTS, per_expert, (0, 0))

    def tail(j, cc):
        plan_ref[j] = last_e
        return cc
    lax.fori_loop(n_used, n_blocks, tail, 0)
    plan_ref[n_blocks] = n_used


def _moe_plan(cnt, n_blocks):
    smem = pl.BlockSpec(memory_space=pltpu.SMEM)
    return pl.pallas_call(
        functools.partial(_plan_kernel, n_blocks),
        in_specs=[smem], out_specs=smem,
        out_shape=jax.ShapeDtypeStruct((n_blocks + N_PLAN_TAIL,), I32),
        name="moe_plan",
    )(cnt)


POSITION_TILES_PER_STEP = 16


def _positions_kernel(n_blocks, plan_ref, e_ref, r_ref, pos_ref):
    tiles, _, tm = pos_ref.shape
    e = e_ref[...]
    first = jnp.zeros(e.shape, I32)
    for ex in range(N_EXPERTS):
        first = jnp.where(e == ex, plan_ref[n_blocks + 1 + ex], first)
    pos = first + r_ref[...]
    for i in range(tiles):
        pos_ref[i] = pos[:, i * tm:(i + 1) * tm]


def _positions(plan, e, r, n_blocks, tm):
    n = e.shape[1]
    assert n % tm == 0
    nt = n // tm
    per_step = math.gcd(nt, POSITION_TILES_PER_STEP)
    pos = pl.pallas_call(
        functools.partial(_positions_kernel, n_blocks),
        grid=(nt // per_step,),
        in_specs=[pl.BlockSpec(memory_space=pltpu.SMEM), pl.BlockSpec((TOP_K, per_step * tm), lambda i: (0, i)),
                  pl.BlockSpec((TOP_K, per_step * tm), lambda i: (0, i))],
        out_specs=pl.BlockSpec((per_step, TOP_K, tm), lambda i: (i, 0, 0)),
        out_shape=jax.ShapeDtypeStruct((nt, TOP_K, tm), I32),
        compiler_params=_cparams("arbitrary"),
        name="moe_positions",
    )(plan, e, r)
    return pos.reshape(-1)


def _scatter_rows_kernel(n_blocks, cnt_ref, plan_ref, x_ref, pos_ref, xs_ref, poss_ref, g_ref, h_hbm,
                         hbuf, zbuf, sem, zsem):
    t = pl.program_id(0)
    n_tiles = pl.num_programs(0) - 1
    tm = x_ref.shape[0]
    n_s = xs_ref.shape[0]
    bm = BM_EXPERT
    shift = bm.bit_length() - 1
    slot = t % 2

    def zero_rows(first, count, wait):
        for bit in range(shift):
            n = 1 << bit
            lo = first + (count & (n - 1))

            @pl.when((count >> bit) & 1 == 1)
            def _():
                cp = pltpu.make_async_copy(zbuf.at[pl.ds(0, n * ROW_TILES), :], h_hbm.at[_row_tile_n(lo, n), :],
                                           zsem.at[0])
                cp.wait() if wait else cp.start()

    def zero_block(blk, wait):
        cp = pltpu.make_async_copy(zbuf, h_hbm.at[_row_tile_n(blk * bm, bm), :], zsem.at[0])
        cp.wait() if wait else cp.start()

    @pl.when(t == 0)
    def _():
        zbuf[...] = jnp.zeros(zbuf.shape, F32)
        for wait in (False, True):
            def pad_expert(e, cc, wait=wait):
                cnt = cnt_ref[e]
                zero_rows(plan_ref[n_blocks + 1 + e] + cnt, (-cnt) & (bm - 1), wait)
                return cc
            lax.fori_loop(0, N_EXPERTS, pad_expert, 0)

            def pad_block(blk, cc, wait=wait):
                zero_block(blk, wait)
                return cc
            lax.fori_loop(plan_ref[n_blocks], n_blocks, pad_block, 0)

    def wait_copies(sl, n_tok):
        for _ in range(TOP_K):
            pltpu.make_async_copy(hbuf.at[sl, pl.ds(0, n_tok * ROW_TILES), :],
                                  h_hbm.at[pl.ds(0, n_tok * ROW_TILES), :], sem.at[sl]).wait()

    def copy_rows(src_ref, rows_ref, n_tok):
        h = _rms(src_ref[...], g_ref[...])
        for c in range(ROW_TILES):
            hbuf[slot, pl.ds(c, n_tok, stride=ROW_TILES), :] = h[:, c * LANES:(c + 1) * LANES]

        def body(it, carry):
            for u in range(DMA_ROWS_PER_ITER):
                j = it * DMA_ROWS_PER_ITER + u
                for k in range(TOP_K):
                    pltpu.make_async_copy(hbuf.at[slot, _row_tile(j), :],
                                          h_hbm.at[_row_tile(rows_ref[k * n_tok + j]), :],
                                          sem.at[slot]).start(priority=k % N_DMA_THREADS)
            return carry
        lax.fori_loop(0, n_tok // DMA_ROWS_PER_ITER, body, 0)

    @pl.when(t >= 2)
    def _():
        wait_copies(slot, tm)

    @pl.when(t < n_tiles)
    def _():
        copy_rows(x_ref, pos_ref, tm)

    @pl.when(t == n_tiles)
    def _():
        copy_rows(xs_ref, poss_ref, n_s)
        wait_copies(slot, n_s)

        @pl.when(n_tiles >= 1)
        def _():
            wait_copies(1 - slot, tm)


def _row_tile_n(r, n):
    return pl.ds(pl.multiple_of(r * ROW_TILES, ROW_TILES), n * ROW_TILES)


def _scatter_rows(cnt, plan, x_p, pos_p, x_s, pos_s, norm_ffn, n_blocks, tm):
    n_p, d = x_p.shape
    n_s = x_s.shape[0]
    assert n_p % tm == 0 and tm % DMA_ROWS_PER_ITER == 0 and n_s % DMA_ROWS_PER_ITER == 0 and n_s <= tm
    nt = n_p // tm
    smem = lambda: pl.BlockSpec(memory_space=pltpu.SMEM)
    tile_idx = lambda i: jnp.minimum(i, nt - 1)
    return pl.pallas_call(
        functools.partial(_scatter_rows_kernel, n_blocks),
        grid=(nt + 1,),
        in_specs=[smem(), smem(), pl.BlockSpec((tm, d), lambda i: (tile_idx(i), 0)),
                  pl.BlockSpec((TOP_K * tm,), lambda i: (tile_idx(i),), memory_space=pltpu.SMEM),
                  pl.BlockSpec((n_s, d), lambda i: (0, 0)), smem(), pl.BlockSpec((1, d), lambda i: (0, 0))],
        out_specs=pl.BlockSpec(memory_space=pl.ANY),
        out_shape=jax.ShapeDtypeStruct((n_blocks * BM_EXPERT * ROW_TILES, LANES), F32),
        scratch_shapes=[pltpu.VMEM((2, tm * ROW_TILES, LANES), F32), pltpu.VMEM((BM_EXPERT * ROW_TILES, LANES), F32),
                        pltpu.SemaphoreType.DMA((2,)), pltpu.SemaphoreType.DMA((1,))],
        compiler_params=_cparams("arbitrary"),
        name="moe_scatter_rows",
    )(cnt, plan, x_p, pos_p, x_s, pos_s, norm_ffn)


WEIGHT_CAST_ROWS = 128


def _expert_blocks_kernel(n_blocks, plan_ref, x_ref, wgu_ref, bgu_ref, wd_ref, bd_ref, y_ref, wgu_b, wd_b):
    bm = BM_EXPERT
    i = pl.program_id(0)

    @pl.when((i == 0) | (plan_ref[i] != plan_ref[jnp.maximum(i - 1, 0)]))
    def _():
        for src, dst in ((wgu_ref, wgu_b), (wd_ref, wd_b)):
            for r0 in range(0, src.shape[1], WEIGHT_CAST_ROWS):
                rows = slice(r0, r0 + WEIGHT_CAST_ROWS)
                dst[rows, :] = src[0, rows, :].astype(BF16)

    @pl.when(i < plan_ref[n_blocks])
    def _():
        x = jnp.concatenate([x_ref[pl.ds(c, bm, stride=ROW_TILES), :] for c in range(ROW_TILES)], axis=1)
        gu = jnp.dot(x.astype(BF16), wgu_b[...], preferred_element_type=F32) + bgu_ref[0]
        gate = jnp.minimum(gu[:, :D_FF], SWIGLU_LIMIT)
        up = jnp.clip(gu[:, D_FF:], -SWIGLU_LIMIT, SWIGLU_LIMIT)
        act = gate * jax.nn.sigmoid(SWIGLU_ALPHA * gate) * (up + 1.0)
        y = jnp.dot(act.astype(BF16), wd_b[...], preferred_element_type=F32) + bd_ref[0]
        for c in range(ROW_TILES):
            y_ref[pl.ds(c, bm, stride=ROW_TILES), :] = y[:, c * LANES:(c + 1) * LANES]

    @pl.when(i >= plan_ref[n_blocks])
    def _():
        y_ref[...] = jnp.zeros(y_ref.shape, F32)


def _expert_blocks(plan, h_sorted, wgu, bgu, wd, bd, n_blocks):
    bm = BM_EXPERT
    by_expert = lambda *shape: pl.BlockSpec((1,) + shape, lambda i, plan: (plan[i],) + (0,) * len(shape))
    used = lambda i, plan: (jnp.minimum(i, plan[n_blocks] - 1), 0)
    grid_spec = pltpu.PrefetchScalarGridSpec(
        num_scalar_prefetch=1,
        grid=(n_blocks,),
        in_specs=[pl.BlockSpec((bm * ROW_TILES, LANES), used), by_expert(D_MODEL, 2 * D_FF), by_expert(1, 2 * D_FF),
                  by_expert(D_FF, D_MODEL), by_expert(1, D_MODEL)],
        out_specs=pl.BlockSpec((bm * ROW_TILES, LANES), lambda i, plan: (i, 0)),
        scratch_shapes=[pltpu.VMEM(wgu.shape[1:], BF16), pltpu.VMEM(wd.shape[1:], BF16)],
    )
    return pl.pallas_call(
        functools.partial(_expert_blocks_kernel, n_blocks),
        grid_spec=grid_spec,
        out_shape=jax.ShapeDtypeStruct(h_sorted.shape, F32),
        compiler_params=_cparams("arbitrary"),
        name="moe_experts",
    )(plan, h_sorted, wgu, bgu, wd, bd)


def _gather_combine_kernel(x_ref, pos_ref, pos_next_ref, gate_ref, g_ref, y_hbm, o_ref, ybuf, sem):
    t = pl.program_id(0)
    n_tiles = pl.num_programs(0)
    tm = x_ref.shape[0]
    slot = t % 2

    def gather(rows_ref, sl):
        def body(it, carry):
            for u in range(DMA_ROWS_PER_ITER):
                j = it * DMA_ROWS_PER_ITER + u
                for k in range(TOP_K):
                    pltpu.make_async_copy(y_hbm.at[_row_tile(rows_ref[k * tm + j]), :],
                                          ybuf.at[sl, _row_tile(k * tm + j), :],
                                          sem.at[sl]).start(priority=k % N_DMA_THREADS)
            return carry
        lax.fori_loop(0, tm // DMA_ROWS_PER_ITER, body, 0)

    @pl.when(t == 0)
    def _():
        gather(pos_ref, 0)

    @pl.when(t + 1 < n_tiles)
    def _():
        gather(pos_next_ref, 1 - slot)

    pltpu.make_async_copy(y_hbm.at[pl.ds(0, TOP_K * tm * ROW_TILES), :], ybuf.at[slot], sem.at[slot]).wait()
    gates = gate_ref[...]
    cols = []
    for c in range(ROW_TILES):
        acc = x_ref[:, c * LANES:(c + 1) * LANES]
        for k in range(TOP_K):
            acc = acc + gates[:, k:k + 1] * ybuf[slot, pl.ds(k * tm * ROW_TILES + c, tm, stride=ROW_TILES), :]
        cols.append(acc)
    o_ref[...] = _rms(jnp.concatenate(cols, axis=1), g_ref[...])


def _gather_combine(x, pos, gates, norm_final, y_sorted, tm):
    n, d = x.shape
    assert n % tm == 0 and tm % DMA_ROWS_PER_ITER == 0
    nt = n // tm
    rows = lambda off: pl.BlockSpec((TOP_K * tm,), lambda i: (jnp.minimum(i + off, nt - 1),),
                                    memory_space=pltpu.SMEM)
    return pl.pallas_call(
        _gather_combine_kernel,
        grid=(nt,),
        in_specs=[pl.BlockSpec((tm, d), lambda i: (i, 0)), rows(0), rows(1),
                  pl.BlockSpec((tm, SUBLANES), lambda i: (i, 0)), pl.BlockSpec((1, d), lambda i: (0, 0)),
                  pl.BlockSpec(memory_space=pl.ANY)],
        out_specs=pl.BlockSpec((tm, d), lambda i: (i, 0)),
        out_shape=jax.ShapeDtypeStruct((n, d), F32),
        scratch_shapes=[pltpu.VMEM((2, TOP_K * tm * ROW_TILES, LANES), F32), pltpu.SemaphoreType.DMA((2,))],
        compiler_params=_cparams("arbitrary"),
        name="moe_combine",
    )(x, pos, pos, gates, norm_final, y_sorted)


def _moe_and_final_norm(x_p, routing_p, x_s, routing_s, norm_ffn, w_gate_up, b_gate_up, w_down, b_down, norm_final):
    n_p, n_s = x_p.shape[0], x_s.shape[0]
    n_rows = (n_p + n_s) * TOP_K
    n_blocks = -(-(n_rows + N_EXPERTS * (BM_EXPERT - 1)) // BM_EXPERT)
    g_ffn = norm_ffn[None]
    e_s, r_s, gates_s, _ = routing_s
    e_p, r_p, gates_p, cnt = routing_p
    plan = _moe_plan(cnt[:, 0], n_blocks)
    pos_p = _positions(plan, e_p, r_p, n_blocks, TM_COMBINE)
    pos_s = _positions(plan, e_s, r_s, n_blocks, n_s)
    h_sorted = _scatter_rows(cnt[:, 0], plan, x_p, pos_p, x_s, pos_s, g_ffn, n_blocks, TM_COMBINE)
    y_sorted = _expert_blocks(plan, h_sorted, w_gate_up, b_gate_up[:, None, :], w_down, b_down[:, None, :], n_blocks)
    y_p = _gather_combine(x_p, pos_p, gates_p, norm_final[None], y_sorted, TM_COMBINE)
    y_s = _gather_combine(x_s, pos_s, gates_s, norm_final[None], y_sorted, n_s)
    return y_p, y_s


def _rb(x):
    return x.astype(BF16).astype(F32)


def _sample_proj_kernel(x_ref, g_ref, wm_ref, wg_ref, z_ref):
    hb = _rms(x_ref[...], g_ref[...]).astype(BF16)
    z_ref[:, :MIX_WIDTH] = jnp.dot(hb, wm_ref[...], preferred_element_type=F32)
    z_ref[:, MIX_WIDTH:] = jnp.dot(hb, wg_ref[...], preferred_element_type=F32)


def _sample_proj(x, norm_mix, w_mix, w_gate):
    n, d = x.shape
    args = (x, norm_mix, w_mix, w_gate)
    width = w_mix.shape[1] + w_gate.shape[1]
    return pl.pallas_call(
        _sample_proj_kernel,
        grid=(1,),
        in_specs=[pl.BlockSpec(a.shape, lambda j: (0, 0)) for a in args],
        out_specs=pl.BlockSpec((n, width), lambda j: (0, 0)),
        out_shape=jax.ShapeDtypeStruct((n, width), F32),
        compiler_params=_cparams("arbitrary"),
        name="sample_proj",
    )(*args)


def _sample_mix_kernel(z_ref, c0_ref, c1_ref, c2_ref, mem_ref, sb0_ref, sb1_ref, sb2_ref, nb_ref, lng_ref, lnb_ref,
                       wsp_ref, bsp_ref, bo_ref, co_ref, a_ref, v_ref, n0_ref, n1_ref, n2_ref):
    z = z_ref[0]
    scale = 1.0 / math.sqrt(B_HEAD_DIM)
    row_head = lax.broadcasted_iota(I32, (SUBLANES, B_GROUP_W), 0)
    lane_head = lax.broadcasted_iota(I32, (SUBLANES, B_GROUP_W), 1) // B_HEAD_DIM
    own = (row_head == lane_head).astype(F32)
    kv_cols = jnp.transpose(jnp.broadcast_to(z[:, B_WIDTH:3 * B_WIDTH], (SUBLANES, 2 * B_WIDTH)))[:, 0:1]
    outs, lses = [], []
    for g, (cache_ref, sb_ref, new_ref) in enumerate(((c0_ref, sb0_ref, n0_ref), (c1_ref, sb1_ref, n1_ref),
                                                      (c2_ref, sb2_ref, n2_ref))):
        window = cache_ref.shape[2]
        q, kn, vn = (_rb(z[:, i * B_WIDTH + g * B_GROUP_W:i * B_WIDTH + (g + 1) * B_GROUP_W]) for i in range(3))
        qh = own * q
        kt = cache_ref[0, :B_GROUP_W, :].astype(BF16)
        vt = cache_ref[0, B_GROUP_W:, :].astype(BF16)
        lg = jnp.dot(qh.astype(BF16), kt, preferred_element_type=F32) * scale + sb_ref[...]
        ln = jnp.sum(qh * kn, axis=1, keepdims=True) * scale + nb_ref[g, :, 0:1]
        m = jnp.maximum(jnp.max(lg, axis=1, keepdims=True), ln)
        lse = m + jnp.log(jnp.sum(jnp.exp(lg - m), axis=1, keepdims=True) + jnp.exp(ln - m))
        p = jnp.exp(lg - lse).astype(BF16)
        pn = _rb(jnp.exp(ln - lse))
        pv = lax.dot_general(p, vt, (((1,), (1,)), ((), ())), preferred_element_type=F32) + pn * vn
        outs.append(jnp.sum(own * pv, axis=0, keepdims=True))
        lses.append(jnp.sum(own * lse, axis=0, keepdims=True))
        new_col = jnp.concatenate([kv_cols[i * B_WIDTH + g * B_GROUP_W:i * B_WIDTH + (g + 1) * B_GROUP_W]
                                   for i in range(2)], axis=0)
        lane = lax.broadcasted_iota(I32, (1, window), 1)
        new_ref[0] = jnp.where(lane == window - 1, new_col, pltpu.roll(cache_ref[0], window - 1, axis=1))
    m = jnp.maximum(jnp.maximum(lses[0], lses[1]), lses[2])
    ws = [jnp.exp(l - m) for l in lses]
    den = ws[0] + ws[1] + ws[2]
    bo_ref[0] = _rb(ws[0] / den) * _rb(outs[0]) + _rb(ws[1] / den) * _rb(outs[1]) + _rb(ws[2] / den) * _rb(outs[2])

    cs = []
    for h in range(C_HEADS):
        c0 = MIX_WIDTH - C_WIDTH + h * C_HEAD_DIM
        qh = _rb(z[:, c0:c0 + C_HEAD_DIM])
        kh = _rb(mem_ref[0, pl.ds(h, N_MEM, stride=2 * C_HEADS), :])
        vh = _rb(mem_ref[0, pl.ds(C_HEADS + h, N_MEM, stride=2 * C_HEADS), :])
        s = jnp.sum(kh * qh, axis=-1, keepdims=True) * (1.0 / math.sqrt(C_HEAD_DIM))
        p = jnp.exp(s - jnp.max(s, axis=0, keepdims=True))
        p = _rb(p / jnp.sum(p, axis=0, keepdims=True))
        cs.append(jnp.sum(p * vh, axis=0, keepdims=True))
    co_ref[0] = jnp.concatenate(cs, axis=1)

    u = _gelu(z[:, 3 * B_WIDTH:3 * B_WIDTH + A_WIDTH])
    v = _layernorm(_gelu(z[:, 3 * B_WIDTH + A_WIDTH:3 * B_WIDTH + 2 * A_WIDTH]), lng_ref[...], lnb_ref[...])
    v_ref[0] = v
    a_ref[0] = u * (_rb(wsp_ref[...]) * _rb(v) + bsp_ref[...])


def _sample_mix(z, caches, mem, sbias, nbias, ln_g, ln_b, wsp0, bsp0):
    n = z.shape[0]
    const = lambda arr: pl.BlockSpec(arr.shape, lambda i: (0,) * arr.ndim)
    row = lambda w: pl.BlockSpec((1, 1, w), lambda i: (i, 0, 0))
    per_req = lambda arr: pl.BlockSpec((1,) + arr.shape[1:], lambda i: (i, 0, 0))
    widths = (B_GROUP_W, C_WIDTH, A_WIDTH, A_WIDTH)
    return pl.pallas_call(
        _sample_mix_kernel,
        grid=(n,),
        in_specs=[row(z.shape[2])] + [per_req(c) for c in caches] + [per_req(mem)]
        + [const(a) for a in (*sbias, nbias, ln_g, ln_b, wsp0, bsp0)],
        out_specs=tuple(row(w) for w in widths) + tuple(per_req(c) for c in caches),
        out_shape=tuple(jax.ShapeDtypeStruct((n, 1, w), F32) for w in widths)
        + tuple(jax.ShapeDtypeStruct(c.shape, F32) for c in caches),
        compiler_params=_cparams("arbitrary"),
        name="sample_mix",
    )(z, *caches, mem, *sbias, nbias, ln_g, ln_b, wsp0, bsp0)


def _sample_merge_kernel(x_ref, z_ref, a_ref, b_ref, c_ref, wa_ref, wb_ref, wc_ref, wo_ref, o_ref):
    dot = lambda a, b: jnp.dot(a.astype(BF16), b, preferred_element_type=F32)
    merged = None
    for i, (act_ref, w_ref) in enumerate(((a_ref, wa_ref), (b_ref, wb_ref), (c_ref, wc_ref))):
        gate = jax.nn.sigmoid(z_ref[:, MIX_WIDTH + i * D_MODEL:MIX_WIDTH + (i + 1) * D_MODEL])
        term = gate * dot(act_ref[...], w_ref[...])
        merged = term if merged is None else merged + term
    o_ref[...] = x_ref[...] + dot(merged, wo_ref[...])


def _sample_merge(x, z, a, bo, co, wa, wb, wc, wo):
    args = (x, z, a, bo, co, wa, wb, wc, wo)
    return pl.pallas_call(
        _sample_merge_kernel,
        grid=(1,),
        in_specs=[pl.BlockSpec(arr.shape, lambda i: (0, 0)) for arr in args],
        out_specs=pl.BlockSpec(x.shape, lambda i: (0, 0)),
        out_shape=jax.ShapeDtypeStruct(x.shape, F32),
        compiler_params=_cparams("arbitrary"),
        name="sample_merge",
    )(*args)


def _sample_mixers(x_sample, caches, cache_mem, sbias, nbias, norm_mix, ln_v_g, ln_v_b, w_spatial, b_spatial, wts):
    n = x_sample.shape[0]
    x = x_sample[:, 0]
    z = _sample_proj(x, norm_mix[None], wts["mix"], wts["gate"])
    mem = cache_mem.reshape(n, N_MEM * 2 * C_HEADS, C_HEAD_DIM)
    wsp0 = jnp.repeat(w_spatial[:, 0, 0], LANES)[None]
    bsp0 = jnp.repeat(b_spatial[:, 0], LANES)[None]
    bo, co, a, v, *new_caches = _sample_mix(z[:, None], [_from_window_layout(c) for c in caches], mem, sbias, nbias,
                                            ln_v_g[None], ln_v_b[None], wsp0, bsp0)
    x_new = _sample_merge(x, z, a[:, 0], bo[:, 0], co[:, 0], wts["a"], wts["b"], wts["c"], wts["out"])
    return x_new, [_to_window_layout(c) for c in new_caches], v[:, 0]


def _mixer_weights(w_in, w_branch_a, w_branch_b, w_branch_c, w_out):
    return {"mix": w_in[:, :MIX_WIDTH].astype(BF16), "gate": w_in[:, MIX_WIDTH:].astype(BF16),
            "a": w_branch_a.astype(BF16), "b": w_branch_b.astype(BF16), "c": w_branch_c.astype(BF16),
            "out": w_out.astype(BF16)}


def kernel(x_prompt, x_sample, cache_win0_kv, cache_win1_kv, cache_win2_kv, cache_mem_kv, mem_prompt, rel_bias,
           norm_mix, norm_mem, w_in, ln_v_g, ln_v_b, w_spatial, b_spatial, w_mem_kv, w_branch_a, w_branch_b,
           w_branch_c, w_out, norm_ffn, w_router, b_router, w_gate_up, b_gate_up, w_down, b_down, norm_final):
    assert norm_mix.shape[0] == 1, "one layer"
    b, s, d = x_prompt.shape
    n_s = x_sample.shape[0]
    caches = (cache_win0_kv[0], cache_win1_kv[0], cache_win2_kv[0])
    assert all(c.shape[1] == w for c, (w, _) in zip(caches, B_PAIRS)), "window buffers hold a full window"
    wts = _mixer_weights(w_in[0], w_branch_a[0], w_branch_b[0], w_branch_c[0], w_out[0])
    band_bias, *sbias, nbias = _bias_tables(rel_bias)
    route_params = _route_params(norm_ffn[0], w_router[0], b_router[0])
    xs, win_s, v_s = _sample_mixers(x_sample, caches, cache_mem_kv[0], sbias, nbias, norm_mix[0], ln_v_g[0],
                                    ln_v_b[0], w_spatial[0], b_spatial[0], wts)
    routing_s = _route(xs, route_params, jnp.zeros((N_EXPERTS, LANES), I32), n_s)
    xp, routing_p, win_p, mem_p = _prompt_mixers(x_prompt, mem_prompt, band_bias, norm_mix[0], norm_mem[0], ln_v_g[0],
                                                 ln_v_b[0], w_spatial[0], b_spatial[0], w_mem_kv[0], wts,
                                                 route_params, routing_s[3])
    y_p, y_s = _moe_and_final_norm(xp.reshape(b * s, d), routing_p, xs, routing_s, norm_ffn[0], w_gate_up[0],
                                   b_gate_up[0], w_down[0], b_down[0], norm_final)
    chunk_v = v_s.reshape(1, n_s, 1, A_GROUPS, LANES)
    return (y_p.reshape(b, s, d), y_s[:, None], win_p[0], win_p[1], win_p[2], mem_p,
            win_s[0], win_s[1], win_s[2], chunk_v)
```

```python
import functools
import math

import numpy as np
import jax
import jax.numpy as jnp
from jax import lax
from jax.experimental import pallas as pl
from jax.experimental.pallas import tpu as pltpu

F32 = jnp.float32
BF16 = jnp.bfloat16
I32 = jnp.int32

D_MODEL = 1024
N_MEM = 256
CHUNK = 128
A_GROUPS = 4
A_WIDTH = 512
B_PAIRS = ((128, 1), (512, 4), (2048, 16))
B_GROUP_W = 256
B_WIDTH = 768
B_HEAD_DIM = 64
BAND = 128
C_HEADS = 4
C_HEAD_DIM = 128
C_WIDTH = 512
REL_BUCKETS = 32
REL_MAX_DIST = 2048
N_EXPERTS = 32
TOP_K = 4
D_FF = 1024
SWIGLU_LIMIT = 7.0
SWIGLU_ALPHA = 1.702
EPS = 1e-6
NEG_INF = -1e30
MIX_WIDTH = 3 * B_WIDTH + 2 * A_WIDTH + C_WIDTH

LANES = 128
SUBLANES = 8
ROW_TILES = D_MODEL // LANES
VMEM_LIMIT = 56 * 1024 * 1024

TM_PROJ = 1024
TM_MERGE = 1024
BM_EXPERT = 512
TM_COMBINE = 512
ATTN_BLOCKS_PER_ITER = 8


def _cparams(*sem):
    return pltpu.CompilerParams(dimension_semantics=sem, vmem_limit_bytes=VMEM_LIMIT)


def _t5_bucket(dist):
    dist = np.maximum(np.asarray(dist), 0)
    max_exact = REL_BUCKETS // 2
    log_ratio = np.log(np.maximum(dist, max_exact) / max_exact) / math.log(REL_MAX_DIST / max_exact)
    large = np.minimum(max_exact + (log_ratio * (REL_BUCKETS - max_exact)).astype(np.int32), REL_BUCKETS - 1)
    return np.where(dist < max_exact, dist, large).astype(np.int32)


def _gelu(x):
    c = math.sqrt(2.0 / math.pi)
    return x * (0.5 * (1.0 + jnp.tanh(c * (x + 0.044715 * (x * x * x)))))


def _rms(x, g):
    return x * lax.rsqrt(jnp.mean(x * x, axis=-1, keepdims=True) + EPS) * g


def _layernorm(x, g, b):
    xc = x - jnp.mean(x, axis=-1, keepdims=True)
    return xc * lax.rsqrt(jnp.mean(xc * xc, axis=-1, keepdims=True) + EPS) * g + b


def _proj_kernel(x_ref, g_ref, w_ref, lng_ref, lnb_ref, wsp_ref, bsp_ref,
                 qkv0_ref, qkv1_ref, qkv2_ref, win0_ref, win1_ref, win2_ref, a_ref, qc_ref, scr_ref):
    tm = x_ref.shape[1]
    hb = _rms(x_ref[0], g_ref[...]).astype(BF16)

    def proj(c0, width):
        return jnp.dot(hb, w_ref[:, c0:c0 + width], preferred_element_type=F32)

    for g, (_, dil) in enumerate(B_PAIRS):
        q, k, v = (proj(i * B_WIDTH + g * B_GROUP_W, B_GROUP_W) for i in range(3))
        kv = jnp.concatenate([k, v], axis=1)
        window = min(B_PAIRS[g][0], tm)
        (win0_ref, win1_ref, win2_ref)[g][0] = jnp.transpose(kv[tm - window:])
        if dil == 1:
            qkv0_ref[0] = jnp.concatenate([q, k, v], axis=1).astype(BF16)
            continue
        out_ref = qkv1_ref if g == 1 else qkv2_ref
        for i, arr in enumerate((q, k, v)):
            for ct in range(2):
                scr_ref[2 * i + ct] = arr[:, ct * LANES:(ct + 1) * LANES]
        for r in range(dil):
            rows = [scr_ref[j, pl.ds(r, tm // dil, stride=dil), :] for j in range(6)]
            out_ref[0, r] = jnp.concatenate(rows, axis=1).astype(BF16)

    u = _gelu(proj(3 * B_WIDTH, A_WIDTH))
    v = _layernorm(_gelu(proj(3 * B_WIDTH + A_WIDTH, A_WIDTH)), lng_ref[...], lnb_ref[...]).astype(BF16)
    for c in range(tm // CHUNK):
        rs = slice(c * CHUNK, (c + 1) * CHUNK)
        for g in range(A_GROUPS):
            cs = slice(g * LANES, (g + 1) * LANES)
            s = jnp.dot(wsp_ref[g], v[rs, cs], preferred_element_type=F32) + bsp_ref[:, g:g + 1]
            a_ref[0, rs, cs] = (u[rs, cs] * s).astype(BF16)

    qc_ref[0] = proj(3 * B_WIDTH + 2 * A_WIDTH, C_WIDTH).astype(BF16)


def _prompt_proj(x, norm_mix, w_mix, ln_g, ln_b, wsp, bsp):
    b, s, d = x.shape
    tm = TM_PROJ
    assert s % tm == 0 and s >= B_PAIRS[2][0] and tm >= B_PAIRS[1][0]
    nt = s // tm
    const = lambda *shape: pl.BlockSpec(shape, lambda i, j: (0,) * len(shape))
    out_shape = (
        jax.ShapeDtypeStruct((b, s, 3 * B_GROUP_W), BF16),
        jax.ShapeDtypeStruct((b, 4, s // 4, 3 * B_GROUP_W), BF16),
        jax.ShapeDtypeStruct((b, 16, s // 16, 3 * B_GROUP_W), BF16),
        jax.ShapeDtypeStruct((b, 2 * B_GROUP_W, B_PAIRS[0][0]), F32),
        jax.ShapeDtypeStruct((b, 2 * B_GROUP_W, B_PAIRS[1][0]), F32),
        jax.ShapeDtypeStruct((b, 2 * B_GROUP_W, s), F32),
        jax.ShapeDtypeStruct((b, s, A_WIDTH), BF16),
        jax.ShapeDtypeStruct((b, s, C_WIDTH), BF16),
    )
    out_specs = (
        pl.BlockSpec((1, tm, 3 * B_GROUP_W), lambda i, j: (i, j, 0)),
        pl.BlockSpec((1, 4, tm // 4, 3 * B_GROUP_W), lambda i, j: (i, 0, j, 0)),
        pl.BlockSpec((1, 16, tm // 16, 3 * B_GROUP_W), lambda i, j: (i, 0, j, 0)),
        pl.BlockSpec((1, 2 * B_GROUP_W, B_PAIRS[0][0]), lambda i, j: (i, 0, 0)),
        pl.BlockSpec((1, 2 * B_GROUP_W, B_PAIRS[1][0]), lambda i, j: (i, 0, 0)),
        pl.BlockSpec((1, 2 * B_GROUP_W, tm), lambda i, j: (i, 0, j)),
        pl.BlockSpec((1, tm, A_WIDTH), lambda i, j: (i, j, 0)),
        pl.BlockSpec((1, tm, C_WIDTH), lambda i, j: (i, j, 0)),
    )
    return pl.pallas_call(
        _proj_kernel,
        grid=(b, nt),
        in_specs=[pl.BlockSpec((1, tm, d), lambda i, j: (i, j, 0)), const(1, d), const(d, MIX_WIDTH),
                  const(1, A_WIDTH), const(1, A_WIDTH), const(A_GROUPS, CHUNK, CHUNK), const(CHUNK, A_GROUPS)],
        out_specs=out_specs,
        out_shape=out_shape,
        scratch_shapes=[pltpu.VMEM((6, tm, LANES), F32)],
        compiler_params=_cparams("arbitrary", "arbitrary"),
        name="prompt_proj",
    )(x, norm_mix, w_mix, ln_g, ln_b, wsp, bsp)


def _attend(q, k, v, bias_ref, g, k0, hmask):
    kn = k.shape[0]
    qs = jnp.concatenate([q * hm for hm in hmask], axis=0)
    s = lax.dot_general(qs, k, (((1,), (1,)), ((), ())), preferred_element_type=F32) + bias_ref[g, :, k0:k0 + kn]
    m = jnp.max(s, axis=-1, keepdims=True)
    p = jnp.exp(s - m)
    l = jnp.sum(p, axis=-1, keepdims=True)
    pv = jnp.dot(p.astype(BF16), v, preferred_element_type=F32) / l
    lse_rows = m + jnp.log(l)
    out = jnp.zeros((BAND, B_GROUP_W), F32)
    lse = jnp.zeros((BAND, B_GROUP_W), F32)
    for h, hm in enumerate(hmask):
        rows = slice(h * BAND, (h + 1) * BAND)
        sel = hm > 0
        out = jnp.where(sel, pv[rows], out)
        lse = jnp.where(sel, lse_rows[rows], lse)
    return out, lse


def _attn_kernel(q0_ref, q1_ref, q2_ref, bias_ref, o_ref, out_ref, lse_ref):
    s = q0_ref.shape[1]
    lane_head = lax.broadcasted_iota(I32, (1, B_GROUP_W), 1) // B_HEAD_DIM
    scale = 1.0 / math.sqrt(B_HEAD_DIM)
    hmask = [jnp.where(lane_head == h, scale, 0.0).astype(BF16) for h in range(4)]
    qs, ks, vs = (slice(i * B_GROUP_W, (i + 1) * B_GROUP_W) for i in range(3))

    def store(g, start, dil, out, lse):
        rows = pl.ds(start, BAND) if dil == 1 else pl.ds(start, BAND, stride=dil)
        for ct in range(2):
            out_ref[g, ct, rows, :] = out[:, ct * LANES:(ct + 1) * LANES]
            lse_ref[g, ct, rows, :] = lse[:, ct * LANES:(ct + 1) * LANES]

    for g, (_, dil) in enumerate(B_PAIRS):
        n = s // dil
        nb = n // BAND

        def load(rows, cols, r, g=g):
            if g == 0:
                return q0_ref[0, rows, cols]
            return (q1_ref if g == 1 else q2_ref)[0, r, rows, cols]

        def first_block(r, g=g, dil=dil, load=load):
            rows = pl.ds(0, BAND)
            out, lse = _attend(load(rows, qs, r), load(rows, ks, r), load(rows, vs, r), bias_ref, g, BAND, hmask)
            store(g, r, dil, out, lse)

        def later_block(i, g=g, dil=dil, nb=nb, load=load):
            r = i // (nb - 1)
            qb = i % (nb - 1) + 1
            q0 = pl.multiple_of(qb * BAND, BAND)
            rows_q = pl.ds(q0, BAND)
            rows_k = pl.ds(q0 - BAND, 2 * BAND)
            out, lse = _attend(load(rows_q, qs, r), load(rows_k, ks, r), load(rows_k, vs, r), bias_ref, g, 0, hmask)
            store(g, qb * (BAND * dil) + r, dil, out, lse)

        def run(block_fn, count):
            per = ATTN_BLOCKS_PER_ITER

            def body(it, carry):
                for u in range(per):
                    block_fn(it * per + u)
                return carry
            lax.fori_loop(0, count // per, body, 0)
            for i in range(count - count % per, count):
                block_fn(i)

        run(first_block, dil)
        run(later_block, dil * (nb - 1))

    rc = 256
    def combine(c, carry):
        rows = pl.ds(pl.multiple_of(c * rc, rc), rc)
        for ct in range(2):
            ls = [lse_ref[g, ct, rows, :] for g in range(3)]
            m = jnp.maximum(jnp.maximum(ls[0], ls[1]), ls[2])
            ws = [jnp.exp(l - m) for l in ls]
            den = ws[0] + ws[1] + ws[2]
            num = ws[0] * out_ref[0, ct, rows, :] + ws[1] * out_ref[1, ct, rows, :] + ws[2] * out_ref[2, ct, rows, :]
            o_ref[0, rows, ct * LANES:(ct + 1) * LANES] = (num / den).astype(BF16)
        return carry
    lax.fori_loop(0, s // rc, combine, 0)


def _prompt_attention(qkv0, qkv1, qkv2, bias):
    b, s, w = qkv0.shape
    return pl.pallas_call(
        _attn_kernel,
        grid=(b,),
        in_specs=[pl.BlockSpec((1, s, w), lambda i: (i, 0, 0)),
                  pl.BlockSpec((1, 4, s // 4, w), lambda i: (i, 0, 0, 0)),
                  pl.BlockSpec((1, 16, s // 16, w), lambda i: (i, 0, 0, 0)),
                  pl.BlockSpec(bias.shape, lambda i: (0, 0, 0))],
        out_specs=pl.BlockSpec((1, s, B_GROUP_W), lambda i: (i, 0, 0)),
        out_shape=jax.ShapeDtypeStruct((b, s, B_GROUP_W), BF16),
        scratch_shapes=[pltpu.VMEM((3, 2, s, LANES), F32), pltpu.VMEM((3, 2, s, LANES), F32)],
        compiler_params=_cparams("arbitrary"),
        name="prompt_attention",
    )(qkv0, qkv1, qkv2, bias)


def _bias_kernel(rel_ref, band_idx_ref, s0_ref, s1_ref, s2_ref, band_ref, sb0_ref, sb1_ref, sb2_ref, nb_ref):
    def lookup(idx, col):
        acc = jnp.full(idx.shape, NEG_INF, F32)
        for bucket in range(REL_BUCKETS):
            acc = jnp.where(idx == bucket, rel_ref[bucket, col], acc)
        return acc

    for g, (s_ref, sb_ref) in enumerate(((s0_ref, sb0_ref), (s1_ref, sb1_ref), (s2_ref, sb2_ref))):
        width = s_ref.shape[1]
        for h in range(4):
            band_ref[g, h * BAND:(h + 1) * BAND, :] = lookup(band_idx_ref[g], 4 * g + h)
        sb_ref[...] = jnp.concatenate([lookup(s_ref[...], 4 * g + h) for h in range(4)]
                                      + [jnp.full((SUBLANES - 4, width), NEG_INF, F32)], axis=0)
        nb_ref[g] = jnp.concatenate([jnp.full((1, LANES), rel_ref[0, 4 * g + h], F32) for h in range(4)]
                                    + [jnp.zeros((SUBLANES - 4, LANES), F32)], axis=0)


def _bias_tables(rel_bias):
    steps = np.arange(BAND)[:, None] + BAND - np.arange(2 * BAND)[None, :]
    valid = (steps >= 0) & (steps <= BAND)
    band_idx = np.stack([np.where(valid, _t5_bucket(np.clip(steps, 0, BAND) * dil), -1) for _, dil in B_PAIRS])
    s_idx = []
    for window, dil in B_PAIRS:
        w = np.arange(window)
        s_idx.append(np.where(w % dil == 0, _t5_bucket(window - w), -1)[None].astype(np.int32))
    args = (rel_bias, jnp.asarray(band_idx.astype(np.int32))) + tuple(jnp.asarray(s) for s in s_idx)
    vmem = lambda a: pl.BlockSpec(a.shape, lambda i: (0,) * a.ndim)
    out_shape = (jax.ShapeDtypeStruct((3, 4 * BAND, 2 * BAND), F32),) + tuple(
        jax.ShapeDtypeStruct((SUBLANES, window), F32) for window, _ in B_PAIRS) + (
        jax.ShapeDtypeStruct((3, SUBLANES, LANES), F32),)
    return pl.pallas_call(
        _bias_kernel,
        grid=(1,),
        in_specs=[pl.BlockSpec(memory_space=pltpu.SMEM)] + [vmem(a) for a in args[1:]],
        out_specs=tuple(pl.BlockSpec(s.shape, lambda i, n=len(s.shape): (0,) * n) for s in out_shape),
        out_shape=out_shape,
        compiler_params=_cparams("arbitrary"),
        name="bias_tables",
    )(*args)


def _memkv_kernel(mem_ref, g_ref, w_ref, kv_ref, kvb_ref):
    hb = _rms(mem_ref[0], g_ref[...]).astype(BF16)
    kv = jnp.dot(hb, w_ref[...], preferred_element_type=F32)
    m = kv.shape[0]
    n_heads = kv.shape[1] // C_HEAD_DIM
    for j in range(n_heads):
        kv_ref[0, pl.ds(j, m, stride=n_heads), :] = kv[:, j * C_HEAD_DIM:(j + 1) * C_HEAD_DIM]
    kvb_ref[0] = kv.astype(BF16)


def _memory_kv(mem, norm_mem, w_mem):
    b, m, d = mem.shape
    w = w_mem.shape[1]
    rows = m * w // C_HEAD_DIM
    return pl.pallas_call(
        _memkv_kernel,
        grid=(b,),
        in_specs=[pl.BlockSpec((1, m, d), lambda i: (i, 0, 0)), pl.BlockSpec((1, d), lambda i: (0, 0)),
                  pl.BlockSpec((d, w), lambda i: (0, 0))],
        out_specs=(pl.BlockSpec((1, rows, C_HEAD_DIM), lambda i: (i, 0, 0)),
                   pl.BlockSpec((1, m, w), lambda i: (i, 0, 0))),
        out_shape=(jax.ShapeDtypeStruct((b, rows, C_HEAD_DIM), F32), jax.ShapeDtypeStruct((b, m, w), BF16)),
        compiler_params=_cparams("arbitrary"),
        name="memory_kv",
    )(mem, norm_mem, w_mem)


def _merge_kernel(x_ref, g_ref, a_ref, b_ref, qc_ref, kv_ref, wg_ref, wa_ref, wb_ref, wc_ref, wo_ref,
                  gf_ref, wr_ref, br_ref, cnt_in_ref, o_ref, e_ref, rank_ref, gate_ref, cnt_ref, carry_ref):
    @pl.when((pl.program_id(0) == 0) & (pl.program_id(1) == 0))
    def _():
        carry_ref[...] = cnt_in_ref[...].astype(F32)

    x = x_ref[0]
    hb = _rms(x, g_ref[...]).astype(BF16)
    qc = qc_ref[0]
    cs = []
    for h in range(C_HEADS):
        hs = slice(h * C_HEAD_DIM, (h + 1) * C_HEAD_DIM)
        s = lax.dot_general(qc[:, hs], kv_ref[0, :, hs], (((1,), (1,)), ((), ())), preferred_element_type=F32)
        s = s * (1.0 / math.sqrt(C_HEAD_DIM))
        p = jnp.exp(s - jnp.max(s, axis=-1, keepdims=True))
        p = p / jnp.sum(p, axis=-1, keepdims=True)
        vs = slice(C_WIDTH + h * C_HEAD_DIM, C_WIDTH + (h + 1) * C_HEAD_DIM)
        cs.append(jnp.dot(p.astype(BF16), kv_ref[0, :, vs], preferred_element_type=F32))
    c = jnp.concatenate(cs, axis=1).astype(BF16)
    branches = ((a_ref[0], wa_ref), (b_ref[0], wb_ref), (c, wc_ref))
    merged = None
    for i, (act, w_ref) in enumerate(branches):
        gate = jax.nn.sigmoid(jnp.dot(hb, wg_ref[:, i * D_MODEL:(i + 1) * D_MODEL], preferred_element_type=F32))
        term = gate * jnp.dot(act, w_ref[...], preferred_element_type=F32)
        merged = term if merged is None else merged + term
    x_new = x + jnp.dot(merged.astype(BF16), wo_ref[...], preferred_element_type=F32)
    o_ref[0] = x_new
    _route_rows(x_new, gf_ref, wr_ref, br_ref, e_ref, rank_ref, gate_ref, cnt_ref, carry_ref)


def _prompt_merge(x, norm_mix, a, bo, qc, kvb, wg, wa, wb, wc, wo, route_params, cnt_in):
    b, s, d = x.shape
    tm = TM_MERGE
    nt = s // tm
    n = b * s
    tile = lambda w: pl.BlockSpec((1, tm, w), lambda i, j: (i, j, 0))
    const = lambda arr: pl.BlockSpec(arr.shape, lambda i, j: (0,) * arr.ndim)
    ids = pl.BlockSpec((TOP_K, tm), lambda i, j: (0, i * nt + j))
    return pl.pallas_call(
        _merge_kernel,
        grid=(b, nt),
        in_specs=[tile(d), const(norm_mix), tile(A_WIDTH), tile(B_GROUP_W), tile(C_WIDTH),
                  pl.BlockSpec((1,) + kvb.shape[1:], lambda i, j: (i, 0, 0)),
                  const(wg), const(wa), const(wb), const(wc), const(wo)]
        + [const(p) for p in route_params] + [const(cnt_in)],
        out_specs=(tile(d), ids, ids, pl.BlockSpec((tm, SUBLANES), lambda i, j: (i * nt + j, 0)),
                   pl.BlockSpec((N_EXPERTS, LANES), lambda i, j: (0, 0))),
        out_shape=(jax.ShapeDtypeStruct((b, s, d), F32), jax.ShapeDtypeStruct((TOP_K, n), I32),
                   jax.ShapeDtypeStruct((TOP_K, n), I32), jax.ShapeDtypeStruct((n, SUBLANES), F32),
                   jax.ShapeDtypeStruct((N_EXPERTS, LANES), I32)),
        scratch_shapes=[pltpu.VMEM((N_EXPERTS, LANES), F32)],
        compiler_params=_cparams("arbitrary", "arbitrary"),
        name="prompt_merge",
    )(x, norm_mix, a, bo, qc, kvb, wg, wa, wb, wc, wo, *route_params, cnt_in)


def _to_window_layout(w):
    b, _, window = w.shape
    return jnp.transpose(w.reshape(b, 2, 4, B_HEAD_DIM, window), (0, 4, 1, 2, 3))[None]


def _from_window_layout(c):
    b, window = c.shape[:2]
    return jnp.transpose(c, (0, 2, 3, 4, 1)).reshape(b, 2 * B_GROUP_W, window)


def _prompt_mixers(x_prompt, mem_prompt, band_bias, norm_mix, norm_mem, ln_v_g, ln_v_b, w_spatial, b_spatial,
                   w_mem_kv, wts, route_params, cnt_in):
    b, s, _ = x_prompt.shape
    wsp = jnp.where(np.tril(np.ones((CHUNK, CHUNK), bool)), w_spatial, 0.0).astype(BF16)
    qkv0, qkv1, qkv2, win0, win1, win2, a_out, qc = _prompt_proj(
        x_prompt, norm_mix[None], wts["mix"], ln_v_g[None], ln_v_b[None], wsp, b_spatial.T)
    b_out = _prompt_attention(qkv0, qkv1, qkv2, band_bias)
    mem_kv, mem_kv_b = _memory_kv(mem_prompt, norm_mem[None], w_mem_kv.astype(BF16))
    x_new, *routing = _prompt_merge(x_prompt, norm_mix[None], a_out, b_out, qc, mem_kv_b, wts["gate"],
                                    wts["a"], wts["b"], wts["c"], wts["out"], route_params, cnt_in)
    wins = [_to_window_layout(w) for w in (win0, win1, win2)]
    return x_new, routing, wins, mem_kv.reshape(1, b, N_MEM, 2, C_HEADS, C_HEAD_DIM)


def _route_kernel(x_ref, g_ref, wr_ref, br_ref, cnt_in_ref, e_ref, rank_ref, gate_ref, cnt_ref, carry_ref):
    @pl.when(pl.program_id(0) == 0)
    def _():
        carry_ref[...] = cnt_in_ref[...].astype(F32)

    _route_rows(x_ref[...], g_ref, wr_ref, br_ref, e_ref, rank_ref, gate_ref, cnt_ref, carry_ref)


def _route_rows(x, g_ref, wr_ref, br_ref, e_ref, rank_ref, gate_ref, cnt_ref, carry_ref):
    tm = x.shape[0]
    h = _rms(x, g_ref[...])
    logits = lax.dot_general(wr_ref[...], h.astype(BF16), (((1,), (1,)), ((), ())),
                             preferred_element_type=F32) + br_ref[...]
    expert = lax.broadcasted_iota(I32, logits.shape, 0)
    vals, idxs = [], []
    member = jnp.zeros(logits.shape, F32)
    for _ in range(TOP_K):
        m = jnp.max(logits, axis=0, keepdims=True)
        idx = jnp.min(jnp.where(logits == m, expert, N_EXPERTS), axis=0, keepdims=True)
        hit = expert == idx
        vals.append(m)
        idxs.append(idx)
        member = jnp.where(hit, 1.0, member)
        logits = jnp.where(hit, -jnp.inf, logits)
    p = [jnp.exp(v - vals[0]) for v in vals]
    den = p[0] + p[1] + p[2] + p[3]
    gates = jnp.concatenate([pk / den for pk in p] + [jnp.zeros((SUBLANES - TOP_K, tm), F32)], axis=0)
    gate_ref[...] = jnp.transpose(gates)
    e_ref[...] = jnp.concatenate(idxs, axis=0)
    before = (lax.broadcasted_iota(I32, (tm, tm), 0) < lax.broadcasted_iota(I32, (tm, tm), 1)).astype(BF16)
    prefix = jnp.dot(member.astype(BF16), before, preferred_element_type=F32) + carry_ref[:, 0:1]
    rank_ref[...] = jnp.concatenate(
        [jnp.sum(jnp.where(expert == idx, prefix, 0.0), axis=0, keepdims=True) for idx in idxs], axis=0).astype(I32)
    carry_ref[...] = carry_ref[...] + jnp.sum(member, axis=1, keepdims=True)
    cnt_ref[...] = carry_ref[...].astype(I32)


def _route_params(norm_ffn, w_router, b_router):
    return norm_ffn[None], w_router.T.astype(BF16), b_router[:, None]


def _route(x, route_params, cnt_in, tm):
    n, d = x.shape
    assert n % tm == 0
    norm_ffn, w_router_t, b_router = route_params
    const = lambda arr: pl.BlockSpec(arr.shape, lambda i: (0,) * arr.ndim)
    return pl.pallas_call(
        _route_kernel,
        grid=(n // tm,),
        in_specs=[pl.BlockSpec((tm, d), lambda i: (i, 0)), const(norm_ffn), const(w_router_t), const(b_router),
                  const(cnt_in)],
        out_specs=(pl.BlockSpec((TOP_K, tm), lambda i: (0, i)), pl.BlockSpec((TOP_K, tm), lambda i: (0, i)),
                   pl.BlockSpec((tm, SUBLANES), lambda i: (i, 0)), pl.BlockSpec((N_EXPERTS, LANES), lambda i: (0, 0))),
        out_shape=(jax.ShapeDtypeStruct((TOP_K, n), I32), jax.ShapeDtypeStruct((TOP_K, n), I32),
                   jax.ShapeDtypeStruct((n, SUBLANES), F32), jax.ShapeDtypeStruct((N_EXPERTS, LANES), I32)),
        scratch_shapes=[pltpu.VMEM((N_EXPERTS, LANES), F32)],
        compiler_params=_cparams("arbitrary"),
        name="moe_route",
    )(x, norm_ffn, w_router_t, b_router, cnt_in)


DMA_ROWS_PER_ITER = 8
N_DMA_THREADS = 2
N_PLAN_TAIL = 1 + N_EXPERTS


def _row_tile(r):
    return pl.ds(pl.multiple_of(r * ROW_TILES, ROW_TILES), ROW_TILES)


def _plan_kernel(n_blocks, cnt_ref, plan_ref):
    shift = BM_EXPERT.bit_length() - 1

    def per_expert(e, carry):
        blk0, last_e = carry
        nb = (cnt_ref[e] + (BM_EXPERT - 1)) >> shift
        plan_ref[n_blocks + 1 + e] = blk0 << shift

        def fill(j, cc):
            plan_ref[blk0 + j] = e
            return cc
        lax.fori_loop(0, nb, fill, 0)
        return blk0 + nb, jnp.where(nb > 0, e, last_e)
    n_used, last_e = lax.fori_loop(0, N_EXPERTS, per_expert, (0, 0))

    def tail(j, cc):
        plan_ref[j] = last_e
        return cc
    lax.fori_loop(n_used, n_blocks, tail, 0)
    plan_ref[n_blocks] = n_used


def _moe_plan(cnt, n_blocks):
    smem = pl.BlockSpec(memory_space=pltpu.SMEM)
    return pl.pallas_call(
        functools.partial(_plan_kernel, n_blocks),
        in_specs=[smem], out_specs=smem,
        out_shape=jax.ShapeDtypeStruct((n_blocks + N_PLAN_TAIL,), I32),
        name="moe_plan",
    )(cnt)


POSITION_TILES_PER_STEP = 16


def _positions_kernel(n_blocks, plan_ref, e_ref, r_ref, pos_ref):
    tiles, _, tm = pos_ref.shape
    e = e_ref[...]
    first = jnp.zeros(e.shape, I32)
    for ex in range(N_EXPERTS):
        first = jnp.where(e == ex, plan_ref[n_blocks + 1 + ex], first)
    pos = first + r_ref[...]
    for i in range(tiles):
        pos_ref[i] = pos[:, i * tm:(i + 1) * tm]


def _positions(plan, e, r, n_blocks, tm):
    n = e.shape[1]
    assert n % tm == 0
    nt = n // tm
    per_step = math.gcd(nt, POSITION_TILES_PER_STEP)
    pos = pl.pallas_call(
        functools.partial(_positions_kernel, n_blocks),
        grid=(nt // per_step,),
        in_specs=[pl.BlockSpec(memory_space=pltpu.SMEM), pl.BlockSpec((TOP_K, per_step * tm), lambda i: (0, i)),
                  pl.BlockSpec((TOP_K, per_step * tm), lambda i: (0, i))],
        out_specs=pl.BlockSpec((per_step, TOP_K, tm), lambda i: (i, 0, 0)),
        out_shape=jax.ShapeDtypeStruct((nt, TOP_K, tm), I32),
        compiler_params=_cparams("arbitrary"),
        name="moe_positions",
    )(plan, e, r)
    return pos.reshape(-1)


def _scatter_rows_kernel(n_blocks, cnt_ref, plan_ref, x_ref, pos_ref, xs_ref, poss_ref, g_ref, h_hbm,
                         hbuf, zbuf, sem, zsem):
    t = pl.program_id(0)
    n_tiles = pl.num_programs(0) - 1
    tm = x_ref.shape[0]
    n_s = xs_ref.shape[0]
    bm = BM_EXPERT
    shift = bm.bit_length() - 1
    slot = t % 2

    def zero_rows(first, count, wait):
        for bit in range(shift):
            n = 1 << bit
            lo = first + (count & (n - 1))

            @pl.when((count >> bit) & 1 == 1)
            def _():
                cp = pltpu.make_async_copy(zbuf.at[pl.ds(0, n * ROW_TILES), :], h_hbm.at[_row_tile_n(lo, n), :],
                                           zsem.at[0])
                cp.wait() if wait else cp.start()

    def zero_block(blk, wait):
        cp = pltpu.make_async_copy(zbuf, h_hbm.at[_row_tile_n(blk * bm, bm), :], zsem.at[0])
        cp.wait() if wait else cp.start()

    @pl.when(t == 0)
    def _():
        zbuf[...] = jnp.zeros(zbuf.shape, F32)
        for wait in (False, True):
            def pad_expert(e, cc, wait=wait):
                cnt = cnt_ref[e]
                zero_rows(plan_ref[n_blocks + 1 + e] + cnt, (-cnt) & (bm - 1), wait)
                return cc
            lax.fori_loop(0, N_EXPERTS, pad_expert, 0)

            def pad_block(blk, cc, wait=wait):
                zero_block(blk, wait)
                return cc
            lax.fori_loop(plan_ref[n_blocks], n_blocks, pad_block, 0)

    def wait_copies(sl, n_tok):
        for _ in range(TOP_K):
            pltpu.make_async_copy(hbuf.at[sl, pl.ds(0, n_tok * ROW_TILES), :],
                                  h_hbm.at[pl.ds(0, n_tok * ROW_TILES), :], sem.at[sl]).wait()

    def copy_rows(src_ref, rows_ref, n_tok):
        h = _rms(src_ref[...], g_ref[...])
        for c in range(ROW_TILES):
            hbuf[slot, pl.ds(c, n_tok, stride=ROW_TILES), :] = h[:, c * LANES:(c + 1) * LANES]

        def body(it, carry):
            for u in range(DMA_ROWS_PER_ITER):
                j = it * DMA_ROWS_PER_ITER + u
                for k in range(TOP_K):
                    pltpu.make_async_copy(hbuf.at[slot, _row_tile(j), :],
                                          h_hbm.at[_row_tile(rows_ref[k * n_tok + j]), :],
                                          sem.at[slot]).start(priority=k % N_DMA_THREADS)
            return carry
        lax.fori_loop(0, n_tok // DMA_ROWS_PER_ITER, body, 0)

    @pl.when(t >= 2)
    def _():
        wait_copies(slot, tm)

    @pl.when(t < n_tiles)
    def _():
        copy_rows(x_ref, pos_ref, tm)

    @pl.when(t == n_tiles)
    def _():
        copy_rows(xs_ref, poss_ref, n_s)
        wait_copies(slot, n_s)

        @pl.when(n_tiles >= 1)
        def _():
            wait_copies(1 - slot, tm)


def _row_tile_n(r, n):
    return pl.ds(pl.multiple_of(r * ROW_TILES, ROW_TILES), n * ROW_TILES)


def _scatter_rows(cnt, plan, x_p, pos_p, x_s, pos_s, norm_ffn, n_blocks, tm):
    n_p, d = x_p.shape
    n_s = x_s.shape[0]
    assert n_p % tm == 0 and tm % DMA_ROWS_PER_ITER == 0 and n_s % DMA_ROWS_PER_ITER == 0 and n_s <= tm
    nt = n_p // tm
    smem = lambda: pl.BlockSpec(memory_space=pltpu.SMEM)
    tile_idx = lambda i: jnp.minimum(i, nt - 1)
    return pl.pallas_call(
        functools.partial(_scatter_rows_kernel, n_blocks),
        grid=(nt + 1,),
        in_specs=[smem(), smem(), pl.BlockSpec((tm, d), lambda i: (tile_idx(i), 0)),
                  pl.BlockSpec((TOP_K * tm,), lambda i: (tile_idx(i),), memory_space=pltpu.SMEM),
                  pl.BlockSpec((n_s, d), lambda i: (0, 0)), smem(), pl.BlockSpec((1, d), lambda i: (0, 0))],
        out_specs=pl.BlockSpec(memory_space=pl.ANY),
        out_shape=jax.ShapeDtypeStruct((n_blocks * BM_EXPERT * ROW_TILES, LANES), F32),
        scratch_shapes=[pltpu.VMEM((2, tm * ROW_TILES, LANES), F32), pltpu.VMEM((BM_EXPERT * ROW_TILES, LANES), F32),
                        pltpu.SemaphoreType.DMA((2,)), pltpu.SemaphoreType.DMA((1,))],
        compiler_params=_cparams("arbitrary"),
        name="moe_scatter_rows",
    )(cnt, plan, x_p, pos_p, x_s, pos_s, norm_ffn)


WEIGHT_CAST_ROWS = 128


def _expert_blocks_kernel(n_blocks, plan_ref, x_ref, wgu_ref, bgu_ref, wd_ref, bd_ref, y_ref, wgu_b, wd_b):
    bm = BM_EXPERT
    i = pl.program_id(0)

    @pl.when((i == 0) | (plan_ref[i] != plan_ref[jnp.maximum(i - 1, 0)]))
    def _():
        for src, dst in ((wgu_ref, wgu_b), (wd_ref, wd_b)):
            for r0 in range(0, src.shape[1], WEIGHT_CAST_ROWS):
                rows = slice(r0, r0 + WEIGHT_CAST_ROWS)
                dst[rows, :] = src[0, rows, :].astype(BF16)

    @pl.when(i < plan_ref[n_blocks])
    def _():
        x = jnp.concatenate([x_ref[pl.ds(c, bm, stride=ROW_TILES), :] for c in range(ROW_TILES)], axis=1)
        gu = jnp.dot(x.astype(BF16), wgu_b[...], preferred_element_type=F32) + bgu_ref[0]
        gate = jnp.minimum(gu[:, :D_FF], SWIGLU_LIMIT)
        up = jnp.clip(gu[:, D_FF:], -SWIGLU_LIMIT, SWIGLU_LIMIT)
        act = gate * jax.nn.sigmoid(SWIGLU_ALPHA * gate) * (up + 1.0)
        y = jnp.dot(act.astype(BF16), wd_b[...], preferred_element_type=F32) + bd_ref[0]
        for c in range(ROW_TILES):
            y_ref[pl.ds(c, bm, stride=ROW_TILES), :] = y[:, c * LANES:(c + 1) * LANES]

    @pl.when(i >= plan_ref[n_blocks])
    def _():
        y_ref[...] = jnp.zeros(y_ref.shape, F32)


def _expert_blocks(plan, h_sorted, wgu, bgu, wd, bd, n_blocks):
    bm = BM_EXPERT
    by_expert = lambda *shape: pl.BlockSpec((1,) + shape, lambda i, plan: (plan[i],) + (0,) * len(shape))
    used = lambda i, plan: (jnp.minimum(i, plan[n_blocks] - 1), 0)
    grid_spec = pltpu.PrefetchScalarGridSpec(
        num_scalar_prefetch=1,
        grid=(n_blocks,),
        in_specs=[pl.BlockSpec((bm * ROW_TILES, LANES), used), by_expert(D_MODEL, 2 * D_FF), by_expert(1, 2 * D_FF),
                  by_expert(D_FF, D_MODEL), by_expert(1, D_MODEL)],
        out_specs=pl.BlockSpec((bm * ROW_TILES, LANES), lambda i, plan: (i, 0)),
        scratch_shapes=[pltpu.VMEM(wgu.shape[1:], BF16), pltpu.VMEM(wd.shape[1:], BF16)],
    )
    return pl.pallas_call(
        functools.partial(_expert_blocks_kernel, n_blocks),
        grid_spec=grid_spec,
        out_shape=jax.ShapeDtypeStruct(h_sorted.shape, F32),
        compiler_params=_cparams("arbitrary"),
        name="moe_experts",
    )(plan, h_sorted, wgu, bgu, wd, bd)


def _gather_combine_kernel(x_ref, pos_ref, pos_next_ref, gate_ref, g_ref, y_hbm, o_ref, ybuf, sem):
    t = pl.program_id(0)
    n_tiles = pl.num_programs(0)
    tm = x_ref.shape[0]
    slot = t % 2

    def gather(rows_ref, sl):
        def body(it, carry):
            for u in range(DMA_ROWS_PER_ITER):
                j = it * DMA_ROWS_PER_ITER + u
                for k in range(TOP_K):
                    pltpu.make_async_copy(y_hbm.at[_row_tile(rows_ref[k * tm + j]), :],
                                          ybuf.at[sl, _row_tile(k * tm + j), :],
                                          sem.at[sl]).start(priority=k % N_DMA_THREADS)
            return carry
        lax.fori_loop(0, tm // DMA_ROWS_PER_ITER, body, 0)

    @pl.when(t == 0)
    def _():
        gather(pos_ref, 0)

    @pl.when(t + 1 < n_tiles)
    def _():
        gather(pos_next_ref, 1 - slot)

    pltpu.make_async_copy(y_hbm.at[pl.ds(0, TOP_K * tm * ROW_TILES), :], ybuf.at[slot], sem.at[slot]).wait()
    gates = gate_ref[...]
    cols = []
    for c in range(ROW_TILES):
        acc = x_ref[:, c * LANES:(c + 1) * LANES]
        for k in range(TOP_K):
            acc = acc + gates[:, k:k + 1] * ybuf[slot, pl.ds(k * tm * ROW_TILES + c, tm, stride=ROW_TILES), :]
        cols.append(acc)
    o_ref[...] = _rms(jnp.concatenate(cols, axis=1), g_ref[...])


def _gather_combine(x, pos, gates, norm_final, y_sorted, tm):
    n, d = x.shape
    assert n % tm == 0 and tm % DMA_ROWS_PER_ITER == 0
    nt = n // tm
    rows = lambda off: pl.BlockSpec((TOP_K * tm,), lambda i: (jnp.minimum(i + off, nt - 1),),
                                    memory_space=pltpu.SMEM)
    return pl.pallas_call(
        _gather_combine_kernel,
        grid=(nt,),
        in_specs=[pl.BlockSpec((tm, d), lambda i: (i, 0)), rows(0), rows(1),
                  pl.BlockSpec((tm, SUBLANES), lambda i: (i, 0)), pl.BlockSpec((1, d), lambda i: (0, 0)),
                  pl.BlockSpec(memory_space=pl.ANY)],
        out_specs=pl.BlockSpec((tm, d), lambda i: (i, 0)),
        out_shape=jax.ShapeDtypeStruct((n, d), F32),
        scratch_shapes=[pltpu.VMEM((2, TOP_K * tm * ROW_TILES, LANES), F32), pltpu.SemaphoreType.DMA((2,))],
        compiler_params=_cparams("arbitrary"),
        name="moe_combine",
    )(x, pos, pos, gates, norm_final, y_sorted)


def _moe_and_final_norm(x_p, routing_p, x_s, routing_s, norm_ffn, w_gate_up, b_gate_up, w_down, b_down, norm_final):
    n_p, n_s = x_p.shape[0], x_s.shape[0]
    n_rows = (n_p + n_s) * TOP_K
    n_blocks = -(-(n_rows + N_EXPERTS * (BM_EXPERT - 1)) // BM_EXPERT)
    g_ffn = norm_ffn[None]
    e_s, r_s, gates_s, _ = routing_s
    e_p, r_p, gates_p, cnt = routing_p
    plan = _moe_plan(cnt[:, 0], n_blocks)
    pos_p = _positions(plan, e_p, r_p, n_blocks, TM_COMBINE)
    pos_s = _positions(plan, e_s, r_s, n_blocks, n_s)
    h_sorted = _scatter_rows(cnt[:, 0], plan, x_p, pos_p, x_s, pos_s, g_ffn, n_blocks, TM_COMBINE)
    y_sorted = _expert_blocks(plan, h_sorted, w_gate_up, b_gate_up[:, None, :], w_down, b_down[:, None, :], n_blocks)
    y_p = _gather_combine(x_p, pos_p, gates_p, norm_final[None], y_sorted, TM_COMBINE)
    y_s = _gather_combine(x_s, pos_s, gates_s, norm_final[None], y_sorted, n_s)
    return y_p, y_s


def _rb(x):
    return x.astype(BF16).astype(F32)


def _sample_proj_kernel(x_ref, g_ref, wm_ref, wg_ref, z_ref):
    hb = _rms(x_ref[...], g_ref[...]).astype(BF16)
    z_ref[:, :MIX_WIDTH] = jnp.dot(hb, wm_ref[...], preferred_element_type=F32)
    z_ref[:, MIX_WIDTH:] = jnp.dot(hb, wg_ref[...], preferred_element_type=F32)


def _sample_proj(x, norm_mix, w_mix, w_gate):
    n, d = x.shape
    args = (x, norm_mix, w_mix, w_gate)
    width = w_mix.shape[1] + w_gate.shape[1]
    return pl.pallas_call(
        _sample_proj_kernel,
        grid=(1,),
        in_specs=[pl.BlockSpec(a.shape, lambda j: (0, 0)) for a in args],
        out_specs=pl.BlockSpec((n, width), lambda j: (0, 0)),
        out_shape=jax.ShapeDtypeStruct((n, width), F32),
        compiler_params=_cparams("arbitrary"),
        name="sample_proj",
    )(*args)


def _sample_mix_kernel(z_ref, c0_ref, c1_ref, c2_ref, mem_ref, sb0_ref, sb1_ref, sb2_ref, nb_ref, lng_ref, lnb_ref,
                       wsp_ref, bsp_ref, bo_ref, co_ref, a_ref, v_ref, n0_ref, n1_ref, n2_ref):
    z = z_ref[0]
    scale = 1.0 / math.sqrt(B_HEAD_DIM)
    row_head = lax.broadcasted_iota(I32, (SUBLANES, B_GROUP_W), 0)
    lane_head = lax.broadcasted_iota(I32, (SUBLANES, B_GROUP_W), 1) // B_HEAD_DIM
    own = (row_head == lane_head).astype(F32)
    kv_cols = jnp.transpose(jnp.broadcast_to(z[:, B_WIDTH:3 * B_WIDTH], (SUBLANES, 2 * B_WIDTH)))[:, 0:1]
    outs, lses = [], []
    for g, (cache_ref, sb_ref, new_ref) in enumerate(((c0_ref, sb0_ref, n0_ref), (c1_ref, sb1_ref, n1_ref),
                                                      (c2_ref, sb2_ref, n2_ref))):
        window = cache_ref.shape[2]
        q, kn, vn = (_rb(z[:, i * B_WIDTH + g * B_GROUP_W:i * B_WIDTH + (g + 1) * B_GROUP_W]) for i in range(3))
        qh = own * q
        kt = cache_ref[0, :B_GROUP_W, :].astype(BF16)
        vt = cache_ref[0, B_GROUP_W:, :].astype(BF16)
        lg = jnp.dot(qh.astype(BF16), kt, preferred_element_type=F32) * scale + sb_ref[...]
        ln = jnp.sum(qh * kn, axis=1, keepdims=True) * scale + nb_ref[g, :, 0:1]
        m = jnp.maximum(jnp.max(lg, axis=1, keepdims=True), ln)
        lse = m + jnp.log(jnp.sum(jnp.exp(lg - m), axis=1, keepdims=True) + jnp.exp(ln - m))
        p = jnp.exp(lg - lse).astype(BF16)
        pn = _rb(jnp.exp(ln - lse))
        pv = lax.dot_general(p, vt, (((1,), (1,)), ((), ())), preferred_element_type=F32) + pn * vn
        outs.append(jnp.sum(own * pv, axis=0, keepdims=True))
        lses.append(jnp.sum(own * lse, axis=0, keepdims=True))
        new_col = jnp.concatenate([kv_cols[i * B_WIDTH + g * B_GROUP_W:i * B_WIDTH + (g + 1) * B_GROUP_W]
                                   for i in range(2)], axis=0)
        lane = lax.broadcasted_iota(I32, (1, window), 1)
        new_ref[0] = jnp.where(lane == window - 1, new_col, pltpu.roll(cache_ref[0], window - 1, axis=1))
    m = jnp.maximum(jnp.maximum(lses[0], lses[1]), lses[2])
    ws = [jnp.exp(l - m) for l in lses]
    den = ws[0] + ws[1] + ws[2]
    bo_ref[0] = _rb(ws[0] / den) * _rb(outs[0]) + _rb(ws[1] / den) * _rb(outs[1]) + _rb(ws[2] / den) * _rb(outs[2])

    cs = []
    for h in range(C_HEADS):
        c0 = MIX_WIDTH - C_WIDTH + h * C_HEAD_DIM
        qh = _rb(z[:, c0:c0 + C_HEAD_DIM])
        kh = _rb(mem_ref[0, pl.ds(h, N_MEM, stride=2 * C_HEADS), :])
        vh = _rb(mem_ref[0, pl.ds(C_HEADS + h, N_MEM, stride=2 * C_HEADS), :])
        s = jnp.sum(kh * qh, axis=-1, keepdims=True) * (1.0 / math.sqrt(C_HEAD_DIM))
        p = jnp.exp(s - jnp.max(s, axis=0, keepdims=True))
        p = _rb(p / jnp.sum(p, axis=0, keepdims=True))
        cs.append(jnp.sum(p * vh, axis=0, keepdims=True))
    co_ref[0] = jnp.concatenate(cs, axis=1)

    u = _gelu(z[:, 3 * B_WIDTH:3 * B_WIDTH + A_WIDTH])
    v = _layernorm(_gelu(z[:, 3 * B_WIDTH + A_WIDTH:3 * B_WIDTH + 2 * A_WIDTH]), lng_ref[...], lnb_ref[...])
    v_ref[0] = v
    a_ref[0] = u * (_rb(wsp_ref[...]) * _rb(v) + bsp_ref[...])


def _sample_mix(z, caches, mem, sbias, nbias, ln_g, ln_b, wsp0, bsp0):
    n = z.shape[0]
    const = lambda arr: pl.BlockSpec(arr.shape, lambda i: (0,) * arr.ndim)
    row = lambda w: pl.BlockSpec((1, 1, w), lambda i: (i, 0, 0))
    per_req = lambda arr: pl.BlockSpec((1,) + arr.shape[1:], lambda i: (i, 0, 0))
    widths = (B_GROUP_W, C_WIDTH, A_WIDTH, A_WIDTH)
    return pl.pallas_call(
        _sample_mix_kernel,
        grid=(n,),
        in_specs=[row(z.shape[2])] + [per_req(c) for c in caches] + [per_req(mem)]
        + [const(a) for a in (*sbias, nbias, ln_g, ln_b, wsp0, bsp0)],
        out_specs=tuple(row(w) for w in widths) + tuple(per_req(c) for c in caches),
        out_shape=tuple(jax.ShapeDtypeStruct((n, 1, w), F32) for w in widths)
        + tuple(jax.ShapeDtypeStruct(c.shape, F32) for c in caches),
        compiler_params=_cparams("arbitrary"),
        name="sample_mix",
    )(z, *caches, mem, *sbias, nbias, ln_g, ln_b, wsp0, bsp0)


def _sample_merge_kernel(x_ref, z_ref, a_ref, b_ref, c_ref, wa_ref, wb_ref, wc_ref, wo_ref, o_ref):
    dot = lambda a, b: jnp.dot(a.astype(BF16), b, preferred_element_type=F32)
    merged = None
    for i, (act_ref, w_ref) in enumerate(((a_ref, wa_ref), (b_ref, wb_ref), (c_ref, wc_ref))):
        gate = jax.nn.sigmoid(z_ref[:, MIX_WIDTH + i * D_MODEL:MIX_WIDTH + (i + 1) * D_MODEL])
        term = gate * dot(act_ref[...], w_ref[...])
        merged = term if merged is None else merged + term
    o_ref[...] = x_ref[...] + dot(merged, wo_ref[...])


def _sample_merge(x, z, a, bo, co, wa, wb, wc, wo):
    args = (x, z, a, bo, co, wa, wb, wc, wo)
    return pl.pallas_call(
        _sample_merge_kernel,
        grid=(1,),
        in_specs=[pl.BlockSpec(arr.shape, lambda i: (0, 0)) for arr in args],
        out_specs=pl.BlockSpec(x.shape, lambda i: (0, 0)),
        out_shape=jax.ShapeDtypeStruct(x.shape, F32),
        compiler_params=_cparams("arbitrary"),
        name="sample_merge",
    )(*args)


def _sample_mixers(x_sample, caches, cache_mem, sbias, nbias, norm_mix, ln_v_g, ln_v_b, w_spatial, b_spatial, wts):
    n = x_sample.shape[0]
    x = x_sample[:, 0]
    z = _sample_proj(x, norm_mix[None], wts["mix"], wts["gate"])
    mem = cache_mem.reshape(n, N_MEM * 2 * C_HEADS, C_HEAD_DIM)
    wsp0 = jnp.repeat(w_spatial[:, 0, 0], LANES)[None]
    bsp0 = jnp.repeat(b_spatial[:, 0], LANES)[None]
    bo, co, a, v, *new_caches = _sample_mix(z[:, None], [_from_window_layout(c) for c in caches], mem, sbias, nbias,
                                            ln_v_g[None], ln_v_b[None], wsp0, bsp0)
    x_new = _sample_merge(x, z, a[:, 0], bo[:, 0], co[:, 0], wts["a"], wts["b"], wts["c"], wts["out"])
    return x_new, [_to_window_layout(c) for c in new_caches], v[:, 0]


def _mixer_weights(w_in, w_branch_a, w_branch_b, w_branch_c, w_out):
    return {"mix": w_in[:, :MIX_WIDTH].astype(BF16), "gate": w_in[:, MIX_WIDTH:].astype(BF16),
            "a": w_branch_a.astype(BF16), "b": w_branch_b.astype(BF16), "c": w_branch_c.astype(BF16),
            "out": w_out.astype(BF16)}


def kernel(x_prompt, x_sample, cache_win0_kv, cache_win1_kv, cache_win2_kv, cache_mem_kv, mem_prompt, rel_bias,
           norm_mix, norm_mem, w_in, ln_v_g, ln_v_b, w_spatial, b_spatial, w_mem_kv, w_branch_a, w_branch_b,
           w_branch_c, w_out, norm_ffn, w_router, b_router, w_gate_up, b_gate_up, w_down, b_down, norm_final):
    assert norm_mix.shape[0] == 1, "one layer"
    b, s, d = x_prompt.shape
    n_s = x_sample.shape[0]
    caches = (cache_win0_kv[0], cache_win1_kv[0], cache_win2_kv[0])
    assert all(c.shape[1] == w for c, (w, _) in zip(caches, B_PAIRS)), "window buffers hold a full window"
    wts = _mixer_weights(w_in[0], w_branch_a[0], w_branch_b[0], w_branch_c[0], w_out[0])
    band_bias, *sbias, nbias = _bias_tables(rel_bias)
    route_params = _route_params(norm_ffn[0], w_router[0], b_router[0])
    xs, win_s, v_s = _sample_mixers(x_sample, caches, cache_mem_kv[0], sbias, nbias, norm_mix[0], ln_v_g[0],
                                    ln_v_b[0], w_spatial[0], b_spatial[0], wts)
    routing_s = _route(xs, route_params, jnp.zeros((N_EXPERTS, LANES), I32), n_s)
    xp, routing_p, win_p, mem_p = _prompt_mixers(x_prompt, mem_prompt, band_bias, norm_mix[0], norm_mem[0], ln_v_g[0],
                                                 ln_v_b[0], w_spatial[0], b_spatial[0], w_mem_kv[0], wts,
                                                 route_params, routing_s[3])
    y_p, y_s = _moe_and_final_norm(xp.reshape(b * s, d), routing_p, xs, routing_s, norm_ffn[0], w_gate_up[0],
                                   b_gate_up[0], w_down[0], b_down[0], norm_final)
    chunk_v = v_s.reshape(1, n_s, 1, A_GROUPS, LANES)
    return (y_p.reshape(b, s, d), y_s[:, None], win_p[0], win_p[1], win_p[2], mem_p,
            win_s[0], win_s[1], win_s[2], chunk_v)
```

```python
import functools
import math

import numpy as np
import jax
import jax.numpy as jnp
from jax import lax
from jax.experimental import pallas as pl
from jax.experimental.pallas import tpu as pltpu

F32 = jnp.float32
BF16 = jnp.bfloat16
I32 = jnp.int32

D_MODEL = 1024
N_MEM = 256
CHUNK = 128
A_GROUPS = 4
A_WIDTH = 512
B_PAIRS = ((128, 1), (512, 4), (2048, 16))
B_GROUP_W = 256
B_WIDTH = 768
B_HEAD_DIM = 64
BAND = 128
C_HEADS = 4
C_HEAD_DIM = 128
C_WIDTH = 512
REL_BUCKETS = 32
REL_MAX_DIST = 2048
N_EXPERTS = 32
TOP_K = 4
D_FF = 1024
SWIGLU_LIMIT = 7.0
SWIGLU_ALPHA = 1.702
EPS = 1e-6
NEG_INF = -1e30
MIX_WIDTH = 3 * B_WIDTH + 2 * A_WIDTH + C_WIDTH

LANES = 128
SUBLANES = 8
ROW_TILES = D_MODEL // LANES
VMEM_LIMIT = 56 * 1024 * 1024

TM_PROJ = 1024
TM_MERGE = 1024
BM_EXPERT = 512
TM_COMBINE = 512
ATTN_BLOCKS_PER_ITER = 8


def _cparams(*sem):
    return pltpu.CompilerParams(dimension_semantics=sem, vmem_limit_bytes=VMEM_LIMIT)


def _t5_bucket(dist):
    dist = np.maximum(np.asarray(dist), 0)
    max_exact = REL_BUCKETS // 2
    log_ratio = np.log(np.maximum(dist, max_exact) / max_exact) / math.log(REL_MAX_DIST / max_exact)
    large = np.minimum(max_exact + (log_ratio * (REL_BUCKETS - max_exact)).astype(np.int32), REL_BUCKETS - 1)
    return np.where(dist < max_exact, dist, large).astype(np.int32)


def _gelu(x):
    c = math.sqrt(2.0 / math.pi)
    return x * (0.5 * (1.0 + jnp.tanh(c * (x + 0.044715 * (x * x * x)))))


def _rms(x, g):
    return x * lax.rsqrt(jnp.mean(x * x, axis=-1, keepdims=True) + EPS) * g


def _layernorm(x, g, b):
    xc = x - jnp.mean(x, axis=-1, keepdims=True)
    return xc * lax.rsqrt(jnp.mean(xc * xc, axis=-1, keepdims=True) + EPS) * g + b


def _proj_kernel(x_ref, g_ref, w_ref, lng_ref, lnb_ref, wsp_ref, bsp_ref,
                 qkv0_ref, qkv1_ref, qkv2_ref, win0_ref, win1_ref, win2_ref, a_ref, qc_ref, scr_ref):
    tm = x_ref.shape[1]
    hb = _rms(x_ref[0], g_ref[...]).astype(BF16)

    def proj(c0, width):
        return jnp.dot(hb, w_ref[:, c0:c0 + width], preferred_element_type=F32)

    for g, (_, dil) in enumerate(B_PAIRS):
        q, k, v = (proj(i * B_WIDTH + g * B_GROUP_W, B_GROUP_W) for i in range(3))
        kv = jnp.concatenate([k, v], axis=1)
        window = min(B_PAIRS[g][0], tm)
        (win0_ref, win1_ref, win2_ref)[g][0] = jnp.transpose(kv[tm - window:])
        if dil == 1:
            qkv0_ref[0] = jnp.concatenate([q, k, v], axis=1).astype(BF16)
            continue
        out_ref = qkv1_ref if g == 1 else qkv2_ref
        for i, arr in enumerate((q, k, v)):
            for ct in range(2):
                scr_ref[2 * i + ct] = arr[:, ct * LANES:(ct + 1) * LANES]
        for r in range(dil):
            rows = [scr_ref[j, pl.ds(r, tm // dil, stride=dil), :] for j in range(6)]
            out_ref[0, r] = jnp.concatenate(rows, axis=1).astype(BF16)

    u = _gelu(proj(3 * B_WIDTH, A_WIDTH))
    v = _layernorm(_gelu(proj(3 * B_WIDTH + A_WIDTH, A_WIDTH)), lng_ref[...], lnb_ref[...]).astype(BF16)
    for c in range(tm // CHUNK):
        rs = slice(c * CHUNK, (c + 1) * CHUNK)
        for g in range(A_GROUPS):
            cs = slice(g * LANES, (g + 1) * LANES)
            s = jnp.dot(wsp_ref[g], v[rs, cs], preferred_element_type=F32) + bsp_ref[:, g:g + 1]
            a_ref[0, rs, cs] = (u[rs, cs] * s).astype(BF16)

    qc_ref[0] = proj(3 * B_WIDTH + 2 * A_WIDTH, C_WIDTH).astype(BF16)


def _prompt_proj(x, norm_mix, w_mix, ln_g, ln_b, wsp, bsp):
    b, s, d = x.shape
    tm = TM_PROJ
    assert s % tm == 0 and s >= B_PAIRS[2][0] and tm >= B_PAIRS[1][0]
    nt = s // tm
    const = lambda *shape: pl.BlockSpec(shape, lambda i, j: (0,) * len(shape))
    out_shape = (
        jax.ShapeDtypeStruct((b, s, 3 * B_GROUP_W), BF16),
        jax.ShapeDtypeStruct((b, 4, s // 4, 3 * B_GROUP_W), BF16),
        jax.ShapeDtypeStruct((b, 16, s // 16, 3 * B_GROUP_W), BF16),
        jax.ShapeDtypeStruct((b, 2 * B_GROUP_W, B_PAIRS[0][0]), F32),
        jax.ShapeDtypeStruct((b, 2 * B_GROUP_W, B_PAIRS[1][0]), F32),
        jax.ShapeDtypeStruct((b, 2 * B_GROUP_W, s), F32),
        jax.ShapeDtypeStruct((b, s, A_WIDTH), BF16),
        jax.ShapeDtypeStruct((b, s, C_WIDTH), BF16),
    )
    out_specs = (
        pl.BlockSpec((1, tm, 3 * B_GROUP_W), lambda i, j: (i, j, 0)),
        pl.BlockSpec((1, 4, tm // 4, 3 * B_GROUP_W), lambda i, j: (i, 0, j, 0)),
        pl.BlockSpec((1, 16, tm // 16, 3 * B_GROUP_W), lambda i, j: (i, 0, j, 0)),
        pl.BlockSpec((1, 2 * B_GROUP_W, B_PAIRS[0][0]), lambda i, j: (i, 0, 0)),
        pl.BlockSpec((1, 2 * B_GROUP_W, B_PAIRS[1][0]), lambda i, j: (i, 0, 0)),
        pl.BlockSpec((1, 2 * B_GROUP_W, tm), lambda i, j: (i, 0, j)),
        pl.BlockSpec((1, tm, A_WIDTH), lambda i, j: (i, j, 0)),
        pl.BlockSpec((1, tm, C_WIDTH), lambda i, j: (i, j, 0)),
    )
    return pl.pallas_call(
        _proj_kernel,
        grid=(b, nt),
        in_specs=[pl.BlockSpec((1, tm, d), lambda i, j: (i, j, 0)), const(1, d), const(d, MIX_WIDTH),
                  const(1, A_WIDTH), const(1, A_WIDTH), const(A_GROUPS, CHUNK, CHUNK), const(CHUNK, A_GROUPS)],
        out_specs=out_specs,
        out_shape=out_shape,
        scratch_shapes=[pltpu.VMEM((6, tm, LANES), F32)],
        compiler_params=_cparams("arbitrary", "arbitrary"),
        name="prompt_proj",
    )(x, norm_mix, w_mix, ln_g, ln_b, wsp, bsp)


def _attend(q, k, v, bias_ref, g, k0, hmask):
    kn = k.shape[0]
    qs = jnp.concatenate([q * hm for hm in hmask], axis=0)
    s = lax.dot_general(qs, k, (((1,), (1,)), ((), ())), preferred_element_type=F32) + bias_ref[g, :, k0:k0 + kn]
    m = jnp.max(s, axis=-1, keepdims=True)
    p = jnp.exp(s - m)
    l = jnp.sum(p, axis=-1, keepdims=True)
    pv = jnp.dot(p.astype(BF16), v, preferred_element_type=F32) / l
    lse_rows = m + jnp.log(l)
    out = jnp.zeros((BAND, B_GROUP_W), F32)
    lse = jnp.zeros((BAND, B_GROUP_W), F32)
    for h, hm in enumerate(hmask):
        rows = slice(h * BAND, (h + 1) * BAND)
        sel = hm > 0
        out = jnp.where(sel, pv[rows], out)
        lse = jnp.where(sel, lse_rows[rows], lse)
    return out, lse


def _attn_kernel(q0_ref, q1_ref, q2_ref, bias_ref, o_ref, out_ref, lse_ref):
    s = q0_ref.shape[1]
    lane_head = lax.broadcasted_iota(I32, (1, B_GROUP_W), 1) // B_HEAD_DIM
    scale = 1.0 / math.sqrt(B_HEAD_DIM)
    hmask = [jnp.where(lane_head == h, scale, 0.0).astype(BF16) for h in range(4)]
    qs, ks, vs = (slice(i * B_GROUP_W, (i + 1) * B_GROUP_W) for i in range(3))

    def store(g, start, dil, out, lse):
        rows = pl.ds(start, BAND) if dil == 1 else pl.ds(start, BAND, stride=dil)
        for ct in range(2):
            out_ref[g, ct, rows, :] = out[:, ct * LANES:(ct + 1) * LANES]
            lse_ref[g, ct, rows, :] = lse[:, ct * LANES:(ct + 1) * LANES]

    for g, (_, dil) in enumerate(B_PAIRS):
        n = s // dil
        nb = n // BAND

        def load(rows, cols, r, g=g):
            if g == 0:
                return q0_ref[0, rows, cols]
            return (q1_ref if g == 1 else q2_ref)[0, r, rows, cols]

        def first_block(r, g=g, dil=dil, load=load):
            rows = pl.ds(0, BAND)
            out, lse = _attend(load(rows, qs, r), load(rows, ks, r), load(rows, vs, r), bias_ref, g, BAND, hmask)
            store(g, r, dil, out, lse)

        def later_block(i, g=g, dil=dil, nb=nb, load=load):
            r = i // (nb - 1)
            qb = i % (nb - 1) + 1
            q0 = pl.multiple_of(qb * BAND, BAND)
            rows_q = pl.ds(q0, BAND)
            rows_k = pl.ds(q0 - BAND, 2 * BAND)
            out, lse = _attend(load(rows_q, qs, r), load(rows_k, ks, r), load(rows_k, vs, r), bias_ref, g, 0, hmask)
            store(g, qb * (BAND * dil) + r, dil, out, lse)

        def run(block_fn, count):
            per = ATTN_BLOCKS_PER_ITER

            def body(it, carry):
                for u in range(per):
                    block_fn(it * per + u)
                return carry
            lax.fori_loop(0, count // per, body, 0)
            for i in range(count - count % per, count):
                block_fn(i)

        run(first_block, dil)
        run(later_block, dil * (nb - 1))

    rc = 256
    def combine(c, carry):
        rows = pl.ds(pl.multiple_of(c * rc, rc), rc)
        for ct in range(2):
            ls = [lse_ref[g, ct, rows, :] for g in range(3)]
            m = jnp.maximum(jnp.maximum(ls[0], ls[1]), ls[2])
            ws = [jnp.exp(l - m) for l in ls]
            den = ws[0] + ws[1] + ws[2]
            num = ws[0] * out_ref[0, ct, rows, :] + ws[1] * out_ref[1, ct, rows, :] + ws[2] * out_ref[2, ct, rows, :]
            o_ref[0, rows, ct * LANES:(ct + 1) * LANES] = (num / den).astype(BF16)
        return carry
    lax.fori_loop(0, s // rc, combine, 0)


def _prompt_attention(qkv0, qkv1, qkv2, bias):
    b, s, w = qkv0.shape
    return pl.pallas_call(
        _attn_kernel,
        grid=(b,),
        in_specs=[pl.BlockSpec((1, s, w), lambda i: (i, 0, 0)),
                  pl.BlockSpec((1, 4, s // 4, w), lambda i: (i, 0, 0, 0)),
                  pl.BlockSpec((1, 16, s // 16, w), lambda i: (i, 0, 0, 0)),
                  pl.BlockSpec(bias.shape, lambda i: (0, 0, 0))],
        out_specs=pl.BlockSpec((1, s, B_GROUP_W), lambda i: (i, 0, 0)),
        out_shape=jax.ShapeDtypeStruct((b, s, B_GROUP_W), BF16),
        scratch_shapes=[pltpu.VMEM((3, 2, s, LANES), F32), pltpu.VMEM((3, 2, s, LANES), F32)],
        compiler_params=_cparams("arbitrary"),
        name="prompt_attention",
    )(qkv0, qkv1, qkv2, bias)


def _bias_kernel(rel_ref, band_idx_ref, s0_ref, s1_ref, s2_ref, band_ref, sb0_ref, sb1_ref, sb2_ref, nb_ref):
    def lookup(idx, col):
        acc = jnp.full(idx.shape, NEG_INF, F32)
        for bucket in range(REL_BUCKETS):
            acc = jnp.where(idx == bucket, rel_ref[bucket, col], acc)
        return acc

    for g, (s_ref, sb_ref) in enumerate(((s0_ref, sb0_ref), (s1_ref, sb1_ref), (s2_ref, sb2_ref))):
        width = s_ref.shape[1]
        for h in range(4):
            band_ref[g, h * BAND:(h + 1) * BAND, :] = lookup(band_idx_ref[g], 4 * g + h)
        sb_ref[...] = jnp.concatenate([lookup(s_ref[...], 4 * g + h) for h in range(4)]
                                      + [jnp.full((SUBLANES - 4, width), NEG_INF, F32)], axis=0)
        nb_ref[g] = jnp.concatenate([jnp.full((1, LANES), rel_ref[0, 4 * g + h], F32) for h in range(4)]
                                    + [jnp.zeros((SUBLANES - 4, LANES), F32)], axis=0)


def _bias_tables(rel_bias):
    steps = np.arange(BAND)[:, None] + BAND - np.arange(2 * BAND)[None, :]
    valid = (steps >= 0) & (steps <= BAND)
    band_idx = np.stack([np.where(valid, _t5_bucket(np.clip(steps, 0, BAND) * dil), -1) for _, dil in B_PAIRS])
    s_idx = []
    for window, dil in B_PAIRS:
        w = np.arange(window)
        s_idx.append(np.where(w % dil == 0, _t5_bucket(window - w), -1)[None].astype(np.int32))
    args = (rel_bias, jnp.asarray(band_idx.astype(np.int32))) + tuple(jnp.asarray(s) for s in s_idx)
    vmem = lambda a: pl.BlockSpec(a.shape, lambda i: (0,) * a.ndim)
    out_shape = (jax.ShapeDtypeStruct((3, 4 * BAND, 2 * BAND), F32),) + tuple(
        jax.ShapeDtypeStruct((SUBLANES, window), F32) for window, _ in B_PAIRS) + (
        jax.ShapeDtypeStruct((3, SUBLANES, LANES), F32),)
    return pl.pallas_call(
        _bias_kernel,
        grid=(1,),
        in_specs=[pl.BlockSpec(memory_space=pltpu.SMEM)] + [vmem(a) for a in args[1:]],
        out_specs=tuple(pl.BlockSpec(s.shape, lambda i, n=len(s.shape): (0,) * n) for s in out_shape),
        out_shape=out_shape,
        compiler_params=_cparams("arbitrary"),
        name="bias_tables",
    )(*args)


def _memkv_kernel(mem_ref, g_ref, w_ref, kv_ref, kvb_ref):
    hb = _rms(mem_ref[0], g_ref[...]).astype(BF16)
    kv = jnp.dot(hb, w_ref[...], preferred_element_type=F32)
    m = kv.shape[0]
    n_heads = kv.shape[1] // C_HEAD_DIM
    for j in range(n_heads):
        kv_ref[0, pl.ds(j, m, stride=n_heads), :] = kv[:, j * C_HEAD_DIM:(j + 1) * C_HEAD_DIM]
    kvb_ref[0] = kv.astype(BF16)


def _memory_kv(mem, norm_mem, w_mem):
    b, m, d = mem.shape
    w = w_mem.shape[1]
    rows = m * w // C_HEAD_DIM
    return pl.pallas_call(
        _memkv_kernel,
        grid=(b,),
        in_specs=[pl.BlockSpec((1, m, d), lambda i: (i, 0, 0)), pl.BlockSpec((1, d), lambda i: (0, 0)),
                  pl.BlockSpec((d, w), lambda i: (0, 0))],
        out_specs=(pl.BlockSpec((1, rows, C_HEAD_DIM), lambda i: (i, 0, 0)),
                   pl.BlockSpec((1, m, w), lambda i: (i, 0, 0))),
        out_shape=(jax.ShapeDtypeStruct((b, rows, C_HEAD_DIM), F32), jax.ShapeDtypeStruct((b, m, w), BF16)),
        compiler_params=_cparams("arbitrary"),
        name="memory_kv",
    )(mem, norm_mem, w_mem)


def _merge_kernel(x_ref, g_ref, a_ref, b_ref, qc_ref, kv_ref, wg_ref, wa_ref, wb_ref, wc_ref, wo_ref,
                  gf_ref, wr_ref, br_ref, cnt_in_ref, o_ref, e_ref, rank_ref, gate_ref, cnt_ref, carry_ref):
    @pl.when((pl.program_id(0) == 0) & (pl.program_id(1) == 0))
    def _():
        carry_ref[...] = cnt_in_ref[...].astype(F32)

    x = x_ref[0]
    hb = _rms(x, g_ref[...]).astype(BF16)
    qc = qc_ref[0]
    cs = []
    for h in range(C_HEADS):
        hs = slice(h * C_HEAD_DIM, (h + 1) * C_HEAD_DIM)
        s = lax.dot_general(qc[:, hs], kv_ref[0, :, hs], (((1,), (1,)), ((), ())), preferred_element_type=F32)
        s = s * (1.0 / math.sqrt(C_HEAD_DIM))
        p = jnp.exp(s - jnp.max(s, axis=-1, keepdims=True))
        p = p / jnp.sum(p, axis=-1, keepdims=True)
        vs = slice(C_WIDTH + h * C_HEAD_DIM, C_WIDTH + (h + 1) * C_HEAD_DIM)
        cs.append(jnp.dot(p.astype(BF16), kv_ref[0, :, vs], preferred_element_type=F32))
    c = jnp.concatenate(cs, axis=1).astype(BF16)
    branches = ((a_ref[0], wa_ref), (b_ref[0], wb_ref), (c, wc_ref))
    merged = None
    for i, (act, w_ref) in enumerate(branches):
        gate = jax.nn.sigmoid(jnp.dot(hb, wg_ref[:, i * D_MODEL:(i + 1) * D_MODEL], preferred_element_type=F32))
        term = gate * jnp.dot(act, w_ref[...], preferred_element_type=F32)
        merged = term if merged is None else merged + term
    x_new = x + jnp.dot(merged.astype(BF16), wo_ref[...], preferred_element_type=F32)
    o_ref[0] = x_new
    _route_rows(x_new, gf_ref, wr_ref, br_ref, e_ref, rank_ref, gate_ref, cnt_ref, carry_ref)


def _prompt_merge(x, norm_mix, a, bo, qc, kvb, wg, wa, wb, wc, wo, route_params, cnt_in):
    b, s, d = x.shape
    tm = TM_MERGE
    nt = s // tm
    n = b * s
    tile = lambda w: pl.BlockSpec((1, tm, w), lambda i, j: (i, j, 0))
    const = lambda arr: pl.BlockSpec(arr.shape, lambda i, j: (0,) * arr.ndim)
    ids = pl.BlockSpec((TOP_K, tm), lambda i, j: (0, i * nt + j))
    return pl.pallas_call(
        _merge_kernel,
        grid=(b, nt),
        in_specs=[tile(d), const(norm_mix), tile(A_WIDTH), tile(B_GROUP_W), tile(C_WIDTH),
                  pl.BlockSpec((1,) + kvb.shape[1:], lambda i, j: (i, 0, 0)),
                  const(wg), const(wa), const(wb), const(wc), const(wo)]
        + [const(p) for p in route_params] + [const(cnt_in)],
        out_specs=(tile(d), ids, ids, pl.BlockSpec((tm, SUBLANES), lambda i, j: (i * nt + j, 0)),
                   pl.BlockSpec((N_EXPERTS, LANES), lambda i, j: (0, 0))),
        out_shape=(jax.ShapeDtypeStruct((b, s, d), F32), jax.ShapeDtypeStruct((TOP_K, n), I32),
                   jax.ShapeDtypeStruct((TOP_K, n), I32), jax.ShapeDtypeStruct((n, SUBLANES), F32),
                   jax.ShapeDtypeStruct((N_EXPERTS, LANES), I32)),
        scratch_shapes=[pltpu.VMEM((N_EXPERTS, LANES), F32)],
        compiler_params=_cparams("arbitrary", "arbitrary"),
        name="prompt_merge",
    )(x, norm_mix, a, bo, qc, kvb, wg, wa, wb, wc, wo, *route_params, cnt_in)


def _to_window_layout(w):
    b, _, window = w.shape
    return jnp.transpose(w.reshape(b, 2, 4, B_HEAD_DIM, window), (0, 4, 1, 2, 3))[None]


def _from_window_layout(c):
    b, window = c.shape[:2]
    return jnp.transpose(c, (0, 2, 3, 4, 1)).reshape(b, 2 * B_GROUP_W, window)


def _prompt_mixers(x_prompt, mem_prompt, band_bias, norm_mix, norm_mem, ln_v_g, ln_v_b, w_spatial, b_spatial,
                   w_mem_kv, wts, route_params, cnt_in):
    b, s, _ = x_prompt.shape
    wsp = jnp.where(np.tril(np.ones((CHUNK, CHUNK), bool)), w_spatial, 0.0).astype(BF16)
    qkv0, qkv1, qkv2, win0, win1, win2, a_out, qc = _prompt_proj(
        x_prompt, norm_mix[None], wts["mix"], ln_v_g[None], ln_v_b[None], wsp, b_spatial.T)
    b_out = _prompt_attention(qkv0, qkv1, qkv2, band_bias)
    mem_kv, mem_kv_b = _memory_kv(mem_prompt, norm_mem[None], w_mem_kv.astype(BF16))
    x_new, *routing = _prompt_merge(x_prompt, norm_mix[None], a_out, b_out, qc, mem_kv_b, wts["gate"],
                                    wts["a"], wts["b"], wts["c"], wts["out"], route_params, cnt_in)
    wins = [_to_window_layout(w) for w in (win0, win1, win2)]
    return x_new, routing, wins, mem_kv.reshape(1, b, N_MEM, 2, C_HEADS, C_HEAD_DIM)


def _route_kernel(x_ref, g_ref, wr_ref, br_ref, cnt_in_ref, e_ref, rank_ref, gate_ref, cnt_ref, carry_ref):
    @pl.when(pl.program_id(0) == 0)
    def _():
        carry_ref[...] = cnt_in_ref[...].astype(F32)

    _route_rows(x_ref[...], g_ref, wr_ref, br_ref, e_ref, rank_ref, gate_ref, cnt_ref, carry_ref)


def _route_rows(x, g_ref, wr_ref, br_ref, e_ref, rank_ref, gate_ref, cnt_ref, carry_ref):
    tm = x.shape[0]
    h = _rms(x, g_ref[...])
    logits = lax.dot_general(wr_ref[...], h.astype(BF16), (((1,), (1,)), ((), ())),
                             preferred_element_type=F32) + br_ref[...]
    expert = lax.broadcasted_iota(I32, logits.shape, 0)
    vals, idxs = [], []
    member = jnp.zeros(logits.shape, F32)
    for _ in range(TOP_K):
        m = jnp.max(logits, axis=0, keepdims=True)
        idx = jnp.min(jnp.where(logits == m, expert, N_EXPERTS), axis=0, keepdims=True)
        hit = expert == idx
        vals.append(m)
        idxs.append(idx)
        member = jnp.where(hit, 1.0, member)
        logits = jnp.where(hit, -jnp.inf, logits)
    p = [jnp.exp(v - vals[0]) for v in vals]
    den = p[0] + p[1] + p[2] + p[3]
    gates = jnp.concatenate([pk / den for pk in p] + [jnp.zeros((SUBLANES - TOP_K, tm), F32)], axis=0)
    gate_ref[...] = jnp.transpose(gates)
    e_ref[...] = jnp.concatenate(idxs, axis=0)
    before = (lax.broadcasted_iota(I32, (tm, tm), 0) < lax.broadcasted_iota(I32, (tm, tm), 1)).astype(BF16)
    prefix = jnp.dot(member.astype(BF16), before, preferred_element_type=F32) + carry_ref[:, 0:1]
    rank_ref[...] = jnp.concatenate(
        [jnp.sum(jnp.where(expert == idx, prefix, 0.0), axis=0, keepdims=True) for idx in idxs], axis=0).astype(I32)
    carry_ref[...] = carry_ref[...] + jnp.sum(member, axis=1, keepdims=True)
    cnt_ref[...] = carry_ref[...].astype(I32)


def _route_params(norm_ffn, w_router, b_router):
    return norm_ffn[None], w_router.T.astype(BF16), b_router[:, None]


def _route(x, route_params, cnt_in, tm):
    n, d = x.shape
    assert n % tm == 0
    norm_ffn, w_router_t, b_router = route_params
    const = lambda arr: pl.BlockSpec(arr.shape, lambda i: (0,) * arr.ndim)
    return pl.pallas_call(
        _route_kernel,
        grid=(n // tm,),
        in_specs=[pl.BlockSpec((tm, d), lambda i: (i, 0)), const(norm_ffn), const(w_router_t), const(b_router),
                  const(cnt_in)],
        out_specs=(pl.BlockSpec((TOP_K, tm), lambda i: (0, i)), pl.BlockSpec((TOP_K, tm), lambda i: (0, i)),
                   pl.BlockSpec((tm, SUBLANES), lambda i: (i, 0)), pl.BlockSpec((N_EXPERTS, LANES), lambda i: (0, 0))),
        out_shape=(jax.ShapeDtypeStruct((TOP_K, n), I32), jax.ShapeDtypeStruct((TOP_K, n), I32),
                   jax.ShapeDtypeStruct((n, SUBLANES), F32), jax.ShapeDtypeStruct((N_EXPERTS, LANES), I32)),
        scratch_shapes=[pltpu.VMEM((N_EXPERTS, LANES), F32)],
        compiler_params=_cparams("arbitrary"),
        name="moe_route",
    )(x, norm_ffn, w_router_t, b_router, cnt_in)


DMA_ROWS_PER_ITER = 8
N_DMA_THREADS = 2
N_PLAN_TAIL = 1 + N_EXPERTS


def _row_tile(r):
    return pl.ds(pl.multiple_of(r * ROW_TILES, ROW_TILES), ROW_TILES)


def _plan_kernel(n_blocks, cnt_ref, plan_ref):
    shift = BM_EXPERT.bit_length() - 1

    def per_expert(e, carry):
        blk0, last_e = carry
        nb = (cnt_ref[e] + (BM_EXPERT - 1)) >> shift
        plan_ref[n_blocks + 1 + e] = blk0 << shift

        def fill(j, cc):
            plan_ref[blk0 + j] = e
            return cc
        lax.fori_loop(0, nb, fill, 0)
        return blk0 + nb, jnp.where(nb > 0, e, last_e)
    n_used, last_e = lax.fori_loop(0, N_EXPERTS, per_expert, (0, 0))

    def tail(j, cc):
        plan_ref[j] = last_e
        return cc
    lax.fori_loop(n_used, n_blocks, tail, 0)
    plan_ref[n_blocks] = n_used


def _moe_plan(cnt, n_blocks):
    smem = pl.BlockSpec(memory_space=pltpu.SMEM)
    return pl.pallas_call(
        functools.partial(_plan_kernel, n_blocks),
        in_specs=[smem], out_specs=smem,
        out_shape=jax.ShapeDtypeStruct((n_blocks + N_PLAN_TAIL,), I32),
        name="moe_plan",
    )(cnt)


POSITION_TILES_PER_STEP = 16


def _positions_kernel(n_blocks, plan_ref, e_ref, r_ref, pos_ref):
    tiles, _, tm = pos_ref.shape
    e = e_ref[...]
    first = jnp.zeros(e.shape, I32)
    for ex in range(N_EXPERTS):
        first = jnp.where(e == ex, plan_ref[n_blocks + 1 + ex], first)
    pos = first + r_ref[...]
    for i in range(tiles):
        pos_ref[i] = pos[:, i * tm:(i + 1) * tm]


def _positions(plan, e, r, n_blocks, tm):
    n = e.shape[1]
    assert n % tm == 0
    nt = n // tm
    per_step = math.gcd(nt, POSITION_TILES_PER_STEP)
    pos = pl.pallas_call(
        functools.partial(_positions_kernel, n_blocks),
        grid=(nt // per_step,),
        in_specs=[pl.BlockSpec(memory_space=pltpu.SMEM), pl.BlockSpec((TOP_K, per_step * tm), lambda i: (0, i)),
                  pl.BlockSpec((TOP_K, per_step * tm), lambda i: (0, i))],
        out_specs=pl.BlockSpec((per_step, TOP_K, tm), lambda i: (i, 0, 0)),
        out_shape=jax.ShapeDtypeStruct((nt, TOP_K, tm), I32),
        compiler_params=_cparams("arbitrary"),
        name="moe_positions",
    )(plan, e, r)
    return pos.reshape(-1)


def _scatter_rows_kernel(n_blocks, cnt_ref, plan_ref, x_ref, pos_ref, xs_ref, poss_ref, g_ref, h_hbm,
                         hbuf, zbuf, sem, zsem):
    t = pl.program_id(0)
    n_tiles = pl.num_programs(0) - 1
    tm = x_ref.shape[0]
    n_s = xs_ref.shape[0]
    bm = BM_EXPERT
    shift = bm.bit_length() - 1
    slot = t % 2

    def zero_rows(first, count, wait):
        for bit in range(shift):
            n = 1 << bit
            lo = first + (count & (n - 1))

            @pl.when((count >> bit) & 1 == 1)
            def _():
                cp = pltpu.make_async_copy(zbuf.at[pl.ds(0, n * ROW_TILES), :], h_hbm.at[_row_tile_n(lo, n), :],
                                           zsem.at[0])
                cp.wait() if wait else cp.start()

    def zero_block(blk, wait):
        cp = pltpu.make_async_copy(zbuf, h_hbm.at[_row_tile_n(blk * bm, bm), :], zsem.at[0])
        cp.wait() if wait else cp.start()

    @pl.when(t == 0)
    def _():
        zbuf[...] = jnp.zeros(zbuf.shape, F32)
        for wait in (False, True):
            def pad_expert(e, cc, wait=wait):
                cnt = cnt_ref[e]
                zero_rows(plan_ref[n_blocks + 1 + e] + cnt, (-cnt) & (bm - 1), wait)
                return cc
            lax.fori_loop(0, N_EXPERTS, pad_expert, 0)

            def pad_block(blk, cc, wait=wait):
                zero_block(blk, wait)
                return cc
            lax.fori_loop(plan_ref[n_blocks], n_blocks, pad_block, 0)

    def wait_copies(sl, n_tok):
        for _ in range(TOP_K):
            pltpu.make_async_copy(hbuf.at[sl, pl.ds(0, n_tok * ROW_TILES), :],
                                  h_hbm.at[pl.ds(0, n_tok * ROW_TILES), :], sem.at[sl]).wait()

    def copy_rows(src_ref, rows_ref, n_tok):
        h = _rms(src_ref[...], g_ref[...])
        for c in range(ROW_TILES):
            hbuf[slot, pl.ds(c, n_tok, stride=ROW_TILES), :] = h[:, c * LANES:(c + 1) * LANES]

        def body(it, carry):
            for u in range(DMA_ROWS_PER_ITER):
                j = it * DMA_ROWS_PER_ITER + u
                for k in range(TOP_K):
                    pltpu.make_async_copy(hbuf.at[slot, _row_tile(j), :],
                                          h_hbm.at[_row_tile(rows_ref[k * n_tok + j]), :],
                                          sem.at[slot]).start(priority=k % N_DMA_THREADS)
            return carry
        lax.fori_loop(0, n_tok // DMA_ROWS_PER_ITER, body, 0)

    @pl.when(t >= 2)
    def _():
        wait_copies(slot, tm)

    @pl.when(t < n_tiles)
    def _():
        copy_rows(x_ref, pos_ref, tm)

    @pl.when(t == n_tiles)
    def _():
        copy_rows(xs_ref, poss_ref, n_s)
        wait_copies(slot, n_s)

        @pl.when(n_tiles >= 1)
        def _():
            wait_copies(1 - slot, tm)


def _row_tile_n(r, n):
    return pl.ds(pl.multiple_of(r * ROW_TILES, ROW_TILES), n * ROW_TILES)


def _scatter_rows(cnt, plan, x_p, pos_p, x_s, pos_s, norm_ffn, n_blocks, tm):
    n_p, d = x_p.shape
    n_s = x_s.shape[0]
    assert n_p % tm == 0 and tm % DMA_ROWS_PER_ITER == 0 and n_s % DMA_ROWS_PER_ITER == 0 and n_s <= tm
    nt = n_p // tm
    smem = lambda: pl.BlockSpec(memory_space=pltpu.SMEM)
    tile_idx = lambda i: jnp.minimum(i, nt - 1)
    return pl.pallas_call(
        functools.partial(_scatter_rows_kernel, n_blocks),
        grid=(nt + 1,),
        in_specs=[smem(), smem(), pl.BlockSpec((tm, d), lambda i: (tile_idx(i), 0)),
                  pl.BlockSpec((TOP_K * tm,), lambda i: (tile_idx(i),), memory_space=pltpu.SMEM),
                  pl.BlockSpec((n_s, d), lambda i: (0, 0)), smem(), pl.BlockSpec((1, d), lambda i: (0, 0))],
        out_specs=pl.BlockSpec(memory_space=pl.ANY),
        out_shape=jax.ShapeDtypeStruct((n_blocks * BM_EXPERT * ROW_TILES, LANES), F32),
        scratch_shapes=[pltpu.VMEM((2, tm * ROW_TILES, LANES), F32), pltpu.VMEM((BM_EXPERT * ROW_TILES, LANES), F32),
                        pltpu.SemaphoreType.DMA((2,)), pltpu.SemaphoreType.DMA((1,))],
        compiler_params=_cparams("arbitrary"),
        name="moe_scatter_rows",
    )(cnt, plan, x_p, pos_p, x_s, pos_s, norm_ffn)


FF_SPLIT = 2
WEIGHT_CAST_ROWS = 128


def _expert_blocks_kernel(n_blocks, plan_ref, x_ref, wgu_ref, bgu_ref, wd_ref, bd_ref, y_ref, wgu_b, wd_b):
    bm = BM_EXPERT
    i = pl.program_id(0)

    @pl.when((i == 0) | (plan_ref[i] != plan_ref[jnp.maximum(i - 1, 0)]))
    def _():
        for src, dst in ((wgu_ref, wgu_b), (wd_ref, wd_b)):
            for r0 in range(0, src.shape[1], WEIGHT_CAST_ROWS):
                rows = slice(r0, r0 + WEIGHT_CAST_ROWS)
                dst[rows, :] = src[0, rows, :].astype(BF16)

    @pl.when(i < plan_ref[n_blocks])
    def _():
        x = jnp.concatenate([x_ref[pl.ds(c, bm, stride=ROW_TILES), :] for c in range(ROW_TILES)], axis=1)
        xb = x.astype(BF16)
        y = bd_ref[0]
        half = D_FF // FF_SPLIT
        for j in range(FF_SPLIT):
            g_cols = slice(j * half, (j + 1) * half)
            u_cols = slice(D_FF + j * half, D_FF + (j + 1) * half)
            gate = jnp.dot(xb, wgu_b[:, g_cols], preferred_element_type=F32) + bgu_ref[0, :, g_cols]
            up = jnp.dot(xb, wgu_b[:, u_cols], preferred_element_type=F32) + bgu_ref[0, :, u_cols]
            gate = jnp.minimum(gate, SWIGLU_LIMIT)
            up = jnp.clip(up, -SWIGLU_LIMIT, SWIGLU_LIMIT)
            act = gate * jax.nn.sigmoid(SWIGLU_ALPHA * gate) * (up + 1.0)
            y = y + jnp.dot(act.astype(BF16), wd_b[g_cols, :], preferred_element_type=F32)
        for c in range(ROW_TILES):
            y_ref[pl.ds(c, bm, stride=ROW_TILES), :] = y[:, c * LANES:(c + 1) * LANES]

    @pl.when(i >= plan_ref[n_blocks])
    def _():
        y_ref[...] = jnp.zeros(y_ref.shape, F32)


def _expert_blocks(plan, h_sorted, wgu, bgu, wd, bd, n_blocks):
    bm = BM_EXPERT
    by_expert = lambda *shape: pl.BlockSpec((1,) + shape, lambda i, plan: (plan[i],) + (0,) * len(shape))
    used = lambda i, plan: (jnp.minimum(i, plan[n_blocks] - 1), 0)
    grid_spec = pltpu.PrefetchScalarGridSpec(
        num_scalar_prefetch=1,
        grid=(n_blocks,),
        in_specs=[pl.BlockSpec((bm * ROW_TILES, LANES), used), by_expert(D_MODEL, 2 * D_FF), by_expert(1, 2 * D_FF),
                  by_expert(D_FF, D_MODEL), by_expert(1, D_MODEL)],
        out_specs=pl.BlockSpec((bm * ROW_TILES, LANES), lambda i, plan: (i, 0)),
        scratch_shapes=[pltpu.VMEM(wgu.shape[1:], BF16), pltpu.VMEM(wd.shape[1:], BF16)],
    )
    return pl.pallas_call(
        functools.partial(_expert_blocks_kernel, n_blocks),
        grid_spec=grid_spec,
        out_shape=jax.ShapeDtypeStruct(h_sorted.shape, F32),
        compiler_params=_cparams("arbitrary"),
        name="moe_experts",
    )(plan, h_sorted, wgu, bgu, wd, bd)


def _gather_combine_kernel(x_ref, pos_ref, pos_next_ref, gate_ref, g_ref, y_hbm, o_ref, ybuf, sem):
    t = pl.program_id(0)
    n_tiles = pl.num_programs(0)
    tm = x_ref.shape[0]
    slot = t % 2

    def gather(rows_ref, sl):
        def body(it, carry):
            for u in range(DMA_ROWS_PER_ITER):
                j = it * DMA_ROWS_PER_ITER + u
                for k in range(TOP_K):
                    pltpu.make_async_copy(y_hbm.at[_row_tile(rows_ref[k * tm + j]), :],
                                          ybuf.at[sl, _row_tile(k * tm + j), :],
                                          sem.at[sl]).start(priority=k % N_DMA_THREADS)
            return carry
        lax.fori_loop(0, tm // DMA_ROWS_PER_ITER, body, 0)

    @pl.when(t == 0)
    def _():
        gather(pos_ref, 0)

    @pl.when(t + 1 < n_tiles)
    def _():
        gather(pos_next_ref, 1 - slot)

    pltpu.make_async_copy(y_hbm.at[pl.ds(0, TOP_K * tm * ROW_TILES), :], ybuf.at[slot], sem.at[slot]).wait()
    gates = gate_ref[...]
    cols = []
    for c in range(ROW_TILES):
        acc = x_ref[:, c * LANES:(c + 1) * LANES]
        for k in range(TOP_K):
            acc = acc + gates[:, k:k + 1] * ybuf[slot, pl.ds(k * tm * ROW_TILES + c, tm, stride=ROW_TILES), :]
        cols.append(acc)
    o_ref[...] = _rms(jnp.concatenate(cols, axis=1), g_ref[...])


def _gather_combine(x, pos, gates, norm_final, y_sorted, tm):
    n, d = x.shape
    assert n % tm == 0 and tm % DMA_ROWS_PER_ITER == 0
    nt = n // tm
    rows = lambda off: pl.BlockSpec((TOP_K * tm,), lambda i: (jnp.minimum(i + off, nt - 1),),
                                    memory_space=pltpu.SMEM)
    return pl.pallas_call(
        _gather_combine_kernel,
        grid=(nt,),
        in_specs=[pl.BlockSpec((tm, d), lambda i: (i, 0)), rows(0), rows(1),
                  pl.BlockSpec((tm, SUBLANES), lambda i: (i, 0)), pl.BlockSpec((1, d), lambda i: (0, 0)),
                  pl.BlockSpec(memory_space=pl.ANY)],
        out_specs=pl.BlockSpec((tm, d), lambda i: (i, 0)),
        out_shape=jax.ShapeDtypeStruct((n, d), F32),
        scratch_shapes=[pltpu.VMEM((2, TOP_K * tm * ROW_TILES, LANES), F32), pltpu.SemaphoreType.DMA((2,))],
        compiler_params=_cparams("arbitrary"),
        name="moe_combine",
    )(x, pos, pos, gates, norm_final, y_sorted)


def _moe_and_final_norm(x_p, routing_p, x_s, routing_s, norm_ffn, w_gate_up, b_gate_up, w_down, b_down, norm_final):
    n_p, n_s = x_p.shape[0], x_s.shape[0]
    n_rows = (n_p + n_s) * TOP_K
    n_blocks = -(-(n_rows + N_EXPERTS * (BM_EXPERT - 1)) // BM_EXPERT)
    g_ffn = norm_ffn[None]
    e_s, r_s, gates_s, _ = routing_s
    e_p, r_p, gates_p, cnt = routing_p
    plan = _moe_plan(cnt[:, 0], n_blocks)
    pos_p = _positions(plan, e_p, r_p, n_blocks, TM_COMBINE)
    pos_s = _positions(plan, e_s, r_s, n_blocks, n_s)
    h_sorted = _scatter_rows(cnt[:, 0], plan, x_p, pos_p, x_s, pos_s, g_ffn, n_blocks, TM_COMBINE)
    y_sorted = _expert_blocks(plan, h_sorted, w_gate_up, b_gate_up[:, None, :], w_down, b_down[:, None, :], n_blocks)
    y_p = _gather_combine(x_p, pos_p, gates_p, norm_final[None], y_sorted, TM_COMBINE)
    y_s = _gather_combine(x_s, pos_s, gates_s, norm_final[None], y_sorted, n_s)
    return y_p, y_s


def _rb(x):
    return x.astype(BF16).astype(F32)


def _sample_proj_kernel(x_ref, g_ref, wm_ref, wg_ref, z_ref):
    hb = _rms(x_ref[...], g_ref[...]).astype(BF16)
    z_ref[:, :MIX_WIDTH] = jnp.dot(hb, wm_ref[...], preferred_element_type=F32)
    z_ref[:, MIX_WIDTH:] = jnp.dot(hb, wg_ref[...], preferred_element_type=F32)


def _sample_proj(x, norm_mix, w_mix, w_gate):
    n, d = x.shape
    args = (x, norm_mix, w_mix, w_gate)
    width = w_mix.shape[1] + w_gate.shape[1]
    return pl.pallas_call(
        _sample_proj_kernel,
        grid=(1,),
        in_specs=[pl.BlockSpec(a.shape, lambda j: (0, 0)) for a in args],
        out_specs=pl.BlockSpec((n, width), lambda j: (0, 0)),
        out_shape=jax.ShapeDtypeStruct((n, width), F32),
        compiler_params=_cparams("arbitrary"),
        name="sample_proj",
    )(*args)


def _sample_mix_kernel(z_ref, c0_ref, c1_ref, c2_ref, mem_ref, sb0_ref, sb1_ref, sb2_ref, nb_ref, lng_ref, lnb_ref,
                       wsp_ref, bsp_ref, bo_ref, co_ref, a_ref, v_ref, n0_ref, n1_ref, n2_ref):
    z = z_ref[0]
    scale = 1.0 / math.sqrt(B_HEAD_DIM)
    row_head = lax.broadcasted_iota(I32, (SUBLANES, B_GROUP_W), 0)
    lane_head = lax.broadcasted_iota(I32, (SUBLANES, B_GROUP_W), 1) // B_HEAD_DIM
    own = (row_head == lane_head).astype(F32)
    kv_cols = jnp.transpose(jnp.broadcast_to(z[:, B_WIDTH:3 * B_WIDTH], (SUBLANES, 2 * B_WIDTH)))[:, 0:1]
    outs, lses = [], []
    for g, (cache_ref, sb_ref, new_ref) in enumerate(((c0_ref, sb0_ref, n0_ref), (c1_ref, sb1_ref, n1_ref),
                                                      (c2_ref, sb2_ref, n2_ref))):
        window = cache_ref.shape[2]
        q, kn, vn = (_rb(z[:, i * B_WIDTH + g * B_GROUP_W:i * B_WIDTH + (g + 1) * B_GROUP_W]) for i in range(3))
        qh = own * q
        kt = cache_ref[0, :B_GROUP_W, :].astype(BF16)
        vt = cache_ref[0, B_GROUP_W:, :].astype(BF16)
        lg = jnp.dot(qh.astype(BF16), kt, preferred_element_type=F32) * scale + sb_ref[...]
        ln = jnp.sum(qh * kn, axis=1, keepdims=True) * scale + nb_ref[g, :, 0:1]
        m = jnp.maximum(jnp.max(lg, axis=1, keepdims=True), ln)
        lse = m + jnp.log(jnp.sum(jnp.exp(lg - m), axis=1, keepdims=True) + jnp.exp(ln - m))
        p = jnp.exp(lg - lse).astype(BF16)
        pn = _rb(jnp.exp(ln - lse))
        pv = lax.dot_general(p, vt, (((1,), (1,)), ((), ())), preferred_element_type=F32) + pn * vn
        outs.append(jnp.sum(own * pv, axis=0, keepdims=True))
        lses.append(jnp.sum(own * lse, axis=0, keepdims=True))
        new_col = jnp.concatenate([kv_cols[i * B_WIDTH + g * B_GROUP_W:i * B_WIDTH + (g + 1) * B_GROUP_W]
                                   for i in range(2)], axis=0)
        lane = lax.broadcasted_iota(I32, (1, window), 1)
        new_ref[0] = jnp.where(lane == window - 1, new_col, pltpu.roll(cache_ref[0], window - 1, axis=1))
    m = jnp.maximum(jnp.maximum(lses[0], lses[1]), lses[2])
    ws = [jnp.exp(l - m) for l in lses]
    den = ws[0] + ws[1] + ws[2]
    bo_ref[0] = _rb(ws[0] / den) * _rb(outs[0]) + _rb(ws[1] / den) * _rb(outs[1]) + _rb(ws[2] / den) * _rb(outs[2])

    cs = []
    for h in range(C_HEADS):
        c0 = MIX_WIDTH - C_WIDTH + h * C_HEAD_DIM
        qh = _rb(z[:, c0:c0 + C_HEAD_DIM])
        kh = _rb(mem_ref[0, pl.ds(h, N_MEM, stride=2 * C_HEADS), :])
        vh = _rb(mem_ref[0, pl.ds(C_HEADS + h, N_MEM, stride=2 * C_HEADS), :])
        s = jnp.sum(kh * qh, axis=-1, keepdims=True) * (1.0 / math.sqrt(C_HEAD_DIM))
        p = jnp.exp(s - jnp.max(s, axis=0, keepdims=True))
        p = _rb(p / jnp.sum(p, axis=0, keepdims=True))
        cs.append(jnp.sum(p * vh, axis=0, keepdims=True))
    co_ref[0] = jnp.concatenate(cs, axis=1)

    u = _gelu(z[:, 3 * B_WIDTH:3 * B_WIDTH + A_WIDTH])
    v = _layernorm(_gelu(z[:, 3 * B_WIDTH + A_WIDTH:3 * B_WIDTH + 2 * A_WIDTH]), lng_ref[...], lnb_ref[...])
    v_ref[0] = v
    a_ref[0] = u * (_rb(wsp_ref[...]) * _rb(v) + bsp_ref[...])


def _sample_mix(z, caches, mem, sbias, nbias, ln_g, ln_b, wsp0, bsp0):
    n = z.shape[0]
    const = lambda arr: pl.BlockSpec(arr.shape, lambda i: (0,) * arr.ndim)
    row = lambda w: pl.BlockSpec((1, 1, w), lambda i: (i, 0, 0))
    per_req = lambda arr: pl.BlockSpec((1,) + arr.shape[1:], lambda i: (i, 0, 0))
    widths = (B_GROUP_W, C_WIDTH, A_WIDTH, A_WIDTH)
    return pl.pallas_call(
        _sample_mix_kernel,
        grid=(n,),
        in_specs=[row(z.shape[2])] + [per_req(c) for c in caches] + [per_req(mem)]
        + [const(a) for a in (*sbias, nbias, ln_g, ln_b, wsp0, bsp0)],
        out_specs=tuple(row(w) for w in widths) + tuple(per_req(c) for c in caches),
        out_shape=tuple(jax.ShapeDtypeStruct((n, 1, w), F32) for w in widths)
        + tuple(jax.ShapeDtypeStruct(c.shape, F32) for c in caches),
        compiler_params=_cparams("arbitrary"),
        name="sample_mix",
    )(z, *caches, mem, *sbias, nbias, ln_g, ln_b, wsp0, bsp0)


def _sample_merge_kernel(x_ref, z_ref, a_ref, b_ref, c_ref, wa_ref, wb_ref, wc_ref, wo_ref, o_ref):
    dot = lambda a, b: jnp.dot(a.astype(BF16), b, preferred_element_type=F32)
    merged = None
    for i, (act_ref, w_ref) in enumerate(((a_ref, wa_ref), (b_ref, wb_ref), (c_ref, wc_ref))):
        gate = jax.nn.sigmoid(z_ref[:, MIX_WIDTH + i * D_MODEL:MIX_WIDTH + (i + 1) * D_MODEL])
        term = gate * dot(act_ref[...], w_ref[...])
        merged = term if merged is None else merged + term
    o_ref[...] = x_ref[...] + dot(merged, wo_ref[...])


def _sample_merge(x, z, a, bo, co, wa, wb, wc, wo):
    args = (x, z, a, bo, co, wa, wb, wc, wo)
    return pl.pallas_call(
        _sample_merge_kernel,
        grid=(1,),
        in_specs=[pl.BlockSpec(arr.shape, lambda i: (0, 0)) for arr in args],
        out_specs=pl.BlockSpec(x.shape, lambda i: (0, 0)),
        out_shape=jax.ShapeDtypeStruct(x.shape, F32),
        compiler_params=_cparams("arbitrary"),
        name="sample_merge",
    )(*args)


def _sample_mixers(x_sample, caches, cache_mem, sbias, nbias, norm_mix, ln_v_g, ln_v_b, w_spatial, b_spatial, wts):
    n = x_sample.shape[0]
    x = x_sample[:, 0]
    z = _sample_proj(x, norm_mix[None], wts["mix"], wts["gate"])
    mem = cache_mem.reshape(n, N_MEM * 2 * C_HEADS, C_HEAD_DIM)
    wsp0 = jnp.repeat(w_spatial[:, 0, 0], LANES)[None]
    bsp0 = jnp.repeat(b_spatial[:, 0], LANES)[None]
    bo, co, a, v, *new_caches = _sample_mix(z[:, None], [_from_window_layout(c) for c in caches], mem, sbias, nbias,
                                            ln_v_g[None], ln_v_b[None], wsp0, bsp0)
    x_new = _sample_merge(x, z, a[:, 0], bo[:, 0], co[:, 0], wts["a"], wts["b"], wts["c"], wts["out"])
    return x_new, [_to_window_layout(c) for c in new_caches], v[:, 0]


def _mixer_weights(w_in, w_branch_a, w_branch_b, w_branch_c, w_out):
    return {"mix": w_in[:, :MIX_WIDTH].astype(BF16), "gate": w_in[:, MIX_WIDTH:].astype(BF16),
            "a": w_branch_a.astype(BF16), "b": w_branch_b.astype(BF16), "c": w_branch_c.astype(BF16),
            "out": w_out.astype(BF16)}


def kernel(x_prompt, x_sample, cache_win0_kv, cache_win1_kv, cache_win2_kv, cache_mem_kv, mem_prompt, rel_bias,
           norm_mix, norm_mem, w_in, ln_v_g, ln_v_b, w_spatial, b_spatial, w_mem_kv, w_branch_a, w_branch_b,
           w_branch_c, w_out, norm_ffn, w_router, b_router, w_gate_up, b_gate_up, w_down, b_down, norm_final):
    assert norm_mix.shape[0] == 1, "one layer"
    b, s, d = x_prompt.shape
    n_s = x_sample.shape[0]
    caches = (cache_win0_kv[0], cache_win1_kv[0], cache_win2_kv[0])
    assert all(c.shape[1] == w for c, (w, _) in zip(caches, B_PAIRS)), "window buffers hold a full window"
    wts = _mixer_weights(w_in[0], w_branch_a[0], w_branch_b[0], w_branch_c[0], w_out[0])
    band_bias, *sbias, nbias = _bias_tables(rel_bias)
    route_params = _route_params(norm_ffn[0], w_router[0], b_router[0])
    xs, win_s, v_s = _sample_mixers(x_sample, caches, cache_mem_kv[0], sbias, nbias, norm_mix[0], ln_v_g[0],
                                    ln_v_b[0], w_spatial[0], b_spatial[0], wts)
    routing_s = _route(xs, route_params, jnp.zeros((N_EXPERTS, LANES), I32), n_s)
    xp, routing_p, win_p, mem_p = _prompt_mixers(x_prompt, mem_prompt, band_bias, norm_mix[0], norm_mem[0], ln_v_g[0],
                                                 ln_v_b[0], w_spatial[0], b_spatial[0], w_mem_kv[0], wts,
                                                 route_params, routing_s[3])
    y_p, y_s = _moe_and_final_norm(xp.reshape(b * s, d), routing_p, xs, routing_s, norm_ffn[0], w_gate_up[0],
                                   b_gate_up[0], w_down[0], b_down[0], norm_final)
    chunk_v = v_s.reshape(1, n_s, 1, A_GROUPS, LANES)
    return (y_p.reshape(b, s, d), y_s[:, None], win_p[0], win_p[1], win_p[2], mem_p,
            win_s[0], win_s[1], win_s[2], chunk_v)
```
